```python
import jax
import jax.numpy as jnp
from jax import lax
import numpy as np

D_MODEL = 1024
BATCH = 16
SEQ = 4096
DEPTH = 2
DEC_BATCH = 8
DEC_SEQ = 64
PAST_LEN = 4096

CHUNK = 64
Q_BLOCK = 128
EPS = 1e-6
D_RNN = D_MODEL
LRU_BLOCKS = 16
LRU_BLOCK_W = D_RNN // LRU_BLOCKS
CONV_W = 4
LRU_C = 8.0
N_HEADS = 16
NOPE_DIM = 64
ROPE_DIM = 32
QK_DIM = NOPE_DIM + ROPE_DIM
V_DIM = D_MODEL // N_HEADS
Q_LORA = 384
KV_LORA = 256
ROPE_BASE = 10000.0
N_MEM = 256
MEM_HEADS = 4
MEM_HEAD_DIM = D_MODEL // MEM_HEADS
D_FF = 2816
N_EXPERTS = 8
TOP_K = 2
EXPERT_FF = 1024
N_DENSE = (DEPTH + 1) // 2
N_MOE = DEPTH // 2
SPLIT_POINTS = (D_RNN, 2 * D_RNN, 2 * D_RNN + Q_LORA, 2 * D_RNN + Q_LORA + KV_LORA,
                2 * D_RNN + Q_LORA + KV_LORA + ROPE_DIM, 2 * D_RNN + Q_LORA + KV_LORA + ROPE_DIM + D_MODEL)
IN_COLS = 2 * D_RNN + Q_LORA + KV_LORA + ROPE_DIM + 2 * D_MODEL

kernel_name = 'hybrid_rglru_mla_stream_step'


def rms_norm(x, g):
    xf = x.astype(jnp.float32)
    y = xf * lax.rsqrt(jnp.mean(xf * xf, axis=-1, keepdims=True) + EPS)
    return (y * g.astype(jnp.float32)).astype(x.dtype)


def apply_rope(x, pos):
    half = ROPE_DIM // 2
    inv_freq = ROPE_BASE ** (-jnp.arange(half, dtype=jnp.float32) / half)
    ang = pos.astype(jnp.float32)[:, None] * inv_freq[None, :]
    ang = ang.reshape((ang.shape[0],) + (1,) * (x.ndim - 3) + (half,))
    cos, sin = jnp.cos(ang), jnp.sin(ang)
    xf = x.astype(jnp.float32)
    x1, x2 = xf[..., :half], xf[..., half:]
    return jnp.concatenate([x1 * cos - x2 * sin, x1 * sin + x2 * cos], axis=-1).astype(x.dtype)


def causal_conv(x, prev, w, b):
    L = x.shape[1]
    xp = jnp.concatenate([prev, x], axis=1)
    y = b + w[CONV_W - 1] * x
    for j in range(CONV_W - 1):
        y = y + w[j] * xp[:, j:j + L]
    return y, xp[:, L:]


def rg_lru(x, h0, wa, ba, wi, bi, lam):
    B, L, C = x.shape
    xb = x.reshape(B, L, LRU_BLOCKS, LRU_BLOCK_W)
    gate_r = jnp.einsum('blni,nij->blnj', xb, wa).reshape(B, L, C) + ba
    gate_i = jnp.einsum('blni,nij->blnj', xb, wi).reshape(B, L, C) + bi
    r = jax.nn.sigmoid(gate_r.astype(jnp.float32))
    i = jax.nn.sigmoid(gate_i.astype(jnp.float32))
    log_a = -LRU_C * r * jax.nn.softplus(-lam.astype(jnp.float32))
    a = jnp.exp(log_a)
    u = jnp.sqrt(-jnp.expm1(2.0 * log_a)) * (i * x.astype(jnp.float32))

    def combine(left, right):
        return left[0] * right[0], right[0] * left[1] + right[1]

    a_cum, u_cum = lax.associative_scan(combine, (a, u), axis=1)
    h = a_cum * h0.astype(jnp.float32)[:, None, :] + u_cum
    return h.astype(x.dtype), h[:, -1].astype(x.dtype)


def chunk_causal_attention(q, k, v, q_pos, k_pos):
    s = jnp.einsum('bqhd,bkhd->bhqk', q, k).astype(jnp.float32) * (QK_DIM ** -0.5)
    visible = (k_pos[None, :] // CHUNK) <= (q_pos[:, None] // CHUNK)
    s = jnp.where(visible[None, None], s, -jnp.inf)
    p = jax.nn.softmax(s, axis=-1).astype(v.dtype)
    return jnp.einsum('bhqk,bkhd->bqhd', p, v)


def blocked_attention(q, k, v, q_pos, k_pos):
    B, L, H, Dq = q.shape
    if L <= Q_BLOCK or L % Q_BLOCK:
        return chunk_causal_attention(q, k, v, q_pos, k_pos)
    nb = L // Q_BLOCK
    qb = q.reshape(B, nb, Q_BLOCK, H, Dq).transpose(1, 0, 2, 3, 4)
    pb = q_pos.reshape(nb, Q_BLOCK)
    ob = lax.map(lambda blk: chunk_causal_attention(blk[0], k, v, blk[1], k_pos), (qb, pb))
    return ob.transpose(1, 0, 2, 3, 4).reshape(B, L, H, v.shape[-1])


def memory_kv(mem, g_in, w_k, w_v, g_k):
    B, M, _ = mem.shape
    m = rms_norm(mem, g_in)
    k = rms_norm((m @ w_k).reshape(B, M, MEM_HEADS, MEM_HEAD_DIM), g_k)
    v = (m @ w_v).reshape(B, M, MEM_HEADS, MEM_HEAD_DIM)
    return k, v


def memory_attention(h, mem_k, mem_v, w_q, g_q, w_o):
    B, L, _ = h.shape
    q = rms_norm((h @ w_q).reshape(B, L, MEM_HEADS, MEM_HEAD_DIM), g_q)
    s = jnp.einsum('blhd,bmhd->bhlm', q, mem_k).astype(jnp.float32) * (MEM_HEAD_DIM ** -0.5)
    p = jax.nn.softmax(s, axis=-1).astype(mem_v.dtype)
    o = jnp.einsum('bhlm,bmhd->blhd', p, mem_v).reshape(B, L, D_MODEL)
    return o @ w_o


def swiglu(h, w1, w3, w2):
    return (jax.nn.silu(h @ w1) * (h @ w3)) @ w2


def moe_swiglu(h, w_router, b_router, w1, w3, w2):
    logits = (h @ w_router).astype(jnp.float32) + b_router.astype(jnp.float32)
    probs = jax.nn.softmax(logits, axis=-1)
    top_p, top_i = lax.top_k(probs, TOP_K)
    top_p = top_p / jnp.sum(top_p, axis=-1, keepdims=True)
    gates = jnp.einsum('blk,blke->ble', top_p,
                       jax.nn.one_hot(top_i, N_EXPERTS, dtype=jnp.float32)).astype(h.dtype)
    out = jnp.zeros_like(h)
    for e in range(N_EXPERTS):
        out = out + gates[..., e:e + 1] * swiglu(h, w1[e], w3[e], w2[e])
    return out


def setup_inputs(seed: int = 0) -> dict:
    key = jax.random.key(seed)
    ks = iter(jax.random.split(key, 64))
    f32 = jnp.float32

    def nrm(shape, scale):
        return jax.random.normal(next(ks), shape, f32) * scale

    def gain(shape):
        return 1.0 + nrm(shape, 0.02)

    u = jax.random.uniform(next(ks), (DEPTH, D_RNN), f32, 0.9, 0.999)
    s = u ** (1.0 / LRU_C)
    lru_lambda = jnp.log(s) - jnp.log1p(-s)
    return {
        'x_prompt': nrm((BATCH, SEQ, D_MODEL), 1.0),
        'x_sample': nrm((DEC_BATCH, DEC_SEQ, D_MODEL), 1.0),
        'cache_ckv': nrm((DEPTH, DEC_BATCH, PAST_LEN, KV_LORA), 1.0),
        'cache_krope': nrm((DEPTH, DEC_BATCH, PAST_LEN, ROPE_DIM), 1.0),
        'cache_mem_k': nrm((DEPTH, DEC_BATCH, N_MEM, MEM_HEADS, MEM_HEAD_DIM), 1.0),
        'cache_mem_v': nrm((DEPTH, DEC_BATCH, N_MEM, MEM_HEADS, MEM_HEAD_DIM), 1.0),
        'state_lru': nrm((DEPTH, DEC_BATCH, D_RNN), 0.5),
        'state_conv': nrm((DEPTH, DEC_BATCH, CONV_W - 1, D_RNN), 1.0),
        'mem_prompt': nrm((BATCH, N_MEM, D_MODEL), 1.0),
        'norm_mix': gain((DEPTH, D_MODEL)),
        'w_in': nrm((DEPTH, D_MODEL, IN_COLS), D_MODEL ** -0.5),
        'conv_w': nrm((DEPTH, CONV_W, D_RNN), CONV_W ** -0.5),
        'conv_b': nrm((DEPTH, D_RNN), 0.01),
        'lru_wa': nrm((DEPTH, LRU_BLOCKS, LRU_BLOCK_W, LRU_BLOCK_W), LRU_BLOCK_W ** -0.5),
        'lru_ba': nrm((DEPTH, D_RNN), 0.01),
        'lru_wi': nrm((DEPTH, LRU_BLOCKS, LRU_BLOCK_W, LRU_BLOCK_W), LRU_BLOCK_W ** -0.5),
        'lru_bi': nrm((DEPTH, D_RNN), 0.01),
        'lru_lambda': lru_lambda,
        'q_lat_norm': gain((DEPTH, Q_LORA)),
        'w_uq': nrm((DEPTH, Q_LORA, N_HEADS * QK_DIM), Q_LORA ** -0.5),
        'kv_lat_norm': gain((DEPTH, KV_LORA)),
        'w_uk': nrm((DEPTH, KV_LORA, N_HEADS * NOPE_DIM), KV_LORA ** -0.5),
        'w_uv': nrm((DEPTH, KV_LORA, N_HEADS * V_DIM), KV_LORA ** -0.5),
        'q_head_norm': gain((DEPTH, QK_DIM)),
        'k_head_norm': gain((DEPTH, QK_DIM)),
        'w_out': nrm((DEPTH, D_MODEL, D_MODEL), D_MODEL ** -0.5),
        'norm_mem': gain((DEPTH, D_MODEL)),
        'mem_in_norm': gain((DEPTH, D_MODEL)),
        'w_mq': nrm((DEPTH, D_MODEL, D_MODEL), D_MODEL ** -0.5),
        'w_mk': nrm((DEPTH, D_MODEL, D_MODEL), D_MODEL ** -0.5),
        'w_mv': nrm((DEPTH, D_MODEL, D_MODEL), D_MODEL ** -0.5),
        'w_mo': nrm((DEPTH, D_MODEL, D_MODEL), D_MODEL ** -0.5),
        'mq_head_norm': gain((DEPTH, MEM_HEAD_DIM)),
        'mk_head_norm': gain((DEPTH, MEM_HEAD_DIM)),
        'norm_ffn': gain((DEPTH, D_MODEL)),
        'ffn_w1': nrm((N_DENSE, D_MODEL, D_FF), D_MODEL ** -0.5),
        'ffn_w3': nrm((N_DENSE, D_MODEL, D_FF), D_MODEL ** -0.5),
        'ffn_w2': nrm((N_DENSE, D_FF, D_MODEL), D_FF ** -0.5),
        'moe_router': nrm((N_MOE, D_MODEL, N_EXPERTS), D_MODEL ** -0.5),
        'moe_router_b': nrm((N_MOE, N_EXPERTS), 0.01),
        'moe_w1': nrm((N_MOE, N_EXPERTS, D_MODEL, EXPERT_FF), D_MODEL ** -0.5),
        'moe_w3': nrm((N_MOE, N_EXPERTS, D_MODEL, EXPERT_FF), D_MODEL ** -0.5),
        'moe_w2': nrm((N_MOE, N_EXPERTS, EXPERT_FF, D_MODEL), EXPERT_FF ** -0.5),
    }


def reference(x_prompt, x_sample, cache_ckv, cache_krope, cache_mem_k, cache_mem_v, state_lru, state_conv,
              mem_prompt, norm_mix, w_in, conv_w, conv_b, lru_wa, lru_ba, lru_wi, lru_bi, lru_lambda,
              q_lat_norm, w_uq, kv_lat_norm, w_uk, w_uv, q_head_norm, k_head_norm, w_out,
              norm_mem, mem_in_norm, w_mq, w_mk, w_mv, w_mo, mq_head_norm, mk_head_norm,
              norm_ffn, ffn_w1, ffn_w3, ffn_w2, moe_router, moe_router_b, moe_w1, moe_w3, moe_w2):

    def layer(l, x, pos, past_pos, conv_prev, h0, ckv_past, kr_past, mem_k, mem_v):
        B, L, _ = x.shape
        h = rms_norm(x, norm_mix[l])
        x_rec, x_gate, c_q, c_kv, k_rot, g_rec, g_att = jnp.split(h @ w_in[l], SPLIT_POINTS, axis=-1)
        x_conv, conv_state = causal_conv(x_rec, conv_prev, conv_w[l], conv_b[l])
        h_rec, h_last = rg_lru(x_conv, h0, lru_wa[l], lru_ba[l], lru_wi[l], lru_bi[l], lru_lambda[l])
        rec_out = jax.nn.gelu(x_gate) * h_rec
        c_kv = rms_norm(c_kv, kv_lat_norm[l])
        k_rot = apply_rope(k_rot, pos)
        ckv_all = jnp.concatenate([ckv_past, c_kv], axis=1)
        kr_all = jnp.concatenate([kr_past, k_rot], axis=1)
        k_pos = jnp.concatenate([past_pos, pos])
        T = ckv_all.shape[1]
        q = (rms_norm(c_q, q_lat_norm[l]) @ w_uq[l]).reshape(B, L, N_HEADS, QK_DIM)
        q = jnp.concatenate([q[..., :NOPE_DIM], apply_rope(q[..., NOPE_DIM:], pos)], axis=-1)
        q = rms_norm(q, q_head_norm[l])
        k_nope = (ckv_all @ w_uk[l]).reshape(B, T, N_HEADS, NOPE_DIM)
        k = jnp.concatenate([k_nope, jnp.broadcast_to(kr_all[:, :, None, :], (B, T, N_HEADS, ROPE_DIM))], axis=-1)
        k = rms_norm(k, k_head_norm[l])
        v = (ckv_all @ w_uv[l]).reshape(B, T, N_HEADS, V_DIM)
        att_out = blocked_attention(q, k, v, pos, k_pos).reshape(B, L, N_HEADS * V_DIM)
        mixed = jax.nn.sigmoid(g_rec) * rec_out + jax.nn.sigmoid(g_att) * att_out
        x = x + mixed @ w_out[l]
        x = x + memory_attention(rms_norm(x, norm_mem[l]), mem_k, mem_v, w_mq[l], mq_head_norm[l], w_mo[l])
        hf = rms_norm(x, norm_ffn[l])
        j = l // 2
        if l % 2 == 0:
            x = x + swiglu(hf, ffn_w1[j], ffn_w3[j], ffn_w2[j])
        else:
            x = x + moe_swiglu(hf, moe_router[j], moe_router_b[j], moe_w1[j], moe_w3[j], moe_w2[j])
        return x, conv_state, h_last, c_kv, k_rot

    Bp, Lp, _ = x_prompt.shape
    dt_p = x_prompt.dtype
    pos_p = jnp.arange(Lp, dtype=jnp.int32)
    no_pos = jnp.zeros((0,), jnp.int32)
    zero_conv = jnp.zeros((Bp, CONV_W - 1, D_RNN), dt_p)
    zero_h = jnp.zeros((Bp, D_RNN), dt_p)
    no_ckv = jnp.zeros((Bp, 0, KV_LORA), dt_p)
    no_kr = jnp.zeros((Bp, 0, ROPE_DIM), dt_p)
    x = x_prompt
    p_ckv, p_kr, p_lru, p_conv, p_mk, p_mv = [], [], [], [], [], []
    for l in range(DEPTH):
        mk, mv = memory_kv(mem_prompt, mem_in_norm[l], w_mk[l], w_mv[l], mk_head_norm[l])
        x, cs, hl, ckv, kr = layer(l, x, pos_p, no_pos, zero_conv, zero_h, no_ckv, no_kr, mk, mv)
        p_ckv.append(ckv)
        p_kr.append(kr)
        p_lru.append(hl)
        p_conv.append(cs)
        p_mk.append(mk)
        p_mv.append(mv)
    y_prompt = x

    Ls = x_sample.shape[1]
    T_past = cache_ckv.shape[2]
    past_pos_s = jnp.arange(T_past, dtype=jnp.int32)
    pos_s = T_past + jnp.arange(Ls, dtype=jnp.int32)
    x = x_sample
    s_ckv, s_kr, s_lru, s_conv = [], [], [], []
    for l in range(DEPTH):
        x, cs, hl, ckv, kr = layer(l, x, pos_s, past_pos_s, state_conv[l], state_lru[l],
                                   cache_ckv[l], cache_krope[l], cache_mem_k[l], cache_mem_v[l])
        s_ckv.append(ckv)
        s_kr.append(kr)
        s_lru.append(hl)
        s_conv.append(cs)
    y_sample = x

    new_ckv_p = jnp.stack(p_ckv)
    new_krope_p = jnp.stack(p_kr)
    new_lru_p = jnp.stack(p_lru)
    new_conv_p = jnp.stack(p_conv)
    new_mem_k_p = jnp.stack(p_mk)
    new_mem_v_p = jnp.stack(p_mv)
    new_ckv_s = jnp.stack(s_ckv)
    new_krope_s = jnp.stack(s_kr)
    new_lru_s = jnp.stack(s_lru)
    new_conv_s = jnp.stack(s_conv)
    return (y_prompt, y_sample, new_ckv_p, new_krope_p, new_lru_p, new_conv_p, new_mem_k_p, new_mem_v_p,
            new_ckv_s, new_krope_s, new_lru_s, new_conv_s)
```

```python
import functools
import math

import jax
import jax.numpy as jnp
from jax import lax
from jax.experimental import pallas as pl
from jax.experimental.pallas import tpu as pltpu

F32 = jnp.float32
BF16 = jnp.bfloat16

EPS = 1e-6
CHUNK = 64
LRU_BLOCKS = 16
LRU_C = 8.0
CONV_W = 4
N_HEADS = 16
NOPE_DIM = 64
ROPE_DIM = 32
QK_DIM = NOPE_DIM + ROPE_DIM
ROPE_BASE = 10000.0
MEM_HEADS = 4
TOP_K = 2

LANES = 128
SUBLANES = 8
MXU_DIM = 256
VMEM_LIMIT = 56 * 1024 * 1024

HEAD_PAD = LANES
NEG = -1e30
LOG2E = 1.4426950408889634


def _resident(shape):
    nd = len(shape)
    return pl.BlockSpec(shape, lambda *_: (0,) * nd, pipeline_mode=pl.Buffered(1))


def _params(n_axes):
    return pltpu.CompilerParams(dimension_semantics=("arbitrary",) * n_axes,
                                vmem_limit_bytes=VMEM_LIMIT)


def _rms(x, gain):
    return x * lax.rsqrt(jnp.mean(x * x, axis=-1, keepdims=True) + EPS) * gain


def _dot(a, b):
    return jnp.dot(a, b, preferred_element_type=F32)


def _dot_nt(a, b):
    return lax.dot_general(a, b, (((1,), (1,)), ((), ())), preferred_element_type=F32)


def _token_tile(n, cap=512):
    t = min(n, cap)
    assert n % t == 0 and t % SUBLANES == 0
    return t


def _memkv_kernel(mem_ref, gin_ref, wk_ref, wv_ref, gk_ref, k_ref, v_ref):
    mn = _rms(mem_ref[0], gin_ref[...]).astype(BF16)
    k = _dot(mn, wk_ref[...])
    hd = gk_ref.shape[-1]
    for h in range(MEM_HEADS):
        sl = slice(h * hd, (h + 1) * hd)
        k_ref[0, :, sl] = _rms(k[:, sl], gk_ref[...])
    v_ref[0] = _dot(mn, wv_ref[...])


def _memkv(mem, g_in, w_k, w_v, g_k):
    B, M, D = mem.shape
    tok = pl.BlockSpec((1, M, D), lambda b: (b, 0, 0))
    return pl.pallas_call(
        _memkv_kernel,
        grid=(B,),
        in_specs=[tok, _resident((1, D)), _resident((D, D)), _resident((D, D)),
                  _resident((1, D // MEM_HEADS))],
        out_specs=[tok, tok],
        out_shape=[jax.ShapeDtypeStruct((B, M, D), F32)] * 2,
        compiler_params=_params(1),
        name="memkv",
    )(mem, g_in.reshape(1, D), w_k.astype(BF16), w_v.astype(BF16), g_k.reshape(1, -1))


def _inproj_kernel(x_ref, prev_ref, h0_ref, gmix_ref, win_ref, cw_ref, cb_ref, wa_ref, ba_ref,
                   wi_ref, bi_ref, sp_ref, gq_ref, gkv_ref, cos_ref, sin_ref,
                   grec_ref, gatt_ref, cq_ref, ckv_ref, kr_ref, ctail_ref, htail_ref,
                   xext_ref, hcar_ref, *, tm, d, q_lora, kv_lora):
    @pl.when(pl.program_id(1) == 0)
    def _():
        xext_ref[0:SUBLANES, :] = prev_ref[0]
        hcar_ref[...] = h0_ref[0]

    hn = _rms(x_ref[0], gmix_ref[...]).astype(BF16)

    def proj(c0, c1):
        return _dot(hn, win_ref[:, c0:c1])

    x_rec = proj(0, d)
    xext_ref[SUBLANES:SUBLANES + tm, :] = x_rec
    xc = cb_ref[...] + cw_ref[CONV_W - 1:CONV_W, :] * x_rec
    for j in range(CONV_W - 1):
        off = SUBLANES - (CONV_W - 1) + j
        xc = xc + cw_ref[j:j + 1, :] * xext_ref[off:off + tm, :]
    tail = x_rec[tm - SUBLANES:tm, :]
    ctail_ref[0] = tail
    xext_ref[0:SUBLANES, :] = tail

    xcb = xc.astype(BF16)
    nb = d // MXU_DIM
    gr = jnp.concatenate(
        [_dot(xcb[:, c * MXU_DIM:(c + 1) * MXU_DIM], wa_ref[c]) for c in range(nb)], axis=1)
    gi = jnp.concatenate(
        [_dot(xcb[:, c * MXU_DIM:(c + 1) * MXU_DIM], wi_ref[c]) for c in range(nb)], axis=1)
    r = jax.nn.sigmoid(gr + ba_ref[...])
    ig = jax.nn.sigmoid(gi + bi_ref[...])
    log_a = (-LRU_C) * r * sp_ref[...]
    a = jnp.exp(log_a)
    u = jnp.sqrt(1.0 - a * a) * (ig * xc)

    row = lax.broadcasted_iota(jnp.int32, (tm, 1), 0)
    s = 1
    while s < tm:
        keep = row >= s
        u = a * jnp.where(keep, pltpu.roll(u, s, 0), 0.0) + u
        a = a * jnp.where(keep, pltpu.roll(a, s, 0), 1.0)
        s *= 2
    h = a * hcar_ref[SUBLANES - 1:SUBLANES, :] + u
    htail = h[tm - SUBLANES:tm, :]
    hcar_ref[...] = htail
    htail_ref[0] = htail

    x_gate = proj(d, 2 * d)
    g_rec = proj(2 * d, 3 * d)
    grec_ref[0] = jax.nn.sigmoid(g_rec) * (jax.nn.gelu(x_gate) * h)
    gatt_ref[0] = jax.nn.sigmoid(proj(3 * d, 4 * d))

    o = 4 * d
    cq_ref[0] = _rms(proj(o, o + q_lora), gq_ref[...]).astype(BF16)
    o += q_lora
    ckv_ref[0] = _rms(proj(o, o + kv_lora), gkv_ref[...])
    o += kv_lora
    kr = proj(o, o + LANES) * cos_ref[...] + proj(o + LANES, o + 2 * LANES) * sin_ref[...]
    kr_ref[0] = kr[:, :ROPE_DIM]


def _inproj(x, prev8, h0_8, w, cos_k, sin_k):
    B, L, D = x.shape
    tm = _token_tile(L, 256)
    q_lora, kv_lora = w["gq"].shape[-1], w["gkv"].shape[-1]
    ncol = w["w_in"].shape[-1]
    tok = lambda n: pl.BlockSpec((1, tm, n), lambda b, l: (b, l, 0))
    per_b = pl.BlockSpec((1, SUBLANES, D), lambda b, l: (b, 0, 0))
    tab = pl.BlockSpec((tm, LANES), lambda b, l: (l, 0))
    nb = D // MXU_DIM
    kern = functools.partial(_inproj_kernel, tm=tm, d=D, q_lora=q_lora, kv_lora=kv_lora)
    return pl.pallas_call(
        kern,
        grid=(B, L // tm),
        in_specs=[tok(D), per_b, per_b, _resident((1, D)), _resident((D, ncol)),
                  _resident((CONV_W, D)), _resident((1, D)),
                  _resident((nb, MXU_DIM, MXU_DIM)), _resident((1, D)),
                  _resident((nb, MXU_DIM, MXU_DIM)), _resident((1, D)), _resident((1, D)),
                  _resident((1, q_lora)), _resident((1, kv_lora)), tab, tab],
        out_specs=[tok(D), tok(D), tok(q_lora), tok(kv_lora), tok(ROPE_DIM), per_b, per_b],
        out_shape=[jax.ShapeDtypeStruct((B, L, D), F32), jax.ShapeDtypeStruct((B, L, D), F32),
                   jax.ShapeDtypeStruct((B, L, q_lora), BF16),
                   jax.ShapeDtypeStruct((B, L, kv_lora), F32),
                   jax.ShapeDtypeStruct((B, L, ROPE_DIM), F32),
                   jax.ShapeDtypeStruct((B, SUBLANES, D), F32),
                   jax.ShapeDtypeStruct((B, SUBLANES, D), F32)],
        scratch_shapes=[pltpu.VMEM((SUBLANES + tm, D), F32), pltpu.VMEM((SUBLANES, D), F32)],
        compiler_params=_params(2),
        name="inproj",
    )(x, prev8, h0_8, w["gmix"], w["w_in"], w["conv_w"], w["conv_b"], w["wa"], w["ba"],
      w["wi"], w["bi"], w["sp"], w["gq"], w["gkv"], cos_k, sin_k)


def _qprep_kernel(cq_ref, wa_ref, wb_ref, cos_ref, sin_ref, g_ref, q_ref):
    cq = cq_ref[0]
    qa = _dot(cq, wa_ref[...])
    qb = _dot(cq, wb_ref[...])
    for h in range(N_HEADS):
        sl = slice(h * HEAD_PAD, (h + 1) * HEAD_PAD)
        qh = qa[:, sl] * cos_ref[...] + qb[:, sl] * sin_ref[...]
        ss = jnp.sum(qh * qh, axis=-1, keepdims=True) * (1.0 / QK_DIM)
        q_ref[0, :, sl] = (qh * lax.rsqrt(ss + EPS) * g_ref[...]).astype(BF16)


def _qprep(cq, w, cos_q, sin_q):
    B, L, QL = cq.shape
    tm = _token_tile(L)
    n = N_HEADS * HEAD_PAD
    tab = pl.BlockSpec((tm, HEAD_PAD), lambda b, l: (l, 0))
    return pl.pallas_call(
        _qprep_kernel,
        grid=(B, L // tm),
        in_specs=[pl.BlockSpec((1, tm, QL), lambda b, l: (b, l, 0)),
                  _resident((QL, n)), _resident((QL, n)), tab, tab, _resident((1, HEAD_PAD))],
        out_specs=pl.BlockSpec((1, tm, n), lambda b, l: (b, l, 0)),
        out_shape=jax.ShapeDtypeStruct((B, L, n), BF16),
        compiler_params=_params(2),
        name="qprep",
    )(cq, w["wq_a"], w["wq_b"], cos_q, sin_q, w["gqh"])


def _kvprep_kernel(ckv_ref, kr_ref, wk_ref, place_ref, wvt_ref, g_ref, k_ref, vt_ref):
    c = ckv_ref[0].astype(BF16)
    k = _dot(c, wk_ref[...]) + _dot(kr_ref[0].astype(BF16), place_ref[...])
    for h in range(N_HEADS):
        sl = slice(h * HEAD_PAD, (h + 1) * HEAD_PAD)
        kh = k[:, sl]
        ss = jnp.sum(kh * kh, axis=-1, keepdims=True) * (1.0 / QK_DIM)
        k_ref[0, :, sl] = (kh * lax.rsqrt(ss + EPS) * g_ref[...]).astype(BF16)
    vt_ref[0, 0] = _dot_nt(wvt_ref[...], c).astype(BF16)


def _kvprep(ckv, kr, w, tk):
    B, T, KV = ckv.shape
    n = N_HEADS * HEAD_PAD
    dv = w["wv_t"].shape[0]
    return pl.pallas_call(
        _kvprep_kernel,
        grid=(B, T // tk),
        in_specs=[pl.BlockSpec((1, tk, KV), lambda b, t: (b, t, 0)),
                  pl.BlockSpec((1, tk, ROPE_DIM), lambda b, t: (b, t, 0)),
                  _resident((KV, n)), _resident((ROPE_DIM, n)), _resident((dv, KV)),
                  _resident((1, HEAD_PAD))],
        out_specs=[pl.BlockSpec((1, tk, n), lambda b, t: (b, t, 0)),
                   pl.BlockSpec((1, 1, dv, tk), lambda b, t: (b, t, 0, 0))],
        out_shape=[jax.ShapeDtypeStruct((B, T, n), BF16),
                   jax.ShapeDtypeStruct((B, T // tk, dv, tk), BF16)],
        compiler_params=_params(2),
        name="kvprep",
    )(ckv, kr, w["wk"], w["place"], w["wv_t"], w["gkh"])


def _flash_kernel(q_ref, k_ref, vt_ref, o_ref, *, tq, tk, q_off, t_valid, v_dim):
    q_start = q_off + pl.program_id(2) * tq
    n_full = jnp.minimum(q_start, t_valid) // tk
    n_end = (jnp.minimum(q_start + tq, t_valid) + tk - 1) // tk
    shift = int(math.log2(CHUNK))
    q_chunk = lax.shift_right_logical(q_start + lax.broadcasted_iota(jnp.int32, (1, tq), 1), shift)
    k_iota = lax.broadcasted_iota(jnp.int32, (tk, 1), 0)

    accs = []
    for hh in range(2):
        qh = q_ref[0, :, hh * HEAD_PAD:(hh + 1) * HEAD_PAD]

        def step(j, carry, masked):
            m, l, acc = carry
            ks = pl.multiple_of(j * tk, tk)
            kj = k_ref[0, pl.ds(ks, tk), hh * HEAD_PAD:(hh + 1) * HEAD_PAD]
            s = _dot_nt(kj, qh)
            if masked:
                k_pos = ks + k_iota
                vis = (lax.shift_right_logical(k_pos, shift) <= q_chunk) & (k_pos < t_valid)
                s = jnp.where(vis, s, NEG)
            m_new = jnp.maximum(m, jnp.max(s, axis=0, keepdims=True))
            alpha = jnp.exp2(m - m_new)
            p = jnp.exp2(s - m_new)
            l = alpha * l + jnp.sum(p, axis=0, keepdims=True)
            vj = vt_ref[0, j, hh * v_dim:(hh + 1) * v_dim, :]
            acc = alpha * acc + _dot(vj, p.astype(BF16))
            return m_new, l, acc

        init = (jnp.full((1, tq), NEG, F32), jnp.zeros((1, tq), F32), jnp.zeros((v_dim, tq), F32))
        carry = lax.fori_loop(0, n_full, functools.partial(step, masked=False), init)
        _, l, acc = lax.fori_loop(n_full, n_end, functools.partial(step, masked=True), carry)
        accs.append(acc / l)
    o_ref[0] = jnp.concatenate(accs, axis=0).T


def _flash(q, k, vt, *, q_off, t_valid, tk):
    B, L, n = q.shape
    T = k.shape[1]
    dv = vt.shape[2]
    v_dim = dv // N_HEADS
    tq = _token_tile(L)
    assert tq % CHUNK == 0 and q_off % CHUNK == 0 and 2 * v_dim == LANES
    kern = functools.partial(_flash_kernel, tq=tq, tk=tk, q_off=q_off, t_valid=t_valid, v_dim=v_dim)
    return pl.pallas_call(
        kern,
        grid=(B, N_HEADS // 2, L // tq),
        in_specs=[pl.BlockSpec((1, tq, 2 * HEAD_PAD), lambda b, p, i: (b, i, p)),
                  pl.BlockSpec((1, T, 2 * HEAD_PAD), lambda b, p, i: (b, 0, p)),
                  pl.BlockSpec((1, T // tk, 2 * v_dim, tk), lambda b, p, i: (b, 0, p, 0))],
        out_specs=pl.BlockSpec((1, tq, 2 * v_dim), lambda b, p, i: (b, i, p)),
        out_shape=jax.ShapeDtypeStruct((B, L, dv), F32),
        compiler_params=_params(3),
        name="flash",
    )(q, k, vt)


def _post_kernel(x_ref, grec_ref, gatt_ref, att_ref, wo_ref, gmem_ref, wmq_ref, gqh_ref,
                 mk_ref, mv_ref, wmo_ref, o_ref):
    mixed = (grec_ref[0] + gatt_ref[0] * att_ref[0]).astype(BF16)
    x1 = x_ref[0] + _dot(mixed, wo_ref[...])
    qm = _dot(_rms(x1, gmem_ref[...]).astype(BF16), wmq_ref[...])
    hd = gqh_ref.shape[-1]
    outs = []
    for h in range(MEM_HEADS):
        sl = slice(h * hd, (h + 1) * hd)
        qh = (_rms(qm[:, sl], gqh_ref[...]) * (hd ** -0.5)).astype(BF16)
        s = _dot_nt(qh, mk_ref[0, :, sl])
        p = jnp.exp(s - jnp.max(s, axis=-1, keepdims=True))
        l = jnp.sum(p, axis=-1, keepdims=True)
        outs.append((_dot(p.astype(BF16), mv_ref[0, :, sl]) / l).astype(BF16))
    o_ref[0] = x1 + _dot(jnp.concatenate(outs, axis=1), wmo_ref[...])


def _post(x, grec, gatt, att, mem_k, mem_v, w):
    B, L, D = x.shape
    M = mem_k.shape[1]
    tm = _token_tile(L)
    tok = pl.BlockSpec((1, tm, D), lambda b, l: (b, l, 0))
    mem = pl.BlockSpec((1, M, D), lambda b, l: (b, 0, 0))
    return pl.pallas_call(
        _post_kernel,
        grid=(B, L // tm),
        in_specs=[tok, tok, tok, tok, _resident((D, D)), _resident((1, D)), _resident((D, D)),
                  _resident((1, D // MEM_HEADS)), mem, mem, _resident((D, D))],
        out_specs=tok,
        out_shape=jax.ShapeDtypeStruct((B, L, D), F32),
        compiler_params=_params(2),
        name="post",
    )(x, grec, gatt, att, w["w_out"], w["gmem"], w["w_mq"], w["gmqh"],
      mem_k.astype(BF16), mem_v.astype(BF16), w["w_mo"])


def _swiglu_mid(a, b):
    return (a * jax.nn.sigmoid(a) * b).astype(BF16)


def _ffn_kernel(x_ref, g_ref, w1_ref, w3_ref, w2_ref, o_ref, *, n_split):
    x = x_ref[...]
    hf = _rms(x, g_ref[...]).astype(BF16)
    fc = w1_ref.shape[1] // n_split
    acc = x
    for c in range(n_split):
        sl = slice(c * fc, (c + 1) * fc)
        acc = acc + _dot(_swiglu_mid(_dot(hf, w1_ref[:, sl]), _dot(hf, w3_ref[:, sl])), w2_ref[sl, :])
    o_ref[...] = acc


def _ffn(x, g, w1, w3, w2):
    N, D = x.shape
    F = w1.shape[1]
    tm = _token_tile(N)
    n_split = 2
    assert F % (n_split * LANES) == 0
    tok = pl.BlockSpec((tm, D), lambda i: (i, 0))
    return pl.pallas_call(
        functools.partial(_ffn_kernel, n_split=n_split),
        grid=(N // tm,),
        in_specs=[tok, _resident((1, D)), _resident((D, F)), _resident((D, F)), _resident((F, D))],
        out_specs=tok,
        out_shape=jax.ShapeDtypeStruct((N, D), F32),
        compiler_params=_params(1),
        name="ffn",
    )(x, g, w1, w3, w2)


def _moe_kernel(x_ref, g_ref, wrh_ref, wrl_ref, br_ref, w1_ref, w3_ref, w2_ref, o_ref,
                hf_ref, gate_ref):
    e = pl.program_id(1)
    lane = lax.broadcasted_iota(jnp.int32, (1, LANES), 1)

    @pl.when(e == 0)
    def _():
        x = x_ref[...]
        hf = _rms(x, g_ref[...])
        hi = hf.astype(BF16)
        lo = (hf - hi.astype(F32)).astype(BF16)
        logits = (_dot(hi, wrh_ref[...]) + _dot(lo, wrh_ref[...]) + _dot(hi, wrl_ref[...])
                  + br_ref[...])
        m1 = jnp.max(logits, axis=-1, keepdims=True)
        i1 = jnp.min(jnp.where(logits == m1, lane, LANES), axis=-1, keepdims=True)
        rest = jnp.where(lane == i1, NEG, logits)
        m2 = jnp.max(rest, axis=-1, keepdims=True)
        i2 = jnp.min(jnp.where(rest == m2, lane, LANES), axis=-1, keepdims=True)
        e2 = jnp.exp(m2 - m1)
        den = 1.0 + e2
        gate_ref[...] = jnp.where(lane == i1, 1.0 / den, 0.0) + jnp.where(lane == i2, e2 / den, 0.0)
        hf_ref[...] = hi
        o_ref[...] = x

    hf = hf_ref[...]
    y = _dot(_swiglu_mid(_dot(hf, w1_ref[0]), _dot(hf, w3_ref[0])), w2_ref[0])
    ge = jnp.sum(jnp.where(lane == e, gate_ref[...], 0.0), axis=-1, keepdims=True)
    o_ref[...] += ge * y


def _moe(x, g, wr_hi, wr_lo, br, w1, w3, w2):
    N, D = x.shape
    E, _, F = w1.shape
    tm = _token_tile(N)
    tok = pl.BlockSpec((tm, D), lambda i, e: (i, 0))
    return pl.pallas_call(
        _moe_kernel,
        grid=(N // tm, E),
        in_specs=[tok, _resident((1, D)), _resident((D, LANES)), _resident((D, LANES)),
                  _resident((1, LANES)),
                  pl.BlockSpec((1, D, F), lambda i, e: (e, 0, 0)),
                  pl.BlockSpec((1, D, F), lambda i, e: (e, 0, 0)),
                  pl.BlockSpec((1, F, D), lambda i, e: (e, 0, 0))],
        out_specs=tok,
        out_shape=jax.ShapeDtypeStruct((N, D), F32),
        scratch_shapes=[pltpu.VMEM((tm, D), BF16), pltpu.VMEM((tm, LANES), F32)],
        compiler_params=_params(2),
        name="moe",
    )(x, g, wr_hi, wr_lo, br, w1, w3, w2)


def _rot_half_cols(w):
    half = ROPE_DIM // 2
    return jnp.concatenate([-w[..., half:], w[..., :half]], axis=-1)


def _pad_cols(w, n):
    return jnp.pad(w, [(0, 0)] * (w.ndim - 1) + [(0, n - w.shape[-1])])


def _head_pad(nope, rope):
    z = jnp.zeros(nope.shape[:-1] + (HEAD_PAD - QK_DIM,), nope.dtype)
    out = jnp.concatenate([nope, rope, z], axis=-1)
    return out.reshape(out.shape[:-2] + (N_HEADS * HEAD_PAD,))


def _block_diag_tiles(w):
    nblk, bw, _ = w.shape
    per = MXU_DIM // bw
    w4 = w.reshape(nblk // per, per, bw, bw)
    eye = jnp.eye(per, dtype=w.dtype)
    return jnp.einsum("cpij,pq->cpiqj", w4, eye).reshape(nblk // per, MXU_DIM, MXU_DIM)


def _layer_weights(l, p):
    D = p["w_in"].shape[1]
    q_lora = p["q_lat_norm"].shape[-1]
    kv_lora = p["kv_lat_norm"].shape[-1]
    w_in = p["w_in"][l]
    sp = (D, 2 * D, 2 * D + q_lora, 2 * D + q_lora + kv_lora, 2 * D + q_lora + kv_lora + ROPE_DIM,
          3 * D + q_lora + kv_lora + ROPE_DIM)
    x_rec, x_gate, c_q, c_kv, k_rot, g_rec, g_att = jnp.split(w_in, sp, axis=-1)
    w_in_perm = jnp.concatenate(
        [x_rec, x_gate, g_rec, g_att, c_q, c_kv, _pad_cols(k_rot, LANES),
         _pad_cols(_rot_half_cols(k_rot), LANES)], axis=-1).astype(BF16)

    wq = p["w_uq"][l].reshape(q_lora, N_HEADS, QK_DIM)
    wq_n, wq_r = wq[..., :NOPE_DIM], wq[..., NOPE_DIM:]
    wk = p["w_uk"][l].reshape(kv_lora, N_HEADS, NOPE_DIM)
    eye_r = jnp.broadcast_to(jnp.eye(ROPE_DIM, dtype=F32)[:, None, :], (ROPE_DIM, N_HEADS, ROPE_DIM))
    gq, gk = p["q_head_norm"][l], p["k_head_norm"][l]
    q_scale = (QK_DIM ** -0.5) * LOG2E
    return dict(
        gmix=p["norm_mix"][l].reshape(1, D), w_in=w_in_perm,
        conv_w=p["conv_w"][l], conv_b=p["conv_b"][l].reshape(1, D),
        wa=_block_diag_tiles(p["lru_wa"][l]).astype(BF16), ba=p["lru_ba"][l].reshape(1, D),
        wi=_block_diag_tiles(p["lru_wi"][l]).astype(BF16), bi=p["lru_bi"][l].reshape(1, D),
        sp=jax.nn.softplus(-p["lru_lambda"][l]).reshape(1, D),
        gq=p["q_lat_norm"][l].reshape(1, q_lora), gkv=p["kv_lat_norm"][l].reshape(1, kv_lora),
        wq_a=_head_pad(wq_n, wq_r).astype(BF16),
        wq_b=_head_pad(jnp.zeros_like(wq_n), _rot_half_cols(wq_r)).astype(BF16),
        gqh=(_pad_cols(gq, HEAD_PAD) * q_scale).reshape(1, HEAD_PAD),
        wk=_head_pad(wk, jnp.zeros((kv_lora, N_HEADS, ROPE_DIM), F32)).astype(BF16),
        place=_head_pad(jnp.zeros((ROPE_DIM, N_HEADS, NOPE_DIM), F32), eye_r).astype(BF16),
        wv_t=p["w_uv"][l].T.astype(BF16),
        gkh=_pad_cols(gk, HEAD_PAD).reshape(1, HEAD_PAD),
        w_out=p["w_out"][l].astype(BF16), gmem=p["norm_mem"][l].reshape(1, D),
        w_mq=p["w_mq"][l].astype(BF16), gmqh=p["mq_head_norm"][l].reshape(1, -1),
        w_mo=p["w_mo"][l].astype(BF16), gffn=p["norm_ffn"][l].reshape(1, D),
    )


def _rope_tables(pos):
    half = ROPE_DIM // 2
    inv_freq = ROPE_BASE ** (-jnp.arange(half, dtype=F32) / half)
    ang = pos.astype(F32)[:, None] * inv_freq[None, :]
    cos2 = jnp.concatenate([jnp.cos(ang), jnp.cos(ang)], axis=-1)
    sin2 = jnp.concatenate([jnp.sin(ang), jnp.sin(ang)], axis=-1)
    n = pos.shape[0]
    ones, zeros = jnp.ones((n, NOPE_DIM), F32), jnp.zeros((n, NOPE_DIM), F32)
    return dict(cos_k=_pad_cols(cos2, LANES), sin_k=_pad_cols(sin2, LANES),
                cos_q=_pad_cols(jnp.concatenate([ones, cos2], axis=-1), HEAD_PAD),
                sin_q=_pad_cols(jnp.concatenate([zeros, sin2], axis=-1), HEAD_PAD))


def _layer(l, x, tabs, q_off, conv_prev, h0, ckv_past, kr_past, mem_k, mem_v, w, p):
    B, L, D = x.shape
    prev8 = jnp.pad(conv_prev, ((0, 0), (SUBLANES - (CONV_W - 1), 0), (0, 0)))
    h0_8 = jnp.pad(h0[:, None, :], ((0, 0), (SUBLANES - 1, 0), (0, 0)))
    grec, gatt, cq, ckv, kr, ctail, htail = _inproj(x, prev8, h0_8, w, tabs["cos_k"], tabs["sin_k"])

    q = _qprep(cq, w, tabs["cos_q"], tabs["sin_q"])
    if ckv_past is None:
        ckv_all, kr_all = ckv, kr
    else:
        ckv_all = jnp.concatenate([ckv_past, ckv], axis=1)
        kr_all = jnp.concatenate([kr_past, kr], axis=1)
    t_valid = ckv_all.shape[1]
    tk = min(512, t_valid)
    t_pad = -(-t_valid // tk) * tk
    if t_pad != t_valid:
        ckv_all = jnp.pad(ckv_all, ((0, 0), (0, t_pad - t_valid), (0, 0)))
        kr_all = jnp.pad(kr_all, ((0, 0), (0, t_pad - t_valid), (0, 0)))
    k, vt = _kvprep(ckv_all, kr_all, w, tk)
    att = _flash(q, k, vt, q_off=q_off, t_valid=t_valid, tk=tk)

    x2 = _post(x, grec, gatt, att, mem_k, mem_v, w).reshape(B * L, D)
    j = l // 2
    if l % 2 == 0:
        x3 = _ffn(x2, w["gffn"], p["ffn_w1"][j].astype(BF16), p["ffn_w3"][j].astype(BF16),
                  p["ffn_w2"][j].astype(BF16))
    else:
        E = p["moe_router"].shape[-1]
        wr = _pad_cols(p["moe_router"][j], LANES)
        wr_hi = wr.astype(BF16)
        wr_lo = (wr - wr_hi.astype(F32)).astype(BF16)
        br = jnp.concatenate([p["moe_router_b"][j], jnp.full((LANES - E,), NEG, F32)]).reshape(1, LANES)
        x3 = _moe(x2, w["gffn"], wr_hi, wr_lo, br, p["moe_w1"][j].astype(BF16),
                  p["moe_w3"][j].astype(BF16), p["moe_w2"][j].astype(BF16))
    conv_state = ctail[:, SUBLANES - (CONV_W - 1):, :]
    return x3.reshape(B, L, D), conv_state, htail[:, SUBLANES - 1, :], ckv, kr


def kernel(x_prompt, x_sample, cache_ckv, cache_krope, cache_mem_k, cache_mem_v, state_lru, state_conv, mem_prompt, norm_mix, w_in, conv_w, conv_b, lru_wa, lru_ba, lru_wi, lru_bi, lru_lambda, q_lat_norm, w_uq, kv_lat_norm, w_uk, w_uv, q_head_norm, k_head_norm, w_out, norm_mem, mem_in_norm, w_mq, w_mk, w_mv, w_mo, mq_head_norm, mk_head_norm, norm_ffn, ffn_w1, ffn_w3, ffn_w2, moe_router, moe_router_b, moe_w1, moe_w3, moe_w2):
    p = dict(norm_mix=norm_mix, w_in=w_in, conv_w=conv_w, conv_b=conv_b, lru_wa=lru_wa, lru_ba=lru_ba,
             lru_wi=lru_wi, lru_bi=lru_bi, lru_lambda=lru_lambda, q_lat_norm=q_lat_norm, w_uq=w_uq,
             kv_lat_norm=kv_lat_norm, w_uk=w_uk, w_uv=w_uv, q_head_norm=q_head_norm,
             k_head_norm=k_head_norm, w_out=w_out, norm_mem=norm_mem, w_mq=w_mq, w_mo=w_mo,
             mq_head_norm=mq_head_norm, norm_ffn=norm_ffn, ffn_w1=ffn_w1, ffn_w3=ffn_w3,
             ffn_w2=ffn_w2, moe_router=moe_router, moe_router_b=moe_router_b, moe_w1=moe_w1,
             moe_w3=moe_w3, moe_w2=moe_w2)
    depth = w_in.shape[0]
    weights = [_layer_weights(l, p) for l in range(depth)]

    Bp, Lp, D = x_prompt.shape
    Bs, Ls, _ = x_sample.shape
    t_past = cache_ckv.shape[2]
    M = mem_prompt.shape[1]

    tabs_p = _rope_tables(jnp.arange(Lp, dtype=jnp.int32))
    zero_conv = jnp.zeros((Bp, CONV_W - 1, D), F32)
    zero_h = jnp.zeros((Bp, D), F32)
    x = x_prompt
    outs_p = [[] for _ in range(6)]
    for l in range(depth):
        mk, mv = _memkv(mem_prompt, mem_in_norm[l], w_mk[l], w_mv[l], mk_head_norm[l])
        x, cs, hl, ckv, kr = _layer(l, x, tabs_p, 0, zero_conv, zero_h, None, None, mk, mv, weights[l], p)
        for acc, val in zip(outs_p, (ckv, kr, hl, cs, mk.reshape(Bp, M, MEM_HEADS, -1),
                                     mv.reshape(Bp, M, MEM_HEADS, -1))):
            acc.append(val)
    y_prompt = x

    tabs_s = _rope_tables(t_past + jnp.arange(Ls, dtype=jnp.int32))
    x = x_sample
    outs_s = [[] for _ in range(4)]
    for l in range(depth):
        x, cs, hl, ckv, kr = _layer(l, x, tabs_s, t_past, state_conv[l], state_lru[l], cache_ckv[l],
                                    cache_krope[l], cache_mem_k[l].reshape(Bs, M, D),
                                    cache_mem_v[l].reshape(Bs, M, D), weights[l], p)
        for acc, val in zip(outs_s, (ckv, kr, hl, cs)):
            acc.append(val)
    y_sample = x

    return (y_prompt, y_sample) + tuple(jnp.stack(a) for a in outs_p) + tuple(jnp.stack(a) for a in outs_s)
```

```python
import functools
import math

import jax
import jax.numpy as jnp
from jax import lax
from jax.experimental import pallas as pl
from jax.experimental.pallas import tpu as pltpu

F32 = jnp.float32
BF16 = jnp.bfloat16

EPS = 1e-6
CHUNK = 64
LRU_BLOCKS = 16
LRU_C = 8.0
CONV_W = 4
N_HEADS = 16
NOPE_DIM = 64
ROPE_DIM = 32
QK_DIM = NOPE_DIM + ROPE_DIM
ROPE_BASE = 10000.0
MEM_HEADS = 4
TOP_K = 2

LANES = 128
SUBLANES = 8
MXU_DIM = 256
VMEM_LIMIT = 56 * 1024 * 1024

HEAD_PAD = LANES
FLASH_HEADS = 8
NEG = -1e30
LOG2E = 1.4426950408889634


def _resident(shape):
    nd = len(shape)
    return pl.BlockSpec(shape, lambda *_: (0,) * nd, pipeline_mode=pl.Buffered(1))


def _params(n_axes):
    return pltpu.CompilerParams(dimension_semantics=("arbitrary",) * n_axes,
                                vmem_limit_bytes=VMEM_LIMIT)


def _rms(x, gain):
    return x * lax.rsqrt(jnp.mean(x * x, axis=-1, keepdims=True) + EPS) * gain


def _dot(a, b):
    return jnp.dot(a, b, preferred_element_type=F32)


def _dot_nt(a, b):
    return lax.dot_general(a, b, (((1,), (1,)), ((), ())), preferred_element_type=F32)


def _token_tile(n, cap=512):
    t = min(n, cap)
    assert n % t == 0 and t % SUBLANES == 0
    return t


def _memkv_kernel(mem_ref, gin_ref, wk_ref, wv_ref, gk_ref, k_ref, v_ref):
    mn = _rms(mem_ref[0], gin_ref[...]).astype(BF16)
    k = _dot(mn, wk_ref[...])
    hd = gk_ref.shape[-1]
    for h in range(MEM_HEADS):
        sl = slice(h * hd, (h + 1) * hd)
        k_ref[0, :, sl] = _rms(k[:, sl], gk_ref[...])
    v_ref[0] = _dot(mn, wv_ref[...])


def _memkv(mem, g_in, w_k, w_v, g_k):
    B, M, D = mem.shape
    tok = pl.BlockSpec((1, M, D), lambda b: (b, 0, 0))
    return pl.pallas_call(
        _memkv_kernel,
        grid=(B,),
        in_specs=[tok, _resident((1, D)), _resident((D, D)), _resident((D, D)),
                  _resident((1, D // MEM_HEADS))],
        out_specs=[tok, tok],
        out_shape=[jax.ShapeDtypeStruct((B, M, D), F32)] * 2,
        compiler_params=_params(1),
        name="memkv",
    )(mem, g_in.reshape(1, D), w_k.astype(BF16), w_v.astype(BF16), g_k.reshape(1, -1))


def _inproj_kernel(x_ref, prev_ref, h0_ref, gmix_ref, win_ref, cw_ref, cb_ref, wa_ref, ba_ref,
                   wi_ref, bi_ref, sp_ref, gq_ref, gkv_ref, cos_ref, sin_ref,
                   grec_ref, gatt_ref, cq_ref, ckv_ref, kr_ref, ctail_ref, htail_ref,
                   xext_ref, hcar_ref, *, tm, d, q_lora, kv_lora):
    @pl.when(pl.program_id(1) == 0)
    def _():
        xext_ref[0:SUBLANES, :] = prev_ref[0]
        hcar_ref[...] = h0_ref[0]

    hn = _rms(x_ref[0], gmix_ref[...]).astype(BF16)

    def proj(c0, c1):
        return _dot(hn, win_ref[:, c0:c1])

    x_rec = proj(0, d)
    xext_ref[SUBLANES:SUBLANES + tm, :] = x_rec
    xc = cb_ref[...] + cw_ref[CONV_W - 1:CONV_W, :] * x_rec
    for j in range(CONV_W - 1):
        off = SUBLANES - (CONV_W - 1) + j
        xc = xc + cw_ref[j:j + 1, :] * xext_ref[off:off + tm, :]
    tail = x_rec[tm - SUBLANES:tm, :]
    ctail_ref[0] = tail
    xext_ref[0:SUBLANES, :] = tail

    xcb = xc.astype(BF16)
    nb = d // MXU_DIM
    gr = jnp.concatenate(
        [_dot(xcb[:, c * MXU_DIM:(c + 1) * MXU_DIM], wa_ref[c]) for c in range(nb)], axis=1)
    gi = jnp.concatenate(
        [_dot(xcb[:, c * MXU_DIM:(c + 1) * MXU_DIM], wi_ref[c]) for c in range(nb)], axis=1)
    r = jax.nn.sigmoid(gr + ba_ref[...])
    ig = jax.nn.sigmoid(gi + bi_ref[...])
    log_a = (-LRU_C) * r * sp_ref[...]
    a = jnp.exp(log_a)
    u = jnp.sqrt(1.0 - a * a) * (ig * xc)

    row = lax.broadcasted_iota(jnp.int32, (tm, 1), 0) & (SUBLANES - 1)
    s = 1
    while s < SUBLANES:
        keep = row >= s
        u = a * jnp.where(keep, pltpu.roll(u, s, 0), 0.0) + u
        a = a * jnp.where(keep, pltpu.roll(a, s, 0), 1.0)
        s *= 2
    h_prev = hcar_ref[SUBLANES - 1:SUBLANES, :]
    groups = []
    for g in range(tm // SUBLANES):
        rows = slice(g * SUBLANES, (g + 1) * SUBLANES)
        hg = a[rows] * h_prev + u[rows]
        groups.append(hg)
        h_prev = hg[SUBLANES - 1:SUBLANES, :]
    h = jnp.concatenate(groups, axis=0)
    htail = groups[-1]
    hcar_ref[...] = htail
    htail_ref[0] = htail

    x_gate = proj(d, 2 * d)
    g_rec = proj(2 * d, 3 * d)
    grec_ref[0] = jax.nn.sigmoid(g_rec) * (jax.nn.gelu(x_gate) * h)
    gatt_ref[0] = jax.nn.sigmoid(proj(3 * d, 4 * d))

    o = 4 * d
    cq_ref[0] = _rms(proj(o, o + q_lora), gq_ref[...]).astype(BF16)
    o += q_lora
    ckv_ref[0] = _rms(proj(o, o + kv_lora), gkv_ref[...])
    o += kv_lora
    kr = proj(o, o + LANES) * cos_ref[...] + proj(o + LANES, o + 2 * LANES) * sin_ref[...]
    kr_ref[0] = kr[:, :ROPE_DIM]


def _inproj(x, prev8, h0_8, w, cos_k, sin_k):
    B, L, D = x.shape
    tm = _token_tile(L, 256)
    q_lora, kv_lora = w["gq"].shape[-1], w["gkv"].shape[-1]
    ncol = w["w_in"].shape[-1]
    tok = lambda n: pl.BlockSpec((1, tm, n), lambda b, l: (b, l, 0))
    per_b = pl.BlockSpec((1, SUBLANES, D), lambda b, l: (b, 0, 0))
    tab = pl.BlockSpec((tm, LANES), lambda b, l: (l, 0))
    nb = D // MXU_DIM
    kern = functools.partial(_inproj_kernel, tm=tm, d=D, q_lora=q_lora, kv_lora=kv_lora)
    return pl.pallas_call(
        kern,
        grid=(B, L // tm),
        in_specs=[tok(D), per_b, per_b, _resident((1, D)), _resident((D, ncol)),
                  _resident((CONV_W, D)), _resident((1, D)),
                  _resident((nb, MXU_DIM, MXU_DIM)), _resident((1, D)),
                  _resident((nb, MXU_DIM, MXU_DIM)), _resident((1, D)), _resident((1, D)),
                  _resident((1, q_lora)), _resident((1, kv_lora)), tab, tab],
        out_specs=[tok(D), tok(D), tok(q_lora), tok(kv_lora), tok(ROPE_DIM), per_b, per_b],
        out_shape=[jax.ShapeDtypeStruct((B, L, D), F32), jax.ShapeDtypeStruct((B, L, D), F32),
                   jax.ShapeDtypeStruct((B, L, q_lora), BF16),
                   jax.ShapeDtypeStruct((B, L, kv_lora), F32),
                   jax.ShapeDtypeStruct((B, L, ROPE_DIM), F32),
                   jax.ShapeDtypeStruct((B, SUBLANES, D), F32),
                   jax.ShapeDtypeStruct((B, SUBLANES, D), F32)],
        scratch_shapes=[pltpu.VMEM((SUBLANES + tm, D), F32), pltpu.VMEM((SUBLANES, D), F32)],
        compiler_params=_params(2),
        name="inproj",
    )(x, prev8, h0_8, w["gmix"], w["w_in"], w["conv_w"], w["conv_b"], w["wa"], w["ba"],
      w["wi"], w["bi"], w["sp"], w["gq"], w["gkv"], cos_k, sin_k)


def _qprep_kernel(cq_ref, wa_ref, wb_ref, cos_ref, sin_ref, g_ref, q_ref):
    cq = cq_ref[0]
    qa = _dot(cq, wa_ref[...])
    qb = _dot(cq, wb_ref[...])
    for h in range(N_HEADS):
        sl = slice(h * HEAD_PAD, (h + 1) * HEAD_PAD)
        qh = qa[:, sl] * cos_ref[...] + qb[:, sl] * sin_ref[...]
        ss = jnp.sum(qh * qh, axis=-1, keepdims=True) * (1.0 / QK_DIM)
        q_ref[0, :, sl] = (qh * lax.rsqrt(ss + EPS) * g_ref[...]).astype(BF16)


def _qprep(cq, w, cos_q, sin_q):
    B, L, QL = cq.shape
    tm = _token_tile(L)
    n = N_HEADS * HEAD_PAD
    tab = pl.BlockSpec((tm, HEAD_PAD), lambda b, l: (l, 0))
    return pl.pallas_call(
        _qprep_kernel,
        grid=(B, L // tm),
        in_specs=[pl.BlockSpec((1, tm, QL), lambda b, l: (b, l, 0)),
                  _resident((QL, n)), _resident((QL, n)), tab, tab, _resident((1, HEAD_PAD))],
        out_specs=pl.BlockSpec((1, tm, n), lambda b, l: (b, l, 0)),
        out_shape=jax.ShapeDtypeStruct((B, L, n), BF16),
        compiler_params=_params(2),
        name="qprep",
    )(cq, w["wq_a"], w["wq_b"], cos_q, sin_q, w["gqh"])


def _kvprep_kernel(ckv_ref, kr_ref, wk_ref, place_ref, wvt_ref, g_ref, k_ref, vt_ref):
    c = ckv_ref[0].astype(BF16)
    k = _dot(c, wk_ref[...]) + _dot(kr_ref[0].astype(BF16), place_ref[...])
    for h in range(N_HEADS):
        sl = slice(h * HEAD_PAD, (h + 1) * HEAD_PAD)
        kh = k[:, sl]
        ss = jnp.sum(kh * kh, axis=-1, keepdims=True) * (1.0 / QK_DIM)
        k_ref[0, :, sl] = (kh * lax.rsqrt(ss + EPS) * g_ref[...]).astype(BF16)
    vt_ref[0, 0] = _dot_nt(wvt_ref[...], c).astype(BF16)


def _kvprep(ckv, kr, w, tk):
    B, T, KV = ckv.shape
    n = N_HEADS * HEAD_PAD
    dv = w["wv_t"].shape[0]
    return pl.pallas_call(
        _kvprep_kernel,
        grid=(B, T // tk),
        in_specs=[pl.BlockSpec((1, tk, KV), lambda b, t: (b, t, 0)),
                  pl.BlockSpec((1, tk, ROPE_DIM), lambda b, t: (b, t, 0)),
                  _resident((KV, n)), _resident((ROPE_DIM, n)), _resident((dv, KV)),
                  _resident((1, HEAD_PAD))],
        out_specs=[pl.BlockSpec((1, tk, n), lambda b, t: (b, t, 0)),
                   pl.BlockSpec((1, 1, dv, tk), lambda b, t: (b, t, 0, 0))],
        out_shape=[jax.ShapeDtypeStruct((B, T, n), BF16),
                   jax.ShapeDtypeStruct((B, T // tk, dv, tk), BF16)],
        compiler_params=_params(2),
        name="kvprep",
    )(ckv, kr, w["wk"], w["place"], w["wv_t"], w["gkh"])


def _flash_kernel(q_ref, k_ref, vt_ref, o_ref, *, tq, tk, q_off, t_valid, v_dim, heads):
    q_start = q_off + pl.program_id(2) * tq
    n_full = jnp.minimum(q_start, t_valid) // tk
    n_end = (jnp.minimum(q_start + tq, t_valid) + tk - 1) // tk
    shift = int(math.log2(CHUNK))
    q_chunk = lax.shift_right_logical(q_start + lax.broadcasted_iota(jnp.int32, (1, tq), 1), shift)
    k_iota = lax.broadcasted_iota(jnp.int32, (tk, 1), 0)

    def step(j, carry, masked):
        ks = pl.multiple_of(j * tk, tk)
        if masked:
            k_pos = ks + k_iota
            vis = (lax.shift_right_logical(k_pos, shift) <= q_chunk) & (k_pos < t_valid)
        scores = []
        for hh in range(heads):
            qh = q_ref[0, :, hh * HEAD_PAD:(hh + 1) * HEAD_PAD]
            kj = k_ref[0, pl.ds(ks, tk), hh * HEAD_PAD:(hh + 1) * HEAD_PAD]
            scores.append(_dot_nt(kj, qh))
        probs = []
        for hh in range(heads):
            m, l, _ = carry[hh]
            s = jnp.where(vis, scores[hh], NEG) if masked else scores[hh]
            m_new = jnp.maximum(m, jnp.max(s, axis=0, keepdims=True))
            alpha = jnp.exp2(m - m_new)
            p = jnp.exp2(s - m_new)
            l = alpha * l + jnp.sum(p, axis=0, keepdims=True)
            probs.append((m_new, l, alpha, p.astype(BF16)))
        out = []
        for hh in range(heads):
            m_new, l, alpha, p = probs[hh]
            vj = vt_ref[0, j, hh * v_dim:(hh + 1) * v_dim, :]
            out.append((m_new, l, alpha * carry[hh][2] + _dot(vj, p)))
        return tuple(out)

    init = tuple((jnp.full((1, tq), NEG, F32), jnp.zeros((1, tq), F32), jnp.zeros((v_dim, tq), F32))
                 for _ in range(heads))
    carry = lax.fori_loop(0, n_full, functools.partial(step, masked=False), init)
    carry = lax.fori_loop(n_full, n_end, functools.partial(step, masked=True), carry)
    o_ref[0] = jnp.concatenate([acc / l for _, l, acc in carry], axis=0).T


def _flash(q, k, vt, *, q_off, t_valid, tk, heads=FLASH_HEADS):
    B, L, n = q.shape
    T = k.shape[1]
    dv = vt.shape[2]
    v_dim = dv // N_HEADS
    tq = _token_tile(L)
    assert tq % CHUNK == 0 and q_off % CHUNK == 0 and (heads * v_dim) % LANES == 0
    kern = functools.partial(_flash_kernel, tq=tq, tk=tk, q_off=q_off, t_valid=t_valid, v_dim=v_dim,
                             heads=heads)
    return pl.pallas_call(
        kern,
        grid=(B, N_HEADS // heads, L // tq),
        in_specs=[pl.BlockSpec((1, tq, heads * HEAD_PAD), lambda b, p, i: (b, i, p)),
                  pl.BlockSpec((1, T, heads * HEAD_PAD), lambda b, p, i: (b, 0, p)),
                  pl.BlockSpec((1, T // tk, heads * v_dim, tk), lambda b, p, i: (b, 0, p, 0))],
        out_specs=pl.BlockSpec((1, tq, heads * v_dim), lambda b, p, i: (b, i, p)),
        out_shape=jax.ShapeDtypeStruct((B, L, dv), F32),
        compiler_params=_params(3),
        name="flash",
    )(q, k, vt)


def _post_kernel(x_ref, grec_ref, gatt_ref, att_ref, wo_ref, gmem_ref, wmq_ref, gqh_ref,
                 mk_ref, mv_ref, wmo_ref, o_ref):
    mixed = (grec_ref[0] + gatt_ref[0] * att_ref[0]).astype(BF16)
    x1 = x_ref[0] + _dot(mixed, wo_ref[...])
    qm = _dot(_rms(x1, gmem_ref[...]).astype(BF16), wmq_ref[...])
    hd = gqh_ref.shape[-1]
    outs = []
    for h in range(MEM_HEADS):
        sl = slice(h * hd, (h + 1) * hd)
        qh = (_rms(qm[:, sl], gqh_ref[...]) * (hd ** -0.5)).astype(BF16)
        s = _dot_nt(qh, mk_ref[0, :, sl])
        p = jnp.exp(s - jnp.max(s, axis=-1, keepdims=True))
        l = jnp.sum(p, axis=-1, keepdims=True)
        outs.append((_dot(p.astype(BF16), mv_ref[0, :, sl]) / l).astype(BF16))
    o_ref[0] = x1 + _dot(jnp.concatenate(outs, axis=1), wmo_ref[...])


def _post(x, grec, gatt, att, mem_k, mem_v, w):
    B, L, D = x.shape
    M = mem_k.shape[1]
    tm = _token_tile(L)
    tok = pl.BlockSpec((1, tm, D), lambda b, l: (b, l, 0))
    mem = pl.BlockSpec((1, M, D), lambda b, l: (b, 0, 0))
    return pl.pallas_call(
        _post_kernel,
        grid=(B, L // tm),
        in_specs=[tok, tok, tok, tok, _resident((D, D)), _resident((1, D)), _resident((D, D)),
                  _resident((1, D // MEM_HEADS)), mem, mem, _resident((D, D))],
        out_specs=tok,
        out_shape=jax.ShapeDtypeStruct((B, L, D), F32),
        compiler_params=_params(2),
        name="post",
    )(x, grec, gatt, att, w["w_out"], w["gmem"], w["w_mq"], w["gmqh"],
      mem_k.astype(BF16), mem_v.astype(BF16), w["w_mo"])


def _swiglu_mid(a, b):
    return (a * jax.nn.sigmoid(a) * b).astype(BF16)


def _ffn_kernel(x_ref, g_ref, w1_ref, w3_ref, w2_ref, o_ref, *, n_split):
    x = x_ref[...]
    hf = _rms(x, g_ref[...]).astype(BF16)
    fc = w1_ref.shape[1] // n_split
    acc = x
    for c in range(n_split):
        sl = slice(c * fc, (c + 1) * fc)
        acc = acc + _dot(_swiglu_mid(_dot(hf, w1_ref[:, sl]), _dot(hf, w3_ref[:, sl])), w2_ref[sl, :])
    o_ref[...] = acc


def _ffn(x, g, w1, w3, w2):
    N, D = x.shape
    F = w1.shape[1]
    tm = _token_tile(N)
    n_split = 2
    assert F % (n_split * LANES) == 0
    tok = pl.BlockSpec((tm, D), lambda i: (i, 0))
    return pl.pallas_call(
        functools.partial(_ffn_kernel, n_split=n_split),
        grid=(N // tm,),
        in_specs=[tok, _resident((1, D)), _resident((D, F)), _resident((D, F)), _resident((F, D))],
        out_specs=tok,
        out_shape=jax.ShapeDtypeStruct((N, D), F32),
        compiler_params=_params(1),
        name="ffn",
    )(x, g, w1, w3, w2)


def _moe_kernel(x_ref, g_ref, wrh_ref, wrl_ref, br_ref, w1_ref, w3_ref, w2_ref, o_ref,
                hf_ref, gate_ref):
    e = pl.program_id(1)
    lane = lax.broadcasted_iota(jnp.int32, (1, LANES), 1)

    @pl.when(e == 0)
    def _():
        x = x_ref[...]
        hf = _rms(x, g_ref[...])
        hi = hf.astype(BF16)
        lo = (hf - hi.astype(F32)).astype(BF16)
        logits = (_dot(hi, wrh_ref[...]) + _dot(lo, wrh_ref[...]) + _dot(hi, wrl_ref[...])
                  + br_ref[...])
        m1 = jnp.max(logits, axis=-1, keepdims=True)
        i1 = jnp.min(jnp.where(logits == m1, lane, LANES), axis=-1, keepdims=True)
        rest = jnp.where(lane == i1, NEG, logits)
        m2 = jnp.max(rest, axis=-1, keepdims=True)
        i2 = jnp.min(jnp.where(rest == m2, lane, LANES), axis=-1, keepdims=True)
        e2 = jnp.exp(m2 - m1)
        den = 1.0 + e2
        gate_ref[...] = jnp.where(lane == i1, 1.0 / den, 0.0) + jnp.where(lane == i2, e2 / den, 0.0)
        hf_ref[...] = hi
        o_ref[...] = x

    hf = hf_ref[...]
    y = _dot(_swiglu_mid(_dot(hf, w1_ref[0]), _dot(hf, w3_ref[0])), w2_ref[0])
    ge = jnp.sum(jnp.where(lane == e, gate_ref[...], 0.0), axis=-1, keepdims=True)
    o_ref[...] += ge * y


def _moe(x, g, wr_hi, wr_lo, br, w1, w3, w2):
    N, D = x.shape
    E, _, F = w1.shape
    tm = _token_tile(N)
    tok = pl.BlockSpec((tm, D), lambda i, e: (i, 0))
    return pl.pallas_call(
        _moe_kernel,
        grid=(N // tm, E),
        in_specs=[tok, _resident((1, D)), _resident((D, LANES)), _resident((D, LANES)),
                  _resident((1, LANES)),
                  pl.BlockSpec((1, D, F), lambda i, e: (e, 0, 0)),
                  pl.BlockSpec((1, D, F), lambda i, e: (e, 0, 0)),
                  pl.BlockSpec((1, F, D), lambda i, e: (e, 0, 0))],
        out_specs=tok,
        out_shape=jax.ShapeDtypeStruct((N, D), F32),
        scratch_shapes=[pltpu.VMEM((tm, D), BF16), pltpu.VMEM((tm, LANES), F32)],
        compiler_params=_params(2),
        name="moe",
    )(x, g, wr_hi, wr_lo, br, w1, w3, w2)


def _rot_half_cols(w):
    half = ROPE_DIM // 2
    return jnp.concatenate([-w[..., half:], w[..., :half]], axis=-1)


def _pad_cols(w, n):
    return jnp.pad(w, [(0, 0)] * (w.ndim - 1) + [(0, n - w.shape[-1])])


def _head_pad(nope, rope):
    z = jnp.zeros(nope.shape[:-1] + (HEAD_PAD - QK_DIM,), nope.dtype)
    out = jnp.concatenate([nope, rope, z], axis=-1)
    return out.reshape(out.shape[:-2] + (N_HEADS * HEAD_PAD,))


def _block_diag_tiles(w):
    nblk, bw, _ = w.shape
    per = MXU_DIM // bw
    w4 = w.reshape(nblk // per, per, bw, bw)
    eye = jnp.eye(per, dtype=w.dtype)
    return jnp.einsum("cpij,pq->cpiqj", w4, eye).reshape(nblk // per, MXU_DIM, MXU_DIM)


def _layer_weights(l, p):
    D = p["w_in"].shape[1]
    q_lora = p["q_lat_norm"].shape[-1]
    kv_lora = p["kv_lat_norm"].shape[-1]
    w_in = p["w_in"][l]
    sp = (D, 2 * D, 2 * D + q_lora, 2 * D + q_lora + kv_lora, 2 * D + q_lora + kv_lora + ROPE_DIM,
          3 * D + q_lora + kv_lora + ROPE_DIM)
    x_rec, x_gate, c_q, c_kv, k_rot, g_rec, g_att = jnp.split(w_in, sp, axis=-1)
    w_in_perm = jnp.concatenate(
        [x_rec, x_gate, g_rec, g_att, c_q, c_kv, _pad_cols(k_rot, LANES),
         _pad_cols(_rot_half_cols(k_rot), LANES)], axis=-1).astype(BF16)

    wq = p["w_uq"][l].reshape(q_lora, N_HEADS, QK_DIM)
    wq_n, wq_r = wq[..., :NOPE_DIM], wq[..., NOPE_DIM:]
    wk = p["w_uk"][l].reshape(kv_lora, N_HEADS, NOPE_DIM)
    eye_r = jnp.broadcast_to(jnp.eye(ROPE_DIM, dtype=F32)[:, None, :], (ROPE_DIM, N_HEADS, ROPE_DIM))
    gq, gk = p["q_head_norm"][l], p["k_head_norm"][l]
    q_scale = (QK_DIM ** -0.5) * LOG2E
    return dict(
        gmix=p["norm_mix"][l].reshape(1, D), w_in=w_in_perm,
        conv_w=p["conv_w"][l], conv_b=p["conv_b"][l].reshape(1, D),
        wa=_block_diag_tiles(p["lru_wa"][l]).astype(BF16), ba=p["lru_ba"][l].reshape(1, D),
        wi=_block_diag_tiles(p["lru_wi"][l]).astype(BF16), bi=p["lru_bi"][l].reshape(1, D),
        sp=jax.nn.softplus(-p["lru_lambda"][l]).reshape(1, D),
        gq=p["q_lat_norm"][l].reshape(1, q_lora), gkv=p["kv_lat_norm"][l].reshape(1, kv_lora),
        wq_a=_head_pad(wq_n, wq_r).astype(BF16),
        wq_b=_head_pad(jnp.zeros_like(wq_n), _rot_half_cols(wq_r)).astype(BF16),
        gqh=(_pad_cols(gq, HEAD_PAD) * q_scale).reshape(1, HEAD_PAD),
        wk=_head_pad(wk, jnp.zeros((kv_lora, N_HEADS, ROPE_DIM), F32)).astype(BF16),
        place=_head_pad(jnp.zeros((ROPE_DIM, N_HEADS, NOPE_DIM), F32), eye_r).astype(BF16),
        wv_t=p["w_uv"][l].T.astype(BF16),
        gkh=_pad_cols(gk, HEAD_PAD).reshape(1, HEAD_PAD),
        w_out=p["w_out"][l].astype(BF16), gmem=p["norm_mem"][l].reshape(1, D),
        w_mq=p["w_mq"][l].astype(BF16), gmqh=p["mq_head_norm"][l].reshape(1, -1),
        w_mo=p["w_mo"][l].astype(BF16), gffn=p["norm_ffn"][l].reshape(1, D),
    )


def _rope_tables(pos):
    half = ROPE_DIM // 2
    inv_freq = ROPE_BASE ** (-jnp.arange(half, dtype=F32) / half)
    ang = pos.astype(F32)[:, None] * inv_freq[None, :]
    cos2 = jnp.concatenate([jnp.cos(ang), jnp.cos(ang)], axis=-1)
    sin2 = jnp.concatenate([jnp.sin(ang), jnp.sin(ang)], axis=-1)
    n = pos.shape[0]
    ones, zeros = jnp.ones((n, NOPE_DIM), F32), jnp.zeros((n, NOPE_DIM), F32)
    return dict(cos_k=_pad_cols(cos2, LANES), sin_k=_pad_cols(sin2, LANES),
                cos_q=_pad_cols(jnp.concatenate([ones, cos2], axis=-1), HEAD_PAD),
                sin_q=_pad_cols(jnp.concatenate([zeros, sin2], axis=-1), HEAD_PAD))


def _layer(l, x, tabs, q_off, conv_prev, h0, ckv_past, kr_past, mem_k, mem_v, w, p):
    B, L, D = x.shape
    prev8 = jnp.pad(conv_prev, ((0, 0), (SUBLANES - (CONV_W - 1), 0), (0, 0)))
    h0_8 = jnp.pad(h0[:, None, :], ((0, 0), (SUBLANES - 1, 0), (0, 0)))
    grec, gatt, cq, ckv, kr, ctail, htail = _inproj(x, prev8, h0_8, w, tabs["cos_k"], tabs["sin_k"])

    q = _qprep(cq, w, tabs["cos_q"], tabs["sin_q"])
    if ckv_past is None:
        ckv_all, kr_all = ckv, kr
    else:
        ckv_all = jnp.concatenate([ckv_past, ckv], axis=1)
        kr_all = jnp.concatenate([kr_past, kr], axis=1)
    t_valid = ckv_all.shape[1]
    tk = min(512, t_valid)
    t_pad = -(-t_valid // tk) * tk
    if t_pad != t_valid:
        ckv_all = jnp.pad(ckv_all, ((0, 0), (0, t_pad - t_valid), (0, 0)))
        kr_all = jnp.pad(kr_all, ((0, 0), (0, t_pad - t_valid), (0, 0)))
    k, vt = _kvprep(ckv_all, kr_all, w, tk)
    att = _flash(q, k, vt, q_off=q_off, t_valid=t_valid, tk=tk)

    x2 = _post(x, grec, gatt, att, mem_k, mem_v, w).reshape(B * L, D)
    j = l // 2
    if l % 2 == 0:
        x3 = _ffn(x2, w["gffn"], p["ffn_w1"][j].astype(BF16), p["ffn_w3"][j].astype(BF16),
                  p["ffn_w2"][j].astype(BF16))
    else:
        E = p["moe_router"].shape[-1]
        wr = _pad_cols(p["moe_router"][j], LANES)
        wr_hi = wr.astype(BF16)
        wr_lo = (wr - wr_hi.astype(F32)).astype(BF16)
        br = jnp.concatenate([p["moe_router_b"][j], jnp.full((LANES - E,), NEG, F32)]).reshape(1, LANES)
        x3 = _moe(x2, w["gffn"], wr_hi, wr_lo, br, p["moe_w1"][j].astype(BF16),
                  p["moe_w3"][j].astype(BF16), p["moe_w2"][j].astype(BF16))
    conv_state = ctail[:, SUBLANES - (CONV_W - 1):, :]
    return x3.reshape(B, L, D), conv_state, htail[:, SUBLANES - 1, :], ckv, kr


def kernel(x_prompt, x_sample, cache_ckv, cache_krope, cache_mem_k, cache_mem_v, state_lru, state_conv, mem_prompt, norm_mix, w_in, conv_w, conv_b, lru_wa, lru_ba, lru_wi, lru_bi, lru_lambda, q_lat_norm, w_uq, kv_lat_norm, w_uk, w_uv, q_head_norm, k_head_norm, w_out, norm_mem, mem_in_norm, w_mq, w_mk, w_mv, w_mo, mq_head_norm, mk_head_norm, norm_ffn, ffn_w1, ffn_w3, ffn_w2, moe_router, moe_router_b, moe_w1, moe_w3, moe_w2):
    p = dict(norm_mix=norm_mix, w_in=w_in, conv_w=conv_w, conv_b=conv_b, lru_wa=lru_wa, lru_ba=lru_ba,
             lru_wi=lru_wi, lru_bi=lru_bi, lru_lambda=lru_lambda, q_lat_norm=q_lat_norm, w_uq=w_uq,
             kv_lat_norm=kv_lat_norm, w_uk=w_uk, w_uv=w_uv, q_head_norm=q_head_norm,
             k_head_norm=k_head_norm, w_out=w_out, norm_mem=norm_mem, w_mq=w_mq, w_mo=w_mo,
             mq_head_norm=mq_head_norm, norm_ffn=norm_ffn, ffn_w1=ffn_w1, ffn_w3=ffn_w3,
             ffn_w2=ffn_w2, moe_router=moe_router, moe_router_b=moe_router_b, moe_w1=moe_w1,
             moe_w3=moe_w3, moe_w2=moe_w2)
    depth = w_in.shape[0]
    weights = [_layer_weights(l, p) for l in range(depth)]

    Bp, Lp, D = x_prompt.shape
    Bs, Ls, _ = x_sample.shape
    t_past = cache_ckv.shape[2]
    M = mem_prompt.shape[1]

    tabs_p = _rope_tables(jnp.arange(Lp, dtype=jnp.int32))
    zero_conv = jnp.zeros((Bp, CONV_W - 1, D), F32)
    zero_h = jnp.zeros((Bp, D), F32)
    x = x_prompt
    outs_p = [[] for _ in range(6)]
    for l in range(depth):
        mk, mv = _memkv(mem_prompt, mem_in_norm[l], w_mk[l], w_mv[l], mk_head_norm[l])
        x, cs, hl, ckv, kr = _layer(l, x, tabs_p, 0, zero_conv, zero_h, None, None, mk, mv, weights[l], p)
        for acc, val in zip(outs_p, (ckv, kr, hl, cs, mk.reshape(Bp, M, MEM_HEADS, -1),
                                     mv.reshape(Bp, M, MEM_HEADS, -1))):
            acc.append(val)
    y_prompt = x

    tabs_s = _rope_tables(t_past + jnp.arange(Ls, dtype=jnp.int32))
    x = x_sample
    outs_s = [[] for _ in range(4)]
    for l in range(depth):
        x, cs, hl, ckv, kr = _layer(l, x, tabs_s, t_past, state_conv[l], state_lru[l], cache_ckv[l],
                                    cache_krope[l], cache_mem_k[l].reshape(Bs, M, D),
                                    cache_mem_v[l].reshape(Bs, M, D), weights[l], p)
        for acc, val in zip(outs_s, (ckv, kr, hl, cs)):
            acc.append(val)
    y_sample = x

    return (y_prompt, y_sample) + tuple(jnp.stack(a) for a in outs_p) + tuple(jnp.stack(a) for a in outs_s)
```

```python
import functools
import math

import jax
import jax.numpy as jnp
from jax import lax
from jax.experimental import pallas as pl
from jax.experimental.pallas import tpu as pltpu

F32 = jnp.float32
BF16 = jnp.bfloat16

EPS = 1e-6
CHUNK = 64
LRU_BLOCKS = 16
LRU_C = 8.0
CONV_W = 4
N_HEADS = 16
NOPE_DIM = 64
ROPE_DIM = 32
QK_DIM = NOPE_DIM + ROPE_DIM
ROPE_BASE = 10000.0
MEM_HEADS = 4
TOP_K = 2

LANES = 128
SUBLANES = 8
MXU_DIM = 256
VMEM_LIMIT = 56 * 1024 * 1024

HEAD_PAD = LANES
MOE_CHUNK = 160
FLASH_HEADS = 8
NEG = -1e30
LOG2E = 1.4426950408889634


def _resident(shape):
    nd = len(shape)
    return pl.BlockSpec(shape, lambda *_: (0,) * nd, pipeline_mode=pl.Buffered(1))


def _params(n_axes):
    return pltpu.CompilerParams(dimension_semantics=("arbitrary",) * n_axes,
                                vmem_limit_bytes=VMEM_LIMIT)


def _rms(x, gain):
    return x * lax.rsqrt(jnp.mean(x * x, axis=-1, keepdims=True) + EPS) * gain


def _dot(a, b):
    return jnp.dot(a, b, preferred_element_type=F32)


def _dot_nt(a, b):
    return lax.dot_general(a, b, (((1,), (1,)), ((), ())), preferred_element_type=F32)


def _token_tile(n, cap=512):
    t = min(n, cap)
    assert n % t == 0 and t % SUBLANES == 0
    return t


def _memkv_kernel(mem_ref, gin_ref, wk_ref, wv_ref, gk_ref, k_ref, v_ref):
    mn = _rms(mem_ref[0], gin_ref[...]).astype(BF16)
    k = _dot(mn, wk_ref[...])
    hd = gk_ref.shape[-1]
    for h in range(MEM_HEADS):
        sl = slice(h * hd, (h + 1) * hd)
        k_ref[0, :, sl] = _rms(k[:, sl], gk_ref[...])
    v_ref[0] = _dot(mn, wv_ref[...])


def _memkv(mem, g_in, w_k, w_v, g_k):
    B, M, D = mem.shape
    tok = pl.BlockSpec((1, M, D), lambda b: (b, 0, 0))
    return pl.pallas_call(
        _memkv_kernel,
        grid=(B,),
        in_specs=[tok, _resident((1, D)), _resident((D, D)), _resident((D, D)),
                  _resident((1, D // MEM_HEADS))],
        out_specs=[tok, tok],
        out_shape=[jax.ShapeDtypeStruct((B, M, D), F32)] * 2,
        compiler_params=_params(1),
        name="memkv",
    )(mem, g_in.reshape(1, D), w_k.astype(BF16), w_v.astype(BF16), g_k.reshape(1, -1))


def _inproj_kernel(x_ref, prev_ref, h0_ref, gmix_ref, win_ref, cw_ref, cb_ref, wa_ref, ba_ref,
                   wi_ref, bi_ref, sp_ref, gq_ref, gkv_ref, cos_ref, sin_ref,
                   grec_ref, gatt_ref, cq_ref, ckv_ref, kr_ref, ctail_ref, htail_ref,
                   xext_ref, hcar_ref, *, tm, d, q_lora, kv_lora):
    @pl.when(pl.program_id(1) == 0)
    def _():
        xext_ref[0:SUBLANES, :] = prev_ref[0]
        hcar_ref[...] = h0_ref[0]

    hn = _rms(x_ref[0], gmix_ref[...]).astype(BF16)

    def proj(c0, c1):
        return _dot(hn, win_ref[:, c0:c1])

    x_rec = proj(0, d)
    xext_ref[SUBLANES:SUBLANES + tm, :] = x_rec
    xc = cb_ref[...] + cw_ref[CONV_W - 1:CONV_W, :] * x_rec
    for j in range(CONV_W - 1):
        off = SUBLANES - (CONV_W - 1) + j
        xc = xc + cw_ref[j:j + 1, :] * xext_ref[off:off + tm, :]
    tail = x_rec[tm - SUBLANES:tm, :]
    ctail_ref[0] = tail
    xext_ref[0:SUBLANES, :] = tail

    xcb = xc.astype(BF16)
    nb = d // MXU_DIM
    gr = jnp.concatenate(
        [_dot(xcb[:, c * MXU_DIM:(c + 1) * MXU_DIM], wa_ref[c]) for c in range(nb)], axis=1)
    gi = jnp.concatenate(
        [_dot(xcb[:, c * MXU_DIM:(c + 1) * MXU_DIM], wi_ref[c]) for c in range(nb)], axis=1)
    r = jax.nn.sigmoid(gr + ba_ref[...])
    ig = jax.nn.sigmoid(gi + bi_ref[...])
    log_a = (-LRU_C) * r * sp_ref[...]
    a = jnp.exp(log_a)
    u = jnp.sqrt(1.0 - a * a) * (ig * xc)

    row = lax.broadcasted_iota(jnp.int32, (tm, 1), 0) & (SUBLANES - 1)
    s = 1
    while s < SUBLANES:
        keep = row >= s
        u = a * jnp.where(keep, pltpu.roll(u, s, 0), 0.0) + u
        a = a * jnp.where(keep, pltpu.roll(a, s, 0), 1.0)
        s *= 2
    h_prev = hcar_ref[SUBLANES - 1:SUBLANES, :]
    groups = []
    for g in range(tm // SUBLANES):
        rows = slice(g * SUBLANES, (g + 1) * SUBLANES)
        hg = a[rows] * h_prev + u[rows]
        groups.append(hg)
        h_prev = hg[SUBLANES - 1:SUBLANES, :]
    h = jnp.concatenate(groups, axis=0)
    htail = groups[-1]
    hcar_ref[...] = htail
    htail_ref[0] = htail

    x_gate = proj(d, 2 * d)
    g_rec = proj(2 * d, 3 * d)
    grec_ref[0] = jax.nn.sigmoid(g_rec) * (jax.nn.gelu(x_gate) * h)
    gatt_ref[0] = jax.nn.sigmoid(proj(3 * d, 4 * d))

    o = 4 * d
    cq_ref[0] = _rms(proj(o, o + q_lora), gq_ref[...]).astype(BF16)
    o += q_lora
    ckv_ref[0] = _rms(proj(o, o + kv_lora), gkv_ref[...])
    o += kv_lora
    kr = proj(o, o + LANES) * cos_ref[...] + proj(o + LANES, o + 2 * LANES) * sin_ref[...]
    kr_ref[0] = kr[:, :ROPE_DIM]


def _inproj(x, prev8, h0_8, w, cos_k, sin_k):
    B, L, D = x.shape
    tm = _token_tile(L, 256)
    q_lora, kv_lora = w["gq"].shape[-1], w["gkv"].shape[-1]
    ncol = w["w_in"].shape[-1]
    tok = lambda n: pl.BlockSpec((1, tm, n), lambda b, l: (b, l, 0))
    per_b = pl.BlockSpec((1, SUBLANES, D), lambda b, l: (b, 0, 0))
    tab = pl.BlockSpec((tm, LANES), lambda b, l: (l, 0))
    nb = D // MXU_DIM
    kern = functools.partial(_inproj_kernel, tm=tm, d=D, q_lora=q_lora, kv_lora=kv_lora)
    return pl.pallas_call(
        kern,
        grid=(B, L // tm),
        in_specs=[tok(D), per_b, per_b, _resident((1, D)), _resident((D, ncol)),
                  _resident((CONV_W, D)), _resident((1, D)),
                  _resident((nb, MXU_DIM, MXU_DIM)), _resident((1, D)),
                  _resident((nb, MXU_DIM, MXU_DIM)), _resident((1, D)), _resident((1, D)),
                  _resident((1, q_lora)), _resident((1, kv_lora)), tab, tab],
        out_specs=[tok(D), tok(D), tok(q_lora), tok(kv_lora), tok(ROPE_DIM), per_b, per_b],
        out_shape=[jax.ShapeDtypeStruct((B, L, D), F32), jax.ShapeDtypeStruct((B, L, D), F32),
                   jax.ShapeDtypeStruct((B, L, q_lora), BF16),
                   jax.ShapeDtypeStruct((B, L, kv_lora), F32),
                   jax.ShapeDtypeStruct((B, L, ROPE_DIM), F32),
                   jax.ShapeDtypeStruct((B, SUBLANES, D), F32),
                   jax.ShapeDtypeStruct((B, SUBLANES, D), F32)],
        scratch_shapes=[pltpu.VMEM((SUBLANES + tm, D), F32), pltpu.VMEM((SUBLANES, D), F32)],
        compiler_params=_params(2),
        name="inproj",
    )(x, prev8, h0_8, w["gmix"], w["w_in"], w["conv_w"], w["conv_b"], w["wa"], w["ba"],
      w["wi"], w["bi"], w["sp"], w["gq"], w["gkv"], cos_k, sin_k)


def _qprep_kernel(cq_ref, wa_ref, wb_ref, cos_ref, sin_ref, g_ref, q_ref):
    cq = cq_ref[0]
    qa = _dot(cq, wa_ref[...])
    qb = _dot(cq, wb_ref[...])
    for h in range(N_HEADS):
        sl = slice(h * HEAD_PAD, (h + 1) * HEAD_PAD)
        qh = qa[:, sl] * cos_ref[...] + qb[:, sl] * sin_ref[...]
        ss = jnp.sum(qh * qh, axis=-1, keepdims=True) * (1.0 / QK_DIM)
        q_ref[0, :, sl] = (qh * lax.rsqrt(ss + EPS) * g_ref[...]).astype(BF16)


def _qprep(cq, w, cos_q, sin_q):
    B, L, QL = cq.shape
    tm = _token_tile(L)
    n = N_HEADS * HEAD_PAD
    tab = pl.BlockSpec((tm, HEAD_PAD), lambda b, l: (l, 0))
    return pl.pallas_call(
        _qprep_kernel,
        grid=(B, L // tm),
        in_specs=[pl.BlockSpec((1, tm, QL), lambda b, l: (b, l, 0)),
                  _resident((QL, n)), _resident((QL, n)), tab, tab, _resident((1, HEAD_PAD))],
        out_specs=pl.BlockSpec((1, tm, n), lambda b, l: (b, l, 0)),
        out_shape=jax.ShapeDtypeStruct((B, L, n), BF16),
        compiler_params=_params(2),
        name="qprep",
    )(cq, w["wq_a"], w["wq_b"], cos_q, sin_q, w["gqh"])


def _kvprep_kernel(ckv_ref, kr_ref, wk_ref, place_ref, wvt_ref, g_ref, k_ref, vt_ref):
    c = ckv_ref[0].astype(BF16)
    k = _dot(c, wk_ref[...]) + _dot(kr_ref[0].astype(BF16), place_ref[...])
    for h in range(N_HEADS):
        sl = slice(h * HEAD_PAD, (h + 1) * HEAD_PAD)
        kh = k[:, sl]
        ss = jnp.sum(kh * kh, axis=-1, keepdims=True) * (1.0 / QK_DIM)
        k_ref[0, :, sl] = (kh * lax.rsqrt(ss + EPS) * g_ref[...]).astype(BF16)
    vt_ref[0, 0] = _dot_nt(wvt_ref[...], c).astype(BF16)


def _kvprep(ckv, kr, w, tk):
    B, T, KV = ckv.shape
    n = N_HEADS * HEAD_PAD
    dv = w["wv_t"].shape[0]
    return pl.pallas_call(
        _kvprep_kernel,
        grid=(B, T // tk),
        in_specs=[pl.BlockSpec((1, tk, KV), lambda b, t: (b, t, 0)),
                  pl.BlockSpec((1, tk, ROPE_DIM), lambda b, t: (b, t, 0)),
                  _resident((KV, n)), _resident((ROPE_DIM, n)), _resident((dv, KV)),
                  _resident((1, HEAD_PAD))],
        out_specs=[pl.BlockSpec((1, tk, n), lambda b, t: (b, t, 0)),
                   pl.BlockSpec((1, 1, dv, tk), lambda b, t: (b, t, 0, 0))],
        out_shape=[jax.ShapeDtypeStruct((B, T, n), BF16),
                   jax.ShapeDtypeStruct((B, T // tk, dv, tk), BF16)],
        compiler_params=_params(2),
        name="kvprep",
    )(ckv, kr, w["wk"], w["place"], w["wv_t"], w["gkh"])


def _flash_kernel(q_ref, k_ref, vt_ref, o_ref, *, tq, tk, q_off, t_valid, v_dim, heads):
    q_start = q_off + pl.program_id(2) * tq
    n_full = jnp.minimum(q_start, t_valid) // tk
    n_end = (jnp.minimum(q_start + tq, t_valid) + tk - 1) // tk
    shift = int(math.log2(CHUNK))
    q_chunk = lax.shift_right_logical(q_start + lax.broadcasted_iota(jnp.int32, (1, tq), 1), shift)
    k_iota = lax.broadcasted_iota(jnp.int32, (tk, 1), 0)

    def step(j, carry, masked):
        ks = pl.multiple_of(j * tk, tk)
        if masked:
            k_pos = ks + k_iota
            vis = (lax.shift_right_logical(k_pos, shift) <= q_chunk) & (k_pos < t_valid)
        scores = []
        for hh in range(heads):
            qh = q_ref[0, :, hh * HEAD_PAD:(hh + 1) * HEAD_PAD]
            kj = k_ref[0, pl.ds(ks, tk), hh * HEAD_PAD:(hh + 1) * HEAD_PAD]
            scores.append(_dot_nt(kj, qh))
        probs = []
        for hh in range(heads):
            m, l, _ = carry[hh]
            s = jnp.where(vis, scores[hh], NEG) if masked else scores[hh]
            m_new = jnp.maximum(m, jnp.max(s, axis=0, keepdims=True))
            alpha = jnp.exp2(m - m_new)
            p = jnp.exp2(s - m_new)
            l = alpha * l + jnp.sum(p, axis=0, keepdims=True)
            probs.append((m_new, l, alpha, p.astype(BF16)))
        out = []
        for hh in range(heads):
            m_new, l, alpha, p = probs[hh]
            vj = vt_ref[0, j, hh * v_dim:(hh + 1) * v_dim, :]
            out.append((m_new, l, alpha * carry[hh][2] + _dot(vj, p)))
        return tuple(out)

    init = tuple((jnp.full((1, tq), NEG, F32), jnp.zeros((1, tq), F32), jnp.zeros((v_dim, tq), F32))
                 for _ in range(heads))
    carry = lax.fori_loop(0, n_full, functools.partial(step, masked=False), init)
    carry = lax.fori_loop(n_full, n_end, functools.partial(step, masked=True), carry)
    o_ref[0] = jnp.concatenate([acc / l for _, l, acc in carry], axis=0).T


def _flash(q, k, vt, *, q_off, t_valid, tk, heads=FLASH_HEADS):
    B, L, n = q.shape
    T = k.shape[1]
    dv = vt.shape[2]
    v_dim = dv // N_HEADS
    tq = _token_tile(L)
    assert tq % CHUNK == 0 and q_off % CHUNK == 0 and (heads * v_dim) % LANES == 0
    kern = functools.partial(_flash_kernel, tq=tq, tk=tk, q_off=q_off, t_valid=t_valid, v_dim=v_dim,
                             heads=heads)
    return pl.pallas_call(
        kern,
        grid=(B, N_HEADS // heads, L // tq),
        in_specs=[pl.BlockSpec((1, tq, heads * HEAD_PAD), lambda b, p, i: (b, i, p)),
                  pl.BlockSpec((1, T, heads * HEAD_PAD), lambda b, p, i: (b, 0, p)),
                  pl.BlockSpec((1, T // tk, heads * v_dim, tk), lambda b, p, i: (b, 0, p, 0))],
        out_specs=pl.BlockSpec((1, tq, heads * v_dim), lambda b, p, i: (b, i, p)),
        out_shape=jax.ShapeDtypeStruct((B, L, dv), F32),
        compiler_params=_params(3),
        name="flash",
    )(q, k, vt)


def _post_kernel(x_ref, grec_ref, gatt_ref, att_ref, wo_ref, gmem_ref, wmq_ref, gqh_ref,
                 mk_ref, mv_ref, wmo_ref, o_ref):
    mixed = (grec_ref[0] + gatt_ref[0] * att_ref[0]).astype(BF16)
    x1 = x_ref[0] + _dot(mixed, wo_ref[...])
    qm = _dot(_rms(x1, gmem_ref[...]).astype(BF16), wmq_ref[...])
    hd = gqh_ref.shape[-1]
    outs = []
    for h in range(MEM_HEADS):
        sl = slice(h * hd, (h + 1) * hd)
        qh = (_rms(qm[:, sl], gqh_ref[...]) * (hd ** -0.5)).astype(BF16)
        s = _dot_nt(qh, mk_ref[0, :, sl])
        p = jnp.exp(s - jnp.max(s, axis=-1, keepdims=True))
        l = jnp.sum(p, axis=-1, keepdims=True)
        outs.append((_dot(p.astype(BF16), mv_ref[0, :, sl]) / l).astype(BF16))
    o_ref[0] = x1 + _dot(jnp.concatenate(outs, axis=1), wmo_ref[...])


def _post(x, grec, gatt, att, mem_k, mem_v, w):
    B, L, D = x.shape
    M = mem_k.shape[1]
    tm = _token_tile(L)
    tok = pl.BlockSpec((1, tm, D), lambda b, l: (b, l, 0))
    mem = pl.BlockSpec((1, M, D), lambda b, l: (b, 0, 0))
    return pl.pallas_call(
        _post_kernel,
        grid=(B, L // tm),
        in_specs=[tok, tok, tok, tok, _resident((D, D)), _resident((1, D)), _resident((D, D)),
                  _resident((1, D // MEM_HEADS)), mem, mem, _resident((D, D))],
        out_specs=tok,
        out_shape=jax.ShapeDtypeStruct((B, L, D), F32),
        compiler_params=_params(2),
        name="post",
    )(x, grec, gatt, att, w["w_out"], w["gmem"], w["w_mq"], w["gmqh"],
      mem_k.astype(BF16), mem_v.astype(BF16), w["w_mo"])


def _swiglu_mid(a, b):
    return (a * jax.nn.sigmoid(a) * b).astype(BF16)


def _ffn_kernel(x_ref, g_ref, w1_ref, w3_ref, w2_ref, o_ref, *, n_split):
    x = x_ref[...]
    hf = _rms(x, g_ref[...]).astype(BF16)
    fc = w1_ref.shape[1] // n_split
    acc = x
    for c in range(n_split):
        sl = slice(c * fc, (c + 1) * fc)
        acc = acc + _dot(_swiglu_mid(_dot(hf, w1_ref[:, sl]), _dot(hf, w3_ref[:, sl])), w2_ref[sl, :])
    o_ref[...] = acc


def _ffn(x, g, w1, w3, w2):
    N, D = x.shape
    F = w1.shape[1]
    tm = _token_tile(N)
    n_split = 2
    assert F % (n_split * LANES) == 0
    tok = pl.BlockSpec((tm, D), lambda i: (i, 0))
    return pl.pallas_call(
        functools.partial(_ffn_kernel, n_split=n_split),
        grid=(N // tm,),
        in_specs=[tok, _resident((1, D)), _resident((D, F)), _resident((D, F)), _resident((F, D))],
        out_specs=tok,
        out_shape=jax.ShapeDtypeStruct((N, D), F32),
        compiler_params=_params(1),
        name="ffn",
    )(x, g, w1, w3, w2)


def _moe_kernel(x_ref, g_ref, wrh_ref, wrl_ref, br_ref, w1_ref, w3_ref, w2_ref, o_ref,
                hf_ref, gate_ref, rank_ref, *, ts, cap, n_exp):
    e = pl.program_id(1)
    lane = lax.broadcasted_iota(jnp.int32, (1, LANES), 1)
    n_sub = hf_ref.shape[0] // ts

    @pl.when(e == 0)
    def _():
        x = x_ref[...]
        hf = _rms(x, g_ref[...])
        hi = hf.astype(BF16)
        lo = (hf - hi.astype(F32)).astype(BF16)
        logits = (_dot(hi, wrh_ref[...]) + _dot(lo, wrh_ref[...]) + _dot(hi, wrl_ref[...])
                  + br_ref[...])
        m1 = jnp.max(logits, axis=-1, keepdims=True)
        i1 = jnp.min(jnp.where(logits == m1, lane, LANES), axis=-1, keepdims=True)
        rest = jnp.where(lane == i1, NEG, logits)
        m2 = jnp.max(rest, axis=-1, keepdims=True)
        i2 = jnp.min(jnp.where(rest == m2, lane, LANES), axis=-1, keepdims=True)
        e2 = jnp.exp(m2 - m1)
        den = 1.0 + e2
        gate_ref[...] = jnp.where(lane == i1, 1.0 / den, 0.0) + jnp.where(lane == i2, e2 / den, 0.0)
        hf_ref[...] = hi
        o_ref[...] = x
        sel = jnp.where((lane == i1) | (lane == i2), 1.0, 0.0)
        before = (lax.broadcasted_iota(jnp.int32, (ts, ts), 0)
                  < lax.broadcasted_iota(jnp.int32, (ts, ts), 1))
        tri = jnp.where(before, 1.0, 0.0).astype(BF16)
        for s in range(n_sub):
            sel_t = sel[s * ts:(s + 1) * ts, :].T[:n_exp, :]
            rank = _dot(sel_t.astype(BF16), tri)
            rank_ref[s] = jnp.where(sel_t > 0.0, rank, -1.0)

    row = lax.broadcasted_iota(jnp.int32, (cap, 1), 0)
    for s in range(n_sub):
        rows = slice(s * ts, (s + 1) * ts)
        rank_e = rank_ref[s, pl.ds(e, 1), :]
        n_tok = jnp.max(rank_e).astype(jnp.int32) + 1
        ge = jnp.sum(jnp.where(lane == e, gate_ref[rows, :], 0.0), axis=-1, keepdims=True)

        def chunk(c, carry):
            want = (row + c * cap).astype(F32)
            onehot = jnp.where(rank_e == want, 1.0, 0.0).astype(BF16)
            xg = _dot(onehot, hf_ref[rows, :]).astype(BF16)
            y = _dot(_swiglu_mid(_dot(xg, w1_ref[0]), _dot(xg, w3_ref[0])), w2_ref[0])
            back = lax.dot_general(onehot, y.astype(BF16), (((0,), (0,)), ((), ())),
                                   preferred_element_type=F32)
            o_ref[rows, :] += ge * back
            return carry

        lax.fori_loop(0, (n_tok + cap - 1) // cap, chunk, 0)


def _moe(x, g, wr_hi, wr_lo, br, w1, w3, w2):
    N, D = x.shape
    E, _, F = w1.shape
    tm = _token_tile(N, 1024)
    ts = _token_tile(tm, 512)
    assert E <= SUBLANES
    tok = pl.BlockSpec((tm, D), lambda i, e: (i, 0))
    kern = functools.partial(_moe_kernel, ts=ts, cap=MOE_CHUNK, n_exp=E)
    return pl.pallas_call(
        kern,
        grid=(N // tm, E),
        in_specs=[tok, _resident((1, D)), _resident((D, LANES)), _resident((D, LANES)),
                  _resident((1, LANES)),
                  pl.BlockSpec((1, D, F), lambda i, e: (e, 0, 0)),
                  pl.BlockSpec((1, D, F), lambda i, e: (e, 0, 0)),
                  pl.BlockSpec((1, F, D), lambda i, e: (e, 0, 0))],
        out_specs=tok,
        out_shape=jax.ShapeDtypeStruct((N, D), F32),
        scratch_shapes=[pltpu.VMEM((tm, D), BF16), pltpu.VMEM((tm, LANES), F32),
                        pltpu.VMEM((tm // ts, E, ts), F32)],
        compiler_params=_params(2),
        name="moe",
    )(x, g, wr_hi, wr_lo, br, w1, w3, w2)


def _rot_half_cols(w):
    half = ROPE_DIM // 2
    return jnp.concatenate([-w[..., half:], w[..., :half]], axis=-1)


def _pad_cols(w, n):
    return jnp.pad(w, [(0, 0)] * (w.ndim - 1) + [(0, n - w.shape[-1])])


def _head_pad(nope, rope):
    z = jnp.zeros(nope.shape[:-1] + (HEAD_PAD - QK_DIM,), nope.dtype)
    out = jnp.concatenate([nope, rope, z], axis=-1)
    return out.reshape(out.shape[:-2] + (N_HEADS * HEAD_PAD,))


def _block_diag_tiles(w):
    nblk, bw, _ = w.shape
    per = MXU_DIM // bw
    w4 = w.reshape(nblk // per, per, bw, bw)
    eye = jnp.eye(per, dtype=w.dtype)
    return jnp.einsum("cpij,pq->cpiqj", w4, eye).reshape(nblk // per, MXU_DIM, MXU_DIM)


def _layer_weights(l, p):
    D = p["w_in"].shape[1]
    q_lora = p["q_lat_norm"].shape[-1]
    kv_lora = p["kv_lat_norm"].shape[-1]
    w_in = p["w_in"][l]
    sp = (D, 2 * D, 2 * D + q_lora, 2 * D + q_lora + kv_lora, 2 * D + q_lora + kv_lora + ROPE_DIM,
          3 * D + q_lora + kv_lora + ROPE_DIM)
    x_rec, x_gate, c_q, c_kv, k_rot, g_rec, g_att = jnp.split(w_in, sp, axis=-1)
    w_in_perm = jnp.concatenate(
        [x_rec, x_gate, g_rec, g_att, c_q, c_kv, _pad_cols(k_rot, LANES),
         _pad_cols(_rot_half_cols(k_rot), LANES)], axis=-1).astype(BF16)

    wq = p["w_uq"][l].reshape(q_lora, N_HEADS, QK_DIM)
    wq_n, wq_r = wq[..., :NOPE_DIM], wq[..., NOPE_DIM:]
    wk = p["w_uk"][l].reshape(kv_lora, N_HEADS, NOPE_DIM)
    eye_r = jnp.broadcast_to(jnp.eye(ROPE_DIM, dtype=F32)[:, None, :], (ROPE_DIM, N_HEADS, ROPE_DIM))
    gq, gk = p["q_head_norm"][l], p["k_head_norm"][l]
    q_scale = (QK_DIM ** -0.5) * LOG2E
    return dict(
        gmix=p["norm_mix"][l].reshape(1, D), w_in=w_in_perm,
        conv_w=p["conv_w"][l], conv_b=p["conv_b"][l].reshape(1, D),
        wa=_block_diag_tiles(p["lru_wa"][l]).astype(BF16), ba=p["lru_ba"][l].reshape(1, D),
        wi=_block_diag_tiles(p["lru_wi"][l]).astype(BF16), bi=p["lru_bi"][l].reshape(1, D),
        sp=jax.nn.softplus(-p["lru_lambda"][l]).reshape(1, D),
        gq=p["q_lat_norm"][l].reshape(1, q_lora), gkv=p["kv_lat_norm"][l].reshape(1, kv_lora),
        wq_a=_head_pad(wq_n, wq_r).astype(BF16),
        wq_b=_head_pad(jnp.zeros_like(wq_n), _rot_half_cols(wq_r)).astype(BF16),
        gqh=(_pad_cols(gq, HEAD_PAD) * q_scale).reshape(1, HEAD_PAD),
        wk=_head_pad(wk, jnp.zeros((kv_lora, N_HEADS, ROPE_DIM), F32)).astype(BF16),
        place=_head_pad(jnp.zeros((ROPE_DIM, N_HEADS, NOPE_DIM), F32), eye_r).astype(BF16),
        wv_t=p["w_uv"][l].T.astype(BF16),
        gkh=_pad_cols(gk, HEAD_PAD).reshape(1, HEAD_PAD),
        w_out=p["w_out"][l].astype(BF16), gmem=p["norm_mem"][l].reshape(1, D),
        w_mq=p["w_mq"][l].astype(BF16), gmqh=p["mq_head_norm"][l].reshape(1, -1),
        w_mo=p["w_mo"][l].astype(BF16), gffn=p["norm_ffn"][l].reshape(1, D),
    )


def _rope_tables(pos):
    half = ROPE_DIM // 2
    inv_freq = ROPE_BASE ** (-jnp.arange(half, dtype=F32) / half)
    ang = pos.astype(F32)[:, None] * inv_freq[None, :]
    cos2 = jnp.concatenate([jnp.cos(ang), jnp.cos(ang)], axis=-1)
    sin2 = jnp.concatenate([jnp.sin(ang), jnp.sin(ang)], axis=-1)
    n = pos.shape[0]
    ones, zeros = jnp.ones((n, NOPE_DIM), F32), jnp.zeros((n, NOPE_DIM), F32)
    return dict(cos_k=_pad_cols(cos2, LANES), sin_k=_pad_cols(sin2, LANES),
                cos_q=_pad_cols(jnp.concatenate([ones, cos2], axis=-1), HEAD_PAD),
                sin_q=_pad_cols(jnp.concatenate([zeros, sin2], axis=-1), HEAD_PAD))


def _layer(l, x, tabs, q_off, conv_prev, h0, ckv_past, kr_past, mem_k, mem_v, w, p):
    B, L, D = x.shape
    prev8 = jnp.pad(conv_prev, ((0, 0), (SUBLANES - (CONV_W - 1), 0), (0, 0)))
    h0_8 = jnp.pad(h0[:, None, :], ((0, 0), (SUBLANES - 1, 0), (0, 0)))
    grec, gatt, cq, ckv, kr, ctail, htail = _inproj(x, prev8, h0_8, w, tabs["cos_k"], tabs["sin_k"])

    q = _qprep(cq, w, tabs["cos_q"], tabs["sin_q"])
    if ckv_past is None:
        ckv_all, kr_all = ckv, kr
    else:
        ckv_all = jnp.concatenate([ckv_past, ckv], axis=1)
        kr_all = jnp.concatenate([kr_past, kr], axis=1)
    t_valid = ckv_all.shape[1]
    tk = min(512, t_valid)
    t_pad = -(-t_valid // tk) * tk
    if t_pad != t_valid:
        ckv_all = jnp.pad(ckv_all, ((0, 0), (0, t_pad - t_valid), (0, 0)))
        kr_all = jnp.pad(kr_all, ((0, 0), (0, t_pad - t_valid), (0, 0)))
    k, vt = _kvprep(ckv_all, kr_all, w, tk)
    att = _flash(q, k, vt, q_off=q_off, t_valid=t_valid, tk=tk)

    x2 = _post(x, grec, gatt, att, mem_k, mem_v, w).reshape(B * L, D)
    j = l // 2
    if l % 2 == 0:
        x3 = _ffn(x2, w["gffn"], p["ffn_w1"][j].astype(BF16), p["ffn_w3"][j].astype(BF16),
                  p["ffn_w2"][j].astype(BF16))
    else:
        E = p["moe_router"].shape[-1]
        wr = _pad_cols(p["moe_router"][j], LANES)
        wr_hi = wr.astype(BF16)
        wr_lo = (wr - wr_hi.astype(F32)).astype(BF16)
        br = jnp.concatenate([p["moe_router_b"][j], jnp.full((LANES - E,), NEG, F32)]).reshape(1, LANES)
        x3 = _moe(x2, w["gffn"], wr_hi, wr_lo, br, p["moe_w1"][j].astype(BF16),
                  p["moe_w3"][j].astype(BF16), p["moe_w2"][j].astype(BF16))
    conv_state = ctail[:, SUBLANES - (CONV_W - 1):, :]
    return x3.reshape(B, L, D), conv_state, htail[:, SUBLANES - 1, :], ckv, kr


def kernel(x_prompt, x_sample, cache_ckv, cache_krope, cache_mem_k, cache_mem_v, state_lru, state_conv, mem_prompt, norm_mix, w_in, conv_w, conv_b, lru_wa, lru_ba, lru_wi, lru_bi, lru_lambda, q_lat_norm, w_uq, kv_lat_norm, w_uk, w_uv, q_head_norm, k_head_norm, w_out, norm_mem, mem_in_norm, w_mq, w_mk, w_mv, w_mo, mq_head_norm, mk_head_norm, norm_ffn, ffn_w1, ffn_w3, ffn_w2, moe_router, moe_router_b, moe_w1, moe_w3, moe_w2):
    p = dict(norm_mix=norm_mix, w_in=w_in, conv_w=conv_w, conv_b=conv_b, lru_wa=lru_wa, lru_ba=lru_ba,
             lru_wi=lru_wi, lru_bi=lru_bi, lru_lambda=lru_lambda, q_lat_norm=q_lat_norm, w_uq=w_uq,
             kv_lat_norm=kv_lat_norm, w_uk=w_uk, w_uv=w_uv, q_head_norm=q_head_norm,
             k_head_norm=k_head_norm, w_out=w_out, norm_mem=norm_mem, w_mq=w_mq, w_mo=w_mo,
             mq_head_norm=mq_head_norm, norm_ffn=norm_ffn, ffn_w1=ffn_w1, ffn_w3=ffn_w3,
             ffn_w2=ffn_w2, moe_router=moe_router, moe_router_b=moe_router_b, moe_w1=moe_w1,
             moe_w3=moe_w3, moe_w2=moe_w2)
    depth = w_in.shape[0]
    weights = [_layer_weights(l, p) for l in range(depth)]

    Bp, Lp, D = x_prompt.shape
    Bs, Ls, _ = x_sample.shape
    t_past = cache_ckv.shape[2]
    M = mem_prompt.shape[1]

    tabs_p = _rope_tables(jnp.arange(Lp, dtype=jnp.int32))
    zero_conv = jnp.zeros((Bp, CONV_W - 1, D), F32)
    zero_h = jnp.zeros((Bp, D), F32)
    x = x_prompt
    outs_p = [[] for _ in range(6)]
    for l in range(depth):
        mk, mv = _memkv(mem_prompt, mem_in_norm[l], w_mk[l], w_mv[l], mk_head_norm[l])
        x, cs, hl, ckv, kr = _layer(l, x, tabs_p, 0, zero_conv, zero_h, None, None, mk, mv, weights[l], p)
        for acc, val in zip(outs_p, (ckv, kr, hl, cs, mk.reshape(Bp, M, MEM_HEADS, -1),
                                     mv.reshape(Bp, M, MEM_HEADS, -1))):
            acc.append(val)
    y_prompt = x

    tabs_s = _rope_tables(t_past + jnp.arange(Ls, dtype=jnp.int32))
    x = x_sample
    outs_s = [[] for _ in range(4)]
    for l in range(depth):
        x, cs, hl, ckv, kr = _layer(l, x, tabs_s, t_past, state_conv[l], state_lru[l], cache_ckv[l],
                                    cache_krope[l], cache_mem_k[l].reshape(Bs, M, D),
                                    cache_mem_v[l].reshape(Bs, M, D), weights[l], p)
        for acc, val in zip(outs_s, (ckv, kr, hl, cs)):
            acc.append(val)
    y_sample = x

    return (y_prompt, y_sample) + tuple(jnp.stack(a) for a in outs_p) + tuple(jnp.stack(a) for a in outs_s)
```

```python
import functools
import math

import jax
import jax.numpy as jnp
from jax import lax
from jax.experimental import pallas as pl
from jax.experimental.pallas import tpu as pltpu

F32 = jnp.float32
BF16 = jnp.bfloat16

EPS = 1e-6
CHUNK = 64
LRU_BLOCKS = 16
LRU_C = 8.0
CONV_W = 4
N_HEADS = 16
NOPE_DIM = 64
ROPE_DIM = 32
QK_DIM = NOPE_DIM + ROPE_DIM
ROPE_BASE = 10000.0
MEM_HEADS = 4
TOP_K = 2

LANES = 128
SUBLANES = 8
MXU_DIM = 256
VMEM_LIMIT = 56 * 1024 * 1024

HEAD_PAD = LANES
MOE_CHUNK = 160
SAFE_LOG2_RANGE = 60.0
FLASH_HEADS = 8
NEG = -1e30
LOG2E = 1.4426950408889634


def _resident(shape):
    nd = len(shape)
    return pl.BlockSpec(shape, lambda *_: (0,) * nd, pipeline_mode=pl.Buffered(1))


def _params(n_axes):
    return pltpu.CompilerParams(dimension_semantics=("arbitrary",) * n_axes,
                                vmem_limit_bytes=VMEM_LIMIT)


def _rms(x, gain):
    return x * lax.rsqrt(jnp.mean(x * x, axis=-1, keepdims=True) + EPS) * gain


def _dot(a, b):
    return jnp.dot(a, b, preferred_element_type=F32)


def _dot_nt(a, b):
    return lax.dot_general(a, b, (((1,), (1,)), ((), ())), preferred_element_type=F32)


def _token_tile(n, cap=512):
    t = min(n, cap)
    assert n % t == 0 and t % SUBLANES == 0
    return t


def _memkv_kernel(mem_ref, gin_ref, wk_ref, wv_ref, gk_ref, k_ref, v_ref):
    mn = _rms(mem_ref[0], gin_ref[...]).astype(BF16)
    k = _dot(mn, wk_ref[...])
    hd = gk_ref.shape[-1]
    for h in range(MEM_HEADS):
        sl = slice(h * hd, (h + 1) * hd)
        k_ref[0, :, sl] = _rms(k[:, sl], gk_ref[...])
    v_ref[0] = _dot(mn, wv_ref[...])


def _memkv(mem, g_in, w_k, w_v, g_k):
    B, M, D = mem.shape
    tok = pl.BlockSpec((1, M, D), lambda b: (b, 0, 0))
    return pl.pallas_call(
        _memkv_kernel,
        grid=(B,),
        in_specs=[tok, _resident((1, D)), _resident((D, D)), _resident((D, D)),
                  _resident((1, D // MEM_HEADS))],
        out_specs=[tok, tok],
        out_shape=[jax.ShapeDtypeStruct((B, M, D), F32)] * 2,
        compiler_params=_params(1),
        name="memkv",
    )(mem, g_in.reshape(1, D), w_k.astype(BF16), w_v.astype(BF16), g_k.reshape(1, -1))


def _inproj_kernel(x_ref, prev_ref, h0_ref, gmix_ref, win_ref, cw_ref, cb_ref, wa_ref, ba_ref,
                   wi_ref, bi_ref, sp_ref, gq_ref, gkv_ref, cos_ref, sin_ref,
                   grec_ref, gatt_ref, cq_ref, ckv_ref, kr_ref, ctail_ref, htail_ref,
                   xext_ref, hcar_ref, *, tm, d, q_lora, kv_lora):
    @pl.when(pl.program_id(1) == 0)
    def _():
        xext_ref[0:SUBLANES, :] = prev_ref[0]
        hcar_ref[...] = h0_ref[0]

    hn = _rms(x_ref[0], gmix_ref[...]).astype(BF16)

    def proj(c0, c1):
        return _dot(hn, win_ref[:, c0:c1])

    x_rec = proj(0, d)
    xext_ref[SUBLANES:SUBLANES + tm, :] = x_rec
    xc = cb_ref[...] + cw_ref[CONV_W - 1:CONV_W, :] * x_rec
    for j in range(CONV_W - 1):
        off = SUBLANES - (CONV_W - 1) + j
        xc = xc + cw_ref[j:j + 1, :] * xext_ref[off:off + tm, :]
    tail = x_rec[tm - SUBLANES:tm, :]
    ctail_ref[0] = tail
    xext_ref[0:SUBLANES, :] = tail

    xcb = xc.astype(BF16)
    nb = d // MXU_DIM
    gr = jnp.concatenate(
        [_dot(xcb[:, c * MXU_DIM:(c + 1) * MXU_DIM], wa_ref[c]) for c in range(nb)], axis=1)
    gi = jnp.concatenate(
        [_dot(xcb[:, c * MXU_DIM:(c + 1) * MXU_DIM], wi_ref[c]) for c in range(nb)], axis=1)
    r = jax.nn.sigmoid(gr + ba_ref[...])
    ig = jax.nn.sigmoid(gi + bi_ref[...])
    log_a = (-LRU_C) * r * sp_ref[...]
    a = jnp.exp(log_a)
    u = jnp.sqrt(1.0 - a * a) * (ig * xc)

    row = lax.broadcasted_iota(jnp.int32, (tm, 1), 0) & (SUBLANES - 1)
    s = 1
    while s < SUBLANES:
        keep = row >= s
        u = a * jnp.where(keep, pltpu.roll(u, s, 0), 0.0) + u
        a = a * jnp.where(keep, pltpu.roll(a, s, 0), 1.0)
        s *= 2
    h_prev = hcar_ref[SUBLANES - 1:SUBLANES, :]
    groups = []
    for g in range(tm // SUBLANES):
        rows = slice(g * SUBLANES, (g + 1) * SUBLANES)
        hg = a[rows] * h_prev + u[rows]
        groups.append(hg)
        h_prev = hg[SUBLANES - 1:SUBLANES, :]
    h = jnp.concatenate(groups, axis=0)
    htail = groups[-1]
    hcar_ref[...] = htail
    htail_ref[0] = htail

    x_gate = proj(d, 2 * d)
    g_rec = proj(2 * d, 3 * d)
    grec_ref[0] = jax.nn.sigmoid(g_rec) * (jax.nn.gelu(x_gate) * h)
    gatt_ref[0] = jax.nn.sigmoid(proj(3 * d, 4 * d))

    o = 4 * d
    cq_ref[0] = _rms(proj(o, o + q_lora), gq_ref[...]).astype(BF16)
    o += q_lora
    ckv_ref[0] = _rms(proj(o, o + kv_lora), gkv_ref[...])
    o += kv_lora
    kr = proj(o, o + LANES) * cos_ref[...] + proj(o + LANES, o + 2 * LANES) * sin_ref[...]
    kr_ref[0] = kr[:, :ROPE_DIM]


def _inproj(x, prev8, h0_8, w, cos_k, sin_k):
    B, L, D = x.shape
    tm = _token_tile(L, 256)
    q_lora, kv_lora = w["gq"].shape[-1], w["gkv"].shape[-1]
    ncol = w["w_in"].shape[-1]
    tok = lambda n: pl.BlockSpec((1, tm, n), lambda b, l: (b, l, 0))
    per_b = pl.BlockSpec((1, SUBLANES, D), lambda b, l: (b, 0, 0))
    tab = pl.BlockSpec((tm, LANES), lambda b, l: (l, 0))
    nb = D // MXU_DIM
    kern = functools.partial(_inproj_kernel, tm=tm, d=D, q_lora=q_lora, kv_lora=kv_lora)
    return pl.pallas_call(
        kern,
        grid=(B, L // tm),
        in_specs=[tok(D), per_b, per_b, _resident((1, D)), _resident((D, ncol)),
                  _resident((CONV_W, D)), _resident((1, D)),
                  _resident((nb, MXU_DIM, MXU_DIM)), _resident((1, D)),
                  _resident((nb, MXU_DIM, MXU_DIM)), _resident((1, D)), _resident((1, D)),
                  _resident((1, q_lora)), _resident((1, kv_lora)), tab, tab],
        out_specs=[tok(D), tok(D), tok(q_lora), tok(kv_lora), tok(ROPE_DIM), per_b, per_b],
        out_shape=[jax.ShapeDtypeStruct((B, L, D), F32), jax.ShapeDtypeStruct((B, L, D), F32),
                   jax.ShapeDtypeStruct((B, L, q_lora), BF16),
                   jax.ShapeDtypeStruct((B, L, kv_lora), F32),
                   jax.ShapeDtypeStruct((B, L, ROPE_DIM), F32),
                   jax.ShapeDtypeStruct((B, SUBLANES, D), F32),
                   jax.ShapeDtypeStruct((B, SUBLANES, D), F32)],
        scratch_shapes=[pltpu.VMEM((SUBLANES + tm, D), F32), pltpu.VMEM((SUBLANES, D), F32)],
        compiler_params=_params(2),
        name="inproj",
    )(x, prev8, h0_8, w["gmix"], w["w_in"], w["conv_w"], w["conv_b"], w["wa"], w["ba"],
      w["wi"], w["bi"], w["sp"], w["gq"], w["gkv"], cos_k, sin_k)


def _qprep_kernel(cq_ref, wa_ref, wb_ref, cos_ref, sin_ref, g_ref, q_ref):
    cq = cq_ref[0]
    qa = _dot(cq, wa_ref[...])
    qb = _dot(cq, wb_ref[...])
    for h in range(N_HEADS):
        sl = slice(h * HEAD_PAD, (h + 1) * HEAD_PAD)
        qh = qa[:, sl] * cos_ref[...] + qb[:, sl] * sin_ref[...]
        ss = jnp.sum(qh * qh, axis=-1, keepdims=True) * (1.0 / QK_DIM)
        q_ref[0, :, sl] = (qh * lax.rsqrt(ss + EPS) * g_ref[...]).astype(BF16)


def _qprep(cq, w, cos_q, sin_q):
    B, L, QL = cq.shape
    tm = _token_tile(L)
    n = N_HEADS * HEAD_PAD
    tab = pl.BlockSpec((tm, HEAD_PAD), lambda b, l: (l, 0))
    return pl.pallas_call(
        _qprep_kernel,
        grid=(B, L // tm),
        in_specs=[pl.BlockSpec((1, tm, QL), lambda b, l: (b, l, 0)),
                  _resident((QL, n)), _resident((QL, n)), tab, tab, _resident((1, HEAD_PAD))],
        out_specs=pl.BlockSpec((1, tm, n), lambda b, l: (b, l, 0)),
        out_shape=jax.ShapeDtypeStruct((B, L, n), BF16),
        compiler_params=_params(2),
        name="qprep",
    )(cq, w["wq_a"], w["wq_b"], cos_q, sin_q, w["gqh"])


def _kvprep_kernel(ckv_ref, kr_ref, wk_ref, place_ref, wvt_ref, g_ref, k_ref, vt_ref):
    c = ckv_ref[0].astype(BF16)
    k = _dot(c, wk_ref[...]) + _dot(kr_ref[0].astype(BF16), place_ref[...])
    for h in range(N_HEADS):
        sl = slice(h * HEAD_PAD, (h + 1) * HEAD_PAD)
        kh = k[:, sl]
        ss = jnp.sum(kh * kh, axis=-1, keepdims=True) * (1.0 / QK_DIM)
        k_ref[0, :, sl] = (kh * lax.rsqrt(ss + EPS) * g_ref[...]).astype(BF16)
    vt_ref[0, 0] = _dot_nt(wvt_ref[...], c).astype(BF16)


def _kvprep(ckv, kr, w, tk):
    B, T, KV = ckv.shape
    n = N_HEADS * HEAD_PAD
    dv = w["wv_t"].shape[0]
    return pl.pallas_call(
        _kvprep_kernel,
        grid=(B, T // tk),
        in_specs=[pl.BlockSpec((1, tk, KV), lambda b, t: (b, t, 0)),
                  pl.BlockSpec((1, tk, ROPE_DIM), lambda b, t: (b, t, 0)),
                  _resident((KV, n)), _resident((ROPE_DIM, n)), _resident((dv, KV)),
                  _resident((1, HEAD_PAD))],
        out_specs=[pl.BlockSpec((1, tk, n), lambda b, t: (b, t, 0)),
                   pl.BlockSpec((1, 1, dv, tk), lambda b, t: (b, t, 0, 0))],
        out_shape=[jax.ShapeDtypeStruct((B, T, n), BF16),
                   jax.ShapeDtypeStruct((B, T // tk, dv, tk), BF16)],
        compiler_params=_params(2),
        name="kvprep",
    )(ckv, kr, w["wk"], w["place"], w["wv_t"], w["gkh"])


def _flash_kernel(q_ref, k_ref, vt_ref, o_ref, *, tq, tk, q_off, t_valid, v_dim, heads, bounded):
    q_start = q_off + pl.program_id(2) * tq
    n_full = jnp.minimum(q_start, t_valid) // tk
    n_end = (jnp.minimum(q_start + tq, t_valid) + tk - 1) // tk
    shift = int(math.log2(CHUNK))
    q_chunk = lax.shift_right_logical(q_start + lax.broadcasted_iota(jnp.int32, (1, tq), 1), shift)
    k_iota = lax.broadcasted_iota(jnp.int32, (tk, 1), 0)

    def step(j, carry, masked):
        ks = pl.multiple_of(j * tk, tk)
        if masked:
            k_pos = ks + k_iota
            vis = (lax.shift_right_logical(k_pos, shift) <= q_chunk) & (k_pos < t_valid)
        scores = []
        for hh in range(heads):
            qh = q_ref[0, :, hh * HEAD_PAD:(hh + 1) * HEAD_PAD]
            kj = k_ref[0, pl.ds(ks, tk), hh * HEAD_PAD:(hh + 1) * HEAD_PAD]
            scores.append(_dot_nt(kj, qh))
        probs = []
        for hh in range(heads):
            m, l, _ = carry[hh]
            s = jnp.where(vis, scores[hh], NEG) if masked else scores[hh]
            if bounded:
                p = jnp.exp2(s)
                probs.append((m, l + jnp.sum(p, axis=0, keepdims=True), None, p.astype(BF16)))
            else:
                m_new = jnp.maximum(m, jnp.max(s, axis=0, keepdims=True))
                alpha = jnp.exp2(m - m_new)
                p = jnp.exp2(s - m_new)
                l = alpha * l + jnp.sum(p, axis=0, keepdims=True)
                probs.append((m_new, l, alpha, p.astype(BF16)))
        out = []
        for hh in range(heads):
            m_new, l, alpha, p = probs[hh]
            vj = vt_ref[0, j, hh * v_dim:(hh + 1) * v_dim, :]
            acc = carry[hh][2] if bounded else alpha * carry[hh][2]
            out.append((m_new, l, acc + _dot(vj, p)))
        return tuple(out)

    init = tuple((jnp.full((1, tq), NEG, F32), jnp.zeros((1, tq), F32), jnp.zeros((v_dim, tq), F32))
                 for _ in range(heads))
    carry = lax.fori_loop(0, n_full, functools.partial(step, masked=False), init)
    carry = lax.fori_loop(n_full, n_end, functools.partial(step, masked=True), carry)
    o_ref[0] = jnp.concatenate([acc / l for _, l, acc in carry], axis=0).T


def _flash(q, k, vt, *, q_off, t_valid, tk, bounded, heads=FLASH_HEADS):
    B, L, n = q.shape
    T = k.shape[1]
    dv = vt.shape[2]
    v_dim = dv // N_HEADS
    tq = _token_tile(L)
    assert tq % CHUNK == 0 and q_off % CHUNK == 0 and (heads * v_dim) % LANES == 0
    kern = functools.partial(_flash_kernel, tq=tq, tk=tk, q_off=q_off, t_valid=t_valid, v_dim=v_dim,
                             heads=heads, bounded=bounded)
    return pl.pallas_call(
        kern,
        grid=(B, N_HEADS // heads, L // tq),
        in_specs=[pl.BlockSpec((1, tq, heads * HEAD_PAD), lambda b, p, i: (b, i, p)),
                  pl.BlockSpec((1, T, heads * HEAD_PAD), lambda b, p, i: (b, 0, p)),
                  pl.BlockSpec((1, T // tk, heads * v_dim, tk), lambda b, p, i: (b, 0, p, 0))],
        out_specs=pl.BlockSpec((1, tq, heads * v_dim), lambda b, p, i: (b, i, p)),
        out_shape=jax.ShapeDtypeStruct((B, L, dv), F32),
        compiler_params=_params(3),
        name="flash",
    )(q, k, vt)


def _post_kernel(x_ref, grec_ref, gatt_ref, att_ref, wo_ref, gmem_ref, wmq_ref, gqh_ref,
                 mk_ref, mv_ref, wmo_ref, o_ref):
    mixed = (grec_ref[0] + gatt_ref[0] * att_ref[0]).astype(BF16)
    x1 = x_ref[0] + _dot(mixed, wo_ref[...])
    qm = _dot(_rms(x1, gmem_ref[...]).astype(BF16), wmq_ref[...])
    hd = gqh_ref.shape[-1]
    outs = []
    for h in range(MEM_HEADS):
        sl = slice(h * hd, (h + 1) * hd)
        qh = (_rms(qm[:, sl], gqh_ref[...]) * (hd ** -0.5)).astype(BF16)
        s = _dot_nt(qh, mk_ref[0, :, sl])
        p = jnp.exp(s - jnp.max(s, axis=-1, keepdims=True))
        l = jnp.sum(p, axis=-1, keepdims=True)
        outs.append((_dot(p.astype(BF16), mv_ref[0, :, sl]) / l).astype(BF16))
    o_ref[0] = x1 + _dot(jnp.concatenate(outs, axis=1), wmo_ref[...])


def _post(x, grec, gatt, att, mem_k, mem_v, w):
    B, L, D = x.shape
    M = mem_k.shape[1]
    tm = _token_tile(L)
    tok = pl.BlockSpec((1, tm, D), lambda b, l: (b, l, 0))
    mem = pl.BlockSpec((1, M, D), lambda b, l: (b, 0, 0))
    return pl.pallas_call(
        _post_kernel,
        grid=(B, L // tm),
        in_specs=[tok, tok, tok, tok, _resident((D, D)), _resident((1, D)), _resident((D, D)),
                  _resident((1, D // MEM_HEADS)), mem, mem, _resident((D, D))],
        out_specs=tok,
        out_shape=jax.ShapeDtypeStruct((B, L, D), F32),
        compiler_params=_params(2),
        name="post",
    )(x, grec, gatt, att, w["w_out"], w["gmem"], w["w_mq"], w["gmqh"],
      mem_k.astype(BF16), mem_v.astype(BF16), w["w_mo"])


def _swiglu_mid(a, b):
    return (a * jax.nn.sigmoid(a) * b).astype(BF16)


def _ffn_kernel(x_ref, g_ref, w1_ref, w3_ref, w2_ref, o_ref, *, n_split):
    x = x_ref[...]
    hf = _rms(x, g_ref[...]).astype(BF16)
    fc = w1_ref.shape[1] // n_split
    acc = x
    for c in range(n_split):
        sl = slice(c * fc, (c + 1) * fc)
        acc = acc + _dot(_swiglu_mid(_dot(hf, w1_ref[:, sl]), _dot(hf, w3_ref[:, sl])), w2_ref[sl, :])
    o_ref[...] = acc


def _ffn(x, g, w1, w3, w2):
    N, D = x.shape
    F = w1.shape[1]
    tm = _token_tile(N)
    n_split = 2
    assert F % (n_split * LANES) == 0
    tok = pl.BlockSpec((tm, D), lambda i: (i, 0))
    return pl.pallas_call(
        functools.partial(_ffn_kernel, n_split=n_split),
        grid=(N // tm,),
        in_specs=[tok, _resident((1, D)), _resident((D, F)), _resident((D, F)), _resident((F, D))],
        out_specs=tok,
        out_shape=jax.ShapeDtypeStruct((N, D), F32),
        compiler_params=_params(1),
        name="ffn",
    )(x, g, w1, w3, w2)


def _moe_kernel(x_ref, g_ref, wrh_ref, wrl_ref, br_ref, w1_ref, w3_ref, w2_ref, o_ref,
                hf_ref, gate_ref, rank_ref, *, ts, cap, n_exp):
    e = pl.program_id(1)
    lane = lax.broadcasted_iota(jnp.int32, (1, LANES), 1)
    n_sub = hf_ref.shape[0] // ts

    @pl.when(e == 0)
    def _():
        x = x_ref[...]
        hf = _rms(x, g_ref[...])
        hi = hf.astype(BF16)
        lo = (hf - hi.astype(F32)).astype(BF16)
        logits = (_dot(hi, wrh_ref[...]) + _dot(lo, wrh_ref[...]) + _dot(hi, wrl_ref[...])
                  + br_ref[...])
        m1 = jnp.max(logits, axis=-1, keepdims=True)
        i1 = jnp.min(jnp.where(logits == m1, lane, LANES), axis=-1, keepdims=True)
        rest = jnp.where(lane == i1, NEG, logits)
        m2 = jnp.max(rest, axis=-1, keepdims=True)
        i2 = jnp.min(jnp.where(rest == m2, lane, LANES), axis=-1, keepdims=True)
        e2 = jnp.exp(m2 - m1)
        den = 1.0 + e2
        gate_ref[...] = jnp.where(lane == i1, 1.0 / den, 0.0) + jnp.where(lane == i2, e2 / den, 0.0)
        hf_ref[...] = hi
        o_ref[...] = x
        sel = jnp.where((lane == i1) | (lane == i2), 1.0, 0.0)
        before = (lax.broadcasted_iota(jnp.int32, (ts, ts), 0)
                  < lax.broadcasted_iota(jnp.int32, (ts, ts), 1))
        tri = jnp.where(before, 1.0, 0.0).astype(BF16)
        for s in range(n_sub):
            sel_t = sel[s * ts:(s + 1) * ts, :].T[:n_exp, :]
            rank = _dot(sel_t.astype(BF16), tri)
            rank_ref[s] = jnp.where(sel_t > 0.0, rank, -1.0)

    row = lax.broadcasted_iota(jnp.int32, (cap, 1), 0)
    for s in range(n_sub):
        rows = slice(s * ts, (s + 1) * ts)
        rank_e = rank_ref[s, pl.ds(e, 1), :]
        n_tok = jnp.max(rank_e).astype(jnp.int32) + 1
        ge = jnp.sum(jnp.where(lane == e, gate_ref[rows, :], 0.0), axis=-1, keepdims=True)

        def chunk(c, carry):
            want = (row + c * cap).astype(F32)
            onehot = jnp.where(rank_e == want, 1.0, 0.0).astype(BF16)
            xg = _dot(onehot, hf_ref[rows, :]).astype(BF16)
            y = _dot(_swiglu_mid(_dot(xg, w1_ref[0]), _dot(xg, w3_ref[0])), w2_ref[0])
            back = lax.dot_general(onehot, y.astype(BF16), (((0,), (0,)), ((), ())),
                                   preferred_element_type=F32)
            o_ref[rows, :] += ge * back
            return carry

        lax.fori_loop(0, (n_tok + cap - 1) // cap, chunk, 0)


def _moe(x, g, wr_hi, wr_lo, br, w1, w3, w2):
    N, D = x.shape
    E, _, F = w1.shape
    tm = _token_tile(N, 1024)
    ts = _token_tile(tm, 512)
    assert E <= SUBLANES
    tok = pl.BlockSpec((tm, D), lambda i, e: (i, 0))
    kern = functools.partial(_moe_kernel, ts=ts, cap=MOE_CHUNK, n_exp=E)
    return pl.pallas_call(
        kern,
        grid=(N // tm, E),
        in_specs=[tok, _resident((1, D)), _resident((D, LANES)), _resident((D, LANES)),
                  _resident((1, LANES)),
                  pl.BlockSpec((1, D, F), lambda i, e: (e, 0, 0)),
                  pl.BlockSpec((1, D, F), lambda i, e: (e, 0, 0)),
                  pl.BlockSpec((1, F, D), lambda i, e: (e, 0, 0))],
        out_specs=tok,
        out_shape=jax.ShapeDtypeStruct((N, D), F32),
        scratch_shapes=[pltpu.VMEM((tm, D), BF16), pltpu.VMEM((tm, LANES), F32),
                        pltpu.VMEM((tm // ts, E, ts), F32)],
        compiler_params=_params(2),
        name="moe",
    )(x, g, wr_hi, wr_lo, br, w1, w3, w2)


def _rot_half_cols(w):
    half = ROPE_DIM // 2
    return jnp.concatenate([-w[..., half:], w[..., :half]], axis=-1)


def _pad_cols(w, n):
    return jnp.pad(w, [(0, 0)] * (w.ndim - 1) + [(0, n - w.shape[-1])])


def _head_pad(nope, rope):
    z = jnp.zeros(nope.shape[:-1] + (HEAD_PAD - QK_DIM,), nope.dtype)
    out = jnp.concatenate([nope, rope, z], axis=-1)
    return out.reshape(out.shape[:-2] + (N_HEADS * HEAD_PAD,))


def _block_diag_tiles(w):
    nblk, bw, _ = w.shape
    per = MXU_DIM // bw
    w4 = w.reshape(nblk // per, per, bw, bw)
    eye = jnp.eye(per, dtype=w.dtype)
    return jnp.einsum("cpij,pq->cpiqj", w4, eye).reshape(nblk // per, MXU_DIM, MXU_DIM)


def _layer_weights(l, p):
    D = p["w_in"].shape[1]
    q_lora = p["q_lat_norm"].shape[-1]
    kv_lora = p["kv_lat_norm"].shape[-1]
    w_in = p["w_in"][l]
    sp = (D, 2 * D, 2 * D + q_lora, 2 * D + q_lora + kv_lora, 2 * D + q_lora + kv_lora + ROPE_DIM,
          3 * D + q_lora + kv_lora + ROPE_DIM)
    x_rec, x_gate, c_q, c_kv, k_rot, g_rec, g_att = jnp.split(w_in, sp, axis=-1)
    w_in_perm = jnp.concatenate(
        [x_rec, x_gate, g_rec, g_att, c_q, c_kv, _pad_cols(k_rot, LANES),
         _pad_cols(_rot_half_cols(k_rot), LANES)], axis=-1).astype(BF16)

    wq = p["w_uq"][l].reshape(q_lora, N_HEADS, QK_DIM)
    wq_n, wq_r = wq[..., :NOPE_DIM], wq[..., NOPE_DIM:]
    wk = p["w_uk"][l].reshape(kv_lora, N_HEADS, NOPE_DIM)
    eye_r = jnp.broadcast_to(jnp.eye(ROPE_DIM, dtype=F32)[:, None, :], (ROPE_DIM, N_HEADS, ROPE_DIM))
    gq, gk = p["q_head_norm"][l], p["k_head_norm"][l]
    q_scale = (QK_DIM ** -0.5) * LOG2E
    return dict(
        gmix=p["norm_mix"][l].reshape(1, D), w_in=w_in_perm,
        conv_w=p["conv_w"][l], conv_b=p["conv_b"][l].reshape(1, D),
        wa=_block_diag_tiles(p["lru_wa"][l]).astype(BF16), ba=p["lru_ba"][l].reshape(1, D),
        wi=_block_diag_tiles(p["lru_wi"][l]).astype(BF16), bi=p["lru_bi"][l].reshape(1, D),
        sp=jax.nn.softplus(-p["lru_lambda"][l]).reshape(1, D),
        gq=p["q_lat_norm"][l].reshape(1, q_lora), gkv=p["kv_lat_norm"][l].reshape(1, kv_lora),
        wq_a=_head_pad(wq_n, wq_r).astype(BF16),
        wq_b=_head_pad(jnp.zeros_like(wq_n), _rot_half_cols(wq_r)).astype(BF16),
        gqh=(_pad_cols(gq, HEAD_PAD) * q_scale).reshape(1, HEAD_PAD),
        wk=_head_pad(wk, jnp.zeros((kv_lora, N_HEADS, ROPE_DIM), F32)).astype(BF16),
        place=_head_pad(jnp.zeros((ROPE_DIM, N_HEADS, NOPE_DIM), F32), eye_r).astype(BF16),
        wv_t=p["w_uv"][l].T.astype(BF16),
        gkh=_pad_cols(gk, HEAD_PAD).reshape(1, HEAD_PAD),
        score_bound=QK_DIM * jnp.max(jnp.abs(gq)) * jnp.max(jnp.abs(gk)) * q_scale,
        w_out=p["w_out"][l].astype(BF16), gmem=p["norm_mem"][l].reshape(1, D),
        w_mq=p["w_mq"][l].astype(BF16), gmqh=p["mq_head_norm"][l].reshape(1, -1),
        w_mo=p["w_mo"][l].astype(BF16), gffn=p["norm_ffn"][l].reshape(1, D),
    )


def _rope_tables(pos):
    half = ROPE_DIM // 2
    inv_freq = ROPE_BASE ** (-jnp.arange(half, dtype=F32) / half)
    ang = pos.astype(F32)[:, None] * inv_freq[None, :]
    cos2 = jnp.concatenate([jnp.cos(ang), jnp.cos(ang)], axis=-1)
    sin2 = jnp.concatenate([jnp.sin(ang), jnp.sin(ang)], axis=-1)
    n = pos.shape[0]
    ones, zeros = jnp.ones((n, NOPE_DIM), F32), jnp.zeros((n, NOPE_DIM), F32)
    return dict(cos_k=_pad_cols(cos2, LANES), sin_k=_pad_cols(sin2, LANES),
                cos_q=_pad_cols(jnp.concatenate([ones, cos2], axis=-1), HEAD_PAD),
                sin_q=_pad_cols(jnp.concatenate([zeros, sin2], axis=-1), HEAD_PAD))


def _layer(l, x, tabs, q_off, conv_prev, h0, ckv_past, kr_past, mem_k, mem_v, w, p):
    B, L, D = x.shape
    prev8 = jnp.pad(conv_prev, ((0, 0), (SUBLANES - (CONV_W - 1), 0), (0, 0)))
    h0_8 = jnp.pad(h0[:, None, :], ((0, 0), (SUBLANES - 1, 0), (0, 0)))
    grec, gatt, cq, ckv, kr, ctail, htail = _inproj(x, prev8, h0_8, w, tabs["cos_k"], tabs["sin_k"])

    q = _qprep(cq, w, tabs["cos_q"], tabs["sin_q"])
    if ckv_past is None:
        ckv_all, kr_all = ckv, kr
    else:
        ckv_all = jnp.concatenate([ckv_past, ckv], axis=1)
        kr_all = jnp.concatenate([kr_past, kr], axis=1)
    t_valid = ckv_all.shape[1]
    tk = min(512, t_valid)
    t_pad = -(-t_valid // tk) * tk
    if t_pad != t_valid:
        ckv_all = jnp.pad(ckv_all, ((0, 0), (0, t_pad - t_valid), (0, 0)))
        kr_all = jnp.pad(kr_all, ((0, 0), (0, t_pad - t_valid), (0, 0)))
    k, vt = _kvprep(ckv_all, kr_all, w, tk)
    flash = functools.partial(_flash, q_off=q_off, t_valid=t_valid, tk=tk)
    att = lax.cond(w["score_bound"] <= SAFE_LOG2_RANGE, functools.partial(flash, bounded=True),
                   functools.partial(flash, bounded=False), q, k, vt)

    x2 = _post(x, grec, gatt, att, mem_k, mem_v, w).reshape(B * L, D)
    j = l // 2
    if l % 2 == 0:
        x3 = _ffn(x2, w["gffn"], p["ffn_w1"][j].astype(BF16), p["ffn_w3"][j].astype(BF16),
                  p["ffn_w2"][j].astype(BF16))
    else:
        E = p["moe_router"].shape[-1]
        wr = _pad_cols(p["moe_router"][j], LANES)
        wr_hi = wr.astype(BF16)
        wr_lo = (wr - wr_hi.astype(F32)).astype(BF16)
        br = jnp.concatenate([p["moe_router_b"][j], jnp.full((LANES - E,), NEG, F32)]).reshape(1, LANES)
        x3 = _moe(x2, w["gffn"], wr_hi, wr_lo, br, p["moe_w1"][j].astype(BF16),
                  p["moe_w3"][j].astype(BF16), p["moe_w2"][j].astype(BF16))
    conv_state = ctail[:, SUBLANES - (CONV_W - 1):, :]
    return x3.reshape(B, L, D), conv_state, htail[:, SUBLANES - 1, :], ckv, kr


def kernel(x_prompt, x_sample, cache_ckv, cache_krope, cache_mem_k, cache_mem_v, state_lru, state_conv, mem_prompt, norm_mix, w_in, conv_w, conv_b, lru_wa, lru_ba, lru_wi, lru_bi, lru_lambda, q_lat_norm, w_uq, kv_lat_norm, w_uk, w_uv, q_head_norm, k_head_norm, w_out, norm_mem, mem_in_norm, w_mq, w_mk, w_mv, w_mo, mq_head_norm, mk_head_norm, norm_ffn, ffn_w1, ffn_w3, ffn_w2, moe_router, moe_router_b, moe_w1, moe_w3, moe_w2):
    p = dict(norm_mix=norm_mix, w_in=w_in, conv_w=conv_w, conv_b=conv_b, lru_wa=lru_wa, lru_ba=lru_ba,
             lru_wi=lru_wi, lru_bi=lru_bi, lru_lambda=lru_lambda, q_lat_norm=q_lat_norm, w_uq=w_uq,
             kv_lat_norm=kv_lat_norm, w_uk=w_uk, w_uv=w_uv, q_head_norm=q_head_norm,
             k_head_norm=k_head_norm, w_out=w_out, norm_mem=norm_mem, w_mq=w_mq, w_mo=w_mo,
             mq_head_norm=mq_head_norm, norm_ffn=norm_ffn, ffn_w1=ffn_w1, ffn_w3=ffn_w3,
             ffn_w2=ffn_w2, moe_router=moe_router, moe_router_b=moe_router_b, moe_w1=moe_w1,
             moe_w3=moe_w3, moe_w2=moe_w2)
    depth = w_in.shape[0]
    weights = [_layer_weights(l, p) for l in range(depth)]

    Bp, Lp, D = x_prompt.shape
    Bs, Ls, _ = x_sample.shape
    t_past = cache_ckv.shape[2]
    M = mem_prompt.shape[1]

    tabs_p = _rope_tables(jnp.arange(Lp, dtype=jnp.int32))
    zero_conv = jnp.zeros((Bp, CONV_W - 1, D), F32)
    zero_h = jnp.zeros((Bp, D), F32)
    x = x_prompt
    outs_p = [[] for _ in range(6)]
    for l in range(depth):
        mk, mv = _memkv(mem_prompt, mem_in_norm[l], w_mk[l], w_mv[l], mk_head_norm[l])
        x, cs, hl, ckv, kr = _layer(l, x, tabs_p, 0, zero_conv, zero_h, None, None, mk, mv, weights[l], p)
        for acc, val in zip(outs_p, (ckv, kr, hl, cs, mk.reshape(Bp, M, MEM_HEADS, -1),
                                     mv.reshape(Bp, M, MEM_HEADS, -1))):
            acc.append(val)
    y_prompt = x

    tabs_s = _rope_tables(t_past + jnp.arange(Ls, dtype=jnp.int32))
    x = x_sample
    outs_s = [[] for _ in range(4)]
    for l in range(depth):
        x, cs, hl, ckv, kr = _layer(l, x, tabs_s, t_past, state_conv[l], state_lru[l], cache_ckv[l],
                                    cache_krope[l], cache_mem_k[l].reshape(Bs, M, D),
                                    cache_mem_v[l].reshape(Bs, M, D), weights[l], p)
        for acc, val in zip(outs_s, (ckv, kr, hl, cs)):
            acc.append(val)
    y_sample = x

    return (y_prompt, y_sample) + tuple(jnp.stack(a) for a in outs_p) + tuple(jnp.stack(a) for a in outs_s)
```

```python
import functools
import math

import jax
import jax.numpy as jnp
from jax import lax
from jax.experimental import pallas as pl
from jax.experimental.pallas import tpu as pltpu

F32 = jnp.float32
BF16 = jnp.bfloat16

EPS = 1e-6
CHUNK = 64
LRU_BLOCKS = 16
LRU_C = 8.0
CONV_W = 4
N_HEADS = 16
NOPE_DIM = 64
ROPE_DIM = 32
QK_DIM = NOPE_DIM + ROPE_DIM
ROPE_BASE = 10000.0
MEM_HEADS = 4
TOP_K = 2

LANES = 128
SUBLANES = 8
MXU_DIM = 256
VMEM_LIMIT = 56 * 1024 * 1024

HEAD_PAD = LANES
MOE_CHUNK = 160
SAFE_LOG2_RANGE = 60.0
FLASH_HEADS = 8
FLASH_LAG = 8
NEG = -1e30
LOG2E = 1.4426950408889634


def _resident(shape):
    nd = len(shape)
    return pl.BlockSpec(shape, lambda *_: (0,) * nd, pipeline_mode=pl.Buffered(1))


def _params(n_axes):
    return pltpu.CompilerParams(dimension_semantics=("arbitrary",) * n_axes,
                                vmem_limit_bytes=VMEM_LIMIT)


def _rms(x, gain):
    return x * lax.rsqrt(jnp.mean(x * x, axis=-1, keepdims=True) + EPS) * gain


def _dot(a, b):
    return jnp.dot(a, b, preferred_element_type=F32)


def _dot_nt(a, b):
    return lax.dot_general(a, b, (((1,), (1,)), ((), ())), preferred_element_type=F32)


def _token_tile(n, cap=512):
    t = min(n, cap)
    assert n % t == 0 and t % SUBLANES == 0
    return t


def _memkv_kernel(mem_ref, gin_ref, wk_ref, wv_ref, gk_ref, k_ref, v_ref):
    mn = _rms(mem_ref[0], gin_ref[...]).astype(BF16)
    k = _dot(mn, wk_ref[...])
    hd = gk_ref.shape[-1]
    for h in range(MEM_HEADS):
        sl = slice(h * hd, (h + 1) * hd)
        k_ref[0, :, sl] = _rms(k[:, sl], gk_ref[...])
    v_ref[0] = _dot(mn, wv_ref[...])


def _memkv(mem, g_in, w_k, w_v, g_k):
    B, M, D = mem.shape
    tok = pl.BlockSpec((1, M, D), lambda b: (b, 0, 0))
    return pl.pallas_call(
        _memkv_kernel,
        grid=(B,),
        in_specs=[tok, _resident((1, D)), _resident((D, D)), _resident((D, D)),
                  _resident((1, D // MEM_HEADS))],
        out_specs=[tok, tok],
        out_shape=[jax.ShapeDtypeStruct((B, M, D), F32)] * 2,
        compiler_params=_params(1),
        name="memkv",
    )(mem, g_in.reshape(1, D), w_k.astype(BF16), w_v.astype(BF16), g_k.reshape(1, -1))


def _inproj_kernel(x_ref, prev_ref, h0_ref, gmix_ref, win_ref, cw_ref, cb_ref, wa_ref, ba_ref,
                   wi_ref, bi_ref, sp_ref, gq_ref, gkv_ref, cos_ref, sin_ref,
                   grec_ref, gatt_ref, cq_ref, ckv_ref, kr_ref, ctail_ref, htail_ref,
                   xext_ref, hcar_ref, *, tm, d, q_lora, kv_lora):
    @pl.when(pl.program_id(1) == 0)
    def _():
        xext_ref[0:SUBLANES, :] = prev_ref[0]
        hcar_ref[...] = h0_ref[0]

    hn = _rms(x_ref[0], gmix_ref[...]).astype(BF16)

    def proj(c0, c1):
        return _dot(hn, win_ref[:, c0:c1])

    x_rec = proj(0, d)
    xext_ref[SUBLANES:SUBLANES + tm, :] = x_rec
    xc = cb_ref[...] + cw_ref[CONV_W - 1:CONV_W, :] * x_rec
    for j in range(CONV_W - 1):
        off = SUBLANES - (CONV_W - 1) + j
        xc = xc + cw_ref[j:j + 1, :] * xext_ref[off:off + tm, :]
    tail = x_rec[tm - SUBLANES:tm, :]
    ctail_ref[0] = tail
    xext_ref[0:SUBLANES, :] = tail

    xcb = xc.astype(BF16)
    nb = d // MXU_DIM
    gr = jnp.concatenate(
        [_dot(xcb[:, c * MXU_DIM:(c + 1) * MXU_DIM], wa_ref[c]) for c in range(nb)], axis=1)
    gi = jnp.concatenate(
        [_dot(xcb[:, c * MXU_DIM:(c + 1) * MXU_DIM], wi_ref[c]) for c in range(nb)], axis=1)
    r = jax.nn.sigmoid(gr + ba_ref[...])
    ig = jax.nn.sigmoid(gi + bi_ref[...])
    log_a = (-LRU_C) * r * sp_ref[...]
    a = jnp.exp(log_a)
    u = jnp.sqrt(1.0 - a * a) * (ig * xc)

    row = lax.broadcasted_iota(jnp.int32, (tm, 1), 0) & (SUBLANES - 1)
    s = 1
    while s < SUBLANES:
        keep = row >= s
        u = a * jnp.where(keep, pltpu.roll(u, s, 0), 0.0) + u
        a = a * jnp.where(keep, pltpu.roll(a, s, 0), 1.0)
        s *= 2
    h_prev = hcar_ref[SUBLANES - 1:SUBLANES, :]
    groups = []
    for g in range(tm // SUBLANES):
        rows = slice(g * SUBLANES, (g + 1) * SUBLANES)
        hg = a[rows] * h_prev + u[rows]
        groups.append(hg)
        h_prev = hg[SUBLANES - 1:SUBLANES, :]
    h = jnp.concatenate(groups, axis=0)
    htail = groups[-1]
    hcar_ref[...] = htail
    htail_ref[0] = htail

    x_gate = proj(d, 2 * d)
    g_rec = proj(2 * d, 3 * d)
    grec_ref[0] = jax.nn.sigmoid(g_rec) * (jax.nn.gelu(x_gate) * h)
    gatt_ref[0] = proj(3 * d, 4 * d)

    o = 4 * d
    cq_ref[0] = _rms(proj(o, o + q_lora), gq_ref[...]).astype(BF16)
    o += q_lora
    ckv_ref[0] = _rms(proj(o, o + kv_lora), gkv_ref[...])
    o += kv_lora
    kr = proj(o, o + LANES) * cos_ref[...] + proj(o + LANES, o + 2 * LANES) * sin_ref[...]
    kr_ref[0] = kr[:, :ROPE_DIM]


def _inproj(x, prev8, h0_8, w, cos_k, sin_k):
    B, L, D = x.shape
    tm = _token_tile(L, 256)
    q_lora, kv_lora = w["gq"].shape[-1], w["gkv"].shape[-1]
    ncol = w["w_in"].shape[-1]
    tok = lambda n: pl.BlockSpec((1, tm, n), lambda b, l: (b, l, 0))
    per_b = pl.BlockSpec((1, SUBLANES, D), lambda b, l: (b, 0, 0))
    tab = pl.BlockSpec((tm, LANES), lambda b, l: (l, 0))
    nb = D // MXU_DIM
    kern = functools.partial(_inproj_kernel, tm=tm, d=D, q_lora=q_lora, kv_lora=kv_lora)
    return pl.pallas_call(
        kern,
        grid=(B, L // tm),
        in_specs=[tok(D), per_b, per_b, _resident((1, D)), _resident((D, ncol)),
                  _resident((CONV_W, D)), _resident((1, D)),
                  _resident((nb, MXU_DIM, MXU_DIM)), _resident((1, D)),
                  _resident((nb, MXU_DIM, MXU_DIM)), _resident((1, D)), _resident((1, D)),
                  _resident((1, q_lora)), _resident((1, kv_lora)), tab, tab],
        out_specs=[tok(D), tok(D), tok(q_lora), tok(kv_lora), tok(ROPE_DIM), per_b, per_b],
        out_shape=[jax.ShapeDtypeStruct((B, L, D), F32), jax.ShapeDtypeStruct((B, L, D), F32),
                   jax.ShapeDtypeStruct((B, L, q_lora), BF16),
                   jax.ShapeDtypeStruct((B, L, kv_lora), F32),
                   jax.ShapeDtypeStruct((B, L, ROPE_DIM), F32),
                   jax.ShapeDtypeStruct((B, SUBLANES, D), F32),
                   jax.ShapeDtypeStruct((B, SUBLANES, D), F32)],
        scratch_shapes=[pltpu.VMEM((SUBLANES + tm, D), F32), pltpu.VMEM((SUBLANES, D), F32)],
        compiler_params=_params(2),
        name="inproj",
    )(x, prev8, h0_8, w["gmix"], w["w_in"], w["conv_w"], w["conv_b"], w["wa"], w["ba"],
      w["wi"], w["bi"], w["sp"], w["gq"], w["gkv"], cos_k, sin_k)


def _qprep_kernel(cq_ref, wt_ref, cos_ref, sin_ref, g_ref, q_ref):
    qt = _dot_nt(wt_ref[...], cq_ref[0])
    half = ROPE_DIM // 2
    tm = qt.shape[1]
    for h in range(N_HEADS):
        blk = qt[h * HEAD_PAD:(h + 1) * HEAD_PAD, :]
        x1 = blk[NOPE_DIM:NOPE_DIM + half, :]
        x2 = blk[NOPE_DIM + half:QK_DIM, :]
        rot = jnp.concatenate([jnp.zeros((NOPE_DIM, tm), F32), -x2, x1,
                               jnp.zeros((HEAD_PAD - QK_DIM, tm), F32)], axis=0)
        qh = blk * cos_ref[...] + rot * sin_ref[...]
        ss = jnp.sum(qh * qh, axis=0, keepdims=True) * (1.0 / QK_DIM)
        q_ref[0, h * HEAD_PAD:(h + 1) * HEAD_PAD, :] = (qh * lax.rsqrt(ss + EPS) * g_ref[...]).astype(BF16)


def _qprep(cq, w, cos_q, sin_q):
    B, L, QL = cq.shape
    tm = _token_tile(L)
    n = N_HEADS * HEAD_PAD
    tab = pl.BlockSpec((HEAD_PAD, tm), lambda b, l: (0, l))
    return pl.pallas_call(
        _qprep_kernel,
        grid=(B, L // tm),
        in_specs=[pl.BlockSpec((1, tm, QL), lambda b, l: (b, l, 0)),
                  _resident((n, QL)), tab, tab, _resident((HEAD_PAD, 1))],
        out_specs=pl.BlockSpec((1, n, tm), lambda b, l: (b, 0, l)),
        out_shape=jax.ShapeDtypeStruct((B, n, L), BF16),
        compiler_params=_params(2),
        name="qprep",
    )(cq, w["wq_t"], cos_q, sin_q, w["gqh"])


def _kvprep_kernel(ckv_ref, kr_ref, wk_ref, place_ref, wvt_ref, g_ref, k_ref, vt_ref):
    c = ckv_ref[0].astype(BF16)
    k = _dot(c, wk_ref[...]) + _dot(kr_ref[0].astype(BF16), place_ref[...])
    for h in range(N_HEADS):
        sl = slice(h * HEAD_PAD, (h + 1) * HEAD_PAD)
        kh = k[:, sl]
        ss = jnp.sum(kh * kh, axis=-1, keepdims=True) * (1.0 / QK_DIM)
        k_ref[0, :, sl] = (kh * lax.rsqrt(ss + EPS) * g_ref[...]).astype(BF16)
    vt_ref[0, 0] = _dot_nt(wvt_ref[...], c).astype(BF16)


def _kvprep(ckv, kr, w, tk):
    B, T, KV = ckv.shape
    n = N_HEADS * HEAD_PAD
    dv = w["wv_t"].shape[0]
    return pl.pallas_call(
        _kvprep_kernel,
        grid=(B, T // tk),
        in_specs=[pl.BlockSpec((1, tk, KV), lambda b, t: (b, t, 0)),
                  pl.BlockSpec((1, tk, ROPE_DIM), lambda b, t: (b, t, 0)),
                  _resident((KV, n)), _resident((ROPE_DIM, n)), _resident((dv, KV)),
                  _resident((1, HEAD_PAD))],
        out_specs=[pl.BlockSpec((1, tk, n), lambda b, t: (b, t, 0)),
                   pl.BlockSpec((1, 1, dv, tk), lambda b, t: (b, t, 0, 0))],
        out_shape=[jax.ShapeDtypeStruct((B, T, n), BF16),
                   jax.ShapeDtypeStruct((B, T // tk, dv, tk), BF16)],
        compiler_params=_params(2),
        name="kvprep",
    )(ckv, kr, w["wk"], w["place"], w["wv_t"], w["gkh"])


def _flash_kernel(q_ref, k_ref, vt_ref, o_ref, *, tq, tk, q_off, t_valid, v_dim, heads, bounded, lag):
    q_start = q_off + pl.program_id(2) * tq
    n_full = jnp.minimum(q_start, t_valid) // tk
    n_end = (jnp.minimum(q_start + tq, t_valid) + tk - 1) // tk
    shift = int(math.log2(CHUNK))
    q_chunk = lax.shift_right_logical(q_start + lax.broadcasted_iota(jnp.int32, (1, tq), 1), shift)
    k_iota = lax.broadcasted_iota(jnp.int32, (tk, 1), 0)

    def step(tiles, carry, masked):
        def scores(j, hh):
            qh = q_ref[0, hh * HEAD_PAD:(hh + 1) * HEAD_PAD, :]
            kj = k_ref[0, pl.ds(pl.multiple_of(j * tk, tk), tk), hh * HEAD_PAD:(hh + 1) * HEAD_PAD]
            return _dot(kj, qh)

        def update(j, hh, state, s):
            m, l, acc = state
            if masked:
                k_pos = j * tk + k_iota
                vis = (lax.shift_right_logical(k_pos, shift) <= q_chunk) & (k_pos < t_valid)
                s = jnp.where(vis, s, NEG)
            if bounded:
                p = jnp.exp2(s)
                l = l + jnp.sum(p, axis=0, keepdims=True)
            else:
                m_new = jnp.maximum(m, jnp.max(s, axis=0, keepdims=True))
                alpha = jnp.exp2(m - m_new)
                p = jnp.exp2(s - m_new)
                l = alpha * l + jnp.sum(p, axis=0, keepdims=True)
                m, acc = m_new, alpha * acc
            vj = vt_ref[0, j, hh * v_dim:(hh + 1) * v_dim, :]
            return m, l, acc + _dot(vj, p.astype(BF16))

        items = [(j, hh) for j in tiles for hh in range(heads)]
        state, pending = list(carry), {}
        for t in range(len(items) + lag):
            if t < len(items):
                pending[t] = scores(*items[t])
            if t >= lag:
                j, hh = items[t - lag]
                state[hh] = update(j, hh, state[hh], pending.pop(t - lag))
        return tuple(state)

    init = tuple((jnp.full((1, tq), NEG, F32), jnp.zeros((1, tq), F32), jnp.zeros((v_dim, tq), F32))
                 for _ in range(heads))
    carry = lax.fori_loop(0, n_full, lambda j, c: step([j], c, False), init)
    carry = lax.fori_loop(n_full, n_end, lambda j, c: step([j], c, True), carry)
    o_ref[0] = jnp.concatenate([acc / l for _, l, acc in carry], axis=0).T


def _flash(q, k, vt, *, q_off, t_valid, tk, bounded, heads=FLASH_HEADS):
    B, n, L = q.shape
    T = k.shape[1]
    dv = vt.shape[2]
    v_dim = dv // N_HEADS
    tq = _token_tile(L)
    assert tq % CHUNK == 0 and q_off % CHUNK == 0 and (heads * v_dim) % LANES == 0
    kern = functools.partial(_flash_kernel, tq=tq, tk=tk, q_off=q_off, t_valid=t_valid, v_dim=v_dim,
                             heads=heads, bounded=bounded, lag=min(FLASH_LAG, heads))
    return pl.pallas_call(
        kern,
        grid=(B, N_HEADS // heads, L // tq),
        in_specs=[pl.BlockSpec((1, heads * HEAD_PAD, tq), lambda b, p, i: (b, p, i)),
                  pl.BlockSpec((1, T, heads * HEAD_PAD), lambda b, p, i: (b, 0, p)),
                  pl.BlockSpec((1, T // tk, heads * v_dim, tk), lambda b, p, i: (b, 0, p, 0))],
        out_specs=pl.BlockSpec((1, tq, heads * v_dim), lambda b, p, i: (b, i, p)),
        out_shape=jax.ShapeDtypeStruct((B, L, dv), F32),
        compiler_params=_params(3),
        name="flash",
    )(q, k, vt)


def _post_kernel(x_ref, grec_ref, gatt_ref, att_ref, wo_ref, gmem_ref, wmq_ref, gqh_ref,
                 mk_ref, mv_ref, wmo_ref, o_ref):
    mixed = (grec_ref[0] + jax.nn.sigmoid(gatt_ref[0]) * att_ref[0]).astype(BF16)
    x1 = x_ref[0] + _dot(mixed, wo_ref[...])
    qm = _dot(_rms(x1, gmem_ref[...]).astype(BF16), wmq_ref[...])
    hd = gqh_ref.shape[-1]
    outs = []
    for h in range(MEM_HEADS):
        sl = slice(h * hd, (h + 1) * hd)
        qh = (_rms(qm[:, sl], gqh_ref[...]) * (hd ** -0.5)).astype(BF16)
        s = _dot_nt(qh, mk_ref[0, :, sl])
        p = jnp.exp(s - jnp.max(s, axis=-1, keepdims=True))
        l = jnp.sum(p, axis=-1, keepdims=True)
        outs.append((_dot(p.astype(BF16), mv_ref[0, :, sl]) / l).astype(BF16))
    o_ref[0] = x1 + _dot(jnp.concatenate(outs, axis=1), wmo_ref[...])


def _post(x, grec, gatt, att, mem_k, mem_v, w):
    B, L, D = x.shape
    M = mem_k.shape[1]
    tm = _token_tile(L)
    tok = pl.BlockSpec((1, tm, D), lambda b, l: (b, l, 0))
    mem = pl.BlockSpec((1, M, D), lambda b, l: (b, 0, 0))
    return pl.pallas_call(
        _post_kernel,
        grid=(B, L // tm),
        in_specs=[tok, tok, tok, tok, _resident((D, D)), _resident((1, D)), _resident((D, D)),
                  _resident((1, D // MEM_HEADS)), mem, mem, _resident((D, D))],
        out_specs=tok,
        out_shape=jax.ShapeDtypeStruct((B, L, D), F32),
        compiler_params=_params(2),
        name="post",
    )(x, grec, gatt, att, w["w_out"], w["gmem"], w["w_mq"], w["gmqh"],
      mem_k.astype(BF16), mem_v.astype(BF16), w["w_mo"])


def _swiglu_mid(a, b):
    return (a * jax.nn.sigmoid(a) * b).astype(BF16)


def _ffn_kernel(x_ref, g_ref, w1_ref, w3_ref, w2_ref, o_ref, *, n_split):
    x = x_ref[...]
    hf = _rms(x, g_ref[...]).astype(BF16)
    fc = w1_ref.shape[1] // n_split
    acc = x
    for c in range(n_split):
        sl = slice(c * fc, (c + 1) * fc)
        acc = acc + _dot(_swiglu_mid(_dot(hf, w1_ref[:, sl]), _dot(hf, w3_ref[:, sl])), w2_ref[sl, :])
    o_ref[...] = acc


def _ffn(x, g, w1, w3, w2):
    N, D = x.shape
    F = w1.shape[1]
    tm = _token_tile(N)
    n_split = 2
    assert F % (n_split * LANES) == 0
    tok = pl.BlockSpec((tm, D), lambda i: (i, 0))
    return pl.pallas_call(
        functools.partial(_ffn_kernel, n_split=n_split),
        grid=(N // tm,),
        in_specs=[tok, _resident((1, D)), _resident((D, F)), _resident((D, F)), _resident((F, D))],
        out_specs=tok,
        out_shape=jax.ShapeDtypeStruct((N, D), F32),
        compiler_params=_params(1),
        name="ffn",
    )(x, g, w1, w3, w2)


def _moe_kernel(x_ref, g_ref, wrh_ref, wrl_ref, br_ref, w1_ref, w3_ref, w2_ref, o_ref,
                hf_ref, gate_ref, rank_ref, *, ts, cap, n_exp):
    e = pl.program_id(1)
    lane = lax.broadcasted_iota(jnp.int32, (1, LANES), 1)
    n_sub = hf_ref.shape[0] // ts

    @pl.when(e == 0)
    def _():
        x = x_ref[...]
        hf = _rms(x, g_ref[...])
        hi = hf.astype(BF16)
        lo = (hf - hi.astype(F32)).astype(BF16)
        logits = (_dot(hi, wrh_ref[...]) + _dot(lo, wrh_ref[...]) + _dot(hi, wrl_ref[...])
                  + br_ref[...])
        m1 = jnp.max(logits, axis=-1, keepdims=True)
        i1 = jnp.min(jnp.where(logits == m1, lane, LANES), axis=-1, keepdims=True)
        rest = jnp.where(lane == i1, NEG, logits)
        m2 = jnp.max(rest, axis=-1, keepdims=True)
        i2 = jnp.min(jnp.where(rest == m2, lane, LANES), axis=-1, keepdims=True)
        e2 = jnp.exp(m2 - m1)
        den = 1.0 + e2
        gate_ref[...] = jnp.where(lane == i1, 1.0 / den, 0.0) + jnp.where(lane == i2, e2 / den, 0.0)
        hf_ref[...] = hi
        o_ref[...] = x
        sel = jnp.where((lane == i1) | (lane == i2), 1.0, 0.0)
        before = (lax.broadcasted_iota(jnp.int32, (ts, ts), 0)
                  < lax.broadcasted_iota(jnp.int32, (ts, ts), 1))
        tri = jnp.where(before, 1.0, 0.0).astype(BF16)
        for s in range(n_sub):
            sel_t = sel[s * ts:(s + 1) * ts, :].T[:n_exp, :]
            rank = _dot(sel_t.astype(BF16), tri)
            rank_ref[s] = jnp.where(sel_t > 0.0, rank, -1.0)

    row = lax.broadcasted_iota(jnp.int32, (cap, 1), 0)
    for s in range(n_sub):
        rows = slice(s * ts, (s + 1) * ts)
        rank_e = rank_ref[s, pl.ds(e, 1), :]
        n_tok = jnp.max(rank_e).astype(jnp.int32) + 1
        ge = jnp.sum(jnp.where(lane == e, gate_ref[rows, :], 0.0), axis=-1, keepdims=True)

        def chunk(c, carry):
            want = (row + c * cap).astype(F32)
            onehot = jnp.where(rank_e == want, 1.0, 0.0).astype(BF16)
            xg = _dot(onehot, hf_ref[rows, :]).astype(BF16)
            y = _dot(_swiglu_mid(_dot(xg, w1_ref[0]), _dot(xg, w3_ref[0])), w2_ref[0])
            back = lax.dot_general(onehot, y.astype(BF16), (((0,), (0,)), ((), ())),
                                   preferred_element_type=F32)
            o_ref[rows, :] += ge * back
            return carry

        lax.fori_loop(0, (n_tok + cap - 1) // cap, chunk, 0)


def _moe(x, g, wr_hi, wr_lo, br, w1, w3, w2):
    N, D = x.shape
    E, _, F = w1.shape
    tm = _token_tile(N, 1024)
    ts = _token_tile(tm, 512)
    assert E <= SUBLANES
    tok = pl.BlockSpec((tm, D), lambda i, e: (i, 0))
    kern = functools.partial(_moe_kernel, ts=ts, cap=MOE_CHUNK, n_exp=E)
    return pl.pallas_call(
        kern,
        grid=(N // tm, E),
        in_specs=[tok, _resident((1, D)), _resident((D, LANES)), _resident((D, LANES)),
                  _resident((1, LANES)),
                  pl.BlockSpec((1, D, F), lambda i, e: (e, 0, 0)),
                  pl.BlockSpec((1, D, F), lambda i, e: (e, 0, 0)),
                  pl.BlockSpec((1, F, D), lambda i, e: (e, 0, 0))],
        out_specs=tok,
        out_shape=jax.ShapeDtypeStruct((N, D), F32),
        scratch_shapes=[pltpu.VMEM((tm, D), BF16), pltpu.VMEM((tm, LANES), F32),
                        pltpu.VMEM((tm // ts, E, ts), F32)],
        compiler_params=_params(2),
        name="moe",
    )(x, g, wr_hi, wr_lo, br, w1, w3, w2)


def _rot_half_cols(w):
    half = ROPE_DIM // 2
    return jnp.concatenate([-w[..., half:], w[..., :half]], axis=-1)


def _pad_cols(w, n):
    return jnp.pad(w, [(0, 0)] * (w.ndim - 1) + [(0, n - w.shape[-1])])


def _head_pad(nope, rope):
    z = jnp.zeros(nope.shape[:-1] + (HEAD_PAD - QK_DIM,), nope.dtype)
    out = jnp.concatenate([nope, rope, z], axis=-1)
    return out.reshape(out.shape[:-2] + (N_HEADS * HEAD_PAD,))


def _block_diag_tiles(w):
    nblk, bw, _ = w.shape
    per = MXU_DIM // bw
    w4 = w.reshape(nblk // per, per, bw, bw)
    eye = jnp.eye(per, dtype=w.dtype)
    return jnp.einsum("cpij,pq->cpiqj", w4, eye).reshape(nblk // per, MXU_DIM, MXU_DIM)


def _layer_weights(l, p):
    D = p["w_in"].shape[1]
    q_lora = p["q_lat_norm"].shape[-1]
    kv_lora = p["kv_lat_norm"].shape[-1]
    w_in = p["w_in"][l]
    sp = (D, 2 * D, 2 * D + q_lora, 2 * D + q_lora + kv_lora, 2 * D + q_lora + kv_lora + ROPE_DIM,
          3 * D + q_lora + kv_lora + ROPE_DIM)
    x_rec, x_gate, c_q, c_kv, k_rot, g_rec, g_att = jnp.split(w_in, sp, axis=-1)
    w_in_perm = jnp.concatenate(
        [x_rec, x_gate, g_rec, g_att, c_q, c_kv, _pad_cols(k_rot, LANES),
         _pad_cols(_rot_half_cols(k_rot), LANES)], axis=-1).astype(BF16)

    wq = p["w_uq"][l].reshape(q_lora, N_HEADS, QK_DIM)
    wq_n, wq_r = wq[..., :NOPE_DIM], wq[..., NOPE_DIM:]
    wk = p["w_uk"][l].reshape(kv_lora, N_HEADS, NOPE_DIM)
    eye_r = jnp.broadcast_to(jnp.eye(ROPE_DIM, dtype=F32)[:, None, :], (ROPE_DIM, N_HEADS, ROPE_DIM))
    gq, gk = p["q_head_norm"][l], p["k_head_norm"][l]
    q_scale = (QK_DIM ** -0.5) * LOG2E
    return dict(
        gmix=p["norm_mix"][l].reshape(1, D), w_in=w_in_perm,
        conv_w=p["conv_w"][l], conv_b=p["conv_b"][l].reshape(1, D),
        wa=_block_diag_tiles(p["lru_wa"][l]).astype(BF16), ba=p["lru_ba"][l].reshape(1, D),
        wi=_block_diag_tiles(p["lru_wi"][l]).astype(BF16), bi=p["lru_bi"][l].reshape(1, D),
        sp=jax.nn.softplus(-p["lru_lambda"][l]).reshape(1, D),
        gq=p["q_lat_norm"][l].reshape(1, q_lora), gkv=p["kv_lat_norm"][l].reshape(1, kv_lora),
        wq_t=_head_pad(wq_n, wq_r).T.astype(BF16),
        gqh=(_pad_cols(gq, HEAD_PAD) * q_scale).reshape(HEAD_PAD, 1),
        wk=_head_pad(wk, jnp.zeros((kv_lora, N_HEADS, ROPE_DIM), F32)).astype(BF16),
        place=_head_pad(jnp.zeros((ROPE_DIM, N_HEADS, NOPE_DIM), F32), eye_r).astype(BF16),
        wv_t=p["w_uv"][l].T.astype(BF16),
        gkh=_pad_cols(gk, HEAD_PAD).reshape(1, HEAD_PAD),
        score_bound=QK_DIM * jnp.max(jnp.abs(gq)) * jnp.max(jnp.abs(gk)) * q_scale,
        w_out=p["w_out"][l].astype(BF16), gmem=p["norm_mem"][l].reshape(1, D),
        w_mq=p["w_mq"][l].astype(BF16), gmqh=p["mq_head_norm"][l].reshape(1, -1),
        w_mo=p["w_mo"][l].astype(BF16), gffn=p["norm_ffn"][l].reshape(1, D),
    )


def _rope_tables(pos):
    half = ROPE_DIM // 2
    inv_freq = ROPE_BASE ** (-jnp.arange(half, dtype=F32) / half)
    ang = pos.astype(F32)[:, None] * inv_freq[None, :]
    cos2 = jnp.concatenate([jnp.cos(ang), jnp.cos(ang)], axis=-1)
    sin2 = jnp.concatenate([jnp.sin(ang), jnp.sin(ang)], axis=-1)
    n = pos.shape[0]
    ones, zeros = jnp.ones((n, NOPE_DIM), F32), jnp.zeros((n, NOPE_DIM), F32)
    return dict(cos_k=_pad_cols(cos2, LANES), sin_k=_pad_cols(sin2, LANES),
                cos_q=_pad_cols(jnp.concatenate([ones, cos2], axis=-1), HEAD_PAD).T,
                sin_q=_pad_cols(jnp.concatenate([zeros, sin2], axis=-1), HEAD_PAD).T)


def _layer(l, x, tabs, q_off, conv_prev, h0, ckv_past, kr_past, mem_k, mem_v, w, p):
    B, L, D = x.shape
    prev8 = jnp.pad(conv_prev, ((0, 0), (SUBLANES - (CONV_W - 1), 0), (0, 0)))
    h0_8 = jnp.pad(h0[:, None, :], ((0, 0), (SUBLANES - 1, 0), (0, 0)))
    grec, gatt, cq, ckv, kr, ctail, htail = _inproj(x, prev8, h0_8, w, tabs["cos_k"], tabs["sin_k"])

    q = _qprep(cq, w, tabs["cos_q"], tabs["sin_q"])
    if ckv_past is None:
        ckv_all, kr_all = ckv, kr
    else:
        ckv_all = jnp.concatenate([ckv_past, ckv], axis=1)
        kr_all = jnp.concatenate([kr_past, kr], axis=1)
    t_valid = ckv_all.shape[1]
    tk = min(512, t_valid)
    t_pad = -(-t_valid // tk) * tk
    if t_pad != t_valid:
        ckv_all = jnp.pad(ckv_all, ((0, 0), (0, t_pad - t_valid), (0, 0)))
        kr_all = jnp.pad(kr_all, ((0, 0), (0, t_pad - t_valid), (0, 0)))
    k, vt = _kvprep(ckv_all, kr_all, w, tk)
    flash = functools.partial(_flash, q_off=q_off, t_valid=t_valid, tk=tk)
    att = lax.cond(w["score_bound"] <= SAFE_LOG2_RANGE, functools.partial(flash, bounded=True),
                   functools.partial(flash, bounded=False), q, k, vt)

    x2 = _post(x, grec, gatt, att, mem_k, mem_v, w).reshape(B * L, D)
    j = l // 2
    if l % 2 == 0:
        x3 = _ffn(x2, w["gffn"], p["ffn_w1"][j].astype(BF16), p["ffn_w3"][j].astype(BF16),
                  p["ffn_w2"][j].astype(BF16))
    else:
        E = p["moe_router"].shape[-1]
        wr = _pad_cols(p["moe_router"][j], LANES)
        wr_hi = wr.astype(BF16)
        wr_lo = (wr - wr_hi.astype(F32)).astype(BF16)
        br = jnp.concatenate([p["moe_router_b"][j], jnp.full((LANES - E,), NEG, F32)]).reshape(1, LANES)
        x3 = _moe(x2, w["gffn"], wr_hi, wr_lo, br, p["moe_w1"][j].astype(BF16),
                  p["moe_w3"][j].astype(BF16), p["moe_w2"][j].astype(BF16))
    conv_state = ctail[:, SUBLANES - (CONV_W - 1):, :]
    return x3.reshape(B, L, D), conv_state, htail[:, SUBLANES - 1, :], ckv, kr


def kernel(x_prompt, x_sample, cache_ckv, cache_krope, cache_mem_k, cache_mem_v, state_lru, state_conv, mem_prompt, norm_mix, w_in, conv_w, conv_b, lru_wa, lru_ba, lru_wi, lru_bi, lru_lambda, q_lat_norm, w_uq, kv_lat_norm, w_uk, w_uv, q_head_norm, k_head_norm, w_out, norm_mem, mem_in_norm, w_mq, w_mk, w_mv, w_mo, mq_head_norm, mk_head_norm, norm_ffn, ffn_w1, ffn_w3, ffn_w2, moe_router, moe_router_b, moe_w1, moe_w3, moe_w2):
    p = dict(norm_mix=norm_mix, w_in=w_in, conv_w=conv_w, conv_b=conv_b, lru_wa=lru_wa, lru_ba=lru_ba,
             lru_wi=lru_wi, lru_bi=lru_bi, lru_lambda=lru_lambda, q_lat_norm=q_lat_norm, w_uq=w_uq,
             kv_lat_norm=kv_lat_norm, w_uk=w_uk, w_uv=w_uv, q_head_norm=q_head_norm,
             k_head_norm=k_head_norm, w_out=w_out, norm_mem=norm_mem, w_mq=w_mq, w_mo=w_mo,
             mq_head_norm=mq_head_norm, norm_ffn=norm_ffn, ffn_w1=ffn_w1, ffn_w3=ffn_w3,
             ffn_w2=ffn_w2, moe_router=moe_router, moe_router_b=moe_router_b, moe_w1=moe_w1,
             moe_w3=moe_w3, moe_w2=moe_w2)
    depth = w_in.shape[0]
    weights = [_layer_weights(l, p) for l in range(depth)]

    Bp, Lp, D = x_prompt.shape
    Bs, Ls, _ = x_sample.shape
    t_past = cache_ckv.shape[2]
    M = mem_prompt.shape[1]

    tabs_p = _rope_tables(jnp.arange(Lp, dtype=jnp.int32))
    zero_conv = jnp.zeros((Bp, CONV_W - 1, D), F32)
    zero_h = jnp.zeros((Bp, D), F32)
    x = x_prompt
    outs_p = [[] for _ in range(6)]
    for l in range(depth):
        mk, mv = _memkv(mem_prompt, mem_in_norm[l], w_mk[l], w_mv[l], mk_head_norm[l])
        x, cs, hl, ckv, kr = _layer(l, x, tabs_p, 0, zero_conv, zero_h, None, None, mk, mv, weights[l], p)
        for acc, val in zip(outs_p, (ckv, kr, hl, cs, mk.reshape(Bp, M, MEM_HEADS, -1),
                                     mv.reshape(Bp, M, MEM_HEADS, -1))):
            acc.append(val)
    y_prompt = x

    tabs_s = _rope_tables(t_past + jnp.arange(Ls, dtype=jnp.int32))
    x = x_sample
    outs_s = [[] for _ in range(4)]
    for l in range(depth):
        x, cs, hl, ckv, kr = _layer(l, x, tabs_s, t_past, state_conv[l], state_lru[l], cache_ckv[l],
                                    cache_krope[l], cache_mem_k[l].reshape(Bs, M, D),
                                    cache_mem_v[l].reshape(Bs, M, D), weights[l], p)
        for acc, val in zip(outs_s, (ckv, kr, hl, cs)):
            acc.append(val)
    y_sample = x

    return (y_prompt, y_sample) + tuple(jnp.stack(a) for a in outs_p) + tuple(jnp.stack(a) for a in outs_s)
```

```python
import functools
import math

import jax
import jax.numpy as jnp
from jax import lax
from jax.experimental import pallas as pl
from jax.experimental.pallas import tpu as pltpu

F32 = jnp.float32
BF16 = jnp.bfloat16

EPS = 1e-6
CHUNK = 64
LRU_BLOCKS = 16
LRU_C = 8.0
CONV_W = 4
N_HEADS = 16
NOPE_DIM = 64
ROPE_DIM = 32
QK_DIM = NOPE_DIM + ROPE_DIM
ROPE_BASE = 10000.0
MEM_HEADS = 4
TOP_K = 2

LANES = 128
SUBLANES = 8
MXU_DIM = 256
VMEM_LIMIT = 56 * 1024 * 1024

HEAD_PAD = LANES
MOE_CHUNK = 160
SAFE_LOG2_RANGE = 60.0
FLASH_HEADS = 8
FLASH_LAG = 8
NEG = -1e30
LOG2E = 1.4426950408889634


def _resident(shape):
    nd = len(shape)
    return pl.BlockSpec(shape, lambda *_: (0,) * nd, pipeline_mode=pl.Buffered(1))


def _params(n_axes):
    return pltpu.CompilerParams(dimension_semantics=("arbitrary",) * n_axes,
                                vmem_limit_bytes=VMEM_LIMIT)


def _rms(x, gain):
    return x * lax.rsqrt(jnp.mean(x * x, axis=-1, keepdims=True) + EPS) * gain


def _dot(a, b):
    return jnp.dot(a, b, preferred_element_type=F32)


def _dot_nt(a, b):
    return lax.dot_general(a, b, (((1,), (1,)), ((), ())), preferred_element_type=F32)


def _token_tile(n, cap=512):
    t = min(n, cap)
    assert n % t == 0 and t % SUBLANES == 0
    return t


def _memkv_kernel(mem_ref, gin_ref, wk_ref, wv_ref, gk_ref, k_ref, v_ref):
    mn = _rms(mem_ref[0], gin_ref[...]).astype(BF16)
    k = _dot(mn, wk_ref[...])
    hd = gk_ref.shape[-1]
    for h in range(MEM_HEADS):
        sl = slice(h * hd, (h + 1) * hd)
        k_ref[0, :, sl] = _rms(k[:, sl], gk_ref[...])
    v_ref[0] = _dot(mn, wv_ref[...])


def _memkv(mem, g_in, w_k, w_v, g_k):
    B, M, D = mem.shape
    tok = pl.BlockSpec((1, M, D), lambda b: (b, 0, 0))
    return pl.pallas_call(
        _memkv_kernel,
        grid=(B,),
        in_specs=[tok, _resident((1, D)), _resident((D, D)), _resident((D, D)),
                  _resident((1, D // MEM_HEADS))],
        out_specs=[tok, tok],
        out_shape=[jax.ShapeDtypeStruct((B, M, D), F32)] * 2,
        compiler_params=_params(1),
        name="memkv",
    )(mem, g_in.reshape(1, D), w_k.astype(BF16), w_v.astype(BF16), g_k.reshape(1, -1))


def _inproj_kernel(x_ref, prev_ref, h0_ref, gmix_ref, win_ref, cw_ref, cb_ref, wa_ref, ba_ref,
                   wi_ref, bi_ref, sp_ref, gq_ref, gkv_ref, cos_ref, sin_ref,
                   grec_ref, gatt_ref, cq_ref, ckv_ref, kr_ref, ctail_ref, htail_ref,
                   xext_ref, hcar_ref, *, tm, d, q_lora, kv_lora):
    @pl.when(pl.program_id(1) == 0)
    def _():
        xext_ref[0:SUBLANES, :] = prev_ref[0]
        hcar_ref[...] = h0_ref[0]

    hn = _rms(x_ref[0], gmix_ref[...]).astype(BF16)

    def proj(c0, c1):
        return _dot(hn, win_ref[:, c0:c1])

    x_rec = proj(0, d)
    xext_ref[SUBLANES:SUBLANES + tm, :] = x_rec
    xc = cb_ref[...] + cw_ref[CONV_W - 1:CONV_W, :] * x_rec
    for j in range(CONV_W - 1):
        off = SUBLANES - (CONV_W - 1) + j
        xc = xc + cw_ref[j:j + 1, :] * xext_ref[off:off + tm, :]
    tail = x_rec[tm - SUBLANES:tm, :]
    ctail_ref[0] = tail
    xext_ref[0:SUBLANES, :] = tail

    xcb = xc.astype(BF16)
    nb = d // MXU_DIM
    gr = jnp.concatenate(
        [_dot(xcb[:, c * MXU_DIM:(c + 1) * MXU_DIM], wa_ref[c]) for c in range(nb)], axis=1)
    gi = jnp.concatenate(
        [_dot(xcb[:, c * MXU_DIM:(c + 1) * MXU_DIM], wi_ref[c]) for c in range(nb)], axis=1)
    r = jax.nn.sigmoid(gr + ba_ref[...])
    ig = jax.nn.sigmoid(gi + bi_ref[...])
    log_a = (-LRU_C) * r * sp_ref[...]
    a = jnp.exp(log_a)
    u = jnp.sqrt(1.0 - a * a) * (ig * xc)

    row = lax.broadcasted_iota(jnp.int32, (tm, 1), 0) & (SUBLANES - 1)
    s = 1
    while s < SUBLANES:
        keep = row >= s
        u = a * jnp.where(keep, pltpu.roll(u, s, 0), 0.0) + u
        a = a * jnp.where(keep, pltpu.roll(a, s, 0), 1.0)
        s *= 2
    h_prev = hcar_ref[SUBLANES - 1:SUBLANES, :]
    groups = []
    for g in range(tm // SUBLANES):
        rows = slice(g * SUBLANES, (g + 1) * SUBLANES)
        hg = a[rows] * h_prev + u[rows]
        groups.append(hg)
        h_prev = hg[SUBLANES - 1:SUBLANES, :]
    h = jnp.concatenate(groups, axis=0)
    htail = groups[-1]
    hcar_ref[...] = htail
    htail_ref[0] = htail

    x_gate = proj(d, 2 * d)
    g_rec = proj(2 * d, 3 * d)
    grec_ref[0] = (jax.nn.sigmoid(g_rec) * (jax.nn.gelu(x_gate) * h)).astype(BF16)
    gatt_ref[0] = proj(3 * d, 4 * d).astype(BF16)

    o = 4 * d
    cq_ref[0] = _rms(proj(o, o + q_lora), gq_ref[...]).astype(BF16)
    o += q_lora
    ckv_ref[0] = _rms(proj(o, o + kv_lora), gkv_ref[...])
    o += kv_lora
    kr = proj(o, o + LANES) * cos_ref[...] + proj(o + LANES, o + 2 * LANES) * sin_ref[...]
    kr_ref[0] = kr[:, :ROPE_DIM]


def _inproj(x, prev8, h0_8, w, cos_k, sin_k):
    B, L, D = x.shape
    tm = _token_tile(L, 256)
    q_lora, kv_lora = w["gq"].shape[-1], w["gkv"].shape[-1]
    ncol = w["w_in"].shape[-1]
    tok = lambda n: pl.BlockSpec((1, tm, n), lambda b, l: (b, l, 0))
    per_b = pl.BlockSpec((1, SUBLANES, D), lambda b, l: (b, 0, 0))
    tab = pl.BlockSpec((tm, LANES), lambda b, l: (l, 0))
    nb = D // MXU_DIM
    kern = functools.partial(_inproj_kernel, tm=tm, d=D, q_lora=q_lora, kv_lora=kv_lora)
    return pl.pallas_call(
        kern,
        grid=(B, L // tm),
        in_specs=[tok(D), per_b, per_b, _resident((1, D)), _resident((D, ncol)),
                  _resident((CONV_W, D)), _resident((1, D)),
                  _resident((nb, MXU_DIM, MXU_DIM)), _resident((1, D)),
                  _resident((nb, MXU_DIM, MXU_DIM)), _resident((1, D)), _resident((1, D)),
                  _resident((1, q_lora)), _resident((1, kv_lora)), tab, tab],
        out_specs=[tok(D), tok(D), tok(q_lora), tok(kv_lora), tok(ROPE_DIM), per_b, per_b],
        out_shape=[jax.ShapeDtypeStruct((B, L, D), BF16), jax.ShapeDtypeStruct((B, L, D), BF16),
                   jax.ShapeDtypeStruct((B, L, q_lora), BF16),
                   jax.ShapeDtypeStruct((B, L, kv_lora), F32),
                   jax.ShapeDtypeStruct((B, L, ROPE_DIM), F32),
                   jax.ShapeDtypeStruct((B, SUBLANES, D), F32),
                   jax.ShapeDtypeStruct((B, SUBLANES, D), F32)],
        scratch_shapes=[pltpu.VMEM((SUBLANES + tm, D), F32), pltpu.VMEM((SUBLANES, D), F32)],
        compiler_params=_params(2),
        name="inproj",
    )(x, prev8, h0_8, w["gmix"], w["w_in"], w["conv_w"], w["conv_b"], w["wa"], w["ba"],
      w["wi"], w["bi"], w["sp"], w["gq"], w["gkv"], cos_k, sin_k)


def _qprep_kernel(cq_ref, wt_ref, cos_ref, sin_ref, g_ref, q_ref):
    qt = _dot_nt(wt_ref[...], cq_ref[0])
    half = ROPE_DIM // 2
    tm = qt.shape[1]
    for h in range(N_HEADS):
        blk = qt[h * HEAD_PAD:(h + 1) * HEAD_PAD, :]
        x1 = blk[NOPE_DIM:NOPE_DIM + half, :]
        x2 = blk[NOPE_DIM + half:QK_DIM, :]
        rot = jnp.concatenate([jnp.zeros((NOPE_DIM, tm), F32), -x2, x1,
                               jnp.zeros((HEAD_PAD - QK_DIM, tm), F32)], axis=0)
        qh = blk * cos_ref[...] + rot * sin_ref[...]
        ss = jnp.sum(qh * qh, axis=0, keepdims=True) * (1.0 / QK_DIM)
        q_ref[0, h * HEAD_PAD:(h + 1) * HEAD_PAD, :] = (qh * lax.rsqrt(ss + EPS) * g_ref[...]).astype(BF16)


def _qprep(cq, w, cos_q, sin_q):
    B, L, QL = cq.shape
    tm = _token_tile(L)
    n = N_HEADS * HEAD_PAD
    tab = pl.BlockSpec((HEAD_PAD, tm), lambda b, l: (0, l))
    return pl.pallas_call(
        _qprep_kernel,
        grid=(B, L // tm),
        in_specs=[pl.BlockSpec((1, tm, QL), lambda b, l: (b, l, 0)),
                  _resident((n, QL)), tab, tab, _resident((HEAD_PAD, 1))],
        out_specs=pl.BlockSpec((1, n, tm), lambda b, l: (b, 0, l)),
        out_shape=jax.ShapeDtypeStruct((B, n, L), BF16),
        compiler_params=_params(2),
        name="qprep",
    )(cq, w["wq_t"], cos_q, sin_q, w["gqh"])


def _kvprep_kernel(ckv_ref, kr_ref, wk_ref, place_ref, wvt_ref, g_ref, k_ref, vt_ref):
    c = ckv_ref[0].astype(BF16)
    k = _dot(c, wk_ref[...]) + _dot(kr_ref[0].astype(BF16), place_ref[...])
    for h in range(N_HEADS):
        sl = slice(h * HEAD_PAD, (h + 1) * HEAD_PAD)
        kh = k[:, sl]
        ss = jnp.sum(kh * kh, axis=-1, keepdims=True) * (1.0 / QK_DIM)
        k_ref[0, :, sl] = (kh * lax.rsqrt(ss + EPS) * g_ref[...]).astype(BF16)
    vt_ref[0, 0] = _dot_nt(wvt_ref[...], c).astype(BF16)


def _kvprep(ckv, kr, w, tk):
    B, T, KV = ckv.shape
    n = N_HEADS * HEAD_PAD
    dv = w["wv_t"].shape[0]
    return pl.pallas_call(
        _kvprep_kernel,
        grid=(B, T // tk),
        in_specs=[pl.BlockSpec((1, tk, KV), lambda b, t: (b, t, 0)),
                  pl.BlockSpec((1, tk, ROPE_DIM), lambda b, t: (b, t, 0)),
                  _resident((KV, n)), _resident((ROPE_DIM, n)), _resident((dv, KV)),
                  _resident((1, HEAD_PAD))],
        out_specs=[pl.BlockSpec((1, tk, n), lambda b, t: (b, t, 0)),
                   pl.BlockSpec((1, 1, dv, tk), lambda b, t: (b, t, 0, 0))],
        out_shape=[jax.ShapeDtypeStruct((B, T, n), BF16),
                   jax.ShapeDtypeStruct((B, T // tk, dv, tk), BF16)],
        compiler_params=_params(2),
        name="kvprep",
    )(ckv, kr, w["wk"], w["place"], w["wv_t"], w["gkh"])


def _flash_kernel(q_ref, k_ref, vt_ref, o_ref, *, tq, tk, q_off, t_valid, v_dim, heads, bounded, lag):
    q_start = q_off + pl.program_id(2) * tq
    n_full = jnp.minimum(q_start, t_valid) // tk
    n_end = (jnp.minimum(q_start + tq, t_valid) + tk - 1) // tk
    shift = int(math.log2(CHUNK))
    q_chunk = lax.shift_right_logical(q_start + lax.broadcasted_iota(jnp.int32, (1, tq), 1), shift)
    k_iota = lax.broadcasted_iota(jnp.int32, (tk, 1), 0)

    def step(tiles, carry, masked):
        def scores(j, hh):
            qh = q_ref[0, hh * HEAD_PAD:(hh + 1) * HEAD_PAD, :]
            kj = k_ref[0, pl.ds(pl.multiple_of(j * tk, tk), tk), hh * HEAD_PAD:(hh + 1) * HEAD_PAD]
            return _dot(kj, qh)

        def update(j, hh, state, s):
            m, l, acc = state
            if masked:
                k_pos = j * tk + k_iota
                vis = (lax.shift_right_logical(k_pos, shift) <= q_chunk) & (k_pos < t_valid)
                s = jnp.where(vis, s, NEG)
            if bounded:
                p = jnp.exp2(s)
                l = l + jnp.sum(p, axis=0, keepdims=True)
            else:
                m_new = jnp.maximum(m, jnp.max(s, axis=0, keepdims=True))
                alpha = jnp.exp2(m - m_new)
                p = jnp.exp2(s - m_new)
                l = alpha * l + jnp.sum(p, axis=0, keepdims=True)
                m, acc = m_new, alpha * acc
            vj = vt_ref[0, j, hh * v_dim:(hh + 1) * v_dim, :]
            return m, l, acc + _dot(vj, p.astype(BF16))

        items = [(j, hh) for j in tiles for hh in range(heads)]
        state, pending = list(carry), {}
        for t in range(len(items) + lag):
            if t < len(items):
                pending[t] = scores(*items[t])
            if t >= lag:
                j, hh = items[t - lag]
                state[hh] = update(j, hh, state[hh], pending.pop(t - lag))
        return tuple(state)

    init = tuple((jnp.full((1, tq), NEG, F32), jnp.zeros((1, tq), F32), jnp.zeros((v_dim, tq), F32))
                 for _ in range(heads))
    carry = lax.fori_loop(0, n_full, lambda j, c: step([j], c, False), init)
    carry = lax.fori_loop(n_full, n_end, lambda j, c: step([j], c, True), carry)
    o_ref[0] = jnp.concatenate([acc / l for _, l, acc in carry], axis=0).T.astype(BF16)


def _flash(q, k, vt, *, q_off, t_valid, tk, bounded, heads=FLASH_HEADS):
    B, n, L = q.shape
    T = k.shape[1]
    dv = vt.shape[2]
    v_dim = dv // N_HEADS
    tq = _token_tile(L)
    assert tq % CHUNK == 0 and q_off % CHUNK == 0 and (heads * v_dim) % LANES == 0
    kern = functools.partial(_flash_kernel, tq=tq, tk=tk, q_off=q_off, t_valid=t_valid, v_dim=v_dim,
                             heads=heads, bounded=bounded, lag=min(FLASH_LAG, heads))
    return pl.pallas_call(
        kern,
        grid=(B, N_HEADS // heads, L // tq),
        in_specs=[pl.BlockSpec((1, heads * HEAD_PAD, tq), lambda b, p, i: (b, p, i)),
                  pl.BlockSpec((1, T, heads * HEAD_PAD), lambda b, p, i: (b, 0, p)),
                  pl.BlockSpec((1, T // tk, heads * v_dim, tk), lambda b, p, i: (b, 0, p, 0))],
        out_specs=pl.BlockSpec((1, tq, heads * v_dim), lambda b, p, i: (b, i, p)),
        out_shape=jax.ShapeDtypeStruct((B, L, dv), BF16),
        compiler_params=_params(3),
        name="flash",
    )(q, k, vt)


def _post_kernel(x_ref, grec_ref, gatt_ref, att_ref, wo_ref, gmem_ref, wmq_ref, gqh_ref,
                 mk_ref, mv_ref, wmo_ref, o_ref):
    mixed = (grec_ref[0].astype(F32)
             + jax.nn.sigmoid(gatt_ref[0].astype(F32)) * att_ref[0].astype(F32)).astype(BF16)
    x1 = x_ref[0] + _dot(mixed, wo_ref[...])
    qm = _dot(_rms(x1, gmem_ref[...]).astype(BF16), wmq_ref[...])
    hd = gqh_ref.shape[-1]
    outs = []
    for h in range(MEM_HEADS):
        sl = slice(h * hd, (h + 1) * hd)
        qh = (_rms(qm[:, sl], gqh_ref[...]) * (hd ** -0.5)).astype(BF16)
        s = _dot_nt(qh, mk_ref[0, :, sl])
        p = jnp.exp(s - jnp.max(s, axis=-1, keepdims=True))
        l = jnp.sum(p, axis=-1, keepdims=True)
        outs.append((_dot(p.astype(BF16), mv_ref[0, :, sl]) / l).astype(BF16))
    o_ref[0] = x1 + _dot(jnp.concatenate(outs, axis=1), wmo_ref[...])


def _post(x, grec, gatt, att, mem_k, mem_v, w):
    B, L, D = x.shape
    M = mem_k.shape[1]
    tm = _token_tile(L)
    tok = pl.BlockSpec((1, tm, D), lambda b, l: (b, l, 0))
    mem = pl.BlockSpec((1, M, D), lambda b, l: (b, 0, 0))
    return pl.pallas_call(
        _post_kernel,
        grid=(B, L // tm),
        in_specs=[tok, tok, tok, tok, _resident((D, D)), _resident((1, D)), _resident((D, D)),
                  _resident((1, D // MEM_HEADS)), mem, mem, _resident((D, D))],
        out_specs=tok,
        out_shape=jax.ShapeDtypeStruct((B, L, D), F32),
        compiler_params=_params(2),
        name="post",
    )(x, grec, gatt, att, w["w_out"], w["gmem"], w["w_mq"], w["gmqh"],
      mem_k.astype(BF16), mem_v.astype(BF16), w["w_mo"])


def _swiglu_mid(a, b):
    return (a * jax.nn.sigmoid(a) * b).astype(BF16)


def _ffn_kernel(x_ref, g_ref, w1_ref, w3_ref, w2_ref, o_ref, *, n_split):
    x = x_ref[...]
    hf = _rms(x, g_ref[...]).astype(BF16)
    fc = w1_ref.shape[1] // n_split
    acc = x
    for c in range(n_split):
        sl = slice(c * fc, (c + 1) * fc)
        acc = acc + _dot(_swiglu_mid(_dot(hf, w1_ref[:, sl]), _dot(hf, w3_ref[:, sl])), w2_ref[sl, :])
    o_ref[...] = acc


def _ffn(x, g, w1, w3, w2):
    N, D = x.shape
    F = w1.shape[1]
    tm = _token_tile(N)
    n_split = 2
    assert F % (n_split * LANES) == 0
    tok = pl.BlockSpec((tm, D), lambda i: (i, 0))
    return pl.pallas_call(
        functools.partial(_ffn_kernel, n_split=n_split),
        grid=(N // tm,),
        in_specs=[tok, _resident((1, D)), _resident((D, F)), _resident((D, F)), _resident((F, D))],
        out_specs=tok,
        out_shape=jax.ShapeDtypeStruct((N, D), F32),
        compiler_params=_params(1),
        name="ffn",
    )(x, g, w1, w3, w2)


def _moe_kernel(x_ref, g_ref, wrh_ref, wrl_ref, br_ref, w1_ref, w3_ref, w2_ref, o_ref,
                hf_ref, gate_ref, rank_ref, *, ts, cap, n_exp):
    e = pl.program_id(1)
    lane = lax.broadcasted_iota(jnp.int32, (1, LANES), 1)
    n_sub = hf_ref.shape[0] // ts

    @pl.when(e == 0)
    def _():
        x = x_ref[...]
        hf = _rms(x, g_ref[...])
        hi = hf.astype(BF16)
        lo = (hf - hi.astype(F32)).astype(BF16)
        logits = (_dot(hi, wrh_ref[...]) + _dot(lo, wrh_ref[...]) + _dot(hi, wrl_ref[...])
                  + br_ref[...])
        m1 = jnp.max(logits, axis=-1, keepdims=True)
        i1 = jnp.min(jnp.where(logits == m1, lane, LANES), axis=-1, keepdims=True)
        rest = jnp.where(lane == i1, NEG, logits)
        m2 = jnp.max(rest, axis=-1, keepdims=True)
        i2 = jnp.min(jnp.where(rest == m2, lane, LANES), axis=-1, keepdims=True)
        e2 = jnp.exp(m2 - m1)
        den = 1.0 + e2
        gate_ref[...] = jnp.where(lane == i1, 1.0 / den, 0.0) + jnp.where(lane == i2, e2 / den, 0.0)
        hf_ref[...] = hi
        o_ref[...] = x
        sel = jnp.where((lane == i1) | (lane == i2), 1.0, 0.0)
        before = (lax.broadcasted_iota(jnp.int32, (ts, ts), 0)
                  < lax.broadcasted_iota(jnp.int32, (ts, ts), 1))
        tri = jnp.where(before, 1.0, 0.0).astype(BF16)
        for s in range(n_sub):
            sel_t = sel[s * ts:(s + 1) * ts, :].T[:n_exp, :]
            rank = _dot(sel_t.astype(BF16), tri)
            rank_ref[s] = jnp.where(sel_t > 0.0, rank, -1.0)

    row = lax.broadcasted_iota(jnp.int32, (cap, 1), 0)
    for s in range(n_sub):
        rows = slice(s * ts, (s + 1) * ts)
        rank_e = rank_ref[s, pl.ds(e, 1), :]
        n_tok = jnp.max(rank_e).astype(jnp.int32) + 1
        ge = jnp.sum(jnp.where(lane == e, gate_ref[rows, :], 0.0), axis=-1, keepdims=True)

        def chunk(c, carry):
            want = (row + c * cap).astype(F32)
            onehot = jnp.where(rank_e == want, 1.0, 0.0).astype(BF16)
            xg = _dot(onehot, hf_ref[rows, :]).astype(BF16)
            y = _dot(_swiglu_mid(_dot(xg, w1_ref[0]), _dot(xg, w3_ref[0])), w2_ref[0])
            back = lax.dot_general(onehot, y.astype(BF16), (((0,), (0,)), ((), ())),
                                   preferred_element_type=F32)
            o_ref[rows, :] += ge * back
            return carry

        lax.fori_loop(0, (n_tok + cap - 1) // cap, chunk, 0)


def _moe(x, g, wr_hi, wr_lo, br, w1, w3, w2):
    N, D = x.shape
    E, _, F = w1.shape
    tm = _token_tile(N, 1024)
    ts = _token_tile(tm, 512)
    assert E <= SUBLANES
    tok = pl.BlockSpec((tm, D), lambda i, e: (i, 0))
    kern = functools.partial(_moe_kernel, ts=ts, cap=MOE_CHUNK, n_exp=E)
    return pl.pallas_call(
        kern,
        grid=(N // tm, E),
        in_specs=[tok, _resident((1, D)), _resident((D, LANES)), _resident((D, LANES)),
                  _resident((1, LANES)),
                  pl.BlockSpec((1, D, F), lambda i, e: (e, 0, 0)),
                  pl.BlockSpec((1, D, F), lambda i, e: (e, 0, 0)),
                  pl.BlockSpec((1, F, D), lambda i, e: (e, 0, 0))],
        out_specs=tok,
        out_shape=jax.ShapeDtypeStruct((N, D), F32),
        scratch_shapes=[pltpu.VMEM((tm, D), BF16), pltpu.VMEM((tm, LANES), F32),
                        pltpu.VMEM((tm // ts, E, ts), F32)],
        compiler_params=_params(2),
        name="moe",
    )(x, g, wr_hi, wr_lo, br, w1, w3, w2)


def _rot_half_cols(w):
    half = ROPE_DIM // 2
    return jnp.concatenate([-w[..., half:], w[..., :half]], axis=-1)


def _pad_cols(w, n):
    return jnp.pad(w, [(0, 0)] * (w.ndim - 1) + [(0, n - w.shape[-1])])


def _head_pad(nope, rope):
    z = jnp.zeros(nope.shape[:-1] + (HEAD_PAD - QK_DIM,), nope.dtype)
    out = jnp.concatenate([nope, rope, z], axis=-1)
    return out.reshape(out.shape[:-2] + (N_HEADS * HEAD_PAD,))


def _block_diag_tiles(w):
    nblk, bw, _ = w.shape
    per = MXU_DIM // bw
    w4 = w.reshape(nblk // per, per, bw, bw)
    eye = jnp.eye(per, dtype=w.dtype)
    return jnp.einsum("cpij,pq->cpiqj", w4, eye).reshape(nblk // per, MXU_DIM, MXU_DIM)


def _layer_weights(l, p):
    D = p["w_in"].shape[1]
    q_lora = p["q_lat_norm"].shape[-1]
    kv_lora = p["kv_lat_norm"].shape[-1]
    w_in = p["w_in"][l]
    sp = (D, 2 * D, 2 * D + q_lora, 2 * D + q_lora + kv_lora, 2 * D + q_lora + kv_lora + ROPE_DIM,
          3 * D + q_lora + kv_lora + ROPE_DIM)
    x_rec, x_gate, c_q, c_kv, k_rot, g_rec, g_att = jnp.split(w_in, sp, axis=-1)
    w_in_perm = jnp.concatenate(
        [x_rec, x_gate, g_rec, g_att, c_q, c_kv, _pad_cols(k_rot, LANES),
         _pad_cols(_rot_half_cols(k_rot), LANES)], axis=-1).astype(BF16)

    wq = p["w_uq"][l].reshape(q_lora, N_HEADS, QK_DIM)
    wq_n, wq_r = wq[..., :NOPE_DIM], wq[..., NOPE_DIM:]
    wk = p["w_uk"][l].reshape(kv_lora, N_HEADS, NOPE_DIM)
    eye_r = jnp.broadcast_to(jnp.eye(ROPE_DIM, dtype=F32)[:, None, :], (ROPE_DIM, N_HEADS, ROPE_DIM))
    gq, gk = p["q_head_norm"][l], p["k_head_norm"][l]
    q_scale = (QK_DIM ** -0.5) * LOG2E
    return dict(
        gmix=p["norm_mix"][l].reshape(1, D), w_in=w_in_perm,
        conv_w=p["conv_w"][l], conv_b=p["conv_b"][l].reshape(1, D),
        wa=_block_diag_tiles(p["lru_wa"][l]).astype(BF16), ba=p["lru_ba"][l].reshape(1, D),
        wi=_block_diag_tiles(p["lru_wi"][l]).astype(BF16), bi=p["lru_bi"][l].reshape(1, D),
        sp=jax.nn.softplus(-p["lru_lambda"][l]).reshape(1, D),
        gq=p["q_lat_norm"][l].reshape(1, q_lora), gkv=p["kv_lat_norm"][l].reshape(1, kv_lora),
        wq_t=_head_pad(wq_n, wq_r).T.astype(BF16),
        gqh=(_pad_cols(gq, HEAD_PAD) * q_scale).reshape(HEAD_PAD, 1),
        wk=_head_pad(wk, jnp.zeros((kv_lora, N_HEADS, ROPE_DIM), F32)).astype(BF16),
        place=_head_pad(jnp.zeros((ROPE_DIM, N_HEADS, NOPE_DIM), F32), eye_r).astype(BF16),
        wv_t=p["w_uv"][l].T.astype(BF16),
        gkh=_pad_cols(gk, HEAD_PAD).reshape(1, HEAD_PAD),
        score_bound=QK_DIM * jnp.max(jnp.abs(gq)) * jnp.max(jnp.abs(gk)) * q_scale,
        w_out=p["w_out"][l].astype(BF16), gmem=p["norm_mem"][l].reshape(1, D),
        w_mq=p["w_mq"][l].astype(BF16), gmqh=p["mq_head_norm"][l].reshape(1, -1),
        w_mo=p["w_mo"][l].astype(BF16), gffn=p["norm_ffn"][l].reshape(1, D),
    )


def _rope_tables(pos):
    half = ROPE_DIM // 2
    inv_freq = ROPE_BASE ** (-jnp.arange(half, dtype=F32) / half)
    ang = pos.astype(F32)[:, None] * inv_freq[None, :]
    cos2 = jnp.concatenate([jnp.cos(ang), jnp.cos(ang)], axis=-1)
    sin2 = jnp.concatenate([jnp.sin(ang), jnp.sin(ang)], axis=-1)
    n = pos.shape[0]
    ones, zeros = jnp.ones((n, NOPE_DIM), F32), jnp.zeros((n, NOPE_DIM), F32)
    return dict(cos_k=_pad_cols(cos2, LANES), sin_k=_pad_cols(sin2, LANES),
                cos_q=_pad_cols(jnp.concatenate([ones, cos2], axis=-1), HEAD_PAD).T,
                sin_q=_pad_cols(jnp.concatenate([zeros, sin2], axis=-1), HEAD_PAD).T)


def _layer(l, x, tabs, q_off, conv_prev, h0, ckv_past, kr_past, mem_k, mem_v, w, p):
    B, L, D = x.shape
    prev8 = jnp.pad(conv_prev, ((0, 0), (SUBLANES - (CONV_W - 1), 0), (0, 0)))
    h0_8 = jnp.pad(h0[:, None, :], ((0, 0), (SUBLANES - 1, 0), (0, 0)))
    grec, gatt, cq, ckv, kr, ctail, htail = _inproj(x, prev8, h0_8, w, tabs["cos_k"], tabs["sin_k"])

    q = _qprep(cq, w, tabs["cos_q"], tabs["sin_q"])
    if ckv_past is None:
        ckv_all, kr_all = ckv, kr
    else:
        ckv_all = jnp.concatenate([ckv_past, ckv], axis=1)
        kr_all = jnp.concatenate([kr_past, kr], axis=1)
    t_valid = ckv_all.shape[1]
    tk = min(512, t_valid)
    t_pad = -(-t_valid // tk) * tk
    if t_pad != t_valid:
        ckv_all = jnp.pad(ckv_all, ((0, 0), (0, t_pad - t_valid), (0, 0)))
        kr_all = jnp.pad(kr_all, ((0, 0), (0, t_pad - t_valid), (0, 0)))
    k, vt = _kvprep(ckv_all, kr_all, w, tk)
    flash = functools.partial(_flash, q_off=q_off, t_valid=t_valid, tk=tk)
    att = lax.cond(w["score_bound"] <= SAFE_LOG2_RANGE, functools.partial(flash, bounded=True),
                   functools.partial(flash, bounded=False), q, k, vt)

    x2 = _post(x, grec, gatt, att, mem_k, mem_v, w).reshape(B * L, D)
    j = l // 2
    if l % 2 == 0:
        x3 = _ffn(x2, w["gffn"], p["ffn_w1"][j].astype(BF16), p["ffn_w3"][j].astype(BF16),
                  p["ffn_w2"][j].astype(BF16))
    else:
        E = p["moe_router"].shape[-1]
        wr = _pad_cols(p["moe_router"][j], LANES)
        wr_hi = wr.astype(BF16)
        wr_lo = (wr - wr_hi.astype(F32)).astype(BF16)
        br = jnp.concatenate([p["moe_router_b"][j], jnp.full((LANES - E,), NEG, F32)]).reshape(1, LANES)
        x3 = _moe(x2, w["gffn"], wr_hi, wr_lo, br, p["moe_w1"][j].astype(BF16),
                  p["moe_w3"][j].astype(BF16), p["moe_w2"][j].astype(BF16))
    conv_state = ctail[:, SUBLANES - (CONV_W - 1):, :]
    return x3.reshape(B, L, D), conv_state, htail[:, SUBLANES - 1, :], ckv, kr


def kernel(x_prompt, x_sample, cache_ckv, cache_krope, cache_mem_k, cache_mem_v, state_lru, state_conv, mem_prompt, norm_mix, w_in, conv_w, conv_b, lru_wa, lru_ba, lru_wi, lru_bi, lru_lambda, q_lat_norm, w_uq, kv_lat_norm, w_uk, w_uv, q_head_norm, k_head_norm, w_out, norm_mem, mem_in_norm, w_mq, w_mk, w_mv, w_mo, mq_head_norm, mk_head_norm, norm_ffn, ffn_w1, ffn_w3, ffn_w2, moe_router, moe_router_b, moe_w1, moe_w3, moe_w2):
    p = dict(norm_mix=norm_mix, w_in=w_in, conv_w=conv_w, conv_b=conv_b, lru_wa=lru_wa, lru_ba=lru_ba,
             lru_wi=lru_wi, lru_bi=lru_bi, lru_lambda=lru_lambda, q_lat_norm=q_lat_norm, w_uq=w_uq,
             kv_lat_norm=kv_lat_norm, w_uk=w_uk, w_uv=w_uv, q_head_norm=q_head_norm,
             k_head_norm=k_head_norm, w_out=w_out, norm_mem=norm_mem, w_mq=w_mq, w_mo=w_mo,
             mq_head_norm=mq_head_norm, norm_ffn=norm_ffn, ffn_w1=ffn_w1, ffn_w3=ffn_w3,
             ffn_w2=ffn_w2, moe_router=moe_router, moe_router_b=moe_router_b, moe_w1=moe_w1,
             moe_w3=moe_w3, moe_w2=moe_w2)
    depth = w_in.shape[0]
    weights = [_layer_weights(l, p) for l in range(depth)]

    Bp, Lp, D = x_prompt.shape
    Bs, Ls, _ = x_sample.shape
    t_past = cache_ckv.shape[2]
    M = mem_prompt.shape[1]

    tabs_p = _rope_tables(jnp.arange(Lp, dtype=jnp.int32))
    zero_conv = jnp.zeros((Bp, CONV_W - 1, D), F32)
    zero_h = jnp.zeros((Bp, D), F32)
    x = x_prompt
    outs_p = [[] for _ in range(6)]
    for l in range(depth):
        mk, mv = _memkv(mem_prompt, mem_in_norm[l], w_mk[l], w_mv[l], mk_head_norm[l])
        x, cs, hl, ckv, kr = _layer(l, x, tabs_p, 0, zero_conv, zero_h, None, None, mk, mv, weights[l], p)
        for acc, val in zip(outs_p, (ckv, kr, hl, cs, mk.reshape(Bp, M, MEM_HEADS, -1),
                                     mv.reshape(Bp, M, MEM_HEADS, -1))):
            acc.append(val)
    y_prompt = x

    tabs_s = _rope_tables(t_past + jnp.arange(Ls, dtype=jnp.int32))
    x = x_sample
    outs_s = [[] for _ in range(4)]
    for l in range(depth):
        x, cs, hl, ckv, kr = _layer(l, x, tabs_s, t_past, state_conv[l], state_lru[l], cache_ckv[l],
                                    cache_krope[l], cache_mem_k[l].reshape(Bs, M, D),
                                    cache_mem_v[l].reshape(Bs, M, D), weights[l], p)
        for acc, val in zip(outs_s, (ckv, kr, hl, cs)):
            acc.append(val)
    y_sample = x

    return (y_prompt, y_sample) + tuple(jnp.stack(a) for a in outs_p) + tuple(jnp.stack(a) for a in outs_s)
```

```python
import functools
import math

import jax
import jax.numpy as jnp
from jax import lax
from jax.experimental import pallas as pl
from jax.experimental.pallas import tpu as pltpu

F32 = jnp.float32
BF16 = jnp.bfloat16

EPS = 1e-6
CHUNK = 64
LRU_BLOCKS = 16
LRU_C = 8.0
CONV_W = 4
N_HEADS = 16
NOPE_DIM = 64
ROPE_DIM = 32
QK_DIM = NOPE_DIM + ROPE_DIM
ROPE_BASE = 10000.0
MEM_HEADS = 4
TOP_K = 2

LANES = 128
SUBLANES = 8
MXU_DIM = 256
VMEM_LIMIT = 56 * 1024 * 1024

HEAD_PAD = LANES
MOE_CHUNK = 160
SAFE_LOG2_RANGE = 60.0
FLASH_HEADS = 8
FLASH_LAG = 8
NEG = -1e30
LOG2E = 1.4426950408889634


def _resident(shape):
    nd = len(shape)
    return pl.BlockSpec(shape, lambda *_: (0,) * nd, pipeline_mode=pl.Buffered(1))


def _params(n_axes):
    return pltpu.CompilerParams(dimension_semantics=("arbitrary",) * n_axes,
                                vmem_limit_bytes=VMEM_LIMIT)


def _rms(x, gain):
    return x * lax.rsqrt(jnp.mean(x * x, axis=-1, keepdims=True) + EPS) * gain


def _dot(a, b):
    return jnp.dot(a, b, preferred_element_type=F32)


def _dot_nt(a, b):
    return lax.dot_general(a, b, (((1,), (1,)), ((), ())), preferred_element_type=F32)


def _token_tile(n, cap=512):
    t = min(n, cap)
    assert n % t == 0 and t % SUBLANES == 0
    return t


def _memkv_kernel(mem_ref, gin_ref, wk_ref, wv_ref, gk_ref, k_ref, v_ref):
    mn = _rms(mem_ref[0], gin_ref[...]).astype(BF16)
    k = _dot(mn, wk_ref[...])
    hd = gk_ref.shape[-1]
    for h in range(MEM_HEADS):
        sl = slice(h * hd, (h + 1) * hd)
        k_ref[0, :, sl] = _rms(k[:, sl], gk_ref[...])
    v_ref[0] = _dot(mn, wv_ref[...])


def _memkv(mem, g_in, w_k, w_v, g_k):
    B, M, D = mem.shape
    tok = pl.BlockSpec((1, M, D), lambda b: (b, 0, 0))
    return pl.pallas_call(
        _memkv_kernel,
        grid=(B,),
        in_specs=[tok, _resident((1, D)), _resident((D, D)), _resident((D, D)),
                  _resident((1, D // MEM_HEADS))],
        out_specs=[tok, tok],
        out_shape=[jax.ShapeDtypeStruct((B, M, D), F32)] * 2,
        compiler_params=_params(1),
        name="memkv",
    )(mem, g_in.reshape(1, D), w_k.astype(BF16), w_v.astype(BF16), g_k.reshape(1, -1))


def _inproj_kernel(x_ref, prev_ref, h0_ref, gmix_ref, win_ref, cw_ref, cb_ref, wa_ref, ba_ref,
                   wi_ref, bi_ref, sp_ref, gq_ref, gkv_ref, cos_ref, sin_ref,
                   grec_ref, gatt_ref, cq_ref, ckv_ref, kr_ref, ctail_ref, htail_ref,
                   xext_ref, hcar_ref, *, tm, d, q_lora, kv_lora):
    @pl.when(pl.program_id(1) == 0)
    def _():
        xext_ref[0:SUBLANES, :] = prev_ref[0]
        hcar_ref[...] = h0_ref[0]

    hn = _rms(x_ref[0], gmix_ref[...]).astype(BF16)

    def proj(c0, c1):
        return _dot(hn, win_ref[:, c0:c1])

    x_rec = proj(0, d)
    xext_ref[SUBLANES:SUBLANES + tm, :] = x_rec
    xc = cb_ref[...] + cw_ref[CONV_W - 1:CONV_W, :] * x_rec
    for j in range(CONV_W - 1):
        off = SUBLANES - (CONV_W - 1) + j
        xc = xc + cw_ref[j:j + 1, :] * xext_ref[off:off + tm, :]
    tail = x_rec[tm - SUBLANES:tm, :]
    ctail_ref[0] = tail
    xext_ref[0:SUBLANES, :] = tail

    xcb = xc.astype(BF16)
    nb = d // MXU_DIM
    gr = jnp.concatenate(
        [_dot(xcb[:, c * MXU_DIM:(c + 1) * MXU_DIM], wa_ref[c]) for c in range(nb)], axis=1)
    gi = jnp.concatenate(
        [_dot(xcb[:, c * MXU_DIM:(c + 1) * MXU_DIM], wi_ref[c]) for c in range(nb)], axis=1)
    r = jax.nn.sigmoid(gr + ba_ref[...])
    ig = jax.nn.sigmoid(gi + bi_ref[...])
    log_a = (-LRU_C) * r * sp_ref[...]
    a = jnp.exp(log_a)
    u = jnp.sqrt(1.0 - a * a) * (ig * xc)

    row = lax.broadcasted_iota(jnp.int32, (tm, 1), 0) & (SUBLANES - 1)
    s = 1
    while s < SUBLANES:
        keep = row >= s
        u = a * jnp.where(keep, pltpu.roll(u, s, 0), 0.0) + u
        a = a * jnp.where(keep, pltpu.roll(a, s, 0), 1.0)
        s *= 2
    h_prev = hcar_ref[SUBLANES - 1:SUBLANES, :]
    groups = []
    for g in range(tm // SUBLANES):
        rows = slice(g * SUBLANES, (g + 1) * SUBLANES)
        hg = a[rows] * h_prev + u[rows]
        groups.append(hg)
        h_prev = hg[SUBLANES - 1:SUBLANES, :]
    h = jnp.concatenate(groups, axis=0)
    htail = groups[-1]
    hcar_ref[...] = htail
    htail_ref[0] = htail

    x_gate = proj(d, 2 * d)
    g_rec = proj(2 * d, 3 * d)
    grec_ref[0] = jax.nn.sigmoid(g_rec) * (jax.nn.gelu(x_gate) * h)
    gatt_ref[0] = proj(3 * d, 4 * d)

    o = 4 * d
    cq_ref[0] = _rms(proj(o, o + q_lora), gq_ref[...]).astype(BF16)
    o += q_lora
    ckv_ref[0] = _rms(proj(o, o + kv_lora), gkv_ref[...])
    o += kv_lora
    kr = proj(o, o + LANES) * cos_ref[...] + proj(o + LANES, o + 2 * LANES) * sin_ref[...]
    kr_ref[0] = kr


def _inproj(x, prev8, h0_8, w, cos_k, sin_k):
    B, L, D = x.shape
    tm = _token_tile(L, 256)
    q_lora, kv_lora = w["gq"].shape[-1], w["gkv"].shape[-1]
    ncol = w["w_in"].shape[-1]
    tok = lambda n: pl.BlockSpec((1, tm, n), lambda b, l: (b, l, 0))
    per_b = pl.BlockSpec((1, SUBLANES, D), lambda b, l: (b, 0, 0))
    tab = pl.BlockSpec((tm, LANES), lambda b, l: (l, 0))
    nb = D // MXU_DIM
    kern = functools.partial(_inproj_kernel, tm=tm, d=D, q_lora=q_lora, kv_lora=kv_lora)
    return pl.pallas_call(
        kern,
        grid=(B, L // tm),
        in_specs=[tok(D), per_b, per_b, _resident((1, D)), _resident((D, ncol)),
                  _resident((CONV_W, D)), _resident((1, D)),
                  _resident((nb, MXU_DIM, MXU_DIM)), _resident((1, D)),
                  _resident((nb, MXU_DIM, MXU_DIM)), _resident((1, D)), _resident((1, D)),
                  _resident((1, q_lora)), _resident((1, kv_lora)), tab, tab],
        out_specs=[tok(D), tok(D), tok(q_lora), tok(kv_lora), tok(HEAD_PAD), per_b, per_b],
        out_shape=[jax.ShapeDtypeStruct((B, L, D), F32), jax.ShapeDtypeStruct((B, L, D), F32),
                   jax.ShapeDtypeStruct((B, L, q_lora), BF16),
                   jax.ShapeDtypeStruct((B, L, kv_lora), F32),
                   jax.ShapeDtypeStruct((B, L, HEAD_PAD), F32),
                   jax.ShapeDtypeStruct((B, SUBLANES, D), F32),
                   jax.ShapeDtypeStruct((B, SUBLANES, D), F32)],
        scratch_shapes=[pltpu.VMEM((SUBLANES + tm, D), F32), pltpu.VMEM((SUBLANES, D), F32)],
        compiler_params=_params(2),
        name="inproj",
    )(x, prev8, h0_8, w["gmix"], w["w_in"], w["conv_w"], w["conv_b"], w["wa"], w["ba"],
      w["wi"], w["bi"], w["sp"], w["gq"], w["gkv"], cos_k, sin_k)


def _qprep_kernel(cq_ref, wt_ref, cos_ref, sin_ref, g_ref, q_ref):
    qt = _dot_nt(wt_ref[...], cq_ref[0])
    half = ROPE_DIM // 2
    tm = qt.shape[1]
    for h in range(N_HEADS):
        blk = qt[h * HEAD_PAD:(h + 1) * HEAD_PAD, :]
        x1 = blk[NOPE_DIM:NOPE_DIM + half, :]
        x2 = blk[NOPE_DIM + half:QK_DIM, :]
        rot = jnp.concatenate([jnp.zeros((NOPE_DIM, tm), F32), -x2, x1,
                               jnp.zeros((HEAD_PAD - QK_DIM, tm), F32)], axis=0)
        qh = blk * cos_ref[...] + rot * sin_ref[...]
        ss = jnp.sum(qh * qh, axis=0, keepdims=True) * (1.0 / QK_DIM)
        q_ref[0, h * HEAD_PAD:(h + 1) * HEAD_PAD, :] = (qh * lax.rsqrt(ss + EPS) * g_ref[...]).astype(BF16)


def _qprep(cq, w, cos_q, sin_q):
    B, L, QL = cq.shape
    tm = _token_tile(L)
    n = N_HEADS * HEAD_PAD
    tab = pl.BlockSpec((HEAD_PAD, tm), lambda b, l: (0, l))
    return pl.pallas_call(
        _qprep_kernel,
        grid=(B, L // tm),
        in_specs=[pl.BlockSpec((1, tm, QL), lambda b, l: (b, l, 0)),
                  _resident((n, QL)), tab, tab, _resident((HEAD_PAD, 1))],
        out_specs=pl.BlockSpec((1, n, tm), lambda b, l: (b, 0, l)),
        out_shape=jax.ShapeDtypeStruct((B, n, L), BF16),
        compiler_params=_params(2),
        name="qprep",
    )(cq, w["wq_t"], cos_q, sin_q, w["gqh"])


def _kvprep_kernel(ckv_ref, kr_ref, wk_ref, wvt_ref, g_ref, k_ref, vt_ref):
    c = ckv_ref[0].astype(BF16)
    k = _dot(c, wk_ref[...])
    kr = kr_ref[0]
    for h in range(N_HEADS):
        sl = slice(h * HEAD_PAD, (h + 1) * HEAD_PAD)
        kh = k[:, sl] + kr
        ss = jnp.sum(kh * kh, axis=-1, keepdims=True) * (1.0 / QK_DIM)
        k_ref[0, :, sl] = (kh * lax.rsqrt(ss + EPS) * g_ref[...]).astype(BF16)
    vt_ref[0, 0] = _dot_nt(wvt_ref[...], c).astype(BF16)


def _kvprep(ckv, kr, w, tk):
    B, T, KV = ckv.shape
    n = N_HEADS * HEAD_PAD
    dv = w["wv_t"].shape[0]
    return pl.pallas_call(
        _kvprep_kernel,
        grid=(B, T // tk),
        in_specs=[pl.BlockSpec((1, tk, KV), lambda b, t: (b, t, 0)),
                  pl.BlockSpec((1, tk, HEAD_PAD), lambda b, t: (b, t, 0)),
                  _resident((KV, n)), _resident((dv, KV)),
                  _resident((1, HEAD_PAD))],
        out_specs=[pl.BlockSpec((1, tk, n), lambda b, t: (b, t, 0)),
                   pl.BlockSpec((1, 1, dv, tk), lambda b, t: (b, t, 0, 0))],
        out_shape=[jax.ShapeDtypeStruct((B, T, n), BF16),
                   jax.ShapeDtypeStruct((B, T // tk, dv, tk), BF16)],
        compiler_params=_params(2),
        name="kvprep",
    )(ckv, kr, w["wk"], w["wv_t"], w["gkh"])


def _flash_kernel(q_ref, k_ref, vt_ref, o_ref, m_ref, l_ref, acc_ref, *, tq, tk, q_off, t_valid,
                  v_dim, heads, bounded, lag, split_diag):
    q_start = q_off + pl.program_id(2) * tq
    n_full = jnp.minimum(q_start, t_valid) // tk
    n_end = (jnp.minimum(q_start + tq, t_valid) + tk - 1) // tk
    shift = int(math.log2(CHUNK))

    def chunk_of(pos):
        return lax.shift_right_logical(pos, shift)

    m_ref[...] = jnp.full(m_ref.shape, NEG, F32)
    l_ref[...] = jnp.zeros(l_ref.shape, F32)
    acc_ref[...] = jnp.zeros(acc_ref.shape, F32)

    def accumulate(hh, s, vj, q0):
        lanes = slice(q0, tq)
        if bounded:
            p = jnp.exp2(s)
            l_ref[hh, :, lanes] += jnp.sum(p, axis=0, keepdims=True)
        else:
            m = m_ref[hh, :, lanes]
            m_new = jnp.maximum(m, jnp.max(s, axis=0, keepdims=True))
            alpha = jnp.exp2(m - m_new)
            p = jnp.exp2(s - m_new)
            l_ref[hh, :, lanes] = alpha * l_ref[hh, :, lanes] + jnp.sum(p, axis=0, keepdims=True)
            m_ref[hh, :, lanes] = m_new
            acc_ref[hh, :, lanes] = alpha * acc_ref[hh, :, lanes]
        acc_ref[hh, :, lanes] += _dot(vj, p.astype(BF16))

    def tile(j, masked):
        if masked:
            k_pos = j * tk + lax.broadcasted_iota(jnp.int32, (tk, 1), 0)
            q_pos = q_start + lax.broadcasted_iota(jnp.int32, (1, tq), 1)
            vis = (chunk_of(k_pos) <= chunk_of(q_pos)) & (k_pos < t_valid)
        pending = {}
        for t in range(heads + lag):
            if t < heads:
                qh = q_ref[0, t * HEAD_PAD:(t + 1) * HEAD_PAD, :]
                kj = k_ref[0, pl.ds(pl.multiple_of(j * tk, tk), tk), t * HEAD_PAD:(t + 1) * HEAD_PAD]
                pending[t] = _dot(kj, qh)
            if t >= lag:
                hh = t - lag
                s = pending.pop(hh)
                if masked:
                    s = jnp.where(vis, s, NEG)
                accumulate(hh, s, vt_ref[0, j, hh * v_dim:(hh + 1) * v_dim, :], 0)

    def diagonal(j):
        half = tk // 2
        vis = (chunk_of(lax.broadcasted_iota(jnp.int32, (half, 1), 0))
               <= chunk_of(lax.broadcasted_iota(jnp.int32, (1, tq), 1)))
        for k0, q0 in ((0, 0), (half, half)):
            for hh in range(heads):
                qh = q_ref[0, hh * HEAD_PAD:(hh + 1) * HEAD_PAD, q0:]
                kj = k_ref[0, pl.ds(pl.multiple_of(j * tk + k0, half), half),
                           hh * HEAD_PAD:(hh + 1) * HEAD_PAD]
                s = jnp.where(vis[:, :tq - q0], _dot(kj, qh), NEG)
                accumulate(hh, s, vt_ref[0, j, hh * v_dim:(hh + 1) * v_dim, k0:k0 + half], q0)

    def loop(lo, hi, body):
        lax.fori_loop(lo, hi, lambda j, c: (body(j), c)[1], 0)

    loop(0, n_full, lambda j: tile(j, False))
    if split_diag:
        diagonal(n_full)
    else:
        loop(n_full, n_end, lambda j: tile(j, True))
    o_ref[0] = jnp.concatenate([acc_ref[hh] / l_ref[hh] for hh in range(heads)], axis=0).T


def _flash(q, k, vt, *, q_off, t_valid, tk, bounded, heads=FLASH_HEADS):
    B, n, L = q.shape
    T = k.shape[1]
    dv = vt.shape[2]
    v_dim = dv // N_HEADS
    tq = _token_tile(L)
    assert tq % CHUNK == 0 and q_off % CHUNK == 0 and (heads * v_dim) % LANES == 0
    split_diag = tq == tk and q_off == 0 and t_valid == T and (tk // 2) % max(CHUNK, LANES) == 0
    kern = functools.partial(_flash_kernel, tq=tq, tk=tk, q_off=q_off, t_valid=t_valid, v_dim=v_dim,
                             heads=heads, bounded=bounded, lag=min(FLASH_LAG, heads),
                             split_diag=split_diag)
    return pl.pallas_call(
        kern,
        grid=(B, N_HEADS // heads, L // tq),
        in_specs=[pl.BlockSpec((1, heads * HEAD_PAD, tq), lambda b, p, i: (b, p, i)),
                  pl.BlockSpec((1, T, heads * HEAD_PAD), lambda b, p, i: (b, 0, p)),
                  pl.BlockSpec((1, T // tk, heads * v_dim, tk), lambda b, p, i: (b, 0, p, 0))],
        out_specs=pl.BlockSpec((1, tq, heads * v_dim), lambda b, p, i: (b, i, p)),
        out_shape=jax.ShapeDtypeStruct((B, L, dv), F32),
        scratch_shapes=[pltpu.VMEM((heads, 1, tq), F32), pltpu.VMEM((heads, 1, tq), F32),
                        pltpu.VMEM((heads, v_dim, tq), F32)],
        compiler_params=_params(3),
        name="flash",
    )(q, k, vt)


def _post_kernel(x_ref, grec_ref, gatt_ref, att_ref, wo_ref, gmem_ref, wmq_ref, gqh_ref,
                 mk_ref, mv_ref, wmo_ref, o_ref):
    mixed = (grec_ref[0] + jax.nn.sigmoid(gatt_ref[0]) * att_ref[0]).astype(BF16)
    x1 = x_ref[0] + _dot(mixed, wo_ref[...])
    qm = _dot(_rms(x1, gmem_ref[...]).astype(BF16), wmq_ref[...])
    hd = gqh_ref.shape[-1]
    outs = []
    for h in range(MEM_HEADS):
        sl = slice(h * hd, (h + 1) * hd)
        qh = (_rms(qm[:, sl], gqh_ref[...]) * (hd ** -0.5)).astype(BF16)
        s = _dot_nt(qh, mk_ref[0, :, sl])
        p = jnp.exp(s - jnp.max(s, axis=-1, keepdims=True))
        l = jnp.sum(p, axis=-1, keepdims=True)
        outs.append((_dot(p.astype(BF16), mv_ref[0, :, sl]) / l).astype(BF16))
    o_ref[0] = x1 + _dot(jnp.concatenate(outs, axis=1), wmo_ref[...])


def _post(x, grec, gatt, att, mem_k, mem_v, w):
    B, L, D = x.shape
    M = mem_k.shape[1]
    tm = _token_tile(L)
    tok = pl.BlockSpec((1, tm, D), lambda b, l: (b, l, 0))
    mem = pl.BlockSpec((1, M, D), lambda b, l: (b, 0, 0))
    return pl.pallas_call(
        _post_kernel,
        grid=(B, L // tm),
        in_specs=[tok, tok, tok, tok, _resident((D, D)), _resident((1, D)), _resident((D, D)),
                  _resident((1, D // MEM_HEADS)), mem, mem, _resident((D, D))],
        out_specs=tok,
        out_shape=jax.ShapeDtypeStruct((B, L, D), F32),
        compiler_params=_params(2),
        name="post",
    )(x, grec, gatt, att, w["w_out"], w["gmem"], w["w_mq"], w["gmqh"],
      mem_k.astype(BF16), mem_v.astype(BF16), w["w_mo"])


def _swiglu_mid(a, b):
    return (a * jax.nn.sigmoid(a) * b).astype(BF16)


def _ffn_kernel(x_ref, g_ref, w1_ref, w3_ref, w2_ref, o_ref, *, n_split):
    x = x_ref[...]
    hf = _rms(x, g_ref[...]).astype(BF16)
    fc = w1_ref.shape[1] // n_split
    acc = x
    for c in range(n_split):
        sl = slice(c * fc, (c + 1) * fc)
        acc = acc + _dot(_swiglu_mid(_dot(hf, w1_ref[:, sl]), _dot(hf, w3_ref[:, sl])), w2_ref[sl, :])
    o_ref[...] = acc


def _ffn(x, g, w1, w3, w2):
    N, D = x.shape
    F = w1.shape[1]
    tm = _token_tile(N)
    n_split = 2
    assert F % (n_split * LANES) == 0
    tok = pl.BlockSpec((tm, D), lambda i: (i, 0))
    return pl.pallas_call(
        functools.partial(_ffn_kernel, n_split=n_split),
        grid=(N // tm,),
        in_specs=[tok, _resident((1, D)), _resident((D, F)), _resident((D, F)), _resident((F, D))],
        out_specs=tok,
        out_shape=jax.ShapeDtypeStruct((N, D), F32),
        compiler_params=_params(1),
        name="ffn",
    )(x, g, w1, w3, w2)


def _moe_kernel(x_ref, g_ref, wrh_ref, wrl_ref, br_ref, w1_ref, w3_ref, w2_ref, o_ref,
                hf_ref, gate_ref, rank_ref, *, ts, cap, n_exp):
    e = pl.program_id(1)
    lane = lax.broadcasted_iota(jnp.int32, (1, LANES), 1)
    n_sub = hf_ref.shape[0] // ts

    @pl.when(e == 0)
    def _():
        x = x_ref[...]
        hf = _rms(x, g_ref[...])
        hi = hf.astype(BF16)
        lo = (hf - hi.astype(F32)).astype(BF16)
        logits = (_dot(hi, wrh_ref[...]) + _dot(lo, wrh_ref[...]) + _dot(hi, wrl_ref[...])
                  + br_ref[...])
        m1 = jnp.max(logits, axis=-1, keepdims=True)
        i1 = jnp.min(jnp.where(logits == m1, lane, LANES), axis=-1, keepdims=True)
        rest = jnp.where(lane == i1, NEG, logits)
        m2 = jnp.max(rest, axis=-1, keepdims=True)
        i2 = jnp.min(jnp.where(rest == m2, lane, LANES), axis=-1, keepdims=True)
        e2 = jnp.exp(m2 - m1)
        den = 1.0 + e2
        gate_ref[...] = jnp.where(lane == i1, 1.0 / den, 0.0) + jnp.where(lane == i2, e2 / den, 0.0)
        hf_ref[...] = hi
        o_ref[...] = x
        sel = jnp.where((lane == i1) | (lane == i2), 1.0, 0.0)
        before = (lax.broadcasted_iota(jnp.int32, (ts, ts), 0)
                  < lax.broadcasted_iota(jnp.int32, (ts, ts), 1))
        tri = jnp.where(before, 1.0, 0.0).astype(BF16)
        for s in range(n_sub):
            sel_t = sel[s * ts:(s + 1) * ts, :].T[:n_exp, :]
            rank = _dot(sel_t.astype(BF16), tri)
            rank_ref[s] = jnp.where(sel_t > 0.0, rank, -1.0)

    row = lax.broadcasted_iota(jnp.int32, (cap, 1), 0)
    for s in range(n_sub):
        rows = slice(s * ts, (s + 1) * ts)
        rank_e = rank_ref[s, pl.ds(e, 1), :]
        n_tok = jnp.max(rank_e).astype(jnp.int32) + 1
        ge = jnp.sum(jnp.where(lane == e, gate_ref[rows, :], 0.0), axis=-1, keepdims=True)

        def chunk(c, carry):
            want = (row + c * cap).astype(F32)
            onehot = jnp.where(rank_e == want, 1.0, 0.0).astype(BF16)
            xg = _dot(onehot, hf_ref[rows, :]).astype(BF16)
            y = _dot(_swiglu_mid(_dot(xg, w1_ref[0]), _dot(xg, w3_ref[0])), w2_ref[0])
            back = lax.dot_general(onehot, y.astype(BF16), (((0,), (0,)), ((), ())),
                                   preferred_element_type=F32)
            o_ref[rows, :] += ge * back
            return carry

        lax.fori_loop(0, (n_tok + cap - 1) // cap, chunk, 0)


def _moe(x, g, wr_hi, wr_lo, br, w1, w3, w2):
    N, D = x.shape
    E, _, F = w1.shape
    tm = _token_tile(N, 1024)
    ts = _token_tile(tm, 512)
    assert E <= SUBLANES
    tok = pl.BlockSpec((tm, D), lambda i, e: (i, 0))
    kern = functools.partial(_moe_kernel, ts=ts, cap=MOE_CHUNK, n_exp=E)
    return pl.pallas_call(
        kern,
        grid=(N // tm, E),
        in_specs=[tok, _resident((1, D)), _resident((D, LANES)), _resident((D, LANES)),
                  _resident((1, LANES)),
                  pl.BlockSpec((1, D, F), lambda i, e: (e, 0, 0)),
                  pl.BlockSpec((1, D, F), lambda i, e: (e, 0, 0)),
                  pl.BlockSpec((1, F, D), lambda i, e: (e, 0, 0))],
        out_specs=tok,
        out_shape=jax.ShapeDtypeStruct((N, D), F32),
        scratch_shapes=[pltpu.VMEM((tm, D), BF16), pltpu.VMEM((tm, LANES), F32),
                        pltpu.VMEM((tm // ts, E, ts), F32)],
        compiler_params=_params(2),
        name="moe",
    )(x, g, wr_hi, wr_lo, br, w1, w3, w2)


def _rot_half_cols(w):
    half = ROPE_DIM // 2
    return jnp.concatenate([-w[..., half:], w[..., :half]], axis=-1)


def _rope_lanes(a):
    return jnp.pad(a, [(0, 0)] * (a.ndim - 1) + [(NOPE_DIM, HEAD_PAD - QK_DIM)])


def _pad_cols(w, n):
    return jnp.pad(w, [(0, 0)] * (w.ndim - 1) + [(0, n - w.shape[-1])])


def _head_pad(nope, rope):
    z = jnp.zeros(nope.shape[:-1] + (HEAD_PAD - QK_DIM,), nope.dtype)
    out = jnp.concatenate([nope, rope, z], axis=-1)
    return out.reshape(out.shape[:-2] + (N_HEADS * HEAD_PAD,))


def _block_diag_tiles(w):
    nblk, bw, _ = w.shape
    per = MXU_DIM // bw
    w4 = w.reshape(nblk // per, per, bw, bw)
    eye = jnp.eye(per, dtype=w.dtype)
    return jnp.einsum("cpij,pq->cpiqj", w4, eye).reshape(nblk // per, MXU_DIM, MXU_DIM)


def _layer_weights(l, p):
    D = p["w_in"].shape[1]
    q_lora = p["q_lat_norm"].shape[-1]
    kv_lora = p["kv_lat_norm"].shape[-1]
    w_in = p["w_in"][l]
    sp = (D, 2 * D, 2 * D + q_lora, 2 * D + q_lora + kv_lora, 2 * D + q_lora + kv_lora + ROPE_DIM,
          3 * D + q_lora + kv_lora + ROPE_DIM)
    x_rec, x_gate, c_q, c_kv, k_rot, g_rec, g_att = jnp.split(w_in, sp, axis=-1)
    w_in_perm = jnp.concatenate(
        [x_rec, x_gate, g_rec, g_att, c_q, c_kv, _rope_lanes(k_rot),
         _rope_lanes(_rot_half_cols(k_rot))], axis=-1).astype(BF16)

    wq = p["w_uq"][l].reshape(q_lora, N_HEADS, QK_DIM)
    wq_n, wq_r = wq[..., :NOPE_DIM], wq[..., NOPE_DIM:]
    wk = p["w_uk"][l].reshape(kv_lora, N_HEADS, NOPE_DIM)
    gq, gk = p["q_head_norm"][l], p["k_head_norm"][l]
    q_scale = (QK_DIM ** -0.5) * LOG2E
    return dict(
        gmix=p["norm_mix"][l].reshape(1, D), w_in=w_in_perm,
        conv_w=p["conv_w"][l], conv_b=p["conv_b"][l].reshape(1, D),
        wa=_block_diag_tiles(p["lru_wa"][l]).astype(BF16), ba=p["lru_ba"][l].reshape(1, D),
        wi=_block_diag_tiles(p["lru_wi"][l]).astype(BF16), bi=p["lru_bi"][l].reshape(1, D),
        sp=jax.nn.softplus(-p["lru_lambda"][l]).reshape(1, D),
        gq=p["q_lat_norm"][l].reshape(1, q_lora), gkv=p["kv_lat_norm"][l].reshape(1, kv_lora),
        wq_t=_head_pad(wq_n, wq_r).T.astype(BF16),
        gqh=(_pad_cols(gq, HEAD_PAD) * q_scale).reshape(HEAD_PAD, 1),
        wk=_head_pad(wk, jnp.zeros((kv_lora, N_HEADS, ROPE_DIM), F32)).astype(BF16),
        wv_t=p["w_uv"][l].T.astype(BF16),
        gkh=_pad_cols(gk, HEAD_PAD).reshape(1, HEAD_PAD),
        score_bound=QK_DIM * jnp.max(jnp.abs(gq)) * jnp.max(jnp.abs(gk)) * q_scale,
        w_out=p["w_out"][l].astype(BF16), gmem=p["norm_mem"][l].reshape(1, D),
        w_mq=p["w_mq"][l].astype(BF16), gmqh=p["mq_head_norm"][l].reshape(1, -1),
        w_mo=p["w_mo"][l].astype(BF16), gffn=p["norm_ffn"][l].reshape(1, D),
    )


def _rope_tables(pos):
    half = ROPE_DIM // 2
    inv_freq = ROPE_BASE ** (-jnp.arange(half, dtype=F32) / half)
    ang = pos.astype(F32)[:, None] * inv_freq[None, :]
    cos2 = jnp.concatenate([jnp.cos(ang), jnp.cos(ang)], axis=-1)
    sin2 = jnp.concatenate([jnp.sin(ang), jnp.sin(ang)], axis=-1)
    n = pos.shape[0]
    ones, zeros = jnp.ones((n, NOPE_DIM), F32), jnp.zeros((n, NOPE_DIM), F32)
    return dict(cos_k=_rope_lanes(cos2), sin_k=_rope_lanes(sin2),
                cos_q=_pad_cols(jnp.concatenate([ones, cos2], axis=-1), HEAD_PAD).T,
                sin_q=_pad_cols(jnp.concatenate([zeros, sin2], axis=-1), HEAD_PAD).T)


def _layer(l, x, tabs, q_off, conv_prev, h0, ckv_past, kr_past, mem_k, mem_v, w, p):
    B, L, D = x.shape
    prev8 = jnp.pad(conv_prev, ((0, 0), (SUBLANES - (CONV_W - 1), 0), (0, 0)))
    h0_8 = jnp.pad(h0[:, None, :], ((0, 0), (SUBLANES - 1, 0), (0, 0)))
    grec, gatt, cq, ckv, kr, ctail, htail = _inproj(x, prev8, h0_8, w, tabs["cos_k"], tabs["sin_k"])

    q = _qprep(cq, w, tabs["cos_q"], tabs["sin_q"])
    if ckv_past is None:
        ckv_all, kr_all = ckv, kr
    else:
        ckv_all = jnp.concatenate([ckv_past, ckv], axis=1)
        kr_all = jnp.concatenate([_rope_lanes(kr_past), kr], axis=1)
    t_valid = ckv_all.shape[1]
    tk = min(512, t_valid)
    t_pad = -(-t_valid // tk) * tk
    if t_pad != t_valid:
        ckv_all = jnp.pad(ckv_all, ((0, 0), (0, t_pad - t_valid), (0, 0)))
        kr_all = jnp.pad(kr_all, ((0, 0), (0, t_pad - t_valid), (0, 0)))
    k, vt = _kvprep(ckv_all, kr_all, w, tk)
    flash = functools.partial(_flash, q_off=q_off, t_valid=t_valid, tk=tk)
    att = lax.cond(w["score_bound"] <= SAFE_LOG2_RANGE, functools.partial(flash, bounded=True),
                   functools.partial(flash, bounded=False), q, k, vt)

    x2 = _post(x, grec, gatt, att, mem_k, mem_v, w).reshape(B * L, D)
    j = l // 2
    if l % 2 == 0:
        x3 = _ffn(x2, w["gffn"], p["ffn_w1"][j].astype(BF16), p["ffn_w3"][j].astype(BF16),
                  p["ffn_w2"][j].astype(BF16))
    else:
        E = p["moe_router"].shape[-1]
        wr = _pad_cols(p["moe_router"][j], LANES)
        wr_hi = wr.astype(BF16)
        wr_lo = (wr - wr_hi.astype(F32)).astype(BF16)
        br = jnp.concatenate([p["moe_router_b"][j], jnp.full((LANES - E,), NEG, F32)]).reshape(1, LANES)
        x3 = _moe(x2, w["gffn"], wr_hi, wr_lo, br, p["moe_w1"][j].astype(BF16),
                  p["moe_w3"][j].astype(BF16), p["moe_w2"][j].astype(BF16))
    conv_state = ctail[:, SUBLANES - (CONV_W - 1):, :]
    return x3.reshape(B, L, D), conv_state, htail[:, SUBLANES - 1, :], ckv, kr[..., NOPE_DIM:QK_DIM]


def kernel(x_prompt, x_sample, cache_ckv, cache_krope, cache_mem_k, cache_mem_v, state_lru, state_conv, mem_prompt, norm_mix, w_in, conv_w, conv_b, lru_wa, lru_ba, lru_wi, lru_bi, lru_lambda, q_lat_norm, w_uq, kv_lat_norm, w_uk, w_uv, q_head_norm, k_head_norm, w_out, norm_mem, mem_in_norm, w_mq, w_mk, w_mv, w_mo, mq_head_norm, mk_head_norm, norm_ffn, ffn_w1, ffn_w3, ffn_w2, moe_router, moe_router_b, moe_w1, moe_w3, moe_w2):
    p = dict(norm_mix=norm_mix, w_in=w_in, conv_w=conv_w, conv_b=conv_b, lru_wa=lru_wa, lru_ba=lru_ba,
             lru_wi=lru_wi, lru_bi=lru_bi, lru_lambda=lru_lambda, q_lat_norm=q_lat_norm, w_uq=w_uq,
             kv_lat_norm=kv_lat_norm, w_uk=w_uk, w_uv=w_uv, q_head_norm=q_head_norm,
             k_head_norm=k_head_norm, w_out=w_out, norm_mem=norm_mem, w_mq=w_mq, w_mo=w_mo,
             mq_head_norm=mq_head_norm, norm_ffn=norm_ffn, ffn_w1=ffn_w1, ffn_w3=ffn_w3,
             ffn_w2=ffn_w2, moe_router=moe_router, moe_router_b=moe_router_b, moe_w1=moe_w1,
             moe_w3=moe_w3, moe_w2=moe_w2)
    depth = w_in.shape[0]
    weights = [_layer_weights(l, p) for l in range(depth)]

    Bp, Lp, D = x_prompt.shape
    Bs, Ls, _ = x_sample.shape
    t_past = cache_ckv.shape[2]
    M = mem_prompt.shape[1]

    tabs_p = _rope_tables(jnp.arange(Lp, dtype=jnp.int32))
    zero_conv = jnp.zeros((Bp, CONV_W - 1, D), F32)
    zero_h = jnp.zeros((Bp, D), F32)
    x = x_prompt
    outs_p = [[] for _ in range(6)]
    for l in range(depth):
        mk, mv = _memkv(mem_prompt, mem_in_norm[l], w_mk[l], w_mv[l], mk_head_norm[l])
        x, cs, hl, ckv, kr = _layer(l, x, tabs_p, 0, zero_conv, zero_h, None, None, mk, mv, weights[l], p)
        for acc, val in zip(outs_p, (ckv, kr, hl, cs, mk.reshape(Bp, M, MEM_HEADS, -1),
                                     mv.reshape(Bp, M, MEM_HEADS, -1))):
            acc.append(val)
    y_prompt = x

    tabs_s = _rope_tables(t_past + jnp.arange(Ls, dtype=jnp.int32))
    x = x_sample
    outs_s = [[] for _ in range(4)]
    for l in range(depth):
        x, cs, hl, ckv, kr = _layer(l, x, tabs_s, t_past, state_conv[l], state_lru[l], cache_ckv[l],
                                    cache_krope[l], cache_mem_k[l].reshape(Bs, M, D),
                                    cache_mem_v[l].reshape(Bs, M, D), weights[l], p)
        for acc, val in zip(outs_s, (ckv, kr, hl, cs)):
            acc.append(val)
    y_sample = x

    return (y_prompt, y_sample) + tuple(jnp.stack(a) for a in outs_p) + tuple(jnp.stack(a) for a in outs_s)
```

```python
import functools
import math

import jax
import jax.numpy as jnp
from jax import lax
from jax.experimental import pallas as pl
from jax.experimental.pallas import tpu as pltpu

F32 = jnp.float32
BF16 = jnp.bfloat16

EPS = 1e-6
CHUNK = 64
LRU_BLOCKS = 16
LRU_C = 8.0
CONV_W = 4
N_HEADS = 16
NOPE_DIM = 64
ROPE_DIM = 32
QK_DIM = NOPE_DIM + ROPE_DIM
ROPE_BASE = 10000.0
MEM_HEADS = 4
TOP_K = 2

LANES = 128
SUBLANES = 8
MXU_DIM = 256
VMEM_LIMIT = 56 * 1024 * 1024

HEAD_PAD = LANES
ROUTER_ROWS = 16
MOE_CHUNK = 160
SAFE_LOG2_RANGE = 60.0
FLASH_HEADS = 8
FLASH_LAG = 8
NEG = -1e30
LOG2E = 1.4426950408889634


def _resident(shape):
    nd = len(shape)
    return pl.BlockSpec(shape, lambda *_: (0,) * nd, pipeline_mode=pl.Buffered(1))


def _params(n_axes):
    return pltpu.CompilerParams(dimension_semantics=("arbitrary",) * n_axes,
                                vmem_limit_bytes=VMEM_LIMIT)


def _rms(x, gain):
    return x * lax.rsqrt(jnp.mean(x * x, axis=-1, keepdims=True) + EPS) * gain


def _dot(a, b):
    return jnp.dot(a, b, preferred_element_type=F32)


def _dot_nt(a, b):
    return lax.dot_general(a, b, (((1,), (1,)), ((), ())), preferred_element_type=F32)


def _token_tile(n, cap=512):
    t = min(n, cap)
    assert n % t == 0 and t % SUBLANES == 0
    return t


def _memkv_kernel(mem_ref, gin_ref, wk_ref, wv_ref, gk_ref, k_ref, v_ref):
    mn = _rms(mem_ref[0], gin_ref[...]).astype(BF16)
    k = _dot(mn, wk_ref[...])
    hd = gk_ref.shape[-1]
    for h in range(MEM_HEADS):
        sl = slice(h * hd, (h + 1) * hd)
        k_ref[0, :, sl] = _rms(k[:, sl], gk_ref[...])
    v_ref[0] = _dot(mn, wv_ref[...])


def _memkv(mem, g_in, w_k, w_v, g_k):
    B, M, D = mem.shape
    tok = pl.BlockSpec((1, M, D), lambda b: (b, 0, 0))
    return pl.pallas_call(
        _memkv_kernel,
        grid=(B,),
        in_specs=[tok, _resident((1, D)), _resident((D, D)), _resident((D, D)),
                  _resident((1, D // MEM_HEADS))],
        out_specs=[tok, tok],
        out_shape=[jax.ShapeDtypeStruct((B, M, D), F32)] * 2,
        compiler_params=_params(1),
        name="memkv",
    )(mem, g_in.reshape(1, D), w_k.astype(BF16), w_v.astype(BF16), g_k.reshape(1, -1))


def _inproj_kernel(x_ref, prev_ref, h0_ref, gmix_ref, win_ref, cw_ref, cb_ref, wa_ref, ba_ref,
                   wi_ref, bi_ref, sp_ref, gq_ref, gkv_ref, cos_ref, sin_ref,
                   grec_ref, gatt_ref, cq_ref, ckv_ref, kr_ref, ctail_ref, htail_ref,
                   xext_ref, hcar_ref, *, tm, d, q_lora, kv_lora):
    @pl.when(pl.program_id(1) == 0)
    def _():
        xext_ref[0:SUBLANES, :] = prev_ref[0]
        hcar_ref[...] = h0_ref[0]

    hn = _rms(x_ref[0], gmix_ref[...]).astype(BF16)

    def proj(c0, c1):
        return _dot(hn, win_ref[:, c0:c1])

    x_rec = proj(0, d)
    xext_ref[SUBLANES:SUBLANES + tm, :] = x_rec
    xc = cb_ref[...] + cw_ref[CONV_W - 1:CONV_W, :] * x_rec
    for j in range(CONV_W - 1):
        off = SUBLANES - (CONV_W - 1) + j
        xc = xc + cw_ref[j:j + 1, :] * xext_ref[off:off + tm, :]
    tail = x_rec[tm - SUBLANES:tm, :]
    ctail_ref[0] = tail
    xext_ref[0:SUBLANES, :] = tail

    xcb = xc.astype(BF16)
    nb = d // MXU_DIM
    gr = jnp.concatenate(
        [_dot(xcb[:, c * MXU_DIM:(c + 1) * MXU_DIM], wa_ref[c]) for c in range(nb)], axis=1)
    gi = jnp.concatenate(
        [_dot(xcb[:, c * MXU_DIM:(c + 1) * MXU_DIM], wi_ref[c]) for c in range(nb)], axis=1)
    r = jax.nn.sigmoid(gr + ba_ref[...])
    ig = jax.nn.sigmoid(gi + bi_ref[...])
    log_a = (-LRU_C) * r * sp_ref[...]
    a = jnp.exp(log_a)
    u = jnp.sqrt(1.0 - a * a) * (ig * xc)

    row = lax.broadcasted_iota(jnp.int32, (tm, 1), 0) & (SUBLANES - 1)
    s = 1
    while s < SUBLANES:
        keep = row >= s
        u = a * jnp.where(keep, pltpu.roll(u, s, 0), 0.0) + u
        a = a * jnp.where(keep, pltpu.roll(a, s, 0), 1.0)
        s *= 2
    h_prev = hcar_ref[SUBLANES - 1:SUBLANES, :]
    groups = []
    for g in range(tm // SUBLANES):
        rows = slice(g * SUBLANES, (g + 1) * SUBLANES)
        hg = a[rows] * h_prev + u[rows]
        groups.append(hg)
        h_prev = hg[SUBLANES - 1:SUBLANES, :]
    h = jnp.concatenate(groups, axis=0)
    htail = groups[-1]
    hcar_ref[...] = htail
    htail_ref[0] = htail

    x_gate = proj(d, 2 * d)
    g_rec = proj(2 * d, 3 * d)
    grec_ref[0] = jax.nn.sigmoid(g_rec) * (jax.nn.gelu(x_gate) * h)
    gatt_ref[0] = proj(3 * d, 4 * d)

    o = 4 * d
    cq_ref[0] = _rms(proj(o, o + q_lora), gq_ref[...]).astype(BF16)
    o += q_lora
    ckv_ref[0] = _rms(proj(o, o + kv_lora), gkv_ref[...])
    o += kv_lora
    kr = proj(o, o + LANES) * cos_ref[...] + proj(o + LANES, o + 2 * LANES) * sin_ref[...]
    kr_ref[0] = kr


def _inproj(x, prev8, h0_8, w, cos_k, sin_k):
    B, L, D = x.shape
    tm = _token_tile(L, 256)
    q_lora, kv_lora = w["gq"].shape[-1], w["gkv"].shape[-1]
    ncol = w["w_in"].shape[-1]
    tok = lambda n: pl.BlockSpec((1, tm, n), lambda b, l: (b, l, 0))
    per_b = pl.BlockSpec((1, SUBLANES, D), lambda b, l: (b, 0, 0))
    tab = pl.BlockSpec((tm, LANES), lambda b, l: (l, 0))
    nb = D // MXU_DIM
    kern = functools.partial(_inproj_kernel, tm=tm, d=D, q_lora=q_lora, kv_lora=kv_lora)
    return pl.pallas_call(
        kern,
        grid=(B, L // tm),
        in_specs=[tok(D), per_b, per_b, _resident((1, D)), _resident((D, ncol)),
                  _resident((CONV_W, D)), _resident((1, D)),
                  _resident((nb, MXU_DIM, MXU_DIM)), _resident((1, D)),
                  _resident((nb, MXU_DIM, MXU_DIM)), _resident((1, D)), _resident((1, D)),
                  _resident((1, q_lora)), _resident((1, kv_lora)), tab, tab],
        out_specs=[tok(D), tok(D), tok(q_lora), tok(kv_lora), tok(HEAD_PAD), per_b, per_b],
        out_shape=[jax.ShapeDtypeStruct((B, L, D), F32), jax.ShapeDtypeStruct((B, L, D), F32),
                   jax.ShapeDtypeStruct((B, L, q_lora), BF16),
                   jax.ShapeDtypeStruct((B, L, kv_lora), F32),
                   jax.ShapeDtypeStruct((B, L, HEAD_PAD), F32),
                   jax.ShapeDtypeStruct((B, SUBLANES, D), F32),
                   jax.ShapeDtypeStruct((B, SUBLANES, D), F32)],
        scratch_shapes=[pltpu.VMEM((SUBLANES + tm, D), F32), pltpu.VMEM((SUBLANES, D), F32)],
        compiler_params=_params(2),
        name="inproj",
    )(x, prev8, h0_8, w["gmix"], w["w_in"], w["conv_w"], w["conv_b"], w["wa"], w["ba"],
      w["wi"], w["bi"], w["sp"], w["gq"], w["gkv"], cos_k, sin_k)


def _qprep_kernel(cq_ref, wt_ref, cos_ref, sin_ref, g_ref, q_ref):
    qt = _dot_nt(wt_ref[...], cq_ref[0])
    half = ROPE_DIM // 2
    tm = qt.shape[1]
    for h in range(N_HEADS):
        blk = qt[h * HEAD_PAD:(h + 1) * HEAD_PAD, :]
        x1 = blk[NOPE_DIM:NOPE_DIM + half, :]
        x2 = blk[NOPE_DIM + half:QK_DIM, :]
        rot = jnp.concatenate([jnp.zeros((NOPE_DIM, tm), F32), -x2, x1,
                               jnp.zeros((HEAD_PAD - QK_DIM, tm), F32)], axis=0)
        qh = blk * cos_ref[...] + rot * sin_ref[...]
        ss = jnp.sum(qh * qh, axis=0, keepdims=True) * (1.0 / QK_DIM)
        q_ref[0, h * HEAD_PAD:(h + 1) * HEAD_PAD, :] = (qh * lax.rsqrt(ss + EPS) * g_ref[...]).astype(BF16)


def _qprep(cq, w, cos_q, sin_q):
    B, L, QL = cq.shape
    tm = _token_tile(L)
    n = N_HEADS * HEAD_PAD
    tab = pl.BlockSpec((HEAD_PAD, tm), lambda b, l: (0, l))
    return pl.pallas_call(
        _qprep_kernel,
        grid=(B, L // tm),
        in_specs=[pl.BlockSpec((1, tm, QL), lambda b, l: (b, l, 0)),
                  _resident((n, QL)), tab, tab, _resident((HEAD_PAD, 1))],
        out_specs=pl.BlockSpec((1, n, tm), lambda b, l: (b, 0, l)),
        out_shape=jax.ShapeDtypeStruct((B, n, L), BF16),
        compiler_params=_params(2),
        name="qprep",
    )(cq, w["wq_t"], cos_q, sin_q, w["gqh"])


def _kvprep_kernel(*refs, n_past, tk):
    if n_past:
        past_c_ref, past_kr_ref, new_c_ref, new_kr_ref, wk_ref, wvt_ref, g_ref, k_ref, vt_ref = refs
        is_new = pl.program_id(1) >= n_past

        def pick(new_ref, past_ref):
            new = new_ref[0]
            fill = jnp.zeros((tk - new.shape[0], new.shape[1]), new.dtype)
            return jnp.where(is_new, jnp.concatenate([new, fill], axis=0), past_ref[0, 0])

        c32, kr = pick(new_c_ref, past_c_ref), pick(new_kr_ref, past_kr_ref)
    else:
        new_c_ref, new_kr_ref, wk_ref, wvt_ref, g_ref, k_ref, vt_ref = refs
        c32, kr = new_c_ref[0], new_kr_ref[0]
    c = c32.astype(BF16)
    k = _dot(c, wk_ref[...])
    for h in range(N_HEADS):
        sl = slice(h * HEAD_PAD, (h + 1) * HEAD_PAD)
        kh = k[:, sl] + kr
        ss = jnp.sum(kh * kh, axis=-1, keepdims=True) * (1.0 / QK_DIM)
        k_ref[0, :, sl] = (kh * lax.rsqrt(ss + EPS) * g_ref[...]).astype(BF16)
    vt_ref[0, 0] = _dot_nt(wvt_ref[...], c).astype(BF16)


def _kvprep(ckv, kr, w, tk, past=None):
    B, L, KV = ckv.shape
    n = N_HEADS * HEAD_PAD
    dv = w["wv_t"].shape[0]
    if past is None:
        assert L % tk == 0
        n_past, n_tiles = 0, L // tk
        new = lambda width: pl.BlockSpec((1, tk, width), lambda b, t: (b, t, 0))
        in_specs, args = [new(KV), new(HEAD_PAD)], (ckv, kr)
    else:
        cache_c, cache_kr, layer = past
        t_past = cache_c.shape[2]
        assert t_past % tk == 0 and L <= tk and L % SUBLANES == 0
        n_past = t_past // tk
        n_tiles = n_past + 1
        old = lambda width: pl.BlockSpec(
            (1, 1, tk, width), lambda b, t: (layer, b, jnp.minimum(t, n_past - 1), 0))
        new = lambda width: pl.BlockSpec((1, L, width), lambda b, t: (b, 0, 0))
        in_specs, args = [old(KV), old(HEAD_PAD), new(KV), new(HEAD_PAD)], (cache_c, cache_kr, ckv, kr)
    return pl.pallas_call(
        functools.partial(_kvprep_kernel, n_past=n_past, tk=tk),
        grid=(B, n_tiles),
        in_specs=in_specs + [_resident((KV, n)), _resident((dv, KV)), _resident((1, HEAD_PAD))],
        out_specs=[pl.BlockSpec((1, tk, n), lambda b, t: (b, t, 0)),
                   pl.BlockSpec((1, 1, dv, tk), lambda b, t: (b, t, 0, 0))],
        out_shape=[jax.ShapeDtypeStruct((B, n_tiles * tk, n), BF16),
                   jax.ShapeDtypeStruct((B, n_tiles, dv, tk), BF16)],
        compiler_params=_params(2),
        name="kvprep",
    )(*args, w["wk"], w["wv_t"], w["gkh"])


def _flash_kernel(q_ref, k_ref, vt_ref, o_ref, m_ref, l_ref, acc_ref, *, tq, tk, q_off, t_valid,
                  v_dim, heads, bounded, lag, split_diag):
    q_start = q_off + pl.program_id(2) * tq
    n_full = jnp.minimum(q_start, t_valid) // tk
    n_end = (jnp.minimum(q_start + tq, t_valid) + tk - 1) // tk
    shift = int(math.log2(CHUNK))

    def chunk_of(pos):
        return lax.shift_right_logical(pos, shift)

    m_ref[...] = jnp.full(m_ref.shape, NEG, F32)
    l_ref[...] = jnp.zeros(l_ref.shape, F32)
    acc_ref[...] = jnp.zeros(acc_ref.shape, F32)

    def accumulate(hh, s, vj, q0):
        lanes = slice(q0, tq)
        if bounded:
            p = jnp.exp2(s)
            l_ref[hh, :, lanes] += jnp.sum(p, axis=0, keepdims=True)
        else:
            m = m_ref[hh, :, lanes]
            m_new = jnp.maximum(m, jnp.max(s, axis=0, keepdims=True))
            alpha = jnp.exp2(m - m_new)
            p = jnp.exp2(s - m_new)
            l_ref[hh, :, lanes] = alpha * l_ref[hh, :, lanes] + jnp.sum(p, axis=0, keepdims=True)
            m_ref[hh, :, lanes] = m_new
            acc_ref[hh, :, lanes] = alpha * acc_ref[hh, :, lanes]
        acc_ref[hh, :, lanes] += _dot(vj, p.astype(BF16))

    def tile(j, masked):
        if masked:
            k_pos = j * tk + lax.broadcasted_iota(jnp.int32, (tk, 1), 0)
            q_pos = q_start + lax.broadcasted_iota(jnp.int32, (1, tq), 1)
            vis = (chunk_of(k_pos) <= chunk_of(q_pos)) & (k_pos < t_valid)
        pending = {}
        for t in range(heads + lag):
            if t < heads:
                qh = q_ref[0, t * HEAD_PAD:(t + 1) * HEAD_PAD, :]
                kj = k_ref[0, pl.ds(pl.multiple_of(j * tk, tk), tk), t * HEAD_PAD:(t + 1) * HEAD_PAD]
                pending[t] = _dot(kj, qh)
            if t >= lag:
                hh = t - lag
                s = pending.pop(hh)
                if masked:
                    s = jnp.where(vis, s, NEG)
                accumulate(hh, s, vt_ref[0, j, hh * v_dim:(hh + 1) * v_dim, :], 0)

    def diagonal(j):
        half = tk // 2
        vis = (chunk_of(lax.broadcasted_iota(jnp.int32, (half, 1), 0))
               <= chunk_of(lax.broadcasted_iota(jnp.int32, (1, tq), 1)))
        for k0, q0 in ((0, 0), (half, half)):
            for hh in range(heads):
                qh = q_ref[0, hh * HEAD_PAD:(hh + 1) * HEAD_PAD, q0:]
                kj = k_ref[0, pl.ds(pl.multiple_of(j * tk + k0, half), half),
                           hh * HEAD_PAD:(hh + 1) * HEAD_PAD]
                s = jnp.where(vis[:, :tq - q0], _dot(kj, qh), NEG)
                accumulate(hh, s, vt_ref[0, j, hh * v_dim:(hh + 1) * v_dim, k0:k0 + half], q0)

    def loop(lo, hi, body):
        lax.fori_loop(lo, hi, lambda j, c: (body(j), c)[1], 0)

    loop(0, n_full, lambda j: tile(j, False))
    if split_diag:
        diagonal(n_full)
    else:
        loop(n_full, n_end, lambda j: tile(j, True))
    o_ref[0] = jnp.concatenate([acc_ref[hh] / l_ref[hh] for hh in range(heads)], axis=0).T


def _flash(q, k, vt, *, q_off, t_valid, tk, bounded, heads=FLASH_HEADS):
    B, n, L = q.shape
    T = k.shape[1]
    dv = vt.shape[2]
    v_dim = dv // N_HEADS
    tq = _token_tile(L)
    assert tq % CHUNK == 0 and q_off % CHUNK == 0 and (heads * v_dim) % LANES == 0
    split_diag = tq == tk and q_off == 0 and t_valid == T and (tk // 2) % max(CHUNK, LANES) == 0
    kern = functools.partial(_flash_kernel, tq=tq, tk=tk, q_off=q_off, t_valid=t_valid, v_dim=v_dim,
                             heads=heads, bounded=bounded, lag=min(FLASH_LAG, heads),
                             split_diag=split_diag)
    return pl.pallas_call(
        kern,
        grid=(B, N_HEADS // heads, L // tq),
        in_specs=[pl.BlockSpec((1, heads * HEAD_PAD, tq), lambda b, p, i: (b, p, i)),
                  pl.BlockSpec((1, T, heads * HEAD_PAD), lambda b, p, i: (b, 0, p)),
                  pl.BlockSpec((1, T // tk, heads * v_dim, tk), lambda b, p, i: (b, 0, p, 0))],
        out_specs=pl.BlockSpec((1, tq, heads * v_dim), lambda b, p, i: (b, i, p)),
        out_shape=jax.ShapeDtypeStruct((B, L, dv), F32),
        scratch_shapes=[pltpu.VMEM((heads, 1, tq), F32), pltpu.VMEM((heads, 1, tq), F32),
                        pltpu.VMEM((heads, v_dim, tq), F32)],
        compiler_params=_params(3),
        name="flash",
    )(q, k, vt)


def _post_kernel(x_ref, grec_ref, gatt_ref, att_ref, wo_ref, gmem_ref, wmq_ref, gqh_ref,
                 mk_ref, mv_ref, wmo_ref, o_ref):
    mixed = (grec_ref[0] + jax.nn.sigmoid(gatt_ref[0]) * att_ref[0]).astype(BF16)
    x1 = x_ref[0] + _dot(mixed, wo_ref[...])
    qm = _dot(_rms(x1, gmem_ref[...]).astype(BF16), wmq_ref[...])
    hd = gqh_ref.shape[-1]
    outs = []
    for h in range(MEM_HEADS):
        sl = slice(h * hd, (h + 1) * hd)
        qh = (_rms(qm[:, sl], gqh_ref[...]) * (hd ** -0.5)).astype(BF16)
        s = _dot_nt(qh, mk_ref[0, :, sl])
        p = jnp.exp(s - jnp.max(s, axis=-1, keepdims=True))
        l = jnp.sum(p, axis=-1, keepdims=True)
        outs.append((_dot(p.astype(BF16), mv_ref[0, :, sl]) / l).astype(BF16))
    o_ref[0] = x1 + _dot(jnp.concatenate(outs, axis=1), wmo_ref[...])


def _post(x, grec, gatt, att, mem_k, mem_v, w):
    B, L, D = x.shape
    M = mem_k.shape[1]
    tm = _token_tile(L)
    tok = pl.BlockSpec((1, tm, D), lambda b, l: (b, l, 0))
    mem = pl.BlockSpec((1, M, D), lambda b, l: (b, 0, 0))
    return pl.pallas_call(
        _post_kernel,
        grid=(B, L // tm),
        in_specs=[tok, tok, tok, tok, _resident((D, D)), _resident((1, D)), _resident((D, D)),
                  _resident((1, D // MEM_HEADS)), mem, mem, _resident((D, D))],
        out_specs=tok,
        out_shape=jax.ShapeDtypeStruct((B, L, D), F32),
        compiler_params=_params(2),
        name="post",
    )(x, grec, gatt, att, w["w_out"], w["gmem"], w["w_mq"], w["gmqh"],
      mem_k.astype(BF16), mem_v.astype(BF16), w["w_mo"])


def _swiglu_mid(a, b):
    return (a * jax.nn.sigmoid(a) * b).astype(BF16)


def _ffn_kernel(x_ref, g_ref, w1_ref, w3_ref, w2_ref, o_ref, *, n_split):
    x = x_ref[...]
    hf = _rms(x, g_ref[...]).astype(BF16)
    fc = w1_ref.shape[1] // n_split
    acc = x
    for c in range(n_split):
        sl = slice(c * fc, (c + 1) * fc)
        acc = acc + _dot(_swiglu_mid(_dot(hf, w1_ref[:, sl]), _dot(hf, w3_ref[:, sl])), w2_ref[sl, :])
    o_ref[...] = acc


def _ffn(x, g, w1, w3, w2):
    N, D = x.shape
    F = w1.shape[1]
    tm = _token_tile(N)
    n_split = 2
    assert F % (n_split * LANES) == 0
    tok = pl.BlockSpec((tm, D), lambda i: (i, 0))
    return pl.pallas_call(
        functools.partial(_ffn_kernel, n_split=n_split),
        grid=(N // tm,),
        in_specs=[tok, _resident((1, D)), _resident((D, F)), _resident((D, F)), _resident((F, D))],
        out_specs=tok,
        out_shape=jax.ShapeDtypeStruct((N, D), F32),
        compiler_params=_params(1),
        name="ffn",
    )(x, g, w1, w3, w2)


def _moe_kernel(x_ref, g_ref, wrh_ref, wrl_ref, br_ref, w1_ref, w3_ref, w2_ref, o_ref,
                hf_ref, gate_ref, rank_ref, *, ts, cap, n_exp):
    e = pl.program_id(1)
    lane = lax.broadcasted_iota(jnp.int32, (1, LANES), 1)
    n_sub = hf_ref.shape[0] // ts

    @pl.when(e == 0)
    def _():
        x = x_ref[...]
        hf = _rms(x, g_ref[...])
        hi = hf.astype(BF16)
        lo = (hf - hi.astype(F32)).astype(BF16)
        logits = (_dot_nt(wrh_ref[...], hi) + _dot_nt(wrh_ref[...], lo) + _dot_nt(wrl_ref[...], hi)
                  + br_ref[...])
        sub = lax.broadcasted_iota(jnp.int32, (ROUTER_ROWS, 1), 0)
        m1 = jnp.max(logits, axis=0, keepdims=True)
        i1 = jnp.min(jnp.where(logits == m1, sub, ROUTER_ROWS), axis=0, keepdims=True)
        rest = jnp.where(sub == i1, NEG, logits)
        m2 = jnp.max(rest, axis=0, keepdims=True)
        i2 = jnp.min(jnp.where(rest == m2, sub, ROUTER_ROWS), axis=0, keepdims=True)
        e2 = jnp.exp(m2 - m1)
        den = 1.0 + e2
        gate_t = jnp.where(sub == i1, 1.0 / den, 0.0) + jnp.where(sub == i2, e2 / den, 0.0)
        tm = gate_t.shape[1]
        gate_ref[...] = jnp.concatenate(
            [gate_t, jnp.zeros((LANES - ROUTER_ROWS, tm), F32)], axis=0).T
        hf_ref[...] = hi
        o_ref[...] = x
        sel_t = ((sub == i1) | (sub == i2))[:n_exp, :]
        before = (lax.broadcasted_iota(jnp.int32, (ts, ts), 0)
                  < lax.broadcasted_iota(jnp.int32, (ts, ts), 1))
        tri = jnp.where(before, 1.0, 0.0).astype(BF16)
        for s in range(n_sub):
            sel_s = sel_t[:, s * ts:(s + 1) * ts]
            rank = _dot(jnp.where(sel_s, 1.0, 0.0).astype(BF16), tri)
            rank_ref[s] = jnp.where(sel_s, rank, -1.0)

    row = lax.broadcasted_iota(jnp.int32, (cap, 1), 0)
    for s in range(n_sub):
        rows = slice(s * ts, (s + 1) * ts)
        rank_e = rank_ref[s, pl.ds(e, 1), :]
        n_tok = jnp.max(rank_e).astype(jnp.int32) + 1
        ge = jnp.sum(jnp.where(lane == e, gate_ref[rows, :], 0.0), axis=-1, keepdims=True)

        def chunk(c, carry):
            want = (row + c * cap).astype(F32)
            onehot = jnp.where(rank_e == want, 1.0, 0.0).astype(BF16)
            xg = _dot(onehot, hf_ref[rows, :]).astype(BF16)
            y = _dot(_swiglu_mid(_dot(xg, w1_ref[0]), _dot(xg, w3_ref[0])), w2_ref[0])
            back = lax.dot_general(onehot, y.astype(BF16), (((0,), (0,)), ((), ())),
                                   preferred_element_type=F32)
            o_ref[rows, :] += ge * back
            return carry

        lax.fori_loop(0, (n_tok + cap - 1) // cap, chunk, 0)


def _moe(x, g, wr_hi, wr_lo, br, w1, w3, w2):
    N, D = x.shape
    E, _, F = w1.shape
    tm = _token_tile(N, 1024)
    ts = _token_tile(tm, 512)
    assert E <= SUBLANES
    tok = pl.BlockSpec((tm, D), lambda i, e: (i, 0))
    kern = functools.partial(_moe_kernel, ts=ts, cap=MOE_CHUNK, n_exp=E)
    return pl.pallas_call(
        kern,
        grid=(N // tm, E),
        in_specs=[tok, _resident((1, D)), _resident((ROUTER_ROWS, D)), _resident((ROUTER_ROWS, D)),
                  _resident((ROUTER_ROWS, 1)),
                  pl.BlockSpec((1, D, F), lambda i, e: (e, 0, 0)),
                  pl.BlockSpec((1, D, F), lambda i, e: (e, 0, 0)),
                  pl.BlockSpec((1, F, D), lambda i, e: (e, 0, 0))],
        out_specs=tok,
        out_shape=jax.ShapeDtypeStruct((N, D), F32),
        scratch_shapes=[pltpu.VMEM((tm, D), BF16), pltpu.VMEM((tm, LANES), F32),
                        pltpu.VMEM((tm // ts, E, ts), F32)],
        compiler_params=_params(2),
        name="moe",
    )(x, g, wr_hi, wr_lo, br, w1, w3, w2)


def _rot_half_cols(w):
    half = ROPE_DIM // 2
    return jnp.concatenate([-w[..., half:], w[..., :half]], axis=-1)


def _rope_lanes(a):
    return jnp.pad(a, [(0, 0)] * (a.ndim - 1) + [(NOPE_DIM, HEAD_PAD - QK_DIM)])


def _pad_cols(w, n):
    return jnp.pad(w, [(0, 0)] * (w.ndim - 1) + [(0, n - w.shape[-1])])


def _head_pad(nope, rope):
    z = jnp.zeros(nope.shape[:-1] + (HEAD_PAD - QK_DIM,), nope.dtype)
    out = jnp.concatenate([nope, rope, z], axis=-1)
    return out.reshape(out.shape[:-2] + (N_HEADS * HEAD_PAD,))


def _block_diag_tiles(w):
    nblk, bw, _ = w.shape
    per = MXU_DIM // bw
    w4 = w.reshape(nblk // per, per, bw, bw)
    eye = jnp.eye(per, dtype=w.dtype)
    return jnp.einsum("cpij,pq->cpiqj", w4, eye).reshape(nblk // per, MXU_DIM, MXU_DIM)


def _layer_weights(l, p):
    D = p["w_in"].shape[1]
    q_lora = p["q_lat_norm"].shape[-1]
    kv_lora = p["kv_lat_norm"].shape[-1]
    w_in = p["w_in"][l]
    sp = (D, 2 * D, 2 * D + q_lora, 2 * D + q_lora + kv_lora, 2 * D + q_lora + kv_lora + ROPE_DIM,
          3 * D + q_lora + kv_lora + ROPE_DIM)
    x_rec, x_gate, c_q, c_kv, k_rot, g_rec, g_att = jnp.split(w_in, sp, axis=-1)
    w_in_perm = jnp.concatenate(
        [x_rec, x_gate, g_rec, g_att, c_q, c_kv, _rope_lanes(k_rot),
         _rope_lanes(_rot_half_cols(k_rot))], axis=-1).astype(BF16)

    wq = p["w_uq"][l].reshape(q_lora, N_HEADS, QK_DIM)
    wq_n, wq_r = wq[..., :NOPE_DIM], wq[..., NOPE_DIM:]
    wk = p["w_uk"][l].reshape(kv_lora, N_HEADS, NOPE_DIM)
    gq, gk = p["q_head_norm"][l], p["k_head_norm"][l]
    q_scale = (QK_DIM ** -0.5) * LOG2E
    return dict(
        gmix=p["norm_mix"][l].reshape(1, D), w_in=w_in_perm,
        conv_w=p["conv_w"][l], conv_b=p["conv_b"][l].reshape(1, D),
        wa=_block_diag_tiles(p["lru_wa"][l]).astype(BF16), ba=p["lru_ba"][l].reshape(1, D),
        wi=_block_diag_tiles(p["lru_wi"][l]).astype(BF16), bi=p["lru_bi"][l].reshape(1, D),
        sp=jax.nn.softplus(-p["lru_lambda"][l]).reshape(1, D),
        gq=p["q_lat_norm"][l].reshape(1, q_lora), gkv=p["kv_lat_norm"][l].reshape(1, kv_lora),
        wq_t=_head_pad(wq_n, wq_r).T.astype(BF16),
        gqh=(_pad_cols(gq, HEAD_PAD) * q_scale).reshape(HEAD_PAD, 1),
        wk=_head_pad(wk, jnp.zeros((kv_lora, N_HEADS, ROPE_DIM), F32)).astype(BF16),
        wv_t=p["w_uv"][l].T.astype(BF16),
        gkh=_pad_cols(gk, HEAD_PAD).reshape(1, HEAD_PAD),
        score_bound=QK_DIM * jnp.max(jnp.abs(gq)) * jnp.max(jnp.abs(gk)) * q_scale,
        w_out=p["w_out"][l].astype(BF16), gmem=p["norm_mem"][l].reshape(1, D),
        w_mq=p["w_mq"][l].astype(BF16), gmqh=p["mq_head_norm"][l].reshape(1, -1),
        w_mo=p["w_mo"][l].astype(BF16), gffn=p["norm_ffn"][l].reshape(1, D),
    )


def _rope_tables(pos):
    half = ROPE_DIM // 2
    inv_freq = ROPE_BASE ** (-jnp.arange(half, dtype=F32) / half)
    ang = pos.astype(F32)[:, None] * inv_freq[None, :]
    cos2 = jnp.concatenate([jnp.cos(ang), jnp.cos(ang)], axis=-1)
    sin2 = jnp.concatenate([jnp.sin(ang), jnp.sin(ang)], axis=-1)
    n = pos.shape[0]
    ones, zeros = jnp.ones((n, NOPE_DIM), F32), jnp.zeros((n, NOPE_DIM), F32)
    return dict(cos_k=_rope_lanes(cos2), sin_k=_rope_lanes(sin2),
                cos_q=_pad_cols(jnp.concatenate([ones, cos2], axis=-1), HEAD_PAD).T,
                sin_q=_pad_cols(jnp.concatenate([zeros, sin2], axis=-1), HEAD_PAD).T)


def _layer(l, x, tabs, q_off, conv_prev, h0, past, mem_k, mem_v, w, p):
    B, L, D = x.shape
    prev8 = jnp.pad(conv_prev, ((0, 0), (SUBLANES - (CONV_W - 1), 0), (0, 0)))
    h0_8 = jnp.pad(h0[:, None, :], ((0, 0), (SUBLANES - 1, 0), (0, 0)))
    grec, gatt, cq, ckv, kr, ctail, htail = _inproj(x, prev8, h0_8, w, tabs["cos_k"], tabs["sin_k"])

    q = _qprep(cq, w, tabs["cos_q"], tabs["sin_q"])
    if past is None:
        t_valid = L
        tk = min(512, L)
    else:
        t_valid = past[0].shape[2] + L
        tk = 512
    k, vt = _kvprep(ckv, kr, w, tk, past)
    flash = functools.partial(_flash, q_off=q_off, t_valid=t_valid, tk=tk)
    att = lax.cond(w["score_bound"] <= SAFE_LOG2_RANGE, functools.partial(flash, bounded=True),
                   functools.partial(flash, bounded=False), q, k, vt)

    x2 = _post(x, grec, gatt, att, mem_k, mem_v, w).reshape(B * L, D)
    j = l // 2
    if l % 2 == 0:
        x3 = _ffn(x2, w["gffn"], p["ffn_w1"][j].astype(BF16), p["ffn_w3"][j].astype(BF16),
                  p["ffn_w2"][j].astype(BF16))
    else:
        E = p["moe_router"].shape[-1]
        wr = _pad_cols(p["moe_router"][j], ROUTER_ROWS).T
        wr_hi = wr.astype(BF16)
        wr_lo = (wr - wr_hi.astype(F32)).astype(BF16)
        br = jnp.concatenate([p["moe_router_b"][j],
                              jnp.full((ROUTER_ROWS - E,), NEG, F32)]).reshape(ROUTER_ROWS, 1)
        x3 = _moe(x2, w["gffn"], wr_hi, wr_lo, br, p["moe_w1"][j].astype(BF16),
                  p["moe_w3"][j].astype(BF16), p["moe_w2"][j].astype(BF16))
    conv_state = ctail[:, SUBLANES - (CONV_W - 1):, :]
    return x3.reshape(B, L, D), conv_state, htail[:, SUBLANES - 1, :], ckv, kr[..., NOPE_DIM:QK_DIM]


def kernel(x_prompt, x_sample, cache_ckv, cache_krope, cache_mem_k, cache_mem_v, state_lru, state_conv, mem_prompt, norm_mix, w_in, conv_w, conv_b, lru_wa, lru_ba, lru_wi, lru_bi, lru_lambda, q_lat_norm, w_uq, kv_lat_norm, w_uk, w_uv, q_head_norm, k_head_norm, w_out, norm_mem, mem_in_norm, w_mq, w_mk, w_mv, w_mo, mq_head_norm, mk_head_norm, norm_ffn, ffn_w1, ffn_w3, ffn_w2, moe_router, moe_router_b, moe_w1, moe_w3, moe_w2):
    p = dict(norm_mix=norm_mix, w_in=w_in, conv_w=conv_w, conv_b=conv_b, lru_wa=lru_wa, lru_ba=lru_ba,
             lru_wi=lru_wi, lru_bi=lru_bi, lru_lambda=lru_lambda, q_lat_norm=q_lat_norm, w_uq=w_uq,
             kv_lat_norm=kv_lat_norm, w_uk=w_uk, w_uv=w_uv, q_head_norm=q_head_norm,
             k_head_norm=k_head_norm, w_out=w_out, norm_mem=norm_mem, w_mq=w_mq, w_mo=w_mo,
             mq_head_norm=mq_head_norm, norm_ffn=norm_ffn, ffn_w1=ffn_w1, ffn_w3=ffn_w3,
             ffn_w2=ffn_w2, moe_router=moe_router, moe_router_b=moe_router_b, moe_w1=moe_w1,
             moe_w3=moe_w3, moe_w2=moe_w2)
    depth = w_in.shape[0]
    weights = [_layer_weights(l, p) for l in range(depth)]

    Bp, Lp, D = x_prompt.shape
    Bs, Ls, _ = x_sample.shape
    t_past = cache_ckv.shape[2]
    M = mem_prompt.shape[1]

    tabs_p = _rope_tables(jnp.arange(Lp, dtype=jnp.int32))
    zero_conv = jnp.zeros((Bp, CONV_W - 1, D), F32)
    zero_h = jnp.zeros((Bp, D), F32)
    x = x_prompt
    outs_p = [[] for _ in range(6)]
    for l in range(depth):
        mk, mv = _memkv(mem_prompt, mem_in_norm[l], w_mk[l], w_mv[l], mk_head_norm[l])
        x, cs, hl, ckv, kr = _layer(l, x, tabs_p, 0, zero_conv, zero_h, None, mk, mv, weights[l], p)
        for acc, val in zip(outs_p, (ckv, kr, hl, cs, mk.reshape(Bp, M, MEM_HEADS, -1),
                                     mv.reshape(Bp, M, MEM_HEADS, -1))):
            acc.append(val)
    y_prompt = x

    tabs_s = _rope_tables(t_past + jnp.arange(Ls, dtype=jnp.int32))
    x = x_sample
    outs_s = [[] for _ in range(4)]
    cache_kr = _rope_lanes(cache_krope)
    for l in range(depth):
        x, cs, hl, ckv, kr = _layer(l, x, tabs_s, t_past, state_conv[l], state_lru[l],
                                    (cache_ckv, cache_kr, l), cache_mem_k[l].reshape(Bs, M, D),
                                    cache_mem_v[l].reshape(Bs, M, D), weights[l], p)
        for acc, val in zip(outs_s, (ckv, kr, hl, cs)):
            acc.append(val)
    y_sample = x

    return (y_prompt, y_sample) + tuple(jnp.stack(a) for a in outs_p) + tuple(jnp.stack(a) for a in outs_s)
```

```python
import functools
import math

import jax
import jax.numpy as jnp
from jax import lax
from jax.experimental import pallas as pl
from jax.experimental.pallas import tpu as pltpu

F32 = jnp.float32
BF16 = jnp.bfloat16

EPS = 1e-6
CHUNK = 64
LRU_C = 8.0
CONV_W = 4
N_HEADS = 16
NOPE_DIM = 64
ROPE_DIM = 32
QK_DIM = NOPE_DIM + ROPE_DIM
ROPE_BASE = 10000.0
MEM_HEADS = 4

LANES = 128
SUBLANES = 8
MXU_DIM = 256
VMEM_LIMIT = 56 * 1024 * 1024

HEAD_PAD = LANES
ROUTER_ROWS = 16
MOE_CHUNK = 160
SAFE_LOG2_RANGE = 60.0
FLASH_HEADS = 8
FLASH_LAG = 8
NEG = -1e30
LOG2E = 1.4426950408889634


def _resident(shape):
    nd = len(shape)
    return pl.BlockSpec(shape, lambda *_: (0,) * nd, pipeline_mode=pl.Buffered(1))


def _params(n_axes):
    return pltpu.CompilerParams(dimension_semantics=("arbitrary",) * n_axes,
                                vmem_limit_bytes=VMEM_LIMIT)


def _rms(x, gain):
    return x * lax.rsqrt(jnp.mean(x * x, axis=-1, keepdims=True) + EPS) * gain


def _dot(a, b):
    return jnp.dot(a, b, preferred_element_type=F32)


def _dot_nt(a, b):
    return lax.dot_general(a, b, (((1,), (1,)), ((), ())), preferred_element_type=F32)


def _token_tile(n, cap=512):
    t = min(n, cap)
    assert n % t == 0 and t % SUBLANES == 0
    return t


def _memkv_kernel(mem_ref, gin_ref, wk_ref, wv_ref, gk_ref, k_ref, v_ref):
    mn = _rms(mem_ref[0], gin_ref[...]).astype(BF16)
    k = _dot(mn, wk_ref[...])
    hd = gk_ref.shape[-1]
    for h in range(MEM_HEADS):
        sl = slice(h * hd, (h + 1) * hd)
        k_ref[0, :, sl] = _rms(k[:, sl], gk_ref[...])
    v_ref[0] = _dot(mn, wv_ref[...])


def _memkv(mem, g_in, w_k, w_v, g_k):
    B, M, D = mem.shape
    tok = pl.BlockSpec((1, M, D), lambda b: (b, 0, 0))
    return pl.pallas_call(
        _memkv_kernel,
        grid=(B,),
        in_specs=[tok, _resident((1, D)), _resident((D, D)), _resident((D, D)),
                  _resident((1, D // MEM_HEADS))],
        out_specs=[tok, tok],
        out_shape=[jax.ShapeDtypeStruct((B, M, D), F32)] * 2,
        compiler_params=_params(1),
        name="memkv",
    )(mem, g_in.reshape(1, D), w_k.astype(BF16), w_v.astype(BF16), g_k.reshape(1, -1))


def _inproj_kernel(x_ref, prev_ref, h0_ref, gmix_ref, win_ref, cw_ref, cb_ref, wa_ref, ba_ref,
                   wi_ref, bi_ref, sp_ref, gq_ref, gkv_ref, cos_ref, sin_ref,
                   grec_ref, gatt_ref, cq_ref, ckv_ref, kr_ref, ctail_ref, htail_ref,
                   xext_ref, hcar_ref, *, tm, d, q_lora, kv_lora):
    @pl.when(pl.program_id(1) == 0)
    def _():
        xext_ref[0:SUBLANES, :] = prev_ref[0]
        hcar_ref[...] = h0_ref[0]

    hn = _rms(x_ref[0], gmix_ref[...]).astype(BF16)

    def proj(c0, c1):
        return _dot(hn, win_ref[:, c0:c1])

    x_rec = proj(0, d)
    xext_ref[SUBLANES:SUBLANES + tm, :] = x_rec
    xc = cb_ref[...] + cw_ref[CONV_W - 1:CONV_W, :] * x_rec
    for j in range(CONV_W - 1):
        off = SUBLANES - (CONV_W - 1) + j
        xc = xc + cw_ref[j:j + 1, :] * xext_ref[off:off + tm, :]
    tail = x_rec[tm - SUBLANES:tm, :]
    ctail_ref[0] = tail
    xext_ref[0:SUBLANES, :] = tail

    xcb = xc.astype(BF16)
    nb = d // MXU_DIM
    gr = jnp.concatenate(
        [_dot(xcb[:, c * MXU_DIM:(c + 1) * MXU_DIM], wa_ref[c]) for c in range(nb)], axis=1)
    gi = jnp.concatenate(
        [_dot(xcb[:, c * MXU_DIM:(c + 1) * MXU_DIM], wi_ref[c]) for c in range(nb)], axis=1)
    r = jax.nn.sigmoid(gr + ba_ref[...])
    ig = jax.nn.sigmoid(gi + bi_ref[...])
    log_a = (-LRU_C) * r * sp_ref[...]
    a = jnp.exp(log_a)
    u = jnp.sqrt(1.0 - a * a) * (ig * xc)

    row = lax.broadcasted_iota(jnp.int32, (tm, 1), 0) & (SUBLANES - 1)
    s = 1
    while s < SUBLANES:
        keep = row >= s
        u = a * jnp.where(keep, pltpu.roll(u, s, 0), 0.0) + u
        a = a * jnp.where(keep, pltpu.roll(a, s, 0), 1.0)
        s *= 2
    h_prev = hcar_ref[SUBLANES - 1:SUBLANES, :]
    groups = []
    for g in range(tm // SUBLANES):
        rows = slice(g * SUBLANES, (g + 1) * SUBLANES)
        hg = a[rows] * h_prev + u[rows]
        groups.append(hg)
        h_prev = hg[SUBLANES - 1:SUBLANES, :]
    h = jnp.concatenate(groups, axis=0)
    htail = groups[-1]
    hcar_ref[...] = htail
    htail_ref[0] = htail

    x_gate = proj(d, 2 * d)
    g_rec = proj(2 * d, 3 * d)
    grec_ref[0] = jax.nn.sigmoid(g_rec) * (jax.nn.gelu(x_gate) * h)
    gatt_ref[0] = proj(3 * d, 4 * d)

    o = 4 * d
    cq_ref[0] = _rms(proj(o, o + q_lora), gq_ref[...]).astype(BF16)
    o += q_lora
    ckv_ref[0] = _rms(proj(o, o + kv_lora), gkv_ref[...])
    o += kv_lora
    kr = proj(o, o + LANES) * cos_ref[...] + proj(o + LANES, o + 2 * LANES) * sin_ref[...]
    kr_ref[0] = kr


def _inproj(x, prev8, h0_8, w, cos_k, sin_k):
    B, L, D = x.shape
    tm = _token_tile(L, 256)
    q_lora, kv_lora = w["gq"].shape[-1], w["gkv"].shape[-1]
    ncol = w["w_in"].shape[-1]
    tok = lambda n: pl.BlockSpec((1, tm, n), lambda b, l: (b, l, 0))
    per_b = pl.BlockSpec((1, SUBLANES, D), lambda b, l: (b, 0, 0))
    tab = pl.BlockSpec((tm, LANES), lambda b, l: (l, 0))
    nb = D // MXU_DIM
    kern = functools.partial(_inproj_kernel, tm=tm, d=D, q_lora=q_lora, kv_lora=kv_lora)
    return pl.pallas_call(
        kern,
        grid=(B, L // tm),
        in_specs=[tok(D), per_b, per_b, _resident((1, D)), _resident((D, ncol)),
                  _resident((CONV_W, D)), _resident((1, D)),
                  _resident((nb, MXU_DIM, MXU_DIM)), _resident((1, D)),
                  _resident((nb, MXU_DIM, MXU_DIM)), _resident((1, D)), _resident((1, D)),
                  _resident((1, q_lora)), _resident((1, kv_lora)), tab, tab],
        out_specs=[tok(D), tok(D), tok(q_lora), tok(kv_lora), tok(HEAD_PAD), per_b, per_b],
        out_shape=[jax.ShapeDtypeStruct((B, L, D), F32), jax.ShapeDtypeStruct((B, L, D), F32),
                   jax.ShapeDtypeStruct((B, L, q_lora), BF16),
                   jax.ShapeDtypeStruct((B, L, kv_lora), F32),
                   jax.ShapeDtypeStruct((B, L, HEAD_PAD), F32),
                   jax.ShapeDtypeStruct((B, SUBLANES, D), F32),
                   jax.ShapeDtypeStruct((B, SUBLANES, D), F32)],
        scratch_shapes=[pltpu.VMEM((SUBLANES + tm, D), F32), pltpu.VMEM((SUBLANES, D), F32)],
        compiler_params=_params(2),
        name="inproj",
    )(x, prev8, h0_8, w["gmix"], w["w_in"], w["conv_w"], w["conv_b"], w["wa"], w["ba"],
      w["wi"], w["bi"], w["sp"], w["gq"], w["gkv"], cos_k, sin_k)


def _qprep_kernel(cq_ref, wt_ref, cos_ref, sin_ref, g_ref, q_ref):
    qt = _dot_nt(wt_ref[...], cq_ref[0])
    half = ROPE_DIM // 2
    tm = qt.shape[1]
    cos, sin = cos_ref[...], sin_ref[...]
    g_nope, g_rope = g_ref[:NOPE_DIM, :], g_ref[NOPE_DIM:QK_DIM, :]
    pad = jnp.zeros((HEAD_PAD - QK_DIM, tm), BF16)
    for h in range(N_HEADS):
        r0 = h * HEAD_PAD
        nope = qt[r0:r0 + NOPE_DIM, :]
        x1 = qt[r0 + NOPE_DIM:r0 + NOPE_DIM + half, :]
        x2 = qt[r0 + NOPE_DIM + half:r0 + QK_DIM, :]
        rope = (jnp.concatenate([x1, x2], axis=0) * cos + jnp.concatenate([-x2, x1], axis=0) * sin)
        ss = (jnp.sum(nope * nope, axis=0, keepdims=True)
              + jnp.sum(rope * rope, axis=0, keepdims=True)) * (1.0 / QK_DIM)
        inv = lax.rsqrt(ss + EPS)
        q_ref[0, r0:r0 + NOPE_DIM, :] = (nope * inv * g_nope).astype(BF16)
        q_ref[0, r0 + NOPE_DIM:r0 + QK_DIM, :] = (rope * inv * g_rope).astype(BF16)
        q_ref[0, r0 + QK_DIM:r0 + HEAD_PAD, :] = pad


def _qprep(cq, w, cos_q, sin_q):
    B, L, QL = cq.shape
    tm = _token_tile(L)
    n = N_HEADS * HEAD_PAD
    tab = pl.BlockSpec((ROPE_DIM, tm), lambda b, l: (0, l))
    return pl.pallas_call(
        _qprep_kernel,
        grid=(B, L // tm),
        in_specs=[pl.BlockSpec((1, tm, QL), lambda b, l: (b, l, 0)),
                  _resident((n, QL)), tab, tab, _resident((HEAD_PAD, 1))],
        out_specs=pl.BlockSpec((1, n, tm), lambda b, l: (b, 0, l)),
        out_shape=jax.ShapeDtypeStruct((B, n, L), BF16),
        compiler_params=_params(2),
        name="qprep",
    )(cq, w["wq_t"], cos_q, sin_q, w["gqh"])


def _kvprep_kernel(*refs, n_past, tk):
    if n_past:
        past_c_ref, past_kr_ref, new_c_ref, new_kr_ref, wk_ref, wvt_ref, g_ref, k_ref, vt_ref = refs
        is_new = pl.program_id(1) >= n_past

        def pick(new_ref, past_ref):
            new = new_ref[0]
            fill = jnp.zeros((tk - new.shape[0], new.shape[1]), new.dtype)
            return jnp.where(is_new, jnp.concatenate([new, fill], axis=0), past_ref[0, 0])

        c32, kr = pick(new_c_ref, past_c_ref), pick(new_kr_ref, past_kr_ref)
    else:
        new_c_ref, new_kr_ref, wk_ref, wvt_ref, g_ref, k_ref, vt_ref = refs
        c32, kr = new_c_ref[0], new_kr_ref[0]
    c = c32.astype(BF16)
    k = _dot(c, wk_ref[...])
    for h in range(N_HEADS):
        sl = slice(h * HEAD_PAD, (h + 1) * HEAD_PAD)
        kh = k[:, sl] + kr
        ss = jnp.sum(kh * kh, axis=-1, keepdims=True) * (1.0 / QK_DIM)
        k_ref[0, :, sl] = (kh * lax.rsqrt(ss + EPS) * g_ref[...]).astype(BF16)
    vt_ref[0, 0] = _dot_nt(wvt_ref[...], c).astype(BF16)


def _kvprep(ckv, kr, w, tk, past=None):
    B, L, KV = ckv.shape
    n = N_HEADS * HEAD_PAD
    dv = w["wv_t"].shape[0]
    if past is None:
        assert L % tk == 0
        n_past, n_tiles = 0, L // tk
        new = lambda width: pl.BlockSpec((1, tk, width), lambda b, t: (b, t, 0))
        in_specs, args = [new(KV), new(HEAD_PAD)], (ckv, kr)
    else:
        cache_c, cache_kr, layer = past
        t_past = cache_c.shape[2]
        assert t_past % tk == 0 and L <= tk and L % SUBLANES == 0
        n_past = t_past // tk
        n_tiles = n_past + 1
        old = lambda width: pl.BlockSpec(
            (1, 1, tk, width), lambda b, t: (layer, b, jnp.minimum(t, n_past - 1), 0))
        new = lambda width: pl.BlockSpec((1, L, width), lambda b, t: (b, 0, 0))
        in_specs, args = [old(KV), old(HEAD_PAD), new(KV), new(HEAD_PAD)], (cache_c, cache_kr, ckv, kr)
    return pl.pallas_call(
        functools.partial(_kvprep_kernel, n_past=n_past, tk=tk),
        grid=(B, n_tiles),
        in_specs=in_specs + [_resident((KV, n)), _resident((dv, KV)), _resident((1, HEAD_PAD))],
        out_specs=[pl.BlockSpec((1, tk, n), lambda b, t: (b, t, 0)),
                   pl.BlockSpec((1, 1, dv, tk), lambda b, t: (b, t, 0, 0))],
        out_shape=[jax.ShapeDtypeStruct((B, n_tiles * tk, n), BF16),
                   jax.ShapeDtypeStruct((B, n_tiles, dv, tk), BF16)],
        compiler_params=_params(2),
        name="kvprep",
    )(*args, w["wk"], w["wv_t"], w["gkh"])


def _flash_kernel(q_ref, k_ref, vt_ref, o_ref, m_ref, l_ref, acc_ref, *, tq, tk, q_off, t_valid,
                  v_dim, heads, bounded, lag, split_diag):
    q_start = q_off + pl.program_id(2) * tq
    n_full = jnp.minimum(q_start, t_valid) // tk
    n_end = (jnp.minimum(q_start + tq, t_valid) + tk - 1) // tk
    shift = int(math.log2(CHUNK))

    def chunk_of(pos):
        return lax.shift_right_logical(pos, shift)

    m_ref[...] = jnp.full(m_ref.shape, NEG, F32)
    l_ref[...] = jnp.zeros(l_ref.shape, F32)
    acc_ref[...] = jnp.zeros(acc_ref.shape, F32)

    def accumulate(hh, s, vj, q0):
        lanes = slice(q0, tq)
        if bounded:
            p = jnp.exp2(s)
            l_ref[hh, :, lanes] += jnp.sum(p, axis=0, keepdims=True)
        else:
            m = m_ref[hh, :, lanes]
            m_new = jnp.maximum(m, jnp.max(s, axis=0, keepdims=True))
            alpha = jnp.exp2(m - m_new)
            p = jnp.exp2(s - m_new)
            l_ref[hh, :, lanes] = alpha * l_ref[hh, :, lanes] + jnp.sum(p, axis=0, keepdims=True)
            m_ref[hh, :, lanes] = m_new
            acc_ref[hh, :, lanes] = alpha * acc_ref[hh, :, lanes]
        acc_ref[hh, :, lanes] += _dot(vj, p.astype(BF16))

    def tile(j, masked):
        if masked:
            k_pos = j * tk + lax.broadcasted_iota(jnp.int32, (tk, 1), 0)
            q_pos = q_start + lax.broadcasted_iota(jnp.int32, (1, tq), 1)
            vis = (chunk_of(k_pos) <= chunk_of(q_pos)) & (k_pos < t_valid)
        pending = {}
        for t in range(heads + lag):
            if t < heads:
                qh = q_ref[0, t * HEAD_PAD:(t + 1) * HEAD_PAD, :]
                kj = k_ref[0, pl.ds(pl.multiple_of(j * tk, tk), tk), t * HEAD_PAD:(t + 1) * HEAD_PAD]
                pending[t] = _dot(kj, qh)
            if t >= lag:
                hh = t - lag
                s = pending.pop(hh)
                if masked:
                    s = jnp.where(vis, s, NEG)
                accumulate(hh, s, vt_ref[0, j, hh * v_dim:(hh + 1) * v_dim, :], 0)

    def diagonal(j):
        half = tk // 2
        vis = (chunk_of(lax.broadcasted_iota(jnp.int32, (half, 1), 0))
               <= chunk_of(lax.broadcasted_iota(jnp.int32, (1, tq), 1)))
        for k0, q0 in ((0, 0), (half, half)):
            for hh in range(heads):
                qh = q_ref[0, hh * HEAD_PAD:(hh + 1) * HEAD_PAD, q0:]
                kj = k_ref[0, pl.ds(pl.multiple_of(j * tk + k0, half), half),
                           hh * HEAD_PAD:(hh + 1) * HEAD_PAD]
                s = jnp.where(vis[:, :tq - q0], _dot(kj, qh), NEG)
                accumulate(hh, s, vt_ref[0, j, hh * v_dim:(hh + 1) * v_dim, k0:k0 + half], q0)

    def loop(lo, hi, body):
        lax.fori_loop(lo, hi, lambda j, c: (body(j), c)[1], 0)

    loop(0, n_full, lambda j: tile(j, False))
    if split_diag:
        diagonal(n_full)
    else:
        loop(n_full, n_end, lambda j: tile(j, True))
    o_ref[0] = jnp.concatenate([acc_ref[hh] / l_ref[hh] for hh in range(heads)], axis=0).T


def _flash(q, k, vt, *, q_off, t_valid, tk, bounded, heads=FLASH_HEADS):
    B, n, L = q.shape
    T = k.shape[1]
    dv = vt.shape[2]
    v_dim = dv // N_HEADS
    tq = _token_tile(L)
    assert tq % CHUNK == 0 and q_off % CHUNK == 0 and (heads * v_dim) % LANES == 0
    split_diag = tq == tk and q_off == 0 and t_valid == T and (tk // 2) % max(CHUNK, LANES) == 0
    kern = functools.partial(_flash_kernel, tq=tq, tk=tk, q_off=q_off, t_valid=t_valid, v_dim=v_dim,
                             heads=heads, bounded=bounded, lag=min(FLASH_LAG, heads),
                             split_diag=split_diag)
    return pl.pallas_call(
        kern,
        grid=(B, N_HEADS // heads, L // tq),
        in_specs=[pl.BlockSpec((1, heads * HEAD_PAD, tq), lambda b, p, i: (b, p, i)),
                  pl.BlockSpec((1, T, heads * HEAD_PAD), lambda b, p, i: (b, 0, p)),
                  pl.BlockSpec((1, T // tk, heads * v_dim, tk), lambda b, p, i: (b, 0, p, 0))],
        out_specs=pl.BlockSpec((1, tq, heads * v_dim), lambda b, p, i: (b, i, p)),
        out_shape=jax.ShapeDtypeStruct((B, L, dv), F32),
        scratch_shapes=[pltpu.VMEM((heads, 1, tq), F32), pltpu.VMEM((heads, 1, tq), F32),
                        pltpu.VMEM((heads, v_dim, tq), F32)],
        compiler_params=_params(3),
        name="flash",
    )(q, k, vt)


def _post_kernel(x_ref, grec_ref, gatt_ref, att_ref, wo_ref, gmem_ref, wmq_ref, gqh_ref,
                 mk_ref, mv_ref, wmo_ref, o_ref):
    mixed = (grec_ref[0] + jax.nn.sigmoid(gatt_ref[0]) * att_ref[0]).astype(BF16)
    x1 = x_ref[0] + _dot(mixed, wo_ref[...])
    qm = _dot(_rms(x1, gmem_ref[...]).astype(BF16), wmq_ref[...])
    hd = gqh_ref.shape[-1]
    outs = []
    for h in range(MEM_HEADS):
        sl = slice(h * hd, (h + 1) * hd)
        qh = (_rms(qm[:, sl], gqh_ref[...]) * (hd ** -0.5)).astype(BF16)
        s = _dot_nt(qh, mk_ref[0, :, sl])
        p = jnp.exp(s - jnp.max(s, axis=-1, keepdims=True))
        l = jnp.sum(p, axis=-1, keepdims=True)
        outs.append((_dot(p.astype(BF16), mv_ref[0, :, sl]) / l).astype(BF16))
    o_ref[0] = x1 + _dot(jnp.concatenate(outs, axis=1), wmo_ref[...])


def _post(x, grec, gatt, att, mem_k, mem_v, w):
    B, L, D = x.shape
    M = mem_k.shape[1]
    tm = _token_tile(L)
    tok = pl.BlockSpec((1, tm, D), lambda b, l: (b, l, 0))
    mem = pl.BlockSpec((1, M, D), lambda b, l: (b, 0, 0))
    return pl.pallas_call(
        _post_kernel,
        grid=(B, L // tm),
        in_specs=[tok, tok, tok, tok, _resident((D, D)), _resident((1, D)), _resident((D, D)),
                  _resident((1, D // MEM_HEADS)), mem, mem, _resident((D, D))],
        out_specs=tok,
        out_shape=jax.ShapeDtypeStruct((B, L, D), F32),
        compiler_params=_params(2),
        name="post",
    )(x, grec, gatt, att, w["w_out"], w["gmem"], w["w_mq"], w["gmqh"],
      mem_k.astype(BF16), mem_v.astype(BF16), w["w_mo"])


def _swiglu_mid(a, b):
    return (a * jax.nn.sigmoid(a) * b).astype(BF16)


def _ffn_kernel(x_ref, g_ref, w1_ref, w3_ref, w2_ref, o_ref, *, n_split):
    x = x_ref[...]
    hf = _rms(x, g_ref[...]).astype(BF16)
    fc = w1_ref.shape[1] // n_split
    acc = x
    for c in range(n_split):
        sl = slice(c * fc, (c + 1) * fc)
        acc = acc + _dot(_swiglu_mid(_dot(hf, w1_ref[:, sl]), _dot(hf, w3_ref[:, sl])), w2_ref[sl, :])
    o_ref[...] = acc


def _ffn(x, g, w1, w3, w2):
    N, D = x.shape
    F = w1.shape[1]
    tm = _token_tile(N)
    n_split = 2
    assert F % (n_split * LANES) == 0
    tok = pl.BlockSpec((tm, D), lambda i: (i, 0))
    return pl.pallas_call(
        functools.partial(_ffn_kernel, n_split=n_split),
        grid=(N // tm,),
        in_specs=[tok, _resident((1, D)), _resident((D, F)), _resident((D, F)), _resident((F, D))],
        out_specs=tok,
        out_shape=jax.ShapeDtypeStruct((N, D), F32),
        compiler_params=_params(1),
        name="ffn",
    )(x, g, w1, w3, w2)


def _moe_kernel(x_ref, g_ref, wrh_ref, wrl_ref, br_ref, w1_ref, w3_ref, w2_ref, o_ref,
                hf_ref, gate_ref, rank_ref, *, ts, cap, n_exp):
    e = pl.program_id(1)
    lane = lax.broadcasted_iota(jnp.int32, (1, LANES), 1)
    n_sub = hf_ref.shape[0] // ts

    @pl.when(e == 0)
    def _():
        x = x_ref[...]
        hf = _rms(x, g_ref[...])
        hi = hf.astype(BF16)
        lo = (hf - hi.astype(F32)).astype(BF16)
        logits = (_dot_nt(wrh_ref[...], hi) + _dot_nt(wrh_ref[...], lo) + _dot_nt(wrl_ref[...], hi)
                  + br_ref[...])
        sub = lax.broadcasted_iota(jnp.int32, (ROUTER_ROWS, 1), 0)
        m1 = jnp.max(logits, axis=0, keepdims=True)
        i1 = jnp.min(jnp.where(logits == m1, sub, ROUTER_ROWS), axis=0, keepdims=True)
        rest = jnp.where(sub == i1, NEG, logits)
        m2 = jnp.max(rest, axis=0, keepdims=True)
        i2 = jnp.min(jnp.where(rest == m2, sub, ROUTER_ROWS), axis=0, keepdims=True)
        e2 = jnp.exp(m2 - m1)
        den = 1.0 + e2
        gate_t = jnp.where(sub == i1, 1.0 / den, 0.0) + jnp.where(sub == i2, e2 / den, 0.0)
        tm = gate_t.shape[1]
        gate_ref[...] = jnp.concatenate(
            [gate_t, jnp.zeros((LANES - ROUTER_ROWS, tm), F32)], axis=0).T
        hf_ref[...] = hi
        o_ref[...] = x
        sel_t = ((sub == i1) | (sub == i2))[:n_exp, :]
        before = (lax.broadcasted_iota(jnp.int32, (ts, ts), 0)
                  < lax.broadcasted_iota(jnp.int32, (ts, ts), 1))
        tri = jnp.where(before, 1.0, 0.0).astype(BF16)
        for s in range(n_sub):
            sel_s = sel_t[:, s * ts:(s + 1) * ts]
            rank = _dot(jnp.where(sel_s, 1.0, 0.0).astype(BF16), tri)
            rank_ref[s] = jnp.where(sel_s, rank, -1.0)

    row = lax.broadcasted_iota(jnp.int32, (cap, 1), 0)
    for s in range(n_sub):
        rows = slice(s * ts, (s + 1) * ts)
        rank_e = rank_ref[s, pl.ds(e, 1), :]
        n_tok = jnp.max(rank_e).astype(jnp.int32) + 1
        ge = jnp.sum(jnp.where(lane == e, gate_ref[rows, :], 0.0), axis=-1, keepdims=True)

        def chunk(c, carry):
            want = (row + c * cap).astype(F32)
            onehot = jnp.where(rank_e == want, 1.0, 0.0).astype(BF16)
            xg = _dot(onehot, hf_ref[rows, :]).astype(BF16)
            y = _dot(_swiglu_mid(_dot(xg, w1_ref[0]), _dot(xg, w3_ref[0])), w2_ref[0])
            back = lax.dot_general(onehot, y.astype(BF16), (((0,), (0,)), ((), ())),
                                   preferred_element_type=F32)
            o_ref[rows, :] += ge * back
            return carry

        lax.fori_loop(0, (n_tok + cap - 1) // cap, chunk, 0)


def _moe(x, g, wr_hi, wr_lo, br, w1, w3, w2):
    N, D = x.shape
    E, _, F = w1.shape
    tm = _token_tile(N, 1024)
    ts = _token_tile(tm, 512)
    assert E <= SUBLANES
    tok = pl.BlockSpec((tm, D), lambda i, e: (i, 0))
    kern = functools.partial(_moe_kernel, ts=ts, cap=MOE_CHUNK, n_exp=E)
    return pl.pallas_call(
        kern,
        grid=(N // tm, E),
        in_specs=[tok, _resident((1, D)), _resident((ROUTER_ROWS, D)), _resident((ROUTER_ROWS, D)),
                  _resident((ROUTER_ROWS, 1)),
                  pl.BlockSpec((1, D, F), lambda i, e: (e, 0, 0)),
                  pl.BlockSpec((1, D, F), lambda i, e: (e, 0, 0)),
                  pl.BlockSpec((1, F, D), lambda i, e: (e, 0, 0))],
        out_specs=tok,
        out_shape=jax.ShapeDtypeStruct((N, D), F32),
        scratch_shapes=[pltpu.VMEM((tm, D), BF16), pltpu.VMEM((tm, LANES), F32),
                        pltpu.VMEM((tm // ts, E, ts), F32)],
        compiler_params=_params(2),
        name="moe",
    )(x, g, wr_hi, wr_lo, br, w1, w3, w2)


def _rot_half_cols(w):
    half = ROPE_DIM // 2
    return jnp.concatenate([-w[..., half:], w[..., :half]], axis=-1)


def _rope_lanes(a):
    return jnp.pad(a, [(0, 0)] * (a.ndim - 1) + [(NOPE_DIM, HEAD_PAD - QK_DIM)])


def _pad_cols(w, n):
    return jnp.pad(w, [(0, 0)] * (w.ndim - 1) + [(0, n - w.shape[-1])])


def _head_pad(nope, rope):
    z = jnp.zeros(nope.shape[:-1] + (HEAD_PAD - QK_DIM,), nope.dtype)
    out = jnp.concatenate([nope, rope, z], axis=-1)
    return out.reshape(out.shape[:-2] + (N_HEADS * HEAD_PAD,))


def _block_diag_tiles(w):
    nblk, bw, _ = w.shape
    per = MXU_DIM // bw
    w4 = w.reshape(nblk // per, per, bw, bw)
    eye = jnp.eye(per, dtype=w.dtype)
    return jnp.einsum("cpij,pq->cpiqj", w4, eye).reshape(nblk // per, MXU_DIM, MXU_DIM)


def _layer_weights(l, p):
    D = p["w_in"].shape[1]
    q_lora = p["q_lat_norm"].shape[-1]
    kv_lora = p["kv_lat_norm"].shape[-1]
    w_in = p["w_in"][l]
    sp = (D, 2 * D, 2 * D + q_lora, 2 * D + q_lora + kv_lora, 2 * D + q_lora + kv_lora + ROPE_DIM,
          3 * D + q_lora + kv_lora + ROPE_DIM)
    x_rec, x_gate, c_q, c_kv, k_rot, g_rec, g_att = jnp.split(w_in, sp, axis=-1)
    w_in_perm = jnp.concatenate(
        [x_rec, x_gate, g_rec, g_att, c_q, c_kv, _rope_lanes(k_rot),
         _rope_lanes(_rot_half_cols(k_rot))], axis=-1).astype(BF16)

    wq = p["w_uq"][l].reshape(q_lora, N_HEADS, QK_DIM)
    wq_n, wq_r = wq[..., :NOPE_DIM], wq[..., NOPE_DIM:]
    wk = p["w_uk"][l].reshape(kv_lora, N_HEADS, NOPE_DIM)
    gq, gk = p["q_head_norm"][l], p["k_head_norm"][l]
    q_scale = (QK_DIM ** -0.5) * LOG2E
    return dict(
        gmix=p["norm_mix"][l].reshape(1, D), w_in=w_in_perm,
        conv_w=p["conv_w"][l], conv_b=p["conv_b"][l].reshape(1, D),
        wa=_block_diag_tiles(p["lru_wa"][l]).astype(BF16), ba=p["lru_ba"][l].reshape(1, D),
        wi=_block_diag_tiles(p["lru_wi"][l]).astype(BF16), bi=p["lru_bi"][l].reshape(1, D),
        sp=jax.nn.softplus(-p["lru_lambda"][l]).reshape(1, D),
        gq=p["q_lat_norm"][l].reshape(1, q_lora), gkv=p["kv_lat_norm"][l].reshape(1, kv_lora),
        wq_t=_head_pad(wq_n, wq_r).T.astype(BF16),
        gqh=(_pad_cols(gq, HEAD_PAD) * q_scale).reshape(HEAD_PAD, 1),
        wk=_head_pad(wk, jnp.zeros((kv_lora, N_HEADS, ROPE_DIM), F32)).astype(BF16),
        wv_t=p["w_uv"][l].T.astype(BF16),
        gkh=_pad_cols(gk, HEAD_PAD).reshape(1, HEAD_PAD),
        score_bound=QK_DIM * jnp.max(jnp.abs(gq)) * jnp.max(jnp.abs(gk)) * q_scale,
        w_out=p["w_out"][l].astype(BF16), gmem=p["norm_mem"][l].reshape(1, D),
        w_mq=p["w_mq"][l].astype(BF16), gmqh=p["mq_head_norm"][l].reshape(1, -1),
        w_mo=p["w_mo"][l].astype(BF16), gffn=p["norm_ffn"][l].reshape(1, D),
    )


def _rope_tables(pos):
    half = ROPE_DIM // 2
    inv_freq = ROPE_BASE ** (-jnp.arange(half, dtype=F32) / half)
    ang = pos.astype(F32)[:, None] * inv_freq[None, :]
    cos2 = jnp.concatenate([jnp.cos(ang), jnp.cos(ang)], axis=-1)
    sin2 = jnp.concatenate([jnp.sin(ang), jnp.sin(ang)], axis=-1)
    return dict(cos_k=_rope_lanes(cos2), sin_k=_rope_lanes(sin2), cos_q=cos2.T, sin_q=sin2.T)


def _layer(l, x, tabs, q_off, conv_prev, h0, past, mem_k, mem_v, w, p):
    B, L, D = x.shape
    prev8 = jnp.pad(conv_prev, ((0, 0), (SUBLANES - (CONV_W - 1), 0), (0, 0)))
    h0_8 = jnp.pad(h0[:, None, :], ((0, 0), (SUBLANES - 1, 0), (0, 0)))
    grec, gatt, cq, ckv, kr, ctail, htail = _inproj(x, prev8, h0_8, w, tabs["cos_k"], tabs["sin_k"])

    q = _qprep(cq, w, tabs["cos_q"], tabs["sin_q"])
    if past is None:
        t_valid = L
        tk = min(512, L)
    else:
        t_valid = past[0].shape[2] + L
        tk = 512
    k, vt = _kvprep(ckv, kr, w, tk, past)
    flash = functools.partial(_flash, q_off=q_off, t_valid=t_valid, tk=tk)
    att = lax.cond(w["score_bound"] <= SAFE_LOG2_RANGE, functools.partial(flash, bounded=True),
                   functools.partial(flash, bounded=False), q, k, vt)

    x2 = _post(x, grec, gatt, att, mem_k, mem_v, w).reshape(B * L, D)
    j = l // 2
    if l % 2 == 0:
        x3 = _ffn(x2, w["gffn"], p["ffn_w1"][j].astype(BF16), p["ffn_w3"][j].astype(BF16),
                  p["ffn_w2"][j].astype(BF16))
    else:
        E = p["moe_router"].shape[-1]
        wr = _pad_cols(p["moe_router"][j], ROUTER_ROWS).T
        wr_hi = wr.astype(BF16)
        wr_lo = (wr - wr_hi.astype(F32)).astype(BF16)
        br = jnp.concatenate([p["moe_router_b"][j],
                              jnp.full((ROUTER_ROWS - E,), NEG, F32)]).reshape(ROUTER_ROWS, 1)
        x3 = _moe(x2, w["gffn"], wr_hi, wr_lo, br, p["moe_w1"][j].astype(BF16),
                  p["moe_w3"][j].astype(BF16), p["moe_w2"][j].astype(BF16))
    conv_state = ctail[:, SUBLANES - (CONV_W - 1):, :]
    return x3.reshape(B, L, D), conv_state, htail[:, SUBLANES - 1, :], ckv, kr[..., NOPE_DIM:QK_DIM]


def kernel(x_prompt, x_sample, cache_ckv, cache_krope, cache_mem_k, cache_mem_v, state_lru, state_conv, mem_prompt, norm_mix, w_in, conv_w, conv_b, lru_wa, lru_ba, lru_wi, lru_bi, lru_lambda, q_lat_norm, w_uq, kv_lat_norm, w_uk, w_uv, q_head_norm, k_head_norm, w_out, norm_mem, mem_in_norm, w_mq, w_mk, w_mv, w_mo, mq_head_norm, mk_head_norm, norm_ffn, ffn_w1, ffn_w3, ffn_w2, moe_router, moe_router_b, moe_w1, moe_w3, moe_w2):
    p = dict(norm_mix=norm_mix, w_in=w_in, conv_w=conv_w, conv_b=conv_b, lru_wa=lru_wa, lru_ba=lru_ba,
             lru_wi=lru_wi, lru_bi=lru_bi, lru_lambda=lru_lambda, q_lat_norm=q_lat_norm, w_uq=w_uq,
             kv_lat_norm=kv_lat_norm, w_uk=w_uk, w_uv=w_uv, q_head_norm=q_head_norm,
             k_head_norm=k_head_norm, w_out=w_out, norm_mem=norm_mem, w_mq=w_mq, w_mo=w_mo,
             mq_head_norm=mq_head_norm, norm_ffn=norm_ffn, ffn_w1=ffn_w1, ffn_w3=ffn_w3,
             ffn_w2=ffn_w2, moe_router=moe_router, moe_router_b=moe_router_b, moe_w1=moe_w1,
             moe_w3=moe_w3, moe_w2=moe_w2)
    depth = w_in.shape[0]
    weights = [_layer_weights(l, p) for l in range(depth)]

    Bp, Lp, D = x_prompt.shape
    Bs, Ls, _ = x_sample.shape
    t_past = cache_ckv.shape[2]
    M = mem_prompt.shape[1]

    tabs_p = _rope_tables(jnp.arange(Lp, dtype=jnp.int32))
    zero_conv = jnp.zeros((Bp, CONV_W - 1, D), F32)
    zero_h = jnp.zeros((Bp, D), F32)
    x = x_prompt
    outs_p = [[] for _ in range(6)]
    for l in range(depth):
        mk, mv = _memkv(mem_prompt, mem_in_norm[l], w_mk[l], w_mv[l], mk_head_norm[l])
        x, cs, hl, ckv, kr = _layer(l, x, tabs_p, 0, zero_conv, zero_h, None, mk, mv, weights[l], p)
        for acc, val in zip(outs_p, (ckv, kr, hl, cs, mk.reshape(Bp, M, MEM_HEADS, -1),
                                     mv.reshape(Bp, M, MEM_HEADS, -1))):
            acc.append(val)
    y_prompt = x

    tabs_s = _rope_tables(t_past + jnp.arange(Ls, dtype=jnp.int32))
    x = x_sample
    outs_s = [[] for _ in range(4)]
    cache_kr = _rope_lanes(cache_krope)
    for l in range(depth):
        x, cs, hl, ckv, kr = _layer(l, x, tabs_s, t_past, state_conv[l], state_lru[l],
                                    (cache_ckv, cache_kr, l), cache_mem_k[l].reshape(Bs, M, D),
                                    cache_mem_v[l].reshape(Bs, M, D), weights[l], p)
        for acc, val in zip(outs_s, (ckv, kr, hl, cs)):
            acc.append(val)
    y_sample = x

    return (y_prompt, y_sample) + tuple(jnp.stack(a) for a in outs_p) + tuple(jnp.stack(a) for a in outs_s)
```

```python
import functools
import math

import jax
import jax.numpy as jnp
from jax import lax
from jax.experimental import pallas as pl
from jax.experimental.pallas import tpu as pltpu

F32 = jnp.float32
BF16 = jnp.bfloat16

EPS = 1e-6
CHUNK = 64
LRU_C = 8.0
CONV_W = 4
N_HEADS = 16
NOPE_DIM = 64
ROPE_DIM = 32
QK_DIM = NOPE_DIM + ROPE_DIM
ROPE_BASE = 10000.0
MEM_HEADS = 4

LANES = 128
SUBLANES = 8
MXU_DIM = 256
VMEM_LIMIT = 56 * 1024 * 1024

HEAD_PAD = LANES
ROUTER_ROWS = 16
N_INPROJ_INPUTS = 16
MOE_CHUNK = 160
SAFE_LOG2_RANGE = 60.0
FLASH_HEADS = 8
FLASH_LAG = 8
NEG = -1e30
LOG2E = 1.4426950408889634


def _resident(shape):
    nd = len(shape)
    return pl.BlockSpec(shape, lambda *_: (0,) * nd, pipeline_mode=pl.Buffered(1))


def _params(n_axes):
    return pltpu.CompilerParams(dimension_semantics=("arbitrary",) * n_axes,
                                vmem_limit_bytes=VMEM_LIMIT)


def _rms(x, gain):
    return x * lax.rsqrt(jnp.mean(x * x, axis=-1, keepdims=True) + EPS) * gain


def _dot(a, b):
    return jnp.dot(a, b, preferred_element_type=F32)


def _dot_nt(a, b):
    return lax.dot_general(a, b, (((1,), (1,)), ((), ())), preferred_element_type=F32)


def _token_tile(n, cap=512):
    t = min(n, cap)
    assert n % t == 0 and t % SUBLANES == 0
    return t


def _memkv_kernel(mem_ref, gin_ref, wk_ref, wv_ref, gk_ref, k_ref, v_ref):
    mn = _rms(mem_ref[0], gin_ref[...]).astype(BF16)
    k = _dot(mn, wk_ref[...])
    hd = gk_ref.shape[-1]
    for h in range(MEM_HEADS):
        sl = slice(h * hd, (h + 1) * hd)
        k_ref[0, :, sl] = _rms(k[:, sl], gk_ref[...])
    v_ref[0] = _dot(mn, wv_ref[...])


def _memkv(mem, g_in, w_k, w_v, g_k):
    B, M, D = mem.shape
    tok = pl.BlockSpec((1, M, D), lambda b: (b, 0, 0))
    return pl.pallas_call(
        _memkv_kernel,
        grid=(B,),
        in_specs=[tok, _resident((1, D)), _resident((D, D)), _resident((D, D)),
                  _resident((1, D // MEM_HEADS))],
        out_specs=[tok, tok],
        out_shape=[jax.ShapeDtypeStruct((B, M, D), F32)] * 2,
        compiler_params=_params(1),
        name="memkv",
    )(mem, g_in.reshape(1, D), w_k.astype(BF16), w_v.astype(BF16), g_k.reshape(1, -1))


def _inproj_kernel(x_ref, prev_ref, h0_ref, gmix_ref, win_ref, cw_ref, cb_ref, wa_ref, ba_ref,
                   wi_ref, bi_ref, sp_ref, gq_ref, gkv_ref, cos_ref, sin_ref, *rest,
                   tm, d, q_lora, kv_lora):
    (grec_ref, gatt_ref, cq_ref, ckv_ref, kr_ref, ctail_ref, htail_ref,
     xext_ref, hcar_ref) = rest[-9:]
    @pl.when(pl.program_id(1) == 0)
    def _():
        xext_ref[0:SUBLANES, :] = prev_ref[0]
        hcar_ref[...] = h0_ref[0]

    hn = _rms(x_ref[0], gmix_ref[...]).astype(BF16)

    def proj(c0, c1):
        return _dot(hn, win_ref[:, c0:c1])

    x_rec = proj(0, d)
    xext_ref[SUBLANES:SUBLANES + tm, :] = x_rec
    xc = cb_ref[...] + cw_ref[CONV_W - 1:CONV_W, :] * x_rec
    for j in range(CONV_W - 1):
        off = SUBLANES - (CONV_W - 1) + j
        xc = xc + cw_ref[j:j + 1, :] * xext_ref[off:off + tm, :]
    tail = x_rec[tm - SUBLANES:tm, :]
    ctail_ref[0] = tail
    xext_ref[0:SUBLANES, :] = tail

    xcb = xc.astype(BF16)
    nb = d // MXU_DIM
    gr = jnp.concatenate(
        [_dot(xcb[:, c * MXU_DIM:(c + 1) * MXU_DIM], wa_ref[c]) for c in range(nb)], axis=1)
    gi = jnp.concatenate(
        [_dot(xcb[:, c * MXU_DIM:(c + 1) * MXU_DIM], wi_ref[c]) for c in range(nb)], axis=1)
    r = jax.nn.sigmoid(gr + ba_ref[...])
    ig = jax.nn.sigmoid(gi + bi_ref[...])
    log_a = (-LRU_C) * r * sp_ref[...]
    a = jnp.exp(log_a)
    u = jnp.sqrt(1.0 - a * a) * (ig * xc)

    row = lax.broadcasted_iota(jnp.int32, (tm, 1), 0) & (SUBLANES - 1)
    s = 1
    while s < SUBLANES:
        keep = row >= s
        u = a * jnp.where(keep, pltpu.roll(u, s, 0), 0.0) + u
        a = a * jnp.where(keep, pltpu.roll(a, s, 0), 1.0)
        s *= 2
    h_prev = hcar_ref[SUBLANES - 1:SUBLANES, :]
    groups = []
    for g in range(tm // SUBLANES):
        rows = slice(g * SUBLANES, (g + 1) * SUBLANES)
        hg = a[rows] * h_prev + u[rows]
        groups.append(hg)
        h_prev = hg[SUBLANES - 1:SUBLANES, :]
    h = jnp.concatenate(groups, axis=0)
    htail = groups[-1]
    hcar_ref[...] = htail
    htail_ref[0] = htail

    x_gate = proj(d, 2 * d)
    g_rec = proj(2 * d, 3 * d)
    grec_ref[0] = jax.nn.sigmoid(g_rec) * (jax.nn.gelu(x_gate) * h)
    gatt_ref[0] = proj(3 * d, 4 * d)

    o = 4 * d
    cq_ref[0] = _rms(proj(o, o + q_lora), gq_ref[...]).astype(BF16)
    o += q_lora
    ckv_ref[0, 0] = _rms(proj(o, o + kv_lora), gkv_ref[...])
    o += kv_lora
    kr = proj(o, o + LANES) * cos_ref[...] + proj(o + LANES, o + 2 * LANES) * sin_ref[...]
    kr_ref[0] = kr


def _inproj(x, prev8, h0_8, w, cos_k, sin_k, layer, depth, ckv_stack):
    B, L, D = x.shape
    tm = _token_tile(L, 256)
    q_lora, kv_lora = w["gq"].shape[-1], w["gkv"].shape[-1]
    ncol = w["w_in"].shape[-1]
    tok = lambda n: pl.BlockSpec((1, tm, n), lambda b, l: (b, l, 0))
    per_b = pl.BlockSpec((1, SUBLANES, D), lambda b, l: (b, 0, 0))
    tab = pl.BlockSpec((tm, LANES), lambda b, l: (l, 0))
    nb = D // MXU_DIM
    kern = functools.partial(_inproj_kernel, tm=tm, d=D, q_lora=q_lora, kv_lora=kv_lora)
    alias_args = () if ckv_stack is None else (ckv_stack,)
    alias_spec = [pl.BlockSpec(memory_space=pl.ANY)] * len(alias_args)
    return pl.pallas_call(
        kern,
        grid=(B, L // tm),
        in_specs=[tok(D), per_b, per_b, _resident((1, D)), _resident((D, ncol)),
                  _resident((CONV_W, D)), _resident((1, D)),
                  _resident((nb, MXU_DIM, MXU_DIM)), _resident((1, D)),
                  _resident((nb, MXU_DIM, MXU_DIM)), _resident((1, D)), _resident((1, D)),
                  _resident((1, q_lora)), _resident((1, kv_lora)), tab, tab] + alias_spec,
        out_specs=[tok(D), tok(D), tok(q_lora),
                   pl.BlockSpec((1, 1, tm, kv_lora), lambda b, l: (layer, b, l, 0)),
                   tok(HEAD_PAD), per_b, per_b],
        out_shape=[jax.ShapeDtypeStruct((B, L, D), F32), jax.ShapeDtypeStruct((B, L, D), F32),
                   jax.ShapeDtypeStruct((B, L, q_lora), BF16),
                   jax.ShapeDtypeStruct((depth, B, L, kv_lora), F32),
                   jax.ShapeDtypeStruct((B, L, HEAD_PAD), F32),
                   jax.ShapeDtypeStruct((B, SUBLANES, D), F32),
                   jax.ShapeDtypeStruct((B, SUBLANES, D), F32)],
        scratch_shapes=[pltpu.VMEM((SUBLANES + tm, D), F32), pltpu.VMEM((SUBLANES, D), F32)],
        input_output_aliases={N_INPROJ_INPUTS: 3} if alias_args else {},
        compiler_params=_params(2),
        name="inproj",
    )(x, prev8, h0_8, w["gmix"], w["w_in"], w["conv_w"], w["conv_b"], w["wa"], w["ba"],
      w["wi"], w["bi"], w["sp"], w["gq"], w["gkv"], cos_k, sin_k, *alias_args)


def _qprep_kernel(cq_ref, wt_ref, cos_ref, sin_ref, g_ref, q_ref):
    qt = _dot_nt(wt_ref[...], cq_ref[0])
    half = ROPE_DIM // 2
    tm = qt.shape[1]
    cos, sin = cos_ref[...], sin_ref[...]
    g_nope, g_rope = g_ref[:NOPE_DIM, :], g_ref[NOPE_DIM:QK_DIM, :]
    pad = jnp.zeros((HEAD_PAD - QK_DIM, tm), BF16)
    for h in range(N_HEADS):
        r0 = h * HEAD_PAD
        nope = qt[r0:r0 + NOPE_DIM, :]
        x1 = qt[r0 + NOPE_DIM:r0 + NOPE_DIM + half, :]
        x2 = qt[r0 + NOPE_DIM + half:r0 + QK_DIM, :]
        rope = (jnp.concatenate([x1, x2], axis=0) * cos + jnp.concatenate([-x2, x1], axis=0) * sin)
        ss = (jnp.sum(nope * nope, axis=0, keepdims=True)
              + jnp.sum(rope * rope, axis=0, keepdims=True)) * (1.0 / QK_DIM)
        inv = lax.rsqrt(ss + EPS)
        q_ref[0, r0:r0 + NOPE_DIM, :] = (nope * inv * g_nope).astype(BF16)
        q_ref[0, r0 + NOPE_DIM:r0 + QK_DIM, :] = (rope * inv * g_rope).astype(BF16)
        q_ref[0, r0 + QK_DIM:r0 + HEAD_PAD, :] = pad


def _qprep(cq, w, cos_q, sin_q):
    B, L, QL = cq.shape
    tm = _token_tile(L)
    n = N_HEADS * HEAD_PAD
    tab = pl.BlockSpec((ROPE_DIM, tm), lambda b, l: (0, l))
    return pl.pallas_call(
        _qprep_kernel,
        grid=(B, L // tm),
        in_specs=[pl.BlockSpec((1, tm, QL), lambda b, l: (b, l, 0)),
                  _resident((n, QL)), tab, tab, _resident((HEAD_PAD, 1))],
        out_specs=pl.BlockSpec((1, n, tm), lambda b, l: (b, 0, l)),
        out_shape=jax.ShapeDtypeStruct((B, n, L), BF16),
        compiler_params=_params(2),
        name="qprep",
    )(cq, w["wq_t"], cos_q, sin_q, w["gqh"])


def _kvprep_kernel(*refs, n_past, tk):
    if n_past:
        past_c_ref, past_kr_ref, new_c_ref, new_kr_ref, wk_ref, wvt_ref, g_ref, k_ref, vt_ref = refs
        is_new = pl.program_id(1) >= n_past

        def pick(new_ref, past_ref):
            new = new_ref[...].reshape(new_ref.shape[-2:])
            fill = jnp.zeros((tk - new.shape[0], new.shape[1]), new.dtype)
            return jnp.where(is_new, jnp.concatenate([new, fill], axis=0), past_ref[0, 0])

        c32, kr = pick(new_c_ref, past_c_ref), pick(new_kr_ref, past_kr_ref)
    else:
        new_c_ref, new_kr_ref, wk_ref, wvt_ref, g_ref, k_ref, vt_ref = refs
        c32, kr = new_c_ref[0, 0], new_kr_ref[0]
    c = c32.astype(BF16)
    k = _dot(c, wk_ref[...])
    for h in range(N_HEADS):
        sl = slice(h * HEAD_PAD, (h + 1) * HEAD_PAD)
        kh = k[:, sl] + kr
        ss = jnp.sum(kh * kh, axis=-1, keepdims=True) * (1.0 / QK_DIM)
        k_ref[0, :, sl] = (kh * lax.rsqrt(ss + EPS) * g_ref[...]).astype(BF16)
    vt_ref[0, 0] = _dot_nt(wvt_ref[...], c).astype(BF16)


def _kvprep(ckv, layer, kr, w, tk, past=None):
    _, B, L, KV = ckv.shape
    n = N_HEADS * HEAD_PAD
    dv = w["wv_t"].shape[0]
    if past is None:
        assert L % tk == 0
        n_past, n_tiles = 0, L // tk
        in_specs = [pl.BlockSpec((1, 1, tk, KV), lambda b, t: (layer, b, t, 0)),
                    pl.BlockSpec((1, tk, HEAD_PAD), lambda b, t: (b, t, 0))]
        args = (ckv, kr)
    else:
        cache_c, cache_kr, layer = past
        t_past = cache_c.shape[2]
        assert t_past % tk == 0 and L <= tk and L % SUBLANES == 0
        n_past = t_past // tk
        n_tiles = n_past + 1
        old = lambda width: pl.BlockSpec(
            (1, 1, tk, width), lambda b, t: (layer, b, jnp.minimum(t, n_past - 1), 0))
        in_specs = [old(KV), old(HEAD_PAD),
                    pl.BlockSpec((1, 1, L, KV), lambda b, t: (layer, b, 0, 0)),
                    pl.BlockSpec((1, L, HEAD_PAD), lambda b, t: (b, 0, 0))]
        args = (cache_c, cache_kr, ckv, kr)
    return pl.pallas_call(
        functools.partial(_kvprep_kernel, n_past=n_past, tk=tk),
        grid=(B, n_tiles),
        in_specs=in_specs + [_resident((KV, n)), _resident((dv, KV)), _resident((1, HEAD_PAD))],
        out_specs=[pl.BlockSpec((1, tk, n), lambda b, t: (b, t, 0)),
                   pl.BlockSpec((1, 1, dv, tk), lambda b, t: (b, t, 0, 0))],
        out_shape=[jax.ShapeDtypeStruct((B, n_tiles * tk, n), BF16),
                   jax.ShapeDtypeStruct((B, n_tiles, dv, tk), BF16)],
        compiler_params=_params(2),
        name="kvprep",
    )(*args, w["wk"], w["wv_t"], w["gkh"])


def _flash_kernel(q_ref, k_ref, vt_ref, o_ref, m_ref, l_ref, acc_ref, *, tq, tk, q_off, t_valid,
                  v_dim, heads, bounded, lag, split_diag):
    q_start = q_off + pl.program_id(2) * tq
    n_full = jnp.minimum(q_start, t_valid) // tk
    n_end = (jnp.minimum(q_start + tq, t_valid) + tk - 1) // tk
    shift = int(math.log2(CHUNK))

    def chunk_of(pos):
        return lax.shift_right_logical(pos, shift)

    m_ref[...] = jnp.full(m_ref.shape, NEG, F32)
    l_ref[...] = jnp.zeros(l_ref.shape, F32)
    acc_ref[...] = jnp.zeros(acc_ref.shape, F32)

    def accumulate(hh, s, vj, q0):
        lanes = slice(q0, tq)
        if bounded:
            p = jnp.exp2(s)
            l_ref[hh, :, lanes] += jnp.sum(p, axis=0, keepdims=True)
        else:
            m = m_ref[hh, :, lanes]
            m_new = jnp.maximum(m, jnp.max(s, axis=0, keepdims=True))
            alpha = jnp.exp2(m - m_new)
            p = jnp.exp2(s - m_new)
            l_ref[hh, :, lanes] = alpha * l_ref[hh, :, lanes] + jnp.sum(p, axis=0, keepdims=True)
            m_ref[hh, :, lanes] = m_new
            acc_ref[hh, :, lanes] = alpha * acc_ref[hh, :, lanes]
        acc_ref[hh, :, lanes] += _dot(vj, p.astype(BF16))

    def tile(j, masked):
        if masked:
            k_pos = j * tk + lax.broadcasted_iota(jnp.int32, (tk, 1), 0)
            q_pos = q_start + lax.broadcasted_iota(jnp.int32, (1, tq), 1)
            vis = (chunk_of(k_pos) <= chunk_of(q_pos)) & (k_pos < t_valid)
        pending = {}
        for t in range(heads + lag):
            if t < heads:
                qh = q_ref[0, t * HEAD_PAD:(t + 1) * HEAD_PAD, :]
                kj = k_ref[0, pl.ds(pl.multiple_of(j * tk, tk), tk), t * HEAD_PAD:(t + 1) * HEAD_PAD]
                pending[t] = _dot(kj, qh)
            if t >= lag:
                hh = t - lag
                s = pending.pop(hh)
                if masked:
                    s = jnp.where(vis, s, NEG)
                accumulate(hh, s, vt_ref[0, j, hh * v_dim:(hh + 1) * v_dim, :], 0)

    def diagonal(j):
        half = tk // 2
        vis = (chunk_of(lax.broadcasted_iota(jnp.int32, (half, 1), 0))
               <= chunk_of(lax.broadcasted_iota(jnp.int32, (1, tq), 1)))
        for k0, q0 in ((0, 0), (half, half)):
            for hh in range(heads):
                qh = q_ref[0, hh * HEAD_PAD:(hh + 1) * HEAD_PAD, q0:]
                kj = k_ref[0, pl.ds(pl.multiple_of(j * tk + k0, half), half),
                           hh * HEAD_PAD:(hh + 1) * HEAD_PAD]
                s = jnp.where(vis[:, :tq - q0], _dot(kj, qh), NEG)
                accumulate(hh, s, vt_ref[0, j, hh * v_dim:(hh + 1) * v_dim, k0:k0 + half], q0)

    def loop(lo, hi, body):
        lax.fori_loop(lo, hi, lambda j, c: (body(j), c)[1], 0)

    loop(0, n_full, lambda j: tile(j, False))
    if split_diag:
        diagonal(n_full)
    else:
        loop(n_full, n_end, lambda j: tile(j, True))
    o_ref[0] = jnp.concatenate([acc_ref[hh] / l_ref[hh] for hh in range(heads)], axis=0).T


def _flash(q, k, vt, *, q_off, t_valid, tk, bounded, heads=FLASH_HEADS):
    B, n, L = q.shape
    T = k.shape[1]
    dv = vt.shape[2]
    v_dim = dv // N_HEADS
    tq = _token_tile(L)
    assert tq % CHUNK == 0 and q_off % CHUNK == 0 and (heads * v_dim) % LANES == 0
    split_diag = tq == tk and q_off == 0 and t_valid == T and (tk // 2) % max(CHUNK, LANES) == 0
    kern = functools.partial(_flash_kernel, tq=tq, tk=tk, q_off=q_off, t_valid=t_valid, v_dim=v_dim,
                             heads=heads, bounded=bounded, lag=min(FLASH_LAG, heads),
                             split_diag=split_diag)
    return pl.pallas_call(
        kern,
        grid=(B, N_HEADS // heads, L // tq),
        in_specs=[pl.BlockSpec((1, heads * HEAD_PAD, tq), lambda b, p, i: (b, p, i)),
                  pl.BlockSpec((1, T, heads * HEAD_PAD), lambda b, p, i: (b, 0, p)),
                  pl.BlockSpec((1, T // tk, heads * v_dim, tk), lambda b, p, i: (b, 0, p, 0))],
        out_specs=pl.BlockSpec((1, tq, heads * v_dim), lambda b, p, i: (b, i, p)),
        out_shape=jax.ShapeDtypeStruct((B, L, dv), F32),
        scratch_shapes=[pltpu.VMEM((heads, 1, tq), F32), pltpu.VMEM((heads, 1, tq), F32),
                        pltpu.VMEM((heads, v_dim, tq), F32)],
        compiler_params=_params(3),
        name="flash",
    )(q, k, vt)


def _post_kernel(x_ref, grec_ref, gatt_ref, att_ref, wo_ref, gmem_ref, wmq_ref, gqh_ref,
                 mk_ref, mv_ref, wmo_ref, o_ref):
    mixed = (grec_ref[0] + jax.nn.sigmoid(gatt_ref[0]) * att_ref[0]).astype(BF16)
    x1 = x_ref[0] + _dot(mixed, wo_ref[...])
    qm = _dot(_rms(x1, gmem_ref[...]).astype(BF16), wmq_ref[...])
    hd = gqh_ref.shape[-1]
    outs = []
    for h in range(MEM_HEADS):
        sl = slice(h * hd, (h + 1) * hd)
        qh = (_rms(qm[:, sl], gqh_ref[...]) * (hd ** -0.5)).astype(BF16)
        s = _dot_nt(qh, mk_ref[0, :, sl])
        p = jnp.exp(s - jnp.max(s, axis=-1, keepdims=True))
        l = jnp.sum(p, axis=-1, keepdims=True)
        outs.append((_dot(p.astype(BF16), mv_ref[0, :, sl]) / l).astype(BF16))
    o_ref[0] = x1 + _dot(jnp.concatenate(outs, axis=1), wmo_ref[...])


def _post(x, grec, gatt, att, mem_k, mem_v, w):
    B, L, D = x.shape
    M = mem_k.shape[1]
    tm = _token_tile(L)
    tok = pl.BlockSpec((1, tm, D), lambda b, l: (b, l, 0))
    mem = pl.BlockSpec((1, M, D), lambda b, l: (b, 0, 0))
    return pl.pallas_call(
        _post_kernel,
        grid=(B, L // tm),
        in_specs=[tok, tok, tok, tok, _resident((D, D)), _resident((1, D)), _resident((D, D)),
                  _resident((1, D // MEM_HEADS)), mem, mem, _resident((D, D))],
        out_specs=tok,
        out_shape=jax.ShapeDtypeStruct((B, L, D), F32),
        compiler_params=_params(2),
        name="post",
    )(x, grec, gatt, att, w["w_out"], w["gmem"], w["w_mq"], w["gmqh"],
      mem_k.astype(BF16), mem_v.astype(BF16), w["w_mo"])


def _swiglu_mid(a, b):
    return (a * jax.nn.sigmoid(a) * b).astype(BF16)


def _ffn_kernel(x_ref, g_ref, w1_ref, w3_ref, w2_ref, o_ref, *, n_split):
    x = x_ref[...]
    hf = _rms(x, g_ref[...]).astype(BF16)
    fc = w1_ref.shape[1] // n_split
    acc = x
    for c in range(n_split):
        sl = slice(c * fc, (c + 1) * fc)
        acc = acc + _dot(_swiglu_mid(_dot(hf, w1_ref[:, sl]), _dot(hf, w3_ref[:, sl])), w2_ref[sl, :])
    o_ref[...] = acc


def _ffn(x, g, w1, w3, w2):
    N, D = x.shape
    F = w1.shape[1]
    tm = _token_tile(N)
    n_split = 2
    assert F % (n_split * LANES) == 0
    tok = pl.BlockSpec((tm, D), lambda i: (i, 0))
    return pl.pallas_call(
        functools.partial(_ffn_kernel, n_split=n_split),
        grid=(N // tm,),
        in_specs=[tok, _resident((1, D)), _resident((D, F)), _resident((D, F)), _resident((F, D))],
        out_specs=tok,
        out_shape=jax.ShapeDtypeStruct((N, D), F32),
        compiler_params=_params(1),
        name="ffn",
    )(x, g, w1, w3, w2)


def _moe_kernel(x_ref, g_ref, wrh_ref, wrl_ref, br_ref, w1_ref, w3_ref, w2_ref, o_ref,
                hf_ref, gate_ref, rank_ref, *, ts, cap, n_exp):
    e = pl.program_id(1)
    lane = lax.broadcasted_iota(jnp.int32, (1, LANES), 1)
    n_sub = hf_ref.shape[0] // ts

    @pl.when(e == 0)
    def _():
        x = x_ref[...]
        hf = _rms(x, g_ref[...])
        hi = hf.astype(BF16)
        lo = (hf - hi.astype(F32)).astype(BF16)
        logits = (_dot_nt(wrh_ref[...], hi) + _dot_nt(wrh_ref[...], lo) + _dot_nt(wrl_ref[...], hi)
                  + br_ref[...])
        sub = lax.broadcasted_iota(jnp.int32, (ROUTER_ROWS, 1), 0)
        m1 = jnp.max(logits, axis=0, keepdims=True)
        i1 = jnp.min(jnp.where(logits == m1, sub, ROUTER_ROWS), axis=0, keepdims=True)
        rest = jnp.where(sub == i1, NEG, logits)
        m2 = jnp.max(rest, axis=0, keepdims=True)
        i2 = jnp.min(jnp.where(rest == m2, sub, ROUTER_ROWS), axis=0, keepdims=True)
        e2 = jnp.exp(m2 - m1)
        den = 1.0 + e2
        gate_t = jnp.where(sub == i1, 1.0 / den, 0.0) + jnp.where(sub == i2, e2 / den, 0.0)
        tm = gate_t.shape[1]
        gate_ref[...] = jnp.concatenate(
            [gate_t, jnp.zeros((LANES - ROUTER_ROWS, tm), F32)], axis=0).T
        hf_ref[...] = hi
        o_ref[...] = x
        sel_t = ((sub == i1) | (sub == i2))[:n_exp, :]
        before = (lax.broadcasted_iota(jnp.int32, (ts, ts), 0)
                  < lax.broadcasted_iota(jnp.int32, (ts, ts), 1))
        tri = jnp.where(before, 1.0, 0.0).astype(BF16)
        for s in range(n_sub):
            sel_s = sel_t[:, s * ts:(s + 1) * ts]
            rank = _dot(jnp.where(sel_s, 1.0, 0.0).astype(BF16), tri)
            rank_ref[s] = jnp.where(sel_s, rank, -1.0)

    row = lax.broadcasted_iota(jnp.int32, (cap, 1), 0)
    for s in range(n_sub):
        rows = slice(s * ts, (s + 1) * ts)
        rank_e = rank_ref[s, pl.ds(e, 1), :]
        n_tok = jnp.max(rank_e).astype(jnp.int32) + 1
        ge = jnp.sum(jnp.where(lane == e, gate_ref[rows, :], 0.0), axis=-1, keepdims=True)

        def chunk(c, carry):
            want = (row + c * cap).astype(F32)
            onehot = jnp.where(rank_e == want, 1.0, 0.0).astype(BF16)
            xg = _dot(onehot, hf_ref[rows, :]).astype(BF16)
            y = _dot(_swiglu_mid(_dot(xg, w1_ref[0]), _dot(xg, w3_ref[0])), w2_ref[0])
            back = lax.dot_general(onehot, y.astype(BF16), (((0,), (0,)), ((), ())),
                                   preferred_element_type=F32)
            o_ref[rows, :] += ge * back
            return carry

        lax.fori_loop(0, (n_tok + cap - 1) // cap, chunk, 0)


def _moe(x, g, wr_hi, wr_lo, br, w1, w3, w2):
    N, D = x.shape
    E, _, F = w1.shape
    tm = _token_tile(N, 1024)
    ts = _token_tile(tm, 512)
    assert E <= SUBLANES
    tok = pl.BlockSpec((tm, D), lambda i, e: (i, 0))
    kern = functools.partial(_moe_kernel, ts=ts, cap=MOE_CHUNK, n_exp=E)
    return pl.pallas_call(
        kern,
        grid=(N // tm, E),
        in_specs=[tok, _resident((1, D)), _resident((ROUTER_ROWS, D)), _resident((ROUTER_ROWS, D)),
                  _resident((ROUTER_ROWS, 1)),
                  pl.BlockSpec((1, D, F), lambda i, e: (e, 0, 0)),
                  pl.BlockSpec((1, D, F), lambda i, e: (e, 0, 0)),
                  pl.BlockSpec((1, F, D), lambda i, e: (e, 0, 0))],
        out_specs=tok,
        out_shape=jax.ShapeDtypeStruct((N, D), F32),
        scratch_shapes=[pltpu.VMEM((tm, D), BF16), pltpu.VMEM((tm, LANES), F32),
                        pltpu.VMEM((tm // ts, E, ts), F32)],
        compiler_params=_params(2),
        name="moe",
    )(x, g, wr_hi, wr_lo, br, w1, w3, w2)


def _rot_half_cols(w):
    half = ROPE_DIM // 2
    return jnp.concatenate([-w[..., half:], w[..., :half]], axis=-1)


def _rope_lanes(a):
    return jnp.pad(a, [(0, 0)] * (a.ndim - 1) + [(NOPE_DIM, HEAD_PAD - QK_DIM)])


def _pad_cols(w, n):
    return jnp.pad(w, [(0, 0)] * (w.ndim - 1) + [(0, n - w.shape[-1])])


def _head_pad(nope, rope):
    z = jnp.zeros(nope.shape[:-1] + (HEAD_PAD - QK_DIM,), nope.dtype)
    out = jnp.concatenate([nope, rope, z], axis=-1)
    return out.reshape(out.shape[:-2] + (N_HEADS * HEAD_PAD,))


def _block_diag_tiles(w):
    nblk, bw, _ = w.shape
    per = MXU_DIM // bw
    w4 = w.reshape(nblk // per, per, bw, bw)
    eye = jnp.eye(per, dtype=w.dtype)
    return jnp.einsum("cpij,pq->cpiqj", w4, eye).reshape(nblk // per, MXU_DIM, MXU_DIM)


def _layer_weights(l, p):
    D = p["w_in"].shape[1]
    q_lora = p["q_lat_norm"].shape[-1]
    kv_lora = p["kv_lat_norm"].shape[-1]
    w_in = p["w_in"][l]
    sp = (D, 2 * D, 2 * D + q_lora, 2 * D + q_lora + kv_lora, 2 * D + q_lora + kv_lora + ROPE_DIM,
          3 * D + q_lora + kv_lora + ROPE_DIM)
    x_rec, x_gate, c_q, c_kv, k_rot, g_rec, g_att = jnp.split(w_in, sp, axis=-1)
    w_in_perm = jnp.concatenate(
        [x_rec, x_gate, g_rec, g_att, c_q, c_kv, _rope_lanes(k_rot),
         _rope_lanes(_rot_half_cols(k_rot))], axis=-1).astype(BF16)

    wq = p["w_uq"][l].reshape(q_lora, N_HEADS, QK_DIM)
    wq_n, wq_r = wq[..., :NOPE_DIM], wq[..., NOPE_DIM:]
    wk = p["w_uk"][l].reshape(kv_lora, N_HEADS, NOPE_DIM)
    gq, gk = p["q_head_norm"][l], p["k_head_norm"][l]
    q_scale = (QK_DIM ** -0.5) * LOG2E
    return dict(
        gmix=p["norm_mix"][l].reshape(1, D), w_in=w_in_perm,
        conv_w=p["conv_w"][l], conv_b=p["conv_b"][l].reshape(1, D),
        wa=_block_diag_tiles(p["lru_wa"][l]).astype(BF16), ba=p["lru_ba"][l].reshape(1, D),
        wi=_block_diag_tiles(p["lru_wi"][l]).astype(BF16), bi=p["lru_bi"][l].reshape(1, D),
        sp=jax.nn.softplus(-p["lru_lambda"][l]).reshape(1, D),
        gq=p["q_lat_norm"][l].reshape(1, q_lora), gkv=p["kv_lat_norm"][l].reshape(1, kv_lora),
        wq_t=_head_pad(wq_n, wq_r).T.astype(BF16),
        gqh=(_pad_cols(gq, HEAD_PAD) * q_scale).reshape(HEAD_PAD, 1),
        wk=_head_pad(wk, jnp.zeros((kv_lora, N_HEADS, ROPE_DIM), F32)).astype(BF16),
        wv_t=p["w_uv"][l].T.astype(BF16),
        gkh=_pad_cols(gk, HEAD_PAD).reshape(1, HEAD_PAD),
        score_bound=QK_DIM * jnp.max(jnp.abs(gq)) * jnp.max(jnp.abs(gk)) * q_scale,
        w_out=p["w_out"][l].astype(BF16), gmem=p["norm_mem"][l].reshape(1, D),
        w_mq=p["w_mq"][l].astype(BF16), gmqh=p["mq_head_norm"][l].reshape(1, -1),
        w_mo=p["w_mo"][l].astype(BF16), gffn=p["norm_ffn"][l].reshape(1, D),
    )


def _rope_tables(pos):
    half = ROPE_DIM // 2
    inv_freq = ROPE_BASE ** (-jnp.arange(half, dtype=F32) / half)
    ang = pos.astype(F32)[:, None] * inv_freq[None, :]
    cos2 = jnp.concatenate([jnp.cos(ang), jnp.cos(ang)], axis=-1)
    sin2 = jnp.concatenate([jnp.sin(ang), jnp.sin(ang)], axis=-1)
    return dict(cos_k=_rope_lanes(cos2), sin_k=_rope_lanes(sin2), cos_q=cos2.T, sin_q=sin2.T)


def _layer(l, depth, x, tabs, q_off, conv_prev, h0, past, ckv_stack, mem_k, mem_v, w, p):
    B, L, D = x.shape
    prev8 = jnp.pad(conv_prev, ((0, 0), (SUBLANES - (CONV_W - 1), 0), (0, 0)))
    h0_8 = jnp.pad(h0[:, None, :], ((0, 0), (SUBLANES - 1, 0), (0, 0)))
    grec, gatt, cq, ckv, kr, ctail, htail = _inproj(x, prev8, h0_8, w, tabs["cos_k"], tabs["sin_k"],
                                                    l, depth, ckv_stack)

    q = _qprep(cq, w, tabs["cos_q"], tabs["sin_q"])
    if past is None:
        t_valid = L
        tk = min(512, L)
    else:
        t_valid = past[0].shape[2] + L
        tk = 512
    k, vt = _kvprep(ckv, l, kr, w, tk, past)
    flash = functools.partial(_flash, q_off=q_off, t_valid=t_valid, tk=tk)
    att = lax.cond(w["score_bound"] <= SAFE_LOG2_RANGE, functools.partial(flash, bounded=True),
                   functools.partial(flash, bounded=False), q, k, vt)

    x2 = _post(x, grec, gatt, att, mem_k, mem_v, w).reshape(B * L, D)
    j = l // 2
    if l % 2 == 0:
        x3 = _ffn(x2, w["gffn"], p["ffn_w1"][j].astype(BF16), p["ffn_w3"][j].astype(BF16),
                  p["ffn_w2"][j].astype(BF16))
    else:
        E = p["moe_router"].shape[-1]
        wr = _pad_cols(p["moe_router"][j], ROUTER_ROWS).T
        wr_hi = wr.astype(BF16)
        wr_lo = (wr - wr_hi.astype(F32)).astype(BF16)
        br = jnp.concatenate([p["moe_router_b"][j],
                              jnp.full((ROUTER_ROWS - E,), NEG, F32)]).reshape(ROUTER_ROWS, 1)
        x3 = _moe(x2, w["gffn"], wr_hi, wr_lo, br, p["moe_w1"][j].astype(BF16),
                  p["moe_w3"][j].astype(BF16), p["moe_w2"][j].astype(BF16))
    conv_state = ctail[:, SUBLANES - (CONV_W - 1):, :]
    return x3.reshape(B, L, D), conv_state, htail[:, SUBLANES - 1, :], ckv, kr[..., NOPE_DIM:QK_DIM]


def kernel(x_prompt, x_sample, cache_ckv, cache_krope, cache_mem_k, cache_mem_v, state_lru, state_conv, mem_prompt, norm_mix, w_in, conv_w, conv_b, lru_wa, lru_ba, lru_wi, lru_bi, lru_lambda, q_lat_norm, w_uq, kv_lat_norm, w_uk, w_uv, q_head_norm, k_head_norm, w_out, norm_mem, mem_in_norm, w_mq, w_mk, w_mv, w_mo, mq_head_norm, mk_head_norm, norm_ffn, ffn_w1, ffn_w3, ffn_w2, moe_router, moe_router_b, moe_w1, moe_w3, moe_w2):
    p = dict(norm_mix=norm_mix, w_in=w_in, conv_w=conv_w, conv_b=conv_b, lru_wa=lru_wa, lru_ba=lru_ba,
             lru_wi=lru_wi, lru_bi=lru_bi, lru_lambda=lru_lambda, q_lat_norm=q_lat_norm, w_uq=w_uq,
             kv_lat_norm=kv_lat_norm, w_uk=w_uk, w_uv=w_uv, q_head_norm=q_head_norm,
             k_head_norm=k_head_norm, w_out=w_out, norm_mem=norm_mem, w_mq=w_mq, w_mo=w_mo,
             mq_head_norm=mq_head_norm, norm_ffn=norm_ffn, ffn_w1=ffn_w1, ffn_w3=ffn_w3,
             ffn_w2=ffn_w2, moe_router=moe_router, moe_router_b=moe_router_b, moe_w1=moe_w1,
             moe_w3=moe_w3, moe_w2=moe_w2)
    depth = w_in.shape[0]
    weights = [_layer_weights(l, p) for l in range(depth)]

    Bp, Lp, D = x_prompt.shape
    Bs, Ls, _ = x_sample.shape
    t_past = cache_ckv.shape[2]
    M = mem_prompt.shape[1]

    tabs_p = _rope_tables(jnp.arange(Lp, dtype=jnp.int32))
    zero_conv = jnp.zeros((Bp, CONV_W - 1, D), F32)
    zero_h = jnp.zeros((Bp, D), F32)
    x = x_prompt
    ckv_p = None
    outs_p = [[] for _ in range(5)]
    for l in range(depth):
        mk, mv = _memkv(mem_prompt, mem_in_norm[l], w_mk[l], w_mv[l], mk_head_norm[l])
        x, cs, hl, ckv_p, kr = _layer(l, depth, x, tabs_p, 0, zero_conv, zero_h, None, ckv_p, mk, mv,
                                      weights[l], p)
        for acc, val in zip(outs_p, (kr, hl, cs, mk.reshape(Bp, M, MEM_HEADS, -1),
                                     mv.reshape(Bp, M, MEM_HEADS, -1))):
            acc.append(val)
    y_prompt = x

    tabs_s = _rope_tables(t_past + jnp.arange(Ls, dtype=jnp.int32))
    x = x_sample
    ckv_s = None
    outs_s = [[] for _ in range(3)]
    cache_kr = _rope_lanes(cache_krope)
    for l in range(depth):
        x, cs, hl, ckv_s, kr = _layer(l, depth, x, tabs_s, t_past, state_conv[l], state_lru[l],
                                      (cache_ckv, cache_kr, l), ckv_s, cache_mem_k[l].reshape(Bs, M, D),
                                      cache_mem_v[l].reshape(Bs, M, D), weights[l], p)
        for acc, val in zip(outs_s, (kr, hl, cs)):
            acc.append(val)
    y_sample = x

    return ((y_prompt, y_sample, ckv_p) + tuple(jnp.stack(a) for a in outs_p)
            + (ckv_s,) + tuple(jnp.stack(a) for a in outs_s))
```

```python
import functools
import math

import jax
import jax.numpy as jnp
from jax import lax
from jax.experimental import pallas as pl
from jax.experimental.pallas import tpu as pltpu

F32 = jnp.float32
BF16 = jnp.bfloat16

EPS = 1e-6
CHUNK = 64
LRU_C = 8.0
CONV_W = 4
N_HEADS = 16
NOPE_DIM = 64
ROPE_DIM = 32
QK_DIM = NOPE_DIM + ROPE_DIM
ROPE_BASE = 10000.0
MEM_HEADS = 4

LANES = 128
SUBLANES = 8
MXU_DIM = 256
VMEM_LIMIT = 56 * 1024 * 1024

HEAD_PAD = LANES
ROUTER_ROWS = 16
N_INPROJ_INPUTS = 16
MOE_CHUNK = 160
SAFE_LOG2_RANGE = 60.0
FLASH_HEADS = 16
FLASH_LAG = 8
NEG = -1e30
LOG2E = 1.4426950408889634


def _resident(shape):
    nd = len(shape)
    return pl.BlockSpec(shape, lambda *_: (0,) * nd, pipeline_mode=pl.Buffered(1))


def _params(n_axes):
    return pltpu.CompilerParams(dimension_semantics=("arbitrary",) * n_axes,
                                vmem_limit_bytes=VMEM_LIMIT)


def _rms(x, gain):
    return x * lax.rsqrt(jnp.mean(x * x, axis=-1, keepdims=True) + EPS) * gain


def _dot(a, b):
    return jnp.dot(a, b, preferred_element_type=F32)


def _dot_nt(a, b):
    return lax.dot_general(a, b, (((1,), (1,)), ((), ())), preferred_element_type=F32)


def _token_tile(n, cap=512):
    t = min(n, cap)
    assert n % t == 0 and t % SUBLANES == 0
    return t


def _memkv_kernel(mem_ref, gin_ref, wk_ref, wv_ref, gk_ref, k_ref, v_ref):
    mn = _rms(mem_ref[0], gin_ref[...]).astype(BF16)
    k = _dot(mn, wk_ref[...])
    hd = gk_ref.shape[-1]
    for h in range(MEM_HEADS):
        sl = slice(h * hd, (h + 1) * hd)
        k_ref[0, :, sl] = _rms(k[:, sl], gk_ref[...])
    v_ref[0] = _dot(mn, wv_ref[...])


def _memkv(mem, g_in, w_k, w_v, g_k):
    B, M, D = mem.shape
    tok = pl.BlockSpec((1, M, D), lambda b: (b, 0, 0))
    return pl.pallas_call(
        _memkv_kernel,
        grid=(B,),
        in_specs=[tok, _resident((1, D)), _resident((D, D)), _resident((D, D)),
                  _resident((1, D // MEM_HEADS))],
        out_specs=[tok, tok],
        out_shape=[jax.ShapeDtypeStruct((B, M, D), F32)] * 2,
        compiler_params=_params(1),
        name="memkv",
    )(mem, g_in.reshape(1, D), w_k.astype(BF16), w_v.astype(BF16), g_k.reshape(1, -1))


def _inproj_kernel(x_ref, prev_ref, h0_ref, gmix_ref, win_ref, cw_ref, cb_ref, wa_ref, ba_ref,
                   wi_ref, bi_ref, sp_ref, gq_ref, gkv_ref, cos_ref, sin_ref, *rest,
                   tm, d, q_lora, kv_lora):
    (grec_ref, gatt_ref, cq_ref, ckv_ref, kr_ref, ctail_ref, htail_ref,
     xext_ref, hcar_ref) = rest[-9:]
    @pl.when(pl.program_id(1) == 0)
    def _():
        xext_ref[0:SUBLANES, :] = prev_ref[0]
        hcar_ref[...] = h0_ref[0]

    hn = _rms(x_ref[0], gmix_ref[...]).astype(BF16)

    def proj(c0, c1):
        return _dot(hn, win_ref[:, c0:c1])

    x_rec = proj(0, d)
    xext_ref[SUBLANES:SUBLANES + tm, :] = x_rec
    xc = cb_ref[...] + cw_ref[CONV_W - 1:CONV_W, :] * x_rec
    for j in range(CONV_W - 1):
        off = SUBLANES - (CONV_W - 1) + j
        xc = xc + cw_ref[j:j + 1, :] * xext_ref[off:off + tm, :]
    tail = x_rec[tm - SUBLANES:tm, :]
    ctail_ref[0] = tail
    xext_ref[0:SUBLANES, :] = tail

    xcb = xc.astype(BF16)
    nb = d // MXU_DIM
    gr = jnp.concatenate(
        [_dot(xcb[:, c * MXU_DIM:(c + 1) * MXU_DIM], wa_ref[c]) for c in range(nb)], axis=1)
    gi = jnp.concatenate(
        [_dot(xcb[:, c * MXU_DIM:(c + 1) * MXU_DIM], wi_ref[c]) for c in range(nb)], axis=1)
    r = jax.nn.sigmoid(gr + ba_ref[...])
    ig = jax.nn.sigmoid(gi + bi_ref[...])
    log_a = (-LRU_C) * r * sp_ref[...]
    a = jnp.exp(log_a)
    u = jnp.sqrt(1.0 - a * a) * (ig * xc)

    row = lax.broadcasted_iota(jnp.int32, (tm, 1), 0) & (SUBLANES - 1)
    s = 1
    while s < SUBLANES:
        keep = row >= s
        u = a * jnp.where(keep, pltpu.roll(u, s, 0), 0.0) + u
        a = a * jnp.where(keep, pltpu.roll(a, s, 0), 1.0)
        s *= 2
    h_prev = hcar_ref[SUBLANES - 1:SUBLANES, :]
    groups = []
    for g in range(tm // SUBLANES):
        rows = slice(g * SUBLANES, (g + 1) * SUBLANES)
        hg = a[rows] * h_prev + u[rows]
        groups.append(hg)
        h_prev = hg[SUBLANES - 1:SUBLANES, :]
    h = jnp.concatenate(groups, axis=0)
    htail = groups[-1]
    hcar_ref[...] = htail
    htail_ref[0] = htail

    x_gate = proj(d, 2 * d)
    g_rec = proj(2 * d, 3 * d)
    grec_ref[0] = jax.nn.sigmoid(g_rec) * (jax.nn.gelu(x_gate) * h)
    gatt_ref[0] = proj(3 * d, 4 * d)

    o = 4 * d
    cq_ref[0] = _rms(proj(o, o + q_lora), gq_ref[...]).astype(BF16)
    o += q_lora
    ckv_ref[0, 0] = _rms(proj(o, o + kv_lora), gkv_ref[...])
    o += kv_lora
    kr = proj(o, o + LANES) * cos_ref[...] + proj(o + LANES, o + 2 * LANES) * sin_ref[...]
    kr_ref[0] = kr


def _inproj(x, prev8, h0_8, w, cos_k, sin_k, layer, depth, ckv_stack):
    B, L, D = x.shape
    tm = _token_tile(L, 256)
    q_lora, kv_lora = w["gq"].shape[-1], w["gkv"].shape[-1]
    ncol = w["w_in"].shape[-1]
    tok = lambda n: pl.BlockSpec((1, tm, n), lambda b, l: (b, l, 0))
    per_b = pl.BlockSpec((1, SUBLANES, D), lambda b, l: (b, 0, 0))
    tab = pl.BlockSpec((tm, LANES), lambda b, l: (l, 0))
    nb = D // MXU_DIM
    kern = functools.partial(_inproj_kernel, tm=tm, d=D, q_lora=q_lora, kv_lora=kv_lora)
    alias_args = () if ckv_stack is None else (ckv_stack,)
    alias_spec = [pl.BlockSpec(memory_space=pl.ANY)] * len(alias_args)
    return pl.pallas_call(
        kern,
        grid=(B, L // tm),
        in_specs=[tok(D), per_b, per_b, _resident((1, D)), _resident((D, ncol)),
                  _resident((CONV_W, D)), _resident((1, D)),
                  _resident((nb, MXU_DIM, MXU_DIM)), _resident((1, D)),
                  _resident((nb, MXU_DIM, MXU_DIM)), _resident((1, D)), _resident((1, D)),
                  _resident((1, q_lora)), _resident((1, kv_lora)), tab, tab] + alias_spec,
        out_specs=[tok(D), tok(D), tok(q_lora),
                   pl.BlockSpec((1, 1, tm, kv_lora), lambda b, l: (layer, b, l, 0)),
                   tok(HEAD_PAD), per_b, per_b],
        out_shape=[jax.ShapeDtypeStruct((B, L, D), F32), jax.ShapeDtypeStruct((B, L, D), F32),
                   jax.ShapeDtypeStruct((B, L, q_lora), BF16),
                   jax.ShapeDtypeStruct((depth, B, L, kv_lora), F32),
                   jax.ShapeDtypeStruct((B, L, HEAD_PAD), F32),
                   jax.ShapeDtypeStruct((B, SUBLANES, D), F32),
                   jax.ShapeDtypeStruct((B, SUBLANES, D), F32)],
        scratch_shapes=[pltpu.VMEM((SUBLANES + tm, D), F32), pltpu.VMEM((SUBLANES, D), F32)],
        input_output_aliases={N_INPROJ_INPUTS: 3} if alias_args else {},
        compiler_params=_params(2),
        name="inproj",
    )(x, prev8, h0_8, w["gmix"], w["w_in"], w["conv_w"], w["conv_b"], w["wa"], w["ba"],
      w["wi"], w["bi"], w["sp"], w["gq"], w["gkv"], cos_k, sin_k, *alias_args)


def _qprep_kernel(cq_ref, wt_ref, cos_ref, sin_ref, g_ref, q_ref):
    qt = _dot_nt(wt_ref[...], cq_ref[0])
    half = ROPE_DIM // 2
    tm = qt.shape[1]
    cos, sin = cos_ref[...], sin_ref[...]
    g_nope, g_rope = g_ref[:NOPE_DIM, :], g_ref[NOPE_DIM:QK_DIM, :]
    pad = jnp.zeros((HEAD_PAD - QK_DIM, tm), BF16)
    for h in range(N_HEADS):
        r0 = h * HEAD_PAD
        nope = qt[r0:r0 + NOPE_DIM, :]
        x1 = qt[r0 + NOPE_DIM:r0 + NOPE_DIM + half, :]
        x2 = qt[r0 + NOPE_DIM + half:r0 + QK_DIM, :]
        rope = (jnp.concatenate([x1, x2], axis=0) * cos + jnp.concatenate([-x2, x1], axis=0) * sin)
        ss = (jnp.sum(nope * nope, axis=0, keepdims=True)
              + jnp.sum(rope * rope, axis=0, keepdims=True)) * (1.0 / QK_DIM)
        inv = lax.rsqrt(ss + EPS)
        q_ref[0, r0:r0 + NOPE_DIM, :] = (nope * inv * g_nope).astype(BF16)
        q_ref[0, r0 + NOPE_DIM:r0 + QK_DIM, :] = (rope * inv * g_rope).astype(BF16)
        q_ref[0, r0 + QK_DIM:r0 + HEAD_PAD, :] = pad


def _qprep(cq, w, cos_q, sin_q):
    B, L, QL = cq.shape
    tm = _token_tile(L)
    n = N_HEADS * HEAD_PAD
    tab = pl.BlockSpec((ROPE_DIM, tm), lambda b, l: (0, l))
    return pl.pallas_call(
        _qprep_kernel,
        grid=(B, L // tm),
        in_specs=[pl.BlockSpec((1, tm, QL), lambda b, l: (b, l, 0)),
                  _resident((n, QL)), tab, tab, _resident((HEAD_PAD, 1))],
        out_specs=pl.BlockSpec((1, n, tm), lambda b, l: (b, 0, l)),
        out_shape=jax.ShapeDtypeStruct((B, n, L), BF16),
        compiler_params=_params(2),
        name="qprep",
    )(cq, w["wq_t"], cos_q, sin_q, w["gqh"])


def _kvprep_kernel(*refs, n_past, tk):
    if n_past:
        past_c_ref, past_kr_ref, new_c_ref, new_kr_ref, wk_ref, wvt_ref, g_ref, k_ref, vt_ref = refs
        is_new = pl.program_id(1) >= n_past

        def pick(new_ref, past_ref):
            new = new_ref[...].reshape(new_ref.shape[-2:])
            fill = jnp.zeros((tk - new.shape[0], new.shape[1]), new.dtype)
            return jnp.where(is_new, jnp.concatenate([new, fill], axis=0), past_ref[0, 0])

        c32, kr = pick(new_c_ref, past_c_ref), pick(new_kr_ref, past_kr_ref)
    else:
        new_c_ref, new_kr_ref, wk_ref, wvt_ref, g_ref, k_ref, vt_ref = refs
        c32, kr = new_c_ref[0, 0], new_kr_ref[0]
    c = c32.astype(BF16)
    k = _dot(c, wk_ref[...])
    for h in range(N_HEADS):
        sl = slice(h * HEAD_PAD, (h + 1) * HEAD_PAD)
        kh = k[:, sl] + kr
        ss = jnp.sum(kh * kh, axis=-1, keepdims=True) * (1.0 / QK_DIM)
        k_ref[0, :, sl] = (kh * lax.rsqrt(ss + EPS) * g_ref[...]).astype(BF16)
    vt_ref[0, 0] = _dot_nt(wvt_ref[...], c).astype(BF16)


def _kvprep(ckv, layer, kr, w, tk, past=None):
    _, B, L, KV = ckv.shape
    n = N_HEADS * HEAD_PAD
    dv = w["wv_t"].shape[0]
    if past is None:
        assert L % tk == 0
        n_past, n_tiles = 0, L // tk
        in_specs = [pl.BlockSpec((1, 1, tk, KV), lambda b, t: (layer, b, t, 0)),
                    pl.BlockSpec((1, tk, HEAD_PAD), lambda b, t: (b, t, 0))]
        args = (ckv, kr)
    else:
        cache_c, cache_kr, layer = past
        t_past = cache_c.shape[2]
        assert t_past % tk == 0 and L <= tk and L % SUBLANES == 0
        n_past = t_past // tk
        n_tiles = n_past + 1
        old = lambda width: pl.BlockSpec(
            (1, 1, tk, width), lambda b, t: (layer, b, jnp.minimum(t, n_past - 1), 0))
        in_specs = [old(KV), old(HEAD_PAD),
                    pl.BlockSpec((1, 1, L, KV), lambda b, t: (layer, b, 0, 0)),
                    pl.BlockSpec((1, L, HEAD_PAD), lambda b, t: (b, 0, 0))]
        args = (cache_c, cache_kr, ckv, kr)
    return pl.pallas_call(
        functools.partial(_kvprep_kernel, n_past=n_past, tk=tk),
        grid=(B, n_tiles),
        in_specs=in_specs + [_resident((KV, n)), _resident((dv, KV)), _resident((1, HEAD_PAD))],
        out_specs=[pl.BlockSpec((1, tk, n), lambda b, t: (b, t, 0)),
                   pl.BlockSpec((1, 1, dv, tk), lambda b, t: (b, t, 0, 0))],
        out_shape=[jax.ShapeDtypeStruct((B, n_tiles * tk, n), BF16),
                   jax.ShapeDtypeStruct((B, n_tiles, dv, tk), BF16)],
        compiler_params=_params(2),
        name="kvprep",
    )(*args, w["wk"], w["wv_t"], w["gkh"])


def _flash_kernel(q_ref, k_ref, vt_ref, o_ref, m_ref, l_ref, acc_ref, *, tq, tk, q_off, t_valid,
                  v_dim, heads, bounded, lag, split_diag):
    q_start = q_off + pl.program_id(2) * tq
    n_full = jnp.minimum(q_start, t_valid) // tk
    n_end = (jnp.minimum(q_start + tq, t_valid) + tk - 1) // tk
    shift = int(math.log2(CHUNK))

    def chunk_of(pos):
        return lax.shift_right_logical(pos, shift)

    m_ref[...] = jnp.full(m_ref.shape, NEG, F32)
    l_ref[...] = jnp.zeros(l_ref.shape, F32)
    acc_ref[...] = jnp.zeros(acc_ref.shape, F32)

    def accumulate(hh, s, vj, q0):
        lanes = slice(q0, tq)
        if bounded:
            p = jnp.exp2(s)
            l_ref[hh, :, lanes] += jnp.sum(p, axis=0, keepdims=True)
        else:
            m = m_ref[hh, :, lanes]
            m_new = jnp.maximum(m, jnp.max(s, axis=0, keepdims=True))
            alpha = jnp.exp2(m - m_new)
            p = jnp.exp2(s - m_new)
            l_ref[hh, :, lanes] = alpha * l_ref[hh, :, lanes] + jnp.sum(p, axis=0, keepdims=True)
            m_ref[hh, :, lanes] = m_new
            acc_ref[hh, :, lanes] = alpha * acc_ref[hh, :, lanes]
        acc_ref[hh, :, lanes] += _dot(vj, p.astype(BF16))

    def tile(j, masked):
        if masked:
            k_pos = j * tk + lax.broadcasted_iota(jnp.int32, (tk, 1), 0)
            q_pos = q_start + lax.broadcasted_iota(jnp.int32, (1, tq), 1)
            vis = (chunk_of(k_pos) <= chunk_of(q_pos)) & (k_pos < t_valid)
        pending = {}
        for t in range(heads + lag):
            if t < heads:
                qh = q_ref[0, t * HEAD_PAD:(t + 1) * HEAD_PAD, :]
                kj = k_ref[0, pl.ds(pl.multiple_of(j * tk, tk), tk), t * HEAD_PAD:(t + 1) * HEAD_PAD]
                pending[t] = _dot(kj, qh)
            if t >= lag:
                hh = t - lag
                s = pending.pop(hh)
                if masked:
                    s = jnp.where(vis, s, NEG)
                accumulate(hh, s, vt_ref[0, j, hh * v_dim:(hh + 1) * v_dim, :], 0)

    def diagonal(j):
        half = tk // 2
        vis = (chunk_of(lax.broadcasted_iota(jnp.int32, (half, 1), 0))
               <= chunk_of(lax.broadcasted_iota(jnp.int32, (1, tq), 1)))
        for k0, q0 in ((0, 0), (half, half)):
            for hh in range(heads):
                qh = q_ref[0, hh * HEAD_PAD:(hh + 1) * HEAD_PAD, q0:]
                kj = k_ref[0, pl.ds(pl.multiple_of(j * tk + k0, half), half),
                           hh * HEAD_PAD:(hh + 1) * HEAD_PAD]
                s = jnp.where(vis[:, :tq - q0], _dot(kj, qh), NEG)
                accumulate(hh, s, vt_ref[0, j, hh * v_dim:(hh + 1) * v_dim, k0:k0 + half], q0)

    def loop(lo, hi, body):
        lax.fori_loop(lo, hi, lambda j, c: (body(j), c)[1], 0)

    loop(0, n_full, lambda j: tile(j, False))
    if split_diag:
        diagonal(n_full)
    else:
        loop(n_full, n_end, lambda j: tile(j, True))
    o_ref[0] = jnp.concatenate([acc_ref[hh] / l_ref[hh] for hh in range(heads)], axis=0).T


def _flash(q, k, vt, *, q_off, t_valid, tk, bounded, heads=FLASH_HEADS):
    B, n, L = q.shape
    T = k.shape[1]
    dv = vt.shape[2]
    v_dim = dv // N_HEADS
    tq = _token_tile(L)
    assert tq % CHUNK == 0 and q_off % CHUNK == 0 and (heads * v_dim) % LANES == 0
    split_diag = tq == tk and q_off == 0 and t_valid == T and (tk // 2) % max(CHUNK, LANES) == 0
    kern = functools.partial(_flash_kernel, tq=tq, tk=tk, q_off=q_off, t_valid=t_valid, v_dim=v_dim,
                             heads=heads, bounded=bounded, lag=min(FLASH_LAG, heads),
                             split_diag=split_diag)
    return pl.pallas_call(
        kern,
        grid=(B, N_HEADS // heads, L // tq),
        in_specs=[pl.BlockSpec((1, heads * HEAD_PAD, tq), lambda b, p, i: (b, p, i)),
                  pl.BlockSpec((1, T, heads * HEAD_PAD), lambda b, p, i: (b, 0, p),
                               pipeline_mode=pl.Buffered(1)),
                  pl.BlockSpec((1, T // tk, heads * v_dim, tk), lambda b, p, i: (b, 0, p, 0))],
        out_specs=pl.BlockSpec((1, tq, heads * v_dim), lambda b, p, i: (b, i, p)),
        out_shape=jax.ShapeDtypeStruct((B, L, dv), F32),
        scratch_shapes=[pltpu.VMEM((heads, 1, tq), F32), pltpu.VMEM((heads, 1, tq), F32),
                        pltpu.VMEM((heads, v_dim, tq), F32)],
        compiler_params=_params(3),
        name="flash",
    )(q, k, vt)


def _post_kernel(x_ref, grec_ref, gatt_ref, att_ref, wo_ref, gmem_ref, wmq_ref, gqh_ref,
                 mk_ref, mv_ref, wmo_ref, o_ref):
    mixed = (grec_ref[0] + jax.nn.sigmoid(gatt_ref[0]) * att_ref[0]).astype(BF16)
    x1 = x_ref[0] + _dot(mixed, wo_ref[...])
    qm = _dot(_rms(x1, gmem_ref[...]).astype(BF16), wmq_ref[...])
    hd = gqh_ref.shape[-1]
    outs = []
    for h in range(MEM_HEADS):
        sl = slice(h * hd, (h + 1) * hd)
        qh = (_rms(qm[:, sl], gqh_ref[...]) * (hd ** -0.5)).astype(BF16)
        s = _dot_nt(qh, mk_ref[0, :, sl])
        p = jnp.exp(s - jnp.max(s, axis=-1, keepdims=True))
        l = jnp.sum(p, axis=-1, keepdims=True)
        outs.append((_dot(p.astype(BF16), mv_ref[0, :, sl]) / l).astype(BF16))
    o_ref[0] = x1 + _dot(jnp.concatenate(outs, axis=1), wmo_ref[...])


def _post(x, grec, gatt, att, mem_k, mem_v, w):
    B, L, D = x.shape
    M = mem_k.shape[1]
    tm = _token_tile(L)
    tok = pl.BlockSpec((1, tm, D), lambda b, l: (b, l, 0))
    mem = pl.BlockSpec((1, M, D), lambda b, l: (b, 0, 0))
    return pl.pallas_call(
        _post_kernel,
        grid=(B, L // tm),
        in_specs=[tok, tok, tok, tok, _resident((D, D)), _resident((1, D)), _resident((D, D)),
                  _resident((1, D // MEM_HEADS)), mem, mem, _resident((D, D))],
        out_specs=tok,
        out_shape=jax.ShapeDtypeStruct((B, L, D), F32),
        compiler_params=_params(2),
        name="post",
    )(x, grec, gatt, att, w["w_out"], w["gmem"], w["w_mq"], w["gmqh"],
      mem_k.astype(BF16), mem_v.astype(BF16), w["w_mo"])


def _swiglu_mid(a, b):
    return (a * jax.nn.sigmoid(a) * b).astype(BF16)


def _ffn_kernel(x_ref, g_ref, w1_ref, w3_ref, w2_ref, o_ref, *, n_split):
    x = x_ref[...]
    hf = _rms(x, g_ref[...]).astype(BF16)
    fc = w1_ref.shape[1] // n_split
    acc = x
    for c in range(n_split):
        sl = slice(c * fc, (c + 1) * fc)
        acc = acc + _dot(_swiglu_mid(_dot(hf, w1_ref[:, sl]), _dot(hf, w3_ref[:, sl])), w2_ref[sl, :])
    o_ref[...] = acc


def _ffn(x, g, w1, w3, w2):
    N, D = x.shape
    F = w1.shape[1]
    tm = _token_tile(N)
    n_split = 2
    assert F % (n_split * LANES) == 0
    tok = pl.BlockSpec((tm, D), lambda i: (i, 0))
    return pl.pallas_call(
        functools.partial(_ffn_kernel, n_split=n_split),
        grid=(N // tm,),
        in_specs=[tok, _resident((1, D)), _resident((D, F)), _resident((D, F)), _resident((F, D))],
        out_specs=tok,
        out_shape=jax.ShapeDtypeStruct((N, D), F32),
        compiler_params=_params(1),
        name="ffn",
    )(x, g, w1, w3, w2)


def _moe_kernel(x_ref, g_ref, wrh_ref, wrl_ref, br_ref, w1_ref, w3_ref, w2_ref, o_ref,
                hf_ref, gate_ref, rank_ref, *, ts, cap, n_exp):
    e = pl.program_id(1)
    lane = lax.broadcasted_iota(jnp.int32, (1, LANES), 1)
    n_sub = hf_ref.shape[0] // ts

    @pl.when(e == 0)
    def _():
        x = x_ref[...]
        hf = _rms(x, g_ref[...])
        hi = hf.astype(BF16)
        lo = (hf - hi.astype(F32)).astype(BF16)
        logits = (_dot_nt(wrh_ref[...], hi) + _dot_nt(wrh_ref[...], lo) + _dot_nt(wrl_ref[...], hi)
                  + br_ref[...])
        sub = lax.broadcasted_iota(jnp.int32, (ROUTER_ROWS, 1), 0)
        m1 = jnp.max(logits, axis=0, keepdims=True)
        i1 = jnp.min(jnp.where(logits == m1, sub, ROUTER_ROWS), axis=0, keepdims=True)
        rest = jnp.where(sub == i1, NEG, logits)
        m2 = jnp.max(rest, axis=0, keepdims=True)
        i2 = jnp.min(jnp.where(rest == m2, sub, ROUTER_ROWS), axis=0, keepdims=True)
        e2 = jnp.exp(m2 - m1)
        den = 1.0 + e2
        gate_t = jnp.where(sub == i1, 1.0 / den, 0.0) + jnp.where(sub == i2, e2 / den, 0.0)
        tm = gate_t.shape[1]
        gate_ref[...] = jnp.concatenate(
            [gate_t, jnp.zeros((LANES - ROUTER_ROWS, tm), F32)], axis=0).T
        hf_ref[...] = hi
        o_ref[...] = x
        sel_t = ((sub == i1) | (sub == i2))[:n_exp, :]
        before = (lax.broadcasted_iota(jnp.int32, (ts, ts), 0)
                  < lax.broadcasted_iota(jnp.int32, (ts, ts), 1))
        tri = jnp.where(before, 1.0, 0.0).astype(BF16)
        for s in range(n_sub):
            sel_s = sel_t[:, s * ts:(s + 1) * ts]
            rank = _dot(jnp.where(sel_s, 1.0, 0.0).astype(BF16), tri)
            rank_ref[s] = jnp.where(sel_s, rank, -1.0)

    row = lax.broadcasted_iota(jnp.int32, (cap, 1), 0)
    for s in range(n_sub):
        rows = slice(s * ts, (s + 1) * ts)
        rank_e = rank_ref[s, pl.ds(e, 1), :]
        n_tok = jnp.max(rank_e).astype(jnp.int32) + 1
        ge = jnp.sum(jnp.where(lane == e, gate_ref[rows, :], 0.0), axis=-1, keepdims=True)

        def chunk(c, carry):
            want = (row + c * cap).astype(F32)
            onehot = jnp.where(rank_e == want, 1.0, 0.0).astype(BF16)
            xg = _dot(onehot, hf_ref[rows, :]).astype(BF16)
            y = _dot(_swiglu_mid(_dot(xg, w1_ref[0]), _dot(xg, w3_ref[0])), w2_ref[0])
            back = lax.dot_general(onehot, y.astype(BF16), (((0,), (0,)), ((), ())),
                                   preferred_element_type=F32)
            o_ref[rows, :] += ge * back
            return carry

        lax.fori_loop(0, (n_tok + cap - 1) // cap, chunk, 0)


def _moe(x, g, wr_hi, wr_lo, br, w1, w3, w2):
    N, D = x.shape
    E, _, F = w1.shape
    tm = _token_tile(N, 1024)
    ts = _token_tile(tm, 512)
    assert E <= SUBLANES
    tok = pl.BlockSpec((tm, D), lambda i, e: (i, 0))
    kern = functools.partial(_moe_kernel, ts=ts, cap=MOE_CHUNK, n_exp=E)
    return pl.pallas_call(
        kern,
        grid=(N // tm, E),
        in_specs=[tok, _resident((1, D)), _resident((ROUTER_ROWS, D)), _resident((ROUTER_ROWS, D)),
                  _resident((ROUTER_ROWS, 1)),
                  pl.BlockSpec((1, D, F), lambda i, e: (e, 0, 0)),
                  pl.BlockSpec((1, D, F), lambda i, e: (e, 0, 0)),
                  pl.BlockSpec((1, F, D), lambda i, e: (e, 0, 0))],
        out_specs=tok,
        out_shape=jax.ShapeDtypeStruct((N, D), F32),
        scratch_shapes=[pltpu.VMEM((tm, D), BF16), pltpu.VMEM((tm, LANES), F32),
                        pltpu.VMEM((tm // ts, E, ts), F32)],
        compiler_params=_params(2),
        name="moe",
    )(x, g, wr_hi, wr_lo, br, w1, w3, w2)


def _rot_half_cols(w):
    half = ROPE_DIM // 2
    return jnp.concatenate([-w[..., half:], w[..., :half]], axis=-1)


def _rope_lanes(a):
    return jnp.pad(a, [(0, 0)] * (a.ndim - 1) + [(NOPE_DIM, HEAD_PAD - QK_DIM)])


def _pad_cols(w, n):
    return jnp.pad(w, [(0, 0)] * (w.ndim - 1) + [(0, n - w.shape[-1])])


def _head_pad(nope, rope):
    z = jnp.zeros(nope.shape[:-1] + (HEAD_PAD - QK_DIM,), nope.dtype)
    out = jnp.concatenate([nope, rope, z], axis=-1)
    return out.reshape(out.shape[:-2] + (N_HEADS * HEAD_PAD,))


def _block_diag_tiles(w):
    nblk, bw, _ = w.shape
    per = MXU_DIM // bw
    w4 = w.reshape(nblk // per, per, bw, bw)
    eye = jnp.eye(per, dtype=w.dtype)
    return jnp.einsum("cpij,pq->cpiqj", w4, eye).reshape(nblk // per, MXU_DIM, MXU_DIM)


def _layer_weights(l, p):
    D = p["w_in"].shape[1]
    q_lora = p["q_lat_norm"].shape[-1]
    kv_lora = p["kv_lat_norm"].shape[-1]
    w_in = p["w_in"][l]
    sp = (D, 2 * D, 2 * D + q_lora, 2 * D + q_lora + kv_lora, 2 * D + q_lora + kv_lora + ROPE_DIM,
          3 * D + q_lora + kv_lora + ROPE_DIM)
    x_rec, x_gate, c_q, c_kv, k_rot, g_rec, g_att = jnp.split(w_in, sp, axis=-1)
    w_in_perm = jnp.concatenate(
        [x_rec, x_gate, g_rec, g_att, c_q, c_kv, _rope_lanes(k_rot),
         _rope_lanes(_rot_half_cols(k_rot))], axis=-1).astype(BF16)

    wq = p["w_uq"][l].reshape(q_lora, N_HEADS, QK_DIM)
    wq_n, wq_r = wq[..., :NOPE_DIM], wq[..., NOPE_DIM:]
    wk = p["w_uk"][l].reshape(kv_lora, N_HEADS, NOPE_DIM)
    gq, gk = p["q_head_norm"][l], p["k_head_norm"][l]
    q_scale = (QK_DIM ** -0.5) * LOG2E
    return dict(
        gmix=p["norm_mix"][l].reshape(1, D), w_in=w_in_perm,
        conv_w=p["conv_w"][l], conv_b=p["conv_b"][l].reshape(1, D),
        wa=_block_diag_tiles(p["lru_wa"][l]).astype(BF16), ba=p["lru_ba"][l].reshape(1, D),
        wi=_block_diag_tiles(p["lru_wi"][l]).astype(BF16), bi=p["lru_bi"][l].reshape(1, D),
        sp=jax.nn.softplus(-p["lru_lambda"][l]).reshape(1, D),
        gq=p["q_lat_norm"][l].reshape(1, q_lora), gkv=p["kv_lat_norm"][l].reshape(1, kv_lora),
        wq_t=_head_pad(wq_n, wq_r).T.astype(BF16),
        gqh=(_pad_cols(gq, HEAD_PAD) * q_scale).reshape(HEAD_PAD, 1),
        wk=_head_pad(wk, jnp.zeros((kv_lora, N_HEADS, ROPE_DIM), F32)).astype(BF16),
        wv_t=p["w_uv"][l].T.astype(BF16),
        gkh=_pad_cols(gk, HEAD_PAD).reshape(1, HEAD_PAD),
        score_bound=QK_DIM * jnp.max(jnp.abs(gq)) * jnp.max(jnp.abs(gk)) * q_scale,
        w_out=p["w_out"][l].astype(BF16), gmem=p["norm_mem"][l].reshape(1, D),
        w_mq=p["w_mq"][l].astype(BF16), gmqh=p["mq_head_norm"][l].reshape(1, -1),
        w_mo=p["w_mo"][l].astype(BF16), gffn=p["norm_ffn"][l].reshape(1, D),
    )


def _rope_tables(pos):
    half = ROPE_DIM // 2
    inv_freq = ROPE_BASE ** (-jnp.arange(half, dtype=F32) / half)
    ang = pos.astype(F32)[:, None] * inv_freq[None, :]
    cos2 = jnp.concatenate([jnp.cos(ang), jnp.cos(ang)], axis=-1)
    sin2 = jnp.concatenate([jnp.sin(ang), jnp.sin(ang)], axis=-1)
    return dict(cos_k=_rope_lanes(cos2), sin_k=_rope_lanes(sin2), cos_q=cos2.T, sin_q=sin2.T)


def _layer(l, depth, x, tabs, q_off, conv_prev, h0, past, ckv_stack, mem_k, mem_v, w, p):
    B, L, D = x.shape
    prev8 = jnp.pad(conv_prev, ((0, 0), (SUBLANES - (CONV_W - 1), 0), (0, 0)))
    h0_8 = jnp.pad(h0[:, None, :], ((0, 0), (SUBLANES - 1, 0), (0, 0)))
    grec, gatt, cq, ckv, kr, ctail, htail = _inproj(x, prev8, h0_8, w, tabs["cos_k"], tabs["sin_k"],
                                                    l, depth, ckv_stack)

    q = _qprep(cq, w, tabs["cos_q"], tabs["sin_q"])
    if past is None:
        t_valid = L
        tk = min(512, L)
    else:
        t_valid = past[0].shape[2] + L
        tk = 512
    k, vt = _kvprep(ckv, l, kr, w, tk, past)
    flash = functools.partial(_flash, q_off=q_off, t_valid=t_valid, tk=tk)
    att = lax.cond(w["score_bound"] <= SAFE_LOG2_RANGE, functools.partial(flash, bounded=True),
                   functools.partial(flash, bounded=False), q, k, vt)

    x2 = _post(x, grec, gatt, att, mem_k, mem_v, w).reshape(B * L, D)
    j = l // 2
    if l % 2 == 0:
        x3 = _ffn(x2, w["gffn"], p["ffn_w1"][j].astype(BF16), p["ffn_w3"][j].astype(BF16),
                  p["ffn_w2"][j].astype(BF16))
    else:
        E = p["moe_router"].shape[-1]
        wr = _pad_cols(p["moe_router"][j], ROUTER_ROWS).T
        wr_hi = wr.astype(BF16)
        wr_lo = (wr - wr_hi.astype(F32)).astype(BF16)
        br = jnp.concatenate([p["moe_router_b"][j],
                              jnp.full((ROUTER_ROWS - E,), NEG, F32)]).reshape(ROUTER_ROWS, 1)
        x3 = _moe(x2, w["gffn"], wr_hi, wr_lo, br, p["moe_w1"][j].astype(BF16),
                  p["moe_w3"][j].astype(BF16), p["moe_w2"][j].astype(BF16))
    conv_state = ctail[:, SUBLANES - (CONV_W - 1):, :]
    return x3.reshape(B, L, D), conv_state, htail[:, SUBLANES - 1, :], ckv, kr[..., NOPE_DIM:QK_DIM]


def kernel(x_prompt, x_sample, cache_ckv, cache_krope, cache_mem_k, cache_mem_v, state_lru, state_conv, mem_prompt, norm_mix, w_in, conv_w, conv_b, lru_wa, lru_ba, lru_wi, lru_bi, lru_lambda, q_lat_norm, w_uq, kv_lat_norm, w_uk, w_uv, q_head_norm, k_head_norm, w_out, norm_mem, mem_in_norm, w_mq, w_mk, w_mv, w_mo, mq_head_norm, mk_head_norm, norm_ffn, ffn_w1, ffn_w3, ffn_w2, moe_router, moe_router_b, moe_w1, moe_w3, moe_w2):
    p = dict(norm_mix=norm_mix, w_in=w_in, conv_w=conv_w, conv_b=conv_b, lru_wa=lru_wa, lru_ba=lru_ba,
             lru_wi=lru_wi, lru_bi=lru_bi, lru_lambda=lru_lambda, q_lat_norm=q_lat_norm, w_uq=w_uq,
             kv_lat_norm=kv_lat_norm, w_uk=w_uk, w_uv=w_uv, q_head_norm=q_head_norm,
             k_head_norm=k_head_norm, w_out=w_out, norm_mem=norm_mem, w_mq=w_mq, w_mo=w_mo,
             mq_head_norm=mq_head_norm, norm_ffn=norm_ffn, ffn_w1=ffn_w1, ffn_w3=ffn_w3,
             ffn_w2=ffn_w2, moe_router=moe_router, moe_router_b=moe_router_b, moe_w1=moe_w1,
             moe_w3=moe_w3, moe_w2=moe_w2)
    depth = w_in.shape[0]
    weights = [_layer_weights(l, p) for l in range(depth)]

    Bp, Lp, D = x_prompt.shape
    Bs, Ls, _ = x_sample.shape
    t_past = cache_ckv.shape[2]
    M = mem_prompt.shape[1]

    tabs_p = _rope_tables(jnp.arange(Lp, dtype=jnp.int32))
    zero_conv = jnp.zeros((Bp, CONV_W - 1, D), F32)
    zero_h = jnp.zeros((Bp, D), F32)
    x = x_prompt
    ckv_p = None
    outs_p = [[] for _ in range(5)]
    for l in range(depth):
        mk, mv = _memkv(mem_prompt, mem_in_norm[l], w_mk[l], w_mv[l], mk_head_norm[l])
        x, cs, hl, ckv_p, kr = _layer(l, depth, x, tabs_p, 0, zero_conv, zero_h, None, ckv_p, mk, mv,
                                      weights[l], p)
        for acc, val in zip(outs_p, (kr, hl, cs, mk.reshape(Bp, M, MEM_HEADS, -1),
                                     mv.reshape(Bp, M, MEM_HEADS, -1))):
            acc.append(val)
    y_prompt = x

    tabs_s = _rope_tables(t_past + jnp.arange(Ls, dtype=jnp.int32))
    x = x_sample
    ckv_s = None
    outs_s = [[] for _ in range(3)]
    cache_kr = _rope_lanes(cache_krope)
    for l in range(depth):
        x, cs, hl, ckv_s, kr = _layer(l, depth, x, tabs_s, t_past, state_conv[l], state_lru[l],
                                      (cache_ckv, cache_kr, l), ckv_s, cache_mem_k[l].reshape(Bs, M, D),
                                      cache_mem_v[l].reshape(Bs, M, D), weights[l], p)
        for acc, val in zip(outs_s, (kr, hl, cs)):
            acc.append(val)
    y_sample = x

    return ((y_prompt, y_sample, ckv_p) + tuple(jnp.stack(a) for a in outs_p)
            + (ckv_s,) + tuple(jnp.stack(a) for a in outs_s))
```

```python
import functools
import math

import jax
import jax.numpy as jnp
from jax import lax
from jax.experimental import pallas as pl
from jax.experimental.pallas import tpu as pltpu

F32 = jnp.float32
BF16 = jnp.bfloat16

EPS = 1e-6
CHUNK = 64
LRU_C = 8.0
CONV_W = 4
N_HEADS = 16
NOPE_DIM = 64
ROPE_DIM = 32
QK_DIM = NOPE_DIM + ROPE_DIM
ROPE_BASE = 10000.0
MEM_HEADS = 4

LANES = 128
SUBLANES = 8
MXU_DIM = 256
VMEM_LIMIT = 56 * 1024 * 1024

HEAD_PAD = LANES
ROUTER_ROWS = 16
N_INPROJ_INPUTS = 16
MOE_CAPS = (112, 144, 176, 208)
SAFE_LOG2_RANGE = 60.0
FLASH_HEADS = 8
FLASH_LAG = 8
NEG = -1e30
LOG2E = 1.4426950408889634


def _resident(shape):
    nd = len(shape)
    return pl.BlockSpec(shape, lambda *_: (0,) * nd, pipeline_mode=pl.Buffered(1))


def _params(n_axes):
    return pltpu.CompilerParams(dimension_semantics=("arbitrary",) * n_axes,
                                vmem_limit_bytes=VMEM_LIMIT)


def _rms(x, gain):
    return x * lax.rsqrt(jnp.mean(x * x, axis=-1, keepdims=True) + EPS) * gain


def _dot(a, b):
    return jnp.dot(a, b, preferred_element_type=F32)


def _dot_nt(a, b):
    return lax.dot_general(a, b, (((1,), (1,)), ((), ())), preferred_element_type=F32)


def _token_tile(n, cap=512):
    t = min(n, cap)
    assert n % t == 0 and t % SUBLANES == 0
    return t


def _memkv_kernel(mem_ref, gin_ref, wk_ref, wv_ref, gk_ref, k_ref, v_ref):
    mn = _rms(mem_ref[0], gin_ref[...]).astype(BF16)
    k = _dot(mn, wk_ref[...])
    hd = gk_ref.shape[-1]
    for h in range(MEM_HEADS):
        sl = slice(h * hd, (h + 1) * hd)
        k_ref[0, :, sl] = _rms(k[:, sl], gk_ref[...])
    v_ref[0] = _dot(mn, wv_ref[...])


def _memkv(mem, g_in, w_k, w_v, g_k):
    B, M, D = mem.shape
    tok = pl.BlockSpec((1, M, D), lambda b: (b, 0, 0))
    return pl.pallas_call(
        _memkv_kernel,
        grid=(B,),
        in_specs=[tok, _resident((1, D)), _resident((D, D)), _resident((D, D)),
                  _resident((1, D // MEM_HEADS))],
        out_specs=[tok, tok],
        out_shape=[jax.ShapeDtypeStruct((B, M, D), F32)] * 2,
        compiler_params=_params(1),
        name="memkv",
    )(mem, g_in.reshape(1, D), w_k.astype(BF16), w_v.astype(BF16), g_k.reshape(1, -1))


def _inproj_kernel(x_ref, prev_ref, h0_ref, gmix_ref, win_ref, cw_ref, cb_ref, wa_ref, ba_ref,
                   wi_ref, bi_ref, sp_ref, gq_ref, gkv_ref, cos_ref, sin_ref, *rest,
                   tm, d, q_lora, kv_lora):
    (grec_ref, gatt_ref, cq_ref, ckv_ref, kr_ref, ctail_ref, htail_ref,
     xext_ref, hcar_ref) = rest[-9:]
    @pl.when(pl.program_id(1) == 0)
    def _():
        xext_ref[0:SUBLANES, :] = prev_ref[0]
        hcar_ref[...] = h0_ref[0]

    hn = _rms(x_ref[0], gmix_ref[...]).astype(BF16)

    def proj(c0, c1):
        return _dot(hn, win_ref[:, c0:c1])

    x_rec = proj(0, d)
    xext_ref[SUBLANES:SUBLANES + tm, :] = x_rec
    xc = cb_ref[...] + cw_ref[CONV_W - 1:CONV_W, :] * x_rec
    for j in range(CONV_W - 1):
        off = SUBLANES - (CONV_W - 1) + j
        xc = xc + cw_ref[j:j + 1, :] * xext_ref[off:off + tm, :]
    tail = x_rec[tm - SUBLANES:tm, :]
    ctail_ref[0] = tail
    xext_ref[0:SUBLANES, :] = tail

    xcb = xc.astype(BF16)
    nb = d // MXU_DIM
    gr = jnp.concatenate(
        [_dot(xcb[:, c * MXU_DIM:(c + 1) * MXU_DIM], wa_ref[c]) for c in range(nb)], axis=1)
    gi = jnp.concatenate(
        [_dot(xcb[:, c * MXU_DIM:(c + 1) * MXU_DIM], wi_ref[c]) for c in range(nb)], axis=1)
    r = jax.nn.sigmoid(gr + ba_ref[...])
    ig = jax.nn.sigmoid(gi + bi_ref[...])
    log_a = (-LRU_C) * r * sp_ref[...]
    a = jnp.exp(log_a)
    u = jnp.sqrt(1.0 - a * a) * (ig * xc)

    row = lax.broadcasted_iota(jnp.int32, (tm, 1), 0) & (SUBLANES - 1)
    s = 1
    while s < SUBLANES:
        keep = row >= s
        u = a * jnp.where(keep, pltpu.roll(u, s, 0), 0.0) + u
        a = a * jnp.where(keep, pltpu.roll(a, s, 0), 1.0)
        s *= 2
    h_prev = hcar_ref[SUBLANES - 1:SUBLANES, :]
    groups = []
    for g in range(tm // SUBLANES):
        rows = slice(g * SUBLANES, (g + 1) * SUBLANES)
        hg = a[rows] * h_prev + u[rows]
        groups.append(hg)
        h_prev = hg[SUBLANES - 1:SUBLANES, :]
    h = jnp.concatenate(groups, axis=0)
    htail = groups[-1]
    hcar_ref[...] = htail
    htail_ref[0] = htail

    x_gate = proj(d, 2 * d)
    g_rec = proj(2 * d, 3 * d)
    grec_ref[0] = jax.nn.sigmoid(g_rec) * (jax.nn.gelu(x_gate) * h)
    gatt_ref[0] = proj(3 * d, 4 * d)

    o = 4 * d
    cq_ref[0] = _rms(proj(o, o + q_lora), gq_ref[...]).astype(BF16)
    o += q_lora
    ckv_ref[0, 0] = _rms(proj(o, o + kv_lora), gkv_ref[...])
    o += kv_lora
    kr = proj(o, o + LANES) * cos_ref[...] + proj(o + LANES, o + 2 * LANES) * sin_ref[...]
    kr_ref[0] = kr


def _inproj(x, prev8, h0_8, w, cos_k, sin_k, layer, depth, ckv_stack):
    B, L, D = x.shape
    tm = _token_tile(L, 256)
    q_lora, kv_lora = w["gq"].shape[-1], w["gkv"].shape[-1]
    ncol = w["w_in"].shape[-1]
    tok = lambda n: pl.BlockSpec((1, tm, n), lambda b, l: (b, l, 0))
    per_b = pl.BlockSpec((1, SUBLANES, D), lambda b, l: (b, 0, 0))
    tab = pl.BlockSpec((tm, LANES), lambda b, l: (l, 0))
    nb = D // MXU_DIM
    kern = functools.partial(_inproj_kernel, tm=tm, d=D, q_lora=q_lora, kv_lora=kv_lora)
    alias_args = () if ckv_stack is None else (ckv_stack,)
    alias_spec = [pl.BlockSpec(memory_space=pl.ANY)] * len(alias_args)
    return pl.pallas_call(
        kern,
        grid=(B, L // tm),
        in_specs=[tok(D), per_b, per_b, _resident((1, D)), _resident((D, ncol)),
                  _resident((CONV_W, D)), _resident((1, D)),
                  _resident((nb, MXU_DIM, MXU_DIM)), _resident((1, D)),
                  _resident((nb, MXU_DIM, MXU_DIM)), _resident((1, D)), _resident((1, D)),
                  _resident((1, q_lora)), _resident((1, kv_lora)), tab, tab] + alias_spec,
        out_specs=[tok(D), tok(D), tok(q_lora),
                   pl.BlockSpec((1, 1, tm, kv_lora), lambda b, l: (layer, b, l, 0)),
                   tok(HEAD_PAD), per_b, per_b],
        out_shape=[jax.ShapeDtypeStruct((B, L, D), F32), jax.ShapeDtypeStruct((B, L, D), F32),
                   jax.ShapeDtypeStruct((B, L, q_lora), BF16),
                   jax.ShapeDtypeStruct((depth, B, L, kv_lora), F32),
                   jax.ShapeDtypeStruct((B, L, HEAD_PAD), F32),
                   jax.ShapeDtypeStruct((B, SUBLANES, D), F32),
                   jax.ShapeDtypeStruct((B, SUBLANES, D), F32)],
        scratch_shapes=[pltpu.VMEM((SUBLANES + tm, D), F32), pltpu.VMEM((SUBLANES, D), F32)],
        input_output_aliases={N_INPROJ_INPUTS: 3} if alias_args else {},
        compiler_params=_params(2),
        name="inproj",
    )(x, prev8, h0_8, w["gmix"], w["w_in"], w["conv_w"], w["conv_b"], w["wa"], w["ba"],
      w["wi"], w["bi"], w["sp"], w["gq"], w["gkv"], cos_k, sin_k, *alias_args)


def _qprep_kernel(cq_ref, wt_ref, cos_ref, sin_ref, g_ref, q_ref):
    qt = _dot_nt(wt_ref[...], cq_ref[0])
    half = ROPE_DIM // 2
    tm = qt.shape[1]
    cos, sin = cos_ref[...], sin_ref[...]
    g_nope, g_rope = g_ref[:NOPE_DIM, :], g_ref[NOPE_DIM:QK_DIM, :]
    pad = jnp.zeros((HEAD_PAD - QK_DIM, tm), BF16)
    for h in range(N_HEADS):
        r0 = h * HEAD_PAD
        nope = qt[r0:r0 + NOPE_DIM, :]
        x1 = qt[r0 + NOPE_DIM:r0 + NOPE_DIM + half, :]
        x2 = qt[r0 + NOPE_DIM + half:r0 + QK_DIM, :]
        rope = (jnp.concatenate([x1, x2], axis=0) * cos + jnp.concatenate([-x2, x1], axis=0) * sin)
        ss = (jnp.sum(nope * nope, axis=0, keepdims=True)
              + jnp.sum(rope * rope, axis=0, keepdims=True)) * (1.0 / QK_DIM)
        inv = lax.rsqrt(ss + EPS)
        q_ref[0, r0:r0 + NOPE_DIM, :] = (nope * inv * g_nope).astype(BF16)
        q_ref[0, r0 + NOPE_DIM:r0 + QK_DIM, :] = (rope * inv * g_rope).astype(BF16)
        q_ref[0, r0 + QK_DIM:r0 + HEAD_PAD, :] = pad


def _qprep(cq, w, cos_q, sin_q):
    B, L, QL = cq.shape
    tm = _token_tile(L)
    n = N_HEADS * HEAD_PAD
    tab = pl.BlockSpec((ROPE_DIM, tm), lambda b, l: (0, l))
    return pl.pallas_call(
        _qprep_kernel,
        grid=(B, L // tm),
        in_specs=[pl.BlockSpec((1, tm, QL), lambda b, l: (b, l, 0)),
                  _resident((n, QL)), tab, tab, _resident((HEAD_PAD, 1))],
        out_specs=pl.BlockSpec((1, n, tm), lambda b, l: (b, 0, l)),
        out_shape=jax.ShapeDtypeStruct((B, n, L), BF16),
        compiler_params=_params(2),
        name="qprep",
    )(cq, w["wq_t"], cos_q, sin_q, w["gqh"])


def _kvprep_kernel(*refs, n_past, tk):
    if n_past:
        past_c_ref, past_kr_ref, new_c_ref, new_kr_ref, wk_ref, wvt_ref, g_ref, k_ref, vt_ref = refs
        is_new = pl.program_id(1) >= n_past

        def pick(new_ref, past_ref):
            new = new_ref[...].reshape(new_ref.shape[-2:])
            fill = jnp.zeros((tk - new.shape[0], new.shape[1]), new.dtype)
            return jnp.where(is_new, jnp.concatenate([new, fill], axis=0), past_ref[0, 0])

        c32, kr = pick(new_c_ref, past_c_ref), pick(new_kr_ref, past_kr_ref)
    else:
        new_c_ref, new_kr_ref, wk_ref, wvt_ref, g_ref, k_ref, vt_ref = refs
        c32, kr = new_c_ref[0, 0], new_kr_ref[0]
    c = c32.astype(BF16)
    k = _dot(c, wk_ref[...])
    for h in range(N_HEADS):
        sl = slice(h * HEAD_PAD, (h + 1) * HEAD_PAD)
        kh = k[:, sl] + kr
        ss = jnp.sum(kh * kh, axis=-1, keepdims=True) * (1.0 / QK_DIM)
        k_ref[0, :, sl] = (kh * lax.rsqrt(ss + EPS) * g_ref[...]).astype(BF16)
    vt_ref[0, 0] = _dot_nt(wvt_ref[...], c).astype(BF16)


def _kvprep(ckv, layer, kr, w, tk, past=None):
    _, B, L, KV = ckv.shape
    n = N_HEADS * HEAD_PAD
    dv = w["wv_t"].shape[0]
    if past is None:
        assert L % tk == 0
        n_past, n_tiles = 0, L // tk
        in_specs = [pl.BlockSpec((1, 1, tk, KV), lambda b, t: (layer, b, t, 0)),
                    pl.BlockSpec((1, tk, HEAD_PAD), lambda b, t: (b, t, 0))]
        args = (ckv, kr)
    else:
        cache_c, cache_kr, layer = past
        t_past = cache_c.shape[2]
        assert t_past % tk == 0 and L <= tk and L % SUBLANES == 0
        n_past = t_past // tk
        n_tiles = n_past + 1
        old = lambda width: pl.BlockSpec(
            (1, 1, tk, width), lambda b, t: (layer, b, jnp.minimum(t, n_past - 1), 0))
        in_specs = [old(KV), old(HEAD_PAD),
                    pl.BlockSpec((1, 1, L, KV), lambda b, t: (layer, b, 0, 0)),
                    pl.BlockSpec((1, L, HEAD_PAD), lambda b, t: (b, 0, 0))]
        args = (cache_c, cache_kr, ckv, kr)
    return pl.pallas_call(
        functools.partial(_kvprep_kernel, n_past=n_past, tk=tk),
        grid=(B, n_tiles),
        in_specs=in_specs + [_resident((KV, n)), _resident((dv, KV)), _resident((1, HEAD_PAD))],
        out_specs=[pl.BlockSpec((1, tk, n), lambda b, t: (b, t, 0)),
                   pl.BlockSpec((1, 1, dv, tk), lambda b, t: (b, t, 0, 0))],
        out_shape=[jax.ShapeDtypeStruct((B, n_tiles * tk, n), BF16),
                   jax.ShapeDtypeStruct((B, n_tiles, dv, tk), BF16)],
        compiler_params=_params(2),
        name="kvprep",
    )(*args, w["wk"], w["wv_t"], w["gkh"])


def _flash_kernel(q_ref, k_ref, vt_ref, o_ref, m_ref, l_ref, acc_ref, *, tq, tk, q_off, t_valid,
                  v_dim, heads, bounded, lag, split_diag):
    q_start = q_off + pl.program_id(2) * tq
    n_full = jnp.minimum(q_start, t_valid) // tk
    n_end = (jnp.minimum(q_start + tq, t_valid) + tk - 1) // tk
    shift = int(math.log2(CHUNK))

    def chunk_of(pos):
        return lax.shift_right_logical(pos, shift)

    m_ref[...] = jnp.full(m_ref.shape, NEG, F32)
    l_ref[...] = jnp.zeros(l_ref.shape, F32)
    acc_ref[...] = jnp.zeros(acc_ref.shape, F32)

    def accumulate(hh, s, vj, q0):
        lanes = slice(q0, tq)
        if bounded:
            p = jnp.exp2(s)
            l_ref[hh, :, lanes] += jnp.sum(p, axis=0, keepdims=True)
        else:
            m = m_ref[hh, :, lanes]
            m_new = jnp.maximum(m, jnp.max(s, axis=0, keepdims=True))
            alpha = jnp.exp2(m - m_new)
            p = jnp.exp2(s - m_new)
            l_ref[hh, :, lanes] = alpha * l_ref[hh, :, lanes] + jnp.sum(p, axis=0, keepdims=True)
            m_ref[hh, :, lanes] = m_new
            acc_ref[hh, :, lanes] = alpha * acc_ref[hh, :, lanes]
        acc_ref[hh, :, lanes] += _dot(vj, p.astype(BF16))

    def tile(j, masked):
        if masked:
            k_pos = j * tk + lax.broadcasted_iota(jnp.int32, (tk, 1), 0)
            q_pos = q_start + lax.broadcasted_iota(jnp.int32, (1, tq), 1)
            vis = (chunk_of(k_pos) <= chunk_of(q_pos)) & (k_pos < t_valid)
        pending = {}
        for t in range(heads + lag):
            if t < heads:
                qh = q_ref[0, t * HEAD_PAD:(t + 1) * HEAD_PAD, :]
                kj = k_ref[0, pl.ds(pl.multiple_of(j * tk, tk), tk), t * HEAD_PAD:(t + 1) * HEAD_PAD]
                pending[t] = _dot(kj, qh)
            if t >= lag:
                hh = t - lag
                s = pending.pop(hh)
                if masked:
                    s = jnp.where(vis, s, NEG)
                accumulate(hh, s, vt_ref[0, j, hh * v_dim:(hh + 1) * v_dim, :], 0)

    def diagonal(j):
        half = tk // 2
        vis = (chunk_of(lax.broadcasted_iota(jnp.int32, (half, 1), 0))
               <= chunk_of(lax.broadcasted_iota(jnp.int32, (1, tq), 1)))
        for k0, q0 in ((0, 0), (half, half)):
            for hh in range(heads):
                qh = q_ref[0, hh * HEAD_PAD:(hh + 1) * HEAD_PAD, q0:]
                kj = k_ref[0, pl.ds(pl.multiple_of(j * tk + k0, half), half),
                           hh * HEAD_PAD:(hh + 1) * HEAD_PAD]
                s = jnp.where(vis[:, :tq - q0], _dot(kj, qh), NEG)
                accumulate(hh, s, vt_ref[0, j, hh * v_dim:(hh + 1) * v_dim, k0:k0 + half], q0)

    def loop(lo, hi, body):
        lax.fori_loop(lo, hi, lambda j, c: (body(j), c)[1], 0)

    loop(0, n_full, lambda j: tile(j, False))
    if split_diag:
        diagonal(n_full)
    else:
        loop(n_full, n_end, lambda j: tile(j, True))
    o_ref[0] = jnp.concatenate([acc_ref[hh] / l_ref[hh] for hh in range(heads)], axis=0).T


def _flash(q, k, vt, *, q_off, t_valid, tk, bounded, heads=FLASH_HEADS):
    B, n, L = q.shape
    T = k.shape[1]
    dv = vt.shape[2]
    v_dim = dv // N_HEADS
    tq = _token_tile(L)
    assert tq % CHUNK == 0 and q_off % CHUNK == 0 and (heads * v_dim) % LANES == 0
    split_diag = tq == tk and q_off == 0 and t_valid == T and (tk // 2) % max(CHUNK, LANES) == 0
    kern = functools.partial(_flash_kernel, tq=tq, tk=tk, q_off=q_off, t_valid=t_valid, v_dim=v_dim,
                             heads=heads, bounded=bounded, lag=min(FLASH_LAG, heads),
                             split_diag=split_diag)
    return pl.pallas_call(
        kern,
        grid=(B, N_HEADS // heads, L // tq),
        in_specs=[pl.BlockSpec((1, heads * HEAD_PAD, tq), lambda b, p, i: (b, p, i)),
                  pl.BlockSpec((1, T, heads * HEAD_PAD), lambda b, p, i: (b, 0, p)),
                  pl.BlockSpec((1, T // tk, heads * v_dim, tk), lambda b, p, i: (b, 0, p, 0))],
        out_specs=pl.BlockSpec((1, tq, heads * v_dim), lambda b, p, i: (b, i, p)),
        out_shape=jax.ShapeDtypeStruct((B, L, dv), F32),
        scratch_shapes=[pltpu.VMEM((heads, 1, tq), F32), pltpu.VMEM((heads, 1, tq), F32),
                        pltpu.VMEM((heads, v_dim, tq), F32)],
        compiler_params=_params(3),
        name="flash",
    )(q, k, vt)


def _post_kernel(x_ref, grec_ref, gatt_ref, att_ref, wo_ref, gmem_ref, wmq_ref, gqh_ref,
                 mk_ref, mv_ref, wmo_ref, o_ref):
    mixed = (grec_ref[0] + jax.nn.sigmoid(gatt_ref[0]) * att_ref[0]).astype(BF16)
    x1 = x_ref[0] + _dot(mixed, wo_ref[...])
    qm = _dot(_rms(x1, gmem_ref[...]).astype(BF16), wmq_ref[...])
    hd = gqh_ref.shape[-1]
    outs = []
    for h in range(MEM_HEADS):
        sl = slice(h * hd, (h + 1) * hd)
        qh = (_rms(qm[:, sl], gqh_ref[...]) * (hd ** -0.5)).astype(BF16)
        s = _dot_nt(qh, mk_ref[0, :, sl])
        p = jnp.exp(s - jnp.max(s, axis=-1, keepdims=True))
        l = jnp.sum(p, axis=-1, keepdims=True)
        outs.append((_dot(p.astype(BF16), mv_ref[0, :, sl]) / l).astype(BF16))
    o_ref[0] = x1 + _dot(jnp.concatenate(outs, axis=1), wmo_ref[...])


def _post(x, grec, gatt, att, mem_k, mem_v, w):
    B, L, D = x.shape
    M = mem_k.shape[1]
    tm = _token_tile(L)
    tok = pl.BlockSpec((1, tm, D), lambda b, l: (b, l, 0))
    mem = pl.BlockSpec((1, M, D), lambda b, l: (b, 0, 0))
    return pl.pallas_call(
        _post_kernel,
        grid=(B, L // tm),
        in_specs=[tok, tok, tok, tok, _resident((D, D)), _resident((1, D)), _resident((D, D)),
                  _resident((1, D // MEM_HEADS)), mem, mem, _resident((D, D))],
        out_specs=tok,
        out_shape=jax.ShapeDtypeStruct((B, L, D), F32),
        compiler_params=_params(2),
        name="post",
    )(x, grec, gatt, att, w["w_out"], w["gmem"], w["w_mq"], w["gmqh"],
      mem_k.astype(BF16), mem_v.astype(BF16), w["w_mo"])


def _swiglu_mid(a, b):
    return (a * jax.nn.sigmoid(a) * b).astype(BF16)


def _ffn_kernel(x_ref, g_ref, w1_ref, w3_ref, w2_ref, o_ref, *, n_split):
    x = x_ref[...]
    hf = _rms(x, g_ref[...]).astype(BF16)
    fc = w1_ref.shape[1] // n_split
    acc = x
    for c in range(n_split):
        sl = slice(c * fc, (c + 1) * fc)
        acc = acc + _dot(_swiglu_mid(_dot(hf, w1_ref[:, sl]), _dot(hf, w3_ref[:, sl])), w2_ref[sl, :])
    o_ref[...] = acc


def _ffn(x, g, w1, w3, w2):
    N, D = x.shape
    F = w1.shape[1]
    tm = _token_tile(N)
    n_split = 2
    assert F % (n_split * LANES) == 0
    tok = pl.BlockSpec((tm, D), lambda i: (i, 0))
    return pl.pallas_call(
        functools.partial(_ffn_kernel, n_split=n_split),
        grid=(N // tm,),
        in_specs=[tok, _resident((1, D)), _resident((D, F)), _resident((D, F)), _resident((F, D))],
        out_specs=tok,
        out_shape=jax.ShapeDtypeStruct((N, D), F32),
        compiler_params=_params(1),
        name="ffn",
    )(x, g, w1, w3, w2)


def _moe_kernel(x_ref, g_ref, wrh_ref, wrl_ref, br_ref, w1_ref, w3_ref, w2_ref, o_ref,
                hf_ref, gate_ref, rank_ref, *, ts, caps, n_exp):
    e = pl.program_id(1)
    lane = lax.broadcasted_iota(jnp.int32, (1, LANES), 1)
    n_sub = hf_ref.shape[0] // ts

    @pl.when(e == 0)
    def _():
        x = x_ref[...]
        hf = _rms(x, g_ref[...])
        hi = hf.astype(BF16)
        lo = (hf - hi.astype(F32)).astype(BF16)
        logits = (_dot_nt(wrh_ref[...], hi) + _dot_nt(wrh_ref[...], lo) + _dot_nt(wrl_ref[...], hi)
                  + br_ref[...])
        sub = lax.broadcasted_iota(jnp.int32, (ROUTER_ROWS, 1), 0)
        m1 = jnp.max(logits, axis=0, keepdims=True)
        i1 = jnp.min(jnp.where(logits == m1, sub, ROUTER_ROWS), axis=0, keepdims=True)
        rest = jnp.where(sub == i1, NEG, logits)
        m2 = jnp.max(rest, axis=0, keepdims=True)
        i2 = jnp.min(jnp.where(rest == m2, sub, ROUTER_ROWS), axis=0, keepdims=True)
        e2 = jnp.exp(m2 - m1)
        den = 1.0 + e2
        gate_t = jnp.where(sub == i1, 1.0 / den, 0.0) + jnp.where(sub == i2, e2 / den, 0.0)
        tm = gate_t.shape[1]
        gate_ref[...] = jnp.concatenate(
            [gate_t, jnp.zeros((LANES - ROUTER_ROWS, tm), F32)], axis=0).T
        hf_ref[...] = hi
        o_ref[...] = x
        sel_t = ((sub == i1) | (sub == i2))[:n_exp, :]
        before = (lax.broadcasted_iota(jnp.int32, (ts, ts), 0)
                  < lax.broadcasted_iota(jnp.int32, (ts, ts), 1))
        tri = jnp.where(before, 1.0, 0.0).astype(BF16)
        for s in range(n_sub):
            sel_s = sel_t[:, s * ts:(s + 1) * ts]
            rank = _dot(jnp.where(sel_s, 1.0, 0.0).astype(BF16), tri)
            rank_ref[s] = jnp.where(sel_s, rank, -1.0)

    for s in range(n_sub):
        rows = slice(s * ts, (s + 1) * ts)
        rank_e = rank_ref[s, pl.ds(e, 1), :]
        n_tok = jnp.max(rank_e).astype(jnp.int32) + 1
        ge = jnp.sum(jnp.where(lane == e, gate_ref[rows, :], 0.0), axis=-1, keepdims=True)

        def chunk(base, size):
            want = (lax.broadcasted_iota(jnp.int32, (size, 1), 0) + base).astype(F32)
            onehot = jnp.where(rank_e == want, 1.0, 0.0).astype(BF16)
            xg = _dot(onehot, hf_ref[rows, :]).astype(BF16)
            y = _dot(_swiglu_mid(_dot(xg, w1_ref[0]), _dot(xg, w3_ref[0])), w2_ref[0])
            back = lax.dot_general(onehot, y.astype(BF16), (((0,), (0,)), ((), ())),
                                   preferred_element_type=F32)
            o_ref[rows, :] += ge * back

        n_big = n_tok // caps[-1]
        lax.fori_loop(0, n_big, lambda c, carry: (chunk(c * caps[-1], caps[-1]), carry)[1], 0)
        rest = n_tok - n_big * caps[-1]
        bucket = sum((rest > c).astype(jnp.int32) for c in (0,) + caps[:-1])
        lax.switch(bucket, [lambda: None]
                   + [functools.partial(chunk, n_big * caps[-1], c) for c in caps])


def _moe(x, g, wr_hi, wr_lo, br, w1, w3, w2):
    N, D = x.shape
    E, _, F = w1.shape
    tm = _token_tile(N, 1024)
    ts = _token_tile(tm, 512)
    assert E <= SUBLANES
    tok = pl.BlockSpec((tm, D), lambda i, e: (i, 0))
    kern = functools.partial(_moe_kernel, ts=ts, caps=MOE_CAPS, n_exp=E)
    return pl.pallas_call(
        kern,
        grid=(N // tm, E),
        in_specs=[tok, _resident((1, D)), _resident((ROUTER_ROWS, D)), _resident((ROUTER_ROWS, D)),
                  _resident((ROUTER_ROWS, 1)),
                  pl.BlockSpec((1, D, F), lambda i, e: (e, 0, 0)),
                  pl.BlockSpec((1, D, F), lambda i, e: (e, 0, 0)),
                  pl.BlockSpec((1, F, D), lambda i, e: (e, 0, 0))],
        out_specs=tok,
        out_shape=jax.ShapeDtypeStruct((N, D), F32),
        scratch_shapes=[pltpu.VMEM((tm, D), BF16), pltpu.VMEM((tm, LANES), F32),
                        pltpu.VMEM((tm // ts, E, ts), F32)],
        compiler_params=_params(2),
        name="moe",
    )(x, g, wr_hi, wr_lo, br, w1, w3, w2)


def _rot_half_cols(w):
    half = ROPE_DIM // 2
    return jnp.concatenate([-w[..., half:], w[..., :half]], axis=-1)


def _rope_lanes(a):
    return jnp.pad(a, [(0, 0)] * (a.ndim - 1) + [(NOPE_DIM, HEAD_PAD - QK_DIM)])


def _pad_cols(w, n):
    return jnp.pad(w, [(0, 0)] * (w.ndim - 1) + [(0, n - w.shape[-1])])


def _head_pad(nope, rope):
    z = jnp.zeros(nope.shape[:-1] + (HEAD_PAD - QK_DIM,), nope.dtype)
    out = jnp.concatenate([nope, rope, z], axis=-1)
    return out.reshape(out.shape[:-2] + (N_HEADS * HEAD_PAD,))


def _block_diag_tiles(w):
    nblk, bw, _ = w.shape
    per = MXU_DIM // bw
    w4 = w.reshape(nblk // per, per, bw, bw)
    eye = jnp.eye(per, dtype=w.dtype)
    return jnp.einsum("cpij,pq->cpiqj", w4, eye).reshape(nblk // per, MXU_DIM, MXU_DIM)


def _layer_weights(l, p):
    D = p["w_in"].shape[1]
    q_lora = p["q_lat_norm"].shape[-1]
    kv_lora = p["kv_lat_norm"].shape[-1]
    w_in = p["w_in"][l]
    sp = (D, 2 * D, 2 * D + q_lora, 2 * D + q_lora + kv_lora, 2 * D + q_lora + kv_lora + ROPE_DIM,
          3 * D + q_lora + kv_lora + ROPE_DIM)
    x_rec, x_gate, c_q, c_kv, k_rot, g_rec, g_att = jnp.split(w_in, sp, axis=-1)
    w_in_perm = jnp.concatenate(
        [x_rec, x_gate, g_rec, g_att, c_q, c_kv, _rope_lanes(k_rot),
         _rope_lanes(_rot_half_cols(k_rot))], axis=-1).astype(BF16)

    wq = p["w_uq"][l].reshape(q_lora, N_HEADS, QK_DIM)
    wq_n, wq_r = wq[..., :NOPE_DIM], wq[..., NOPE_DIM:]
    wk = p["w_uk"][l].reshape(kv_lora, N_HEADS, NOPE_DIM)
    gq, gk = p["q_head_norm"][l], p["k_head_norm"][l]
    q_scale = (QK_DIM ** -0.5) * LOG2E
    return dict(
        gmix=p["norm_mix"][l].reshape(1, D), w_in=w_in_perm,
        conv_w=p["conv_w"][l], conv_b=p["conv_b"][l].reshape(1, D),
        wa=_block_diag_tiles(p["lru_wa"][l]).astype(BF16), ba=p["lru_ba"][l].reshape(1, D),
        wi=_block_diag_tiles(p["lru_wi"][l]).astype(BF16), bi=p["lru_bi"][l].reshape(1, D),
        sp=jax.nn.softplus(-p["lru_lambda"][l]).reshape(1, D),
        gq=p["q_lat_norm"][l].reshape(1, q_lora), gkv=p["kv_lat_norm"][l].reshape(1, kv_lora),
        wq_t=_head_pad(wq_n, wq_r).T.astype(BF16),
        gqh=(_pad_cols(gq, HEAD_PAD) * q_scale).reshape(HEAD_PAD, 1),
        wk=_head_pad(wk, jnp.zeros((kv_lora, N_HEADS, ROPE_DIM), F32)).astype(BF16),
        wv_t=p["w_uv"][l].T.astype(BF16),
        gkh=_pad_cols(gk, HEAD_PAD).reshape(1, HEAD_PAD),
        score_bound=QK_DIM * jnp.max(jnp.abs(gq)) * jnp.max(jnp.abs(gk)) * q_scale,
        w_out=p["w_out"][l].astype(BF16), gmem=p["norm_mem"][l].reshape(1, D),
        w_mq=p["w_mq"][l].astype(BF16), gmqh=p["mq_head_norm"][l].reshape(1, -1),
        w_mo=p["w_mo"][l].astype(BF16), gffn=p["norm_ffn"][l].reshape(1, D),
    )


def _rope_tables(pos):
    half = ROPE_DIM // 2
    inv_freq = ROPE_BASE ** (-jnp.arange(half, dtype=F32) / half)
    ang = pos.astype(F32)[:, None] * inv_freq[None, :]
    cos2 = jnp.concatenate([jnp.cos(ang), jnp.cos(ang)], axis=-1)
    sin2 = jnp.concatenate([jnp.sin(ang), jnp.sin(ang)], axis=-1)
    return dict(cos_k=_rope_lanes(cos2), sin_k=_rope_lanes(sin2), cos_q=cos2.T, sin_q=sin2.T)


def _layer(l, depth, x, tabs, q_off, conv_prev, h0, past, ckv_stack, mem_k, mem_v, w, p):
    B, L, D = x.shape
    prev8 = jnp.pad(conv_prev, ((0, 0), (SUBLANES - (CONV_W - 1), 0), (0, 0)))
    h0_8 = jnp.pad(h0[:, None, :], ((0, 0), (SUBLANES - 1, 0), (0, 0)))
    grec, gatt, cq, ckv, kr, ctail, htail = _inproj(x, prev8, h0_8, w, tabs["cos_k"], tabs["sin_k"],
                                                    l, depth, ckv_stack)

    q = _qprep(cq, w, tabs["cos_q"], tabs["sin_q"])
    if past is None:
        t_valid = L
        tk = min(512, L)
    else:
        t_valid = past[0].shape[2] + L
        tk = 512
    k, vt = _kvprep(ckv, l, kr, w, tk, past)
    flash = functools.partial(_flash, q_off=q_off, t_valid=t_valid, tk=tk)
    att = lax.cond(w["score_bound"] <= SAFE_LOG2_RANGE, functools.partial(flash, bounded=True),
                   functools.partial(flash, bounded=False), q, k, vt)

    x2 = _post(x, grec, gatt, att, mem_k, mem_v, w).reshape(B * L, D)
    j = l // 2
    if l % 2 == 0:
        x3 = _ffn(x2, w["gffn"], p["ffn_w1"][j].astype(BF16), p["ffn_w3"][j].astype(BF16),
                  p["ffn_w2"][j].astype(BF16))
    else:
        E = p["moe_router"].shape[-1]
        wr = _pad_cols(p["moe_router"][j], ROUTER_ROWS).T
        wr_hi = wr.astype(BF16)
        wr_lo = (wr - wr_hi.astype(F32)).astype(BF16)
        br = jnp.concatenate([p["moe_router_b"][j],
                              jnp.full((ROUTER_ROWS - E,), NEG, F32)]).reshape(ROUTER_ROWS, 1)
        x3 = _moe(x2, w["gffn"], wr_hi, wr_lo, br, p["moe_w1"][j].astype(BF16),
                  p["moe_w3"][j].astype(BF16), p["moe_w2"][j].astype(BF16))
    conv_state = ctail[:, SUBLANES - (CONV_W - 1):, :]
    return x3.reshape(B, L, D), conv_state, htail[:, SUBLANES - 1, :], ckv, kr[..., NOPE_DIM:QK_DIM]


def kernel(x_prompt, x_sample, cache_ckv, cache_krope, cache_mem_k, cache_mem_v, state_lru, state_conv, mem_prompt, norm_mix, w_in, conv_w, conv_b, lru_wa, lru_ba, lru_wi, lru_bi, lru_lambda, q_lat_norm, w_uq, kv_lat_norm, w_uk, w_uv, q_head_norm, k_head_norm, w_out, norm_mem, mem_in_norm, w_mq, w_mk, w_mv, w_mo, mq_head_norm, mk_head_norm, norm_ffn, ffn_w1, ffn_w3, ffn_w2, moe_router, moe_router_b, moe_w1, moe_w3, moe_w2):
    p = dict(norm_mix=norm_mix, w_in=w_in, conv_w=conv_w, conv_b=conv_b, lru_wa=lru_wa, lru_ba=lru_ba,
             lru_wi=lru_wi, lru_bi=lru_bi, lru_lambda=lru_lambda, q_lat_norm=q_lat_norm, w_uq=w_uq,
             kv_lat_norm=kv_lat_norm, w_uk=w_uk, w_uv=w_uv, q_head_norm=q_head_norm,
             k_head_norm=k_head_norm, w_out=w_out, norm_mem=norm_mem, w_mq=w_mq, w_mo=w_mo,
             mq_head_norm=mq_head_norm, norm_ffn=norm_ffn, ffn_w1=ffn_w1, ffn_w3=ffn_w3,
             ffn_w2=ffn_w2, moe_router=moe_router, moe_router_b=moe_router_b, moe_w1=moe_w1,
             moe_w3=moe_w3, moe_w2=moe_w2)
    depth = w_in.shape[0]
    weights = [_layer_weights(l, p) for l in range(depth)]

    Bp, Lp, D = x_prompt.shape
    Bs, Ls, _ = x_sample.shape
    t_past = cache_ckv.shape[2]
    M = mem_prompt.shape[1]

    tabs_p = _rope_tables(jnp.arange(Lp, dtype=jnp.int32))
    zero_conv = jnp.zeros((Bp, CONV_W - 1, D), F32)
    zero_h = jnp.zeros((Bp, D), F32)
    x = x_prompt
    ckv_p = None
    outs_p = [[] for _ in range(5)]
    for l in range(depth):
        mk, mv = _memkv(mem_prompt, mem_in_norm[l], w_mk[l], w_mv[l], mk_head_norm[l])
        x, cs, hl, ckv_p, kr = _layer(l, depth, x, tabs_p, 0, zero_conv, zero_h, None, ckv_p, mk, mv,
                                      weights[l], p)
        for acc, val in zip(outs_p, (kr, hl, cs, mk.reshape(Bp, M, MEM_HEADS, -1),
                                     mv.reshape(Bp, M, MEM_HEADS, -1))):
            acc.append(val)
    y_prompt = x

    tabs_s = _rope_tables(t_past + jnp.arange(Ls, dtype=jnp.int32))
    x = x_sample
    ckv_s = None
    outs_s = [[] for _ in range(3)]
    cache_kr = _rope_lanes(cache_krope)
    for l in range(depth):
        x, cs, hl, ckv_s, kr = _layer(l, depth, x, tabs_s, t_past, state_conv[l], state_lru[l],
                                      (cache_ckv, cache_kr, l), ckv_s, cache_mem_k[l].reshape(Bs, M, D),
                                      cache_mem_v[l].reshape(Bs, M, D), weights[l], p)
        for acc, val in zip(outs_s, (kr, hl, cs)):
            acc.append(val)
    y_sample = x

    return ((y_prompt, y_sample, ckv_p) + tuple(jnp.stack(a) for a in outs_p)
            + (ckv_s,) + tuple(jnp.stack(a) for a in outs_s))
```

```python
import functools
import math

import jax
import jax.numpy as jnp
from jax import lax
from jax.experimental import pallas as pl
from jax.experimental.pallas import tpu as pltpu

F32 = jnp.float32
BF16 = jnp.bfloat16

EPS = 1e-6
CHUNK = 64
LRU_C = 8.0
CONV_W = 4
N_HEADS = 16
NOPE_DIM = 64
ROPE_DIM = 32
QK_DIM = NOPE_DIM + ROPE_DIM
ROPE_BASE = 10000.0
MEM_HEADS = 4

LANES = 128
SUBLANES = 8
MXU_DIM = 256
VMEM_LIMIT = 56 * 1024 * 1024

HEAD_PAD = LANES
ROUTER_ROWS = 16
N_INPROJ_INPUTS = 16
MOE_CAPS = (112, 144, 176, 208)
SAFE_LOG2_RANGE = 60.0
FLASH_HEADS = 8
FLASH_LAG = 8
NEG = -1e30
LOG2E = 1.4426950408889634


def _resident(shape):
    nd = len(shape)
    return pl.BlockSpec(shape, lambda *_: (0,) * nd, pipeline_mode=pl.Buffered(1))


def _params(n_axes):
    return pltpu.CompilerParams(dimension_semantics=("arbitrary",) * n_axes,
                                vmem_limit_bytes=VMEM_LIMIT)


def _rms(x, gain):
    return x * lax.rsqrt(jnp.mean(x * x, axis=-1, keepdims=True) + EPS) * gain


def _dot(a, b):
    return jnp.dot(a, b, preferred_element_type=F32)


def _dot_nt(a, b):
    return lax.dot_general(a, b, (((1,), (1,)), ((), ())), preferred_element_type=F32)


def _token_tile(n, cap=512):
    t = min(n, cap)
    assert n % t == 0 and t % SUBLANES == 0
    return t


def _memkv_kernel(mem_ref, gin_ref, wk_ref, wv_ref, gk_ref, k_ref, v_ref, kb_ref, vb_ref):
    mn = _rms(mem_ref[0], gin_ref[...]).astype(BF16)
    k = _dot(mn, wk_ref[...])
    v = _dot(mn, wv_ref[...])
    hd = gk_ref.shape[-1]
    for h in range(MEM_HEADS):
        sl = slice(h * hd, (h + 1) * hd)
        kh = _rms(k[:, sl], gk_ref[...])
        k_ref[0, :, h, :] = kh
        kb_ref[0, :, sl] = kh.astype(BF16)
        v_ref[0, :, h, :] = v[:, sl]
    vb_ref[0] = v.astype(BF16)


def _memkv(mem, g_in, w_k, w_v, g_k):
    B, M, D = mem.shape
    tok = pl.BlockSpec((1, M, D), lambda b: (b, 0, 0))
    hd = D // MEM_HEADS
    tok4 = pl.BlockSpec((1, M, MEM_HEADS, hd), lambda b: (b, 0, 0, 0))
    return pl.pallas_call(
        _memkv_kernel,
        grid=(B,),
        in_specs=[tok, _resident((1, D)), _resident((D, D)), _resident((D, D)),
                  _resident((1, hd))],
        out_specs=[tok4, tok4, tok, tok],
        out_shape=[jax.ShapeDtypeStruct((B, M, MEM_HEADS, hd), F32)] * 2
        + [jax.ShapeDtypeStruct((B, M, D), BF16)] * 2,
        compiler_params=_params(1),
        name="memkv",
    )(mem, g_in.reshape(1, D), w_k.astype(BF16), w_v.astype(BF16), g_k.reshape(1, -1))


def _inproj_kernel(x_ref, prev_ref, h0_ref, gmix_ref, win_ref, cw_ref, cb_ref, wa_ref, ba_ref,
                   wi_ref, bi_ref, sp_ref, gq_ref, gkv_ref, cos_ref, sin_ref, *rest,
                   tm, d, q_lora, kv_lora):
    (grec_ref, gatt_ref, cq_ref, ckv_ref, kr_ref, ctail_ref, htail_ref,
     xext_ref, hcar_ref) = rest[-9:]
    @pl.when(pl.program_id(1) == 0)
    def _():
        xext_ref[0:SUBLANES, :] = prev_ref[0]
        hcar_ref[...] = h0_ref[0]

    hn = _rms(x_ref[0], gmix_ref[...]).astype(BF16)

    def proj(c0, c1):
        return _dot(hn, win_ref[:, c0:c1])

    x_rec = proj(0, d)
    xext_ref[SUBLANES:SUBLANES + tm, :] = x_rec
    xc = cb_ref[...] + cw_ref[CONV_W - 1:CONV_W, :] * x_rec
    for j in range(CONV_W - 1):
        off = SUBLANES - (CONV_W - 1) + j
        xc = xc + cw_ref[j:j + 1, :] * xext_ref[off:off + tm, :]
    tail = x_rec[tm - SUBLANES:tm, :]
    ctail_ref[0] = tail
    xext_ref[0:SUBLANES, :] = tail

    xcb = xc.astype(BF16)
    nb = d // MXU_DIM
    gr = jnp.concatenate(
        [_dot(xcb[:, c * MXU_DIM:(c + 1) * MXU_DIM], wa_ref[c]) for c in range(nb)], axis=1)
    gi = jnp.concatenate(
        [_dot(xcb[:, c * MXU_DIM:(c + 1) * MXU_DIM], wi_ref[c]) for c in range(nb)], axis=1)
    r = jax.nn.sigmoid(gr + ba_ref[...])
    ig = jax.nn.sigmoid(gi + bi_ref[...])
    log_a = (-LRU_C) * r * sp_ref[...]
    a = jnp.exp(log_a)
    u = jnp.sqrt(1.0 - a * a) * (ig * xc)

    row = lax.broadcasted_iota(jnp.int32, (tm, 1), 0) & (SUBLANES - 1)
    s = 1
    while s < SUBLANES:
        keep = row >= s
        u = a * jnp.where(keep, pltpu.roll(u, s, 0), 0.0) + u
        a = a * jnp.where(keep, pltpu.roll(a, s, 0), 1.0)
        s *= 2
    h_prev = hcar_ref[SUBLANES - 1:SUBLANES, :]
    groups = []
    for g in range(tm // SUBLANES):
        rows = slice(g * SUBLANES, (g + 1) * SUBLANES)
        hg = a[rows] * h_prev + u[rows]
        groups.append(hg)
        h_prev = hg[SUBLANES - 1:SUBLANES, :]
    h = jnp.concatenate(groups, axis=0)
    htail = groups[-1]
    hcar_ref[...] = htail
    htail_ref[0] = htail

    x_gate = proj(d, 2 * d)
    g_rec = proj(2 * d, 3 * d)
    grec_ref[0] = jax.nn.sigmoid(g_rec) * (jax.nn.gelu(x_gate) * h)
    gatt_ref[0] = proj(3 * d, 4 * d)

    o = 4 * d
    cq_ref[0] = _rms(proj(o, o + q_lora), gq_ref[...]).astype(BF16)
    o += q_lora
    ckv_ref[0, 0] = _rms(proj(o, o + kv_lora), gkv_ref[...])
    o += kv_lora
    kr = proj(o, o + LANES) * cos_ref[...] + proj(o + LANES, o + 2 * LANES) * sin_ref[...]
    kr_ref[0] = kr


def _inproj(x, prev8, h0_8, w, cos_k, sin_k, layer, depth, ckv_stack):
    B, L, D = x.shape
    tm = _token_tile(L, 256)
    q_lora, kv_lora = w["gq"].shape[-1], w["gkv"].shape[-1]
    ncol = w["w_in"].shape[-1]
    tok = lambda n: pl.BlockSpec((1, tm, n), lambda b, l: (b, l, 0))
    per_b = pl.BlockSpec((1, SUBLANES, D), lambda b, l: (b, 0, 0))
    tab = pl.BlockSpec((tm, LANES), lambda b, l: (l, 0))
    nb = D // MXU_DIM
    kern = functools.partial(_inproj_kernel, tm=tm, d=D, q_lora=q_lora, kv_lora=kv_lora)
    alias_args = () if ckv_stack is None else (ckv_stack,)
    alias_spec = [pl.BlockSpec(memory_space=pl.ANY)] * len(alias_args)
    return pl.pallas_call(
        kern,
        grid=(B, L // tm),
        in_specs=[tok(D), per_b, per_b, _resident((1, D)), _resident((D, ncol)),
                  _resident((CONV_W, D)), _resident((1, D)),
                  _resident((nb, MXU_DIM, MXU_DIM)), _resident((1, D)),
                  _resident((nb, MXU_DIM, MXU_DIM)), _resident((1, D)), _resident((1, D)),
                  _resident((1, q_lora)), _resident((1, kv_lora)), tab, tab] + alias_spec,
        out_specs=[tok(D), tok(D), tok(q_lora),
                   pl.BlockSpec((1, 1, tm, kv_lora), lambda b, l: (layer, b, l, 0)),
                   tok(HEAD_PAD), per_b, per_b],
        out_shape=[jax.ShapeDtypeStruct((B, L, D), F32), jax.ShapeDtypeStruct((B, L, D), F32),
                   jax.ShapeDtypeStruct((B, L, q_lora), BF16),
                   jax.ShapeDtypeStruct((depth, B, L, kv_lora), F32),
                   jax.ShapeDtypeStruct((B, L, HEAD_PAD), F32),
                   jax.ShapeDtypeStruct((B, SUBLANES, D), F32),
                   jax.ShapeDtypeStruct((B, SUBLANES, D), F32)],
        scratch_shapes=[pltpu.VMEM((SUBLANES + tm, D), F32), pltpu.VMEM((SUBLANES, D), F32)],
        input_output_aliases={N_INPROJ_INPUTS: 3} if alias_args else {},
        compiler_params=_params(2),
        name="inproj",
    )(x, prev8, h0_8, w["gmix"], w["w_in"], w["conv_w"], w["conv_b"], w["wa"], w["ba"],
      w["wi"], w["bi"], w["sp"], w["gq"], w["gkv"], cos_k, sin_k, *alias_args)


def _qprep_kernel(cq_ref, wt_ref, cos_ref, sin_ref, g_ref, q_ref):
    qt = _dot_nt(wt_ref[...], cq_ref[0])
    half = ROPE_DIM // 2
    tm = qt.shape[1]
    cos, sin = cos_ref[...], sin_ref[...]
    g_nope, g_rope = g_ref[:NOPE_DIM, :], g_ref[NOPE_DIM:QK_DIM, :]
    pad = jnp.zeros((HEAD_PAD - QK_DIM, tm), BF16)
    for h in range(N_HEADS):
        r0 = h * HEAD_PAD
        nope = qt[r0:r0 + NOPE_DIM, :]
        x1 = qt[r0 + NOPE_DIM:r0 + NOPE_DIM + half, :]
        x2 = qt[r0 + NOPE_DIM + half:r0 + QK_DIM, :]
        rope = (jnp.concatenate([x1, x2], axis=0) * cos + jnp.concatenate([-x2, x1], axis=0) * sin)
        ss = (jnp.sum(nope * nope, axis=0, keepdims=True)
              + jnp.sum(rope * rope, axis=0, keepdims=True)) * (1.0 / QK_DIM)
        inv = lax.rsqrt(ss + EPS)
        q_ref[0, r0:r0 + NOPE_DIM, :] = (nope * inv * g_nope).astype(BF16)
        q_ref[0, r0 + NOPE_DIM:r0 + QK_DIM, :] = (rope * inv * g_rope).astype(BF16)
        q_ref[0, r0 + QK_DIM:r0 + HEAD_PAD, :] = pad


def _qprep(cq, w, cos_q, sin_q):
    B, L, QL = cq.shape
    tm = _token_tile(L)
    n = N_HEADS * HEAD_PAD
    tab = pl.BlockSpec((ROPE_DIM, tm), lambda b, l: (0, l))
    return pl.pallas_call(
        _qprep_kernel,
        grid=(B, L // tm),
        in_specs=[pl.BlockSpec((1, tm, QL), lambda b, l: (b, l, 0)),
                  _resident((n, QL)), tab, tab, _resident((HEAD_PAD, 1))],
        out_specs=pl.BlockSpec((1, n, tm), lambda b, l: (b, 0, l)),
        out_shape=jax.ShapeDtypeStruct((B, n, L), BF16),
        compiler_params=_params(2),
        name="qprep",
    )(cq, w["wq_t"], cos_q, sin_q, w["gqh"])


def _kvprep_kernel(*refs, n_past, tk):
    if n_past:
        past_c_ref, past_kr_ref, new_c_ref, new_kr_ref, wk_ref, wvt_ref, g_ref, k_ref, vt_ref = refs
        is_new = pl.program_id(1) >= n_past

        def pick(new_ref, past_ref):
            new = new_ref[...].reshape(new_ref.shape[-2:])
            fill = jnp.zeros((tk - new.shape[0], new.shape[1]), new.dtype)
            return jnp.where(is_new, jnp.concatenate([new, fill], axis=0), past_ref[0, 0])

        c32, kr = pick(new_c_ref, past_c_ref), pick(new_kr_ref, past_kr_ref)
    else:
        new_c_ref, new_kr_ref, wk_ref, wvt_ref, g_ref, k_ref, vt_ref = refs
        c32, kr = new_c_ref[0, 0], new_kr_ref[0]
    c = c32.astype(BF16)
    k = _dot(c, wk_ref[...])
    for h in range(N_HEADS):
        sl = slice(h * HEAD_PAD, (h + 1) * HEAD_PAD)
        kh = k[:, sl] + kr
        ss = jnp.sum(kh * kh, axis=-1, keepdims=True) * (1.0 / QK_DIM)
        k_ref[0, :, sl] = (kh * lax.rsqrt(ss + EPS) * g_ref[...]).astype(BF16)
    vt_ref[0, 0] = _dot_nt(wvt_ref[...], c).astype(BF16)


def _kvprep(ckv, layer, kr, w, tk, past=None):
    _, B, L, KV = ckv.shape
    n = N_HEADS * HEAD_PAD
    dv = w["wv_t"].shape[0]
    if past is None:
        assert L % tk == 0
        n_past, n_tiles = 0, L // tk
        in_specs = [pl.BlockSpec((1, 1, tk, KV), lambda b, t: (layer, b, t, 0)),
                    pl.BlockSpec((1, tk, HEAD_PAD), lambda b, t: (b, t, 0))]
        args = (ckv, kr)
    else:
        cache_c, cache_kr, layer = past
        t_past = cache_c.shape[2]
        assert t_past % tk == 0 and L <= tk and L % SUBLANES == 0
        n_past = t_past // tk
        n_tiles = n_past + 1
        old = lambda width: pl.BlockSpec(
            (1, 1, tk, width), lambda b, t: (layer, b, jnp.minimum(t, n_past - 1), 0))
        in_specs = [old(KV), old(HEAD_PAD),
                    pl.BlockSpec((1, 1, L, KV), lambda b, t: (layer, b, 0, 0)),
                    pl.BlockSpec((1, L, HEAD_PAD), lambda b, t: (b, 0, 0))]
        args = (cache_c, cache_kr, ckv, kr)
    return pl.pallas_call(
        functools.partial(_kvprep_kernel, n_past=n_past, tk=tk),
        grid=(B, n_tiles),
        in_specs=in_specs + [_resident((KV, n)), _resident((dv, KV)), _resident((1, HEAD_PAD))],
        out_specs=[pl.BlockSpec((1, tk, n), lambda b, t: (b, t, 0)),
                   pl.BlockSpec((1, 1, dv, tk), lambda b, t: (b, t, 0, 0))],
        out_shape=[jax.ShapeDtypeStruct((B, n_tiles * tk, n), BF16),
                   jax.ShapeDtypeStruct((B, n_tiles, dv, tk), BF16)],
        compiler_params=_params(2),
        name="kvprep",
    )(*args, w["wk"], w["wv_t"], w["gkh"])


def _flash_kernel(q_ref, k_ref, vt_ref, o_ref, m_ref, l_ref, acc_ref, *, tq, tk, q_off, t_valid,
                  v_dim, heads, bounded, lag, split_diag):
    q_start = q_off + pl.program_id(2) * tq
    n_full = jnp.minimum(q_start, t_valid) // tk
    n_end = (jnp.minimum(q_start + tq, t_valid) + tk - 1) // tk
    shift = int(math.log2(CHUNK))

    def chunk_of(pos):
        return lax.shift_right_logical(pos, shift)

    m_ref[...] = jnp.full(m_ref.shape, NEG, F32)
    l_ref[...] = jnp.zeros(l_ref.shape, F32)
    acc_ref[...] = jnp.zeros(acc_ref.shape, F32)

    def accumulate(hh, s, vj, q0):
        lanes = slice(q0, tq)
        if bounded:
            p = jnp.exp2(s)
            l_ref[hh, :, lanes] += jnp.sum(p, axis=0, keepdims=True)
        else:
            m = m_ref[hh, :, lanes]
            m_new = jnp.maximum(m, jnp.max(s, axis=0, keepdims=True))
            alpha = jnp.exp2(m - m_new)
            p = jnp.exp2(s - m_new)
            l_ref[hh, :, lanes] = alpha * l_ref[hh, :, lanes] + jnp.sum(p, axis=0, keepdims=True)
            m_ref[hh, :, lanes] = m_new
            acc_ref[hh, :, lanes] = alpha * acc_ref[hh, :, lanes]
        acc_ref[hh, :, lanes] += _dot(vj, p.astype(BF16))

    def tile(j, masked):
        if masked:
            k_pos = j * tk + lax.broadcasted_iota(jnp.int32, (tk, 1), 0)
            q_pos = q_start + lax.broadcasted_iota(jnp.int32, (1, tq), 1)
            vis = (chunk_of(k_pos) <= chunk_of(q_pos)) & (k_pos < t_valid)
        pending = {}
        for t in range(heads + lag):
            if t < heads:
                qh = q_ref[0, t * HEAD_PAD:(t + 1) * HEAD_PAD, :]
                kj = k_ref[0, pl.ds(pl.multiple_of(j * tk, tk), tk), t * HEAD_PAD:(t + 1) * HEAD_PAD]
                pending[t] = _dot(kj, qh)
            if t >= lag:
                hh = t - lag
                s = pending.pop(hh)
                if masked:
                    s = jnp.where(vis, s, NEG)
                accumulate(hh, s, vt_ref[0, j, hh * v_dim:(hh + 1) * v_dim, :], 0)

    def diagonal(j):
        half = tk // 2
        vis = (chunk_of(lax.broadcasted_iota(jnp.int32, (half, 1), 0))
               <= chunk_of(lax.broadcasted_iota(jnp.int32, (1, tq), 1)))
        for k0, q0 in ((0, 0), (half, half)):
            for hh in range(heads):
                qh = q_ref[0, hh * HEAD_PAD:(hh + 1) * HEAD_PAD, q0:]
                kj = k_ref[0, pl.ds(pl.multiple_of(j * tk + k0, half), half),
                           hh * HEAD_PAD:(hh + 1) * HEAD_PAD]
                s = jnp.where(vis[:, :tq - q0], _dot(kj, qh), NEG)
                accumulate(hh, s, vt_ref[0, j, hh * v_dim:(hh + 1) * v_dim, k0:k0 + half], q0)

    def loop(lo, hi, body):
        lax.fori_loop(lo, hi, lambda j, c: (body(j), c)[1], 0)

    loop(0, n_full, lambda j: tile(j, False))
    if split_diag:
        diagonal(n_full)
    else:
        loop(n_full, n_end, lambda j: tile(j, True))
    o_ref[0] = jnp.concatenate([acc_ref[hh] / l_ref[hh] for hh in range(heads)], axis=0).T


def _flash(q, k, vt, *, q_off, t_valid, tk, bounded, heads=FLASH_HEADS):
    B, n, L = q.shape
    T = k.shape[1]
    dv = vt.shape[2]
    v_dim = dv // N_HEADS
    tq = _token_tile(L)
    assert tq % CHUNK == 0 and q_off % CHUNK == 0 and (heads * v_dim) % LANES == 0
    split_diag = tq == tk and q_off == 0 and t_valid == T and (tk // 2) % max(CHUNK, LANES) == 0
    kern = functools.partial(_flash_kernel, tq=tq, tk=tk, q_off=q_off, t_valid=t_valid, v_dim=v_dim,
                             heads=heads, bounded=bounded, lag=min(FLASH_LAG, heads),
                             split_diag=split_diag)
    return pl.pallas_call(
        kern,
        grid=(B, N_HEADS // heads, L // tq),
        in_specs=[pl.BlockSpec((1, heads * HEAD_PAD, tq), lambda b, p, i: (b, p, i)),
                  pl.BlockSpec((1, T, heads * HEAD_PAD), lambda b, p, i: (b, 0, p)),
                  pl.BlockSpec((1, T // tk, heads * v_dim, tk), lambda b, p, i: (b, 0, p, 0))],
        out_specs=pl.BlockSpec((1, tq, heads * v_dim), lambda b, p, i: (b, i, p)),
        out_shape=jax.ShapeDtypeStruct((B, L, dv), F32),
        scratch_shapes=[pltpu.VMEM((heads, 1, tq), F32), pltpu.VMEM((heads, 1, tq), F32),
                        pltpu.VMEM((heads, v_dim, tq), F32)],
        compiler_params=_params(3),
        name="flash",
    )(q, k, vt)


def _post_kernel(x_ref, grec_ref, gatt_ref, att_ref, wo_ref, gmem_ref, wmq_ref, gqh_ref,
                 mk_ref, mv_ref, wmo_ref, o_ref):
    mixed = (grec_ref[0] + jax.nn.sigmoid(gatt_ref[0]) * att_ref[0]).astype(BF16)
    x1 = x_ref[0] + _dot(mixed, wo_ref[...])
    qm = _dot(_rms(x1, gmem_ref[...]).astype(BF16), wmq_ref[...])
    hd = gqh_ref.shape[-1]
    outs = []
    for h in range(MEM_HEADS):
        sl = slice(h * hd, (h + 1) * hd)
        qh = (_rms(qm[:, sl], gqh_ref[...]) * (hd ** -0.5)).astype(BF16)
        s = _dot_nt(qh, mk_ref[0, :, sl])
        p = jnp.exp(s - jnp.max(s, axis=-1, keepdims=True))
        l = jnp.sum(p, axis=-1, keepdims=True)
        outs.append((_dot(p.astype(BF16), mv_ref[0, :, sl]) / l).astype(BF16))
    o_ref[0] = x1 + _dot(jnp.concatenate(outs, axis=1), wmo_ref[...])


def _post(x, grec, gatt, att, mem_k, mem_v, w):
    B, L, D = x.shape
    M = mem_k.shape[1]
    tm = _token_tile(L)
    tok = pl.BlockSpec((1, tm, D), lambda b, l: (b, l, 0))
    mem = pl.BlockSpec((1, M, D), lambda b, l: (b, 0, 0))
    return pl.pallas_call(
        _post_kernel,
        grid=(B, L // tm),
        in_specs=[tok, tok, tok, tok, _resident((D, D)), _resident((1, D)), _resident((D, D)),
                  _resident((1, D // MEM_HEADS)), mem, mem, _resident((D, D))],
        out_specs=tok,
        out_shape=jax.ShapeDtypeStruct((B, L, D), F32),
        compiler_params=_params(2),
        name="post",
    )(x, grec, gatt, att, w["w_out"], w["gmem"], w["w_mq"], w["gmqh"],
      mem_k, mem_v, w["w_mo"])


def _swiglu_mid(a, b):
    return (a * jax.nn.sigmoid(a) * b).astype(BF16)


def _ffn_kernel(x_ref, g_ref, w1_ref, w3_ref, w2_ref, o_ref, *, n_split):
    x = x_ref[...]
    hf = _rms(x, g_ref[...]).astype(BF16)
    fc = w1_ref.shape[1] // n_split
    acc = x
    for c in range(n_split):
        sl = slice(c * fc, (c + 1) * fc)
        acc = acc + _dot(_swiglu_mid(_dot(hf, w1_ref[:, sl]), _dot(hf, w3_ref[:, sl])), w2_ref[sl, :])
    o_ref[...] = acc


def _ffn(x, g, w1, w3, w2):
    N, D = x.shape
    F = w1.shape[1]
    tm = _token_tile(N)
    n_split = 2
    assert F % (n_split * LANES) == 0
    tok = pl.BlockSpec((tm, D), lambda i: (i, 0))
    return pl.pallas_call(
        functools.partial(_ffn_kernel, n_split=n_split),
        grid=(N // tm,),
        in_specs=[tok, _resident((1, D)), _resident((D, F)), _resident((D, F)), _resident((F, D))],
        out_specs=tok,
        out_shape=jax.ShapeDtypeStruct((N, D), F32),
        compiler_params=_params(1),
        name="ffn",
    )(x, g, w1, w3, w2)


def _moe_kernel(x_ref, g_ref, wrh_ref, wrl_ref, br_ref, w1_ref, w3_ref, w2_ref, o_ref,
                hf_ref, gate_ref, rank_ref, *, ts, caps, n_exp):
    e = pl.program_id(1)
    lane = lax.broadcasted_iota(jnp.int32, (1, LANES), 1)
    n_sub = hf_ref.shape[0] // ts

    @pl.when(e == 0)
    def _():
        x = x_ref[...]
        hf = _rms(x, g_ref[...])
        hi = hf.astype(BF16)
        lo = (hf - hi.astype(F32)).astype(BF16)
        logits = (_dot_nt(wrh_ref[...], hi) + _dot_nt(wrh_ref[...], lo) + _dot_nt(wrl_ref[...], hi)
                  + br_ref[...])
        sub = lax.broadcasted_iota(jnp.int32, (ROUTER_ROWS, 1), 0)
        m1 = jnp.max(logits, axis=0, keepdims=True)
        i1 = jnp.min(jnp.where(logits == m1, sub, ROUTER_ROWS), axis=0, keepdims=True)
        rest = jnp.where(sub == i1, NEG, logits)
        m2 = jnp.max(rest, axis=0, keepdims=True)
        i2 = jnp.min(jnp.where(rest == m2, sub, ROUTER_ROWS), axis=0, keepdims=True)
        e2 = jnp.exp(m2 - m1)
        den = 1.0 + e2
        gate_t = jnp.where(sub == i1, 1.0 / den, 0.0) + jnp.where(sub == i2, e2 / den, 0.0)
        tm = gate_t.shape[1]
        gate_ref[...] = jnp.concatenate(
            [gate_t, jnp.zeros((LANES - ROUTER_ROWS, tm), F32)], axis=0).T
        hf_ref[...] = hi
        o_ref[...] = x
        sel_t = ((sub == i1) | (sub == i2))[:n_exp, :]
        before = (lax.broadcasted_iota(jnp.int32, (ts, ts), 0)
                  < lax.broadcasted_iota(jnp.int32, (ts, ts), 1))
        tri = jnp.where(before, 1.0, 0.0).astype(BF16)
        for s in range(n_sub):
            sel_s = sel_t[:, s * ts:(s + 1) * ts]
            rank = _dot(jnp.where(sel_s, 1.0, 0.0).astype(BF16), tri)
            rank_ref[s] = jnp.where(sel_s, rank, -1.0)

    for s in range(n_sub):
        rows = slice(s * ts, (s + 1) * ts)
        rank_e = rank_ref[s, pl.ds(e, 1), :]
        n_tok = jnp.max(rank_e).astype(jnp.int32) + 1
        ge = jnp.sum(jnp.where(lane == e, gate_ref[rows, :], 0.0), axis=-1, keepdims=True)

        def chunk(base, size):
            want = (lax.broadcasted_iota(jnp.int32, (size, 1), 0) + base).astype(F32)
            onehot = jnp.where(rank_e == want, 1.0, 0.0).astype(BF16)
            xg = _dot(onehot, hf_ref[rows, :]).astype(BF16)
            y = _dot(_swiglu_mid(_dot(xg, w1_ref[0]), _dot(xg, w3_ref[0])), w2_ref[0])
            back = lax.dot_general(onehot, y.astype(BF16), (((0,), (0,)), ((), ())),
                                   preferred_element_type=F32)
            o_ref[rows, :] += ge * back

        n_big = n_tok // caps[-1]
        lax.fori_loop(0, n_big, lambda c, carry: (chunk(c * caps[-1], caps[-1]), carry)[1], 0)
        rest = n_tok - n_big * caps[-1]
        bucket = sum((rest > c).astype(jnp.int32) for c in (0,) + caps[:-1])
        lax.switch(bucket, [lambda: None]
                   + [functools.partial(chunk, n_big * caps[-1], c) for c in caps])


def _moe(x, g, wr_hi, wr_lo, br, w1, w3, w2):
    N, D = x.shape
    E, _, F = w1.shape
    tm = _token_tile(N, 1024)
    ts = _token_tile(tm, 512)
    assert E <= SUBLANES
    tok = pl.BlockSpec((tm, D), lambda i, e: (i, 0))
    kern = functools.partial(_moe_kernel, ts=ts, caps=MOE_CAPS, n_exp=E)
    return pl.pallas_call(
        kern,
        grid=(N // tm, E),
        in_specs=[tok, _resident((1, D)), _resident((ROUTER_ROWS, D)), _resident((ROUTER_ROWS, D)),
                  _resident((ROUTER_ROWS, 1)),
                  pl.BlockSpec((1, D, F), lambda i, e: (e, 0, 0)),
                  pl.BlockSpec((1, D, F), lambda i, e: (e, 0, 0)),
                  pl.BlockSpec((1, F, D), lambda i, e: (e, 0, 0))],
        out_specs=tok,
        out_shape=jax.ShapeDtypeStruct((N, D), F32),
        scratch_shapes=[pltpu.VMEM((tm, D), BF16), pltpu.VMEM((tm, LANES), F32),
                        pltpu.VMEM((tm // ts, E, ts), F32)],
        compiler_params=_params(2),
        name="moe",
    )(x, g, wr_hi, wr_lo, br, w1, w3, w2)


def _rot_half_cols(w):
    half = ROPE_DIM // 2
    return jnp.concatenate([-w[..., half:], w[..., :half]], axis=-1)


def _rope_lanes(a):
    return jnp.pad(a, [(0, 0)] * (a.ndim - 1) + [(NOPE_DIM, HEAD_PAD - QK_DIM)])


def _pad_cols(w, n):
    return jnp.pad(w, [(0, 0)] * (w.ndim - 1) + [(0, n - w.shape[-1])])


def _head_pad(nope, rope):
    z = jnp.zeros(nope.shape[:-1] + (HEAD_PAD - QK_DIM,), nope.dtype)
    out = jnp.concatenate([nope, rope, z], axis=-1)
    return out.reshape(out.shape[:-2] + (N_HEADS * HEAD_PAD,))


def _block_diag_tiles(w):
    nblk, bw, _ = w.shape
    per = MXU_DIM // bw
    w4 = w.reshape(nblk // per, per, bw, bw)
    eye = jnp.eye(per, dtype=w.dtype)
    return jnp.einsum("cpij,pq->cpiqj", w4, eye).reshape(nblk // per, MXU_DIM, MXU_DIM)


def _layer_weights(l, p):
    D = p["w_in"].shape[1]
    q_lora = p["q_lat_norm"].shape[-1]
    kv_lora = p["kv_lat_norm"].shape[-1]
    w_in = p["w_in"][l]
    sp = (D, 2 * D, 2 * D + q_lora, 2 * D + q_lora + kv_lora, 2 * D + q_lora + kv_lora + ROPE_DIM,
          3 * D + q_lora + kv_lora + ROPE_DIM)
    x_rec, x_gate, c_q, c_kv, k_rot, g_rec, g_att = jnp.split(w_in, sp, axis=-1)
    w_in_perm = jnp.concatenate(
        [x_rec, x_gate, g_rec, g_att, c_q, c_kv, _rope_lanes(k_rot),
         _rope_lanes(_rot_half_cols(k_rot))], axis=-1).astype(BF16)

    wq = p["w_uq"][l].reshape(q_lora, N_HEADS, QK_DIM)
    wq_n, wq_r = wq[..., :NOPE_DIM], wq[..., NOPE_DIM:]
    wk = p["w_uk"][l].reshape(kv_lora, N_HEADS, NOPE_DIM)
    gq, gk = p["q_head_norm"][l], p["k_head_norm"][l]
    q_scale = (QK_DIM ** -0.5) * LOG2E
    return dict(
        gmix=p["norm_mix"][l].reshape(1, D), w_in=w_in_perm,
        conv_w=p["conv_w"][l], conv_b=p["conv_b"][l].reshape(1, D),
        wa=_block_diag_tiles(p["lru_wa"][l]).astype(BF16), ba=p["lru_ba"][l].reshape(1, D),
        wi=_block_diag_tiles(p["lru_wi"][l]).astype(BF16), bi=p["lru_bi"][l].reshape(1, D),
        sp=jax.nn.softplus(-p["lru_lambda"][l]).reshape(1, D),
        gq=p["q_lat_norm"][l].reshape(1, q_lora), gkv=p["kv_lat_norm"][l].reshape(1, kv_lora),
        wq_t=_head_pad(wq_n, wq_r).T.astype(BF16),
        gqh=(_pad_cols(gq, HEAD_PAD) * q_scale).reshape(HEAD_PAD, 1),
        wk=_head_pad(wk, jnp.zeros((kv_lora, N_HEADS, ROPE_DIM), F32)).astype(BF16),
        wv_t=p["w_uv"][l].T.astype(BF16),
        gkh=_pad_cols(gk, HEAD_PAD).reshape(1, HEAD_PAD),
        score_bound=QK_DIM * jnp.max(jnp.abs(gq)) * jnp.max(jnp.abs(gk)) * q_scale,
        w_out=p["w_out"][l].astype(BF16), gmem=p["norm_mem"][l].reshape(1, D),
        w_mq=p["w_mq"][l].astype(BF16), gmqh=p["mq_head_norm"][l].reshape(1, -1),
        w_mo=p["w_mo"][l].astype(BF16), gffn=p["norm_ffn"][l].reshape(1, D),
    )


def _rope_tables(pos):
    half = ROPE_DIM // 2
    inv_freq = ROPE_BASE ** (-jnp.arange(half, dtype=F32) / half)
    ang = pos.astype(F32)[:, None] * inv_freq[None, :]
    cos2 = jnp.concatenate([jnp.cos(ang), jnp.cos(ang)], axis=-1)
    sin2 = jnp.concatenate([jnp.sin(ang), jnp.sin(ang)], axis=-1)
    return dict(cos_k=_rope_lanes(cos2), sin_k=_rope_lanes(sin2), cos_q=cos2.T, sin_q=sin2.T)


def _layer(l, depth, x, tabs, q_off, conv_prev, h0, past, ckv_stack, mem_k, mem_v, w, p):
    B, L, D = x.shape
    prev8 = jnp.pad(conv_prev, ((0, 0), (SUBLANES - (CONV_W - 1), 0), (0, 0)))
    h0_8 = jnp.pad(h0[:, None, :], ((0, 0), (SUBLANES - 1, 0), (0, 0)))
    grec, gatt, cq, ckv, kr, ctail, htail = _inproj(x, prev8, h0_8, w, tabs["cos_k"], tabs["sin_k"],
                                                    l, depth, ckv_stack)

    q = _qprep(cq, w, tabs["cos_q"], tabs["sin_q"])
    if past is None:
        t_valid = L
        tk = min(512, L)
    else:
        t_valid = past[0].shape[2] + L
        tk = 512
    k, vt = _kvprep(ckv, l, kr, w, tk, past)
    flash = functools.partial(_flash, q_off=q_off, t_valid=t_valid, tk=tk)
    att = lax.cond(w["score_bound"] <= SAFE_LOG2_RANGE, functools.partial(flash, bounded=True),
                   functools.partial(flash, bounded=False), q, k, vt)

    x2 = _post(x, grec, gatt, att, mem_k, mem_v, w).reshape(B * L, D)
    j = l // 2
    if l % 2 == 0:
        x3 = _ffn(x2, w["gffn"], p["ffn_w1"][j].astype(BF16), p["ffn_w3"][j].astype(BF16),
                  p["ffn_w2"][j].astype(BF16))
    else:
        E = p["moe_router"].shape[-1]
        wr = _pad_cols(p["moe_router"][j], ROUTER_ROWS).T
        wr_hi = wr.astype(BF16)
        wr_lo = (wr - wr_hi.astype(F32)).astype(BF16)
        br = jnp.concatenate([p["moe_router_b"][j],
                              jnp.full((ROUTER_ROWS - E,), NEG, F32)]).reshape(ROUTER_ROWS, 1)
        x3 = _moe(x2, w["gffn"], wr_hi, wr_lo, br, p["moe_w1"][j].astype(BF16),
                  p["moe_w3"][j].astype(BF16), p["moe_w2"][j].astype(BF16))
    conv_state = ctail[:, SUBLANES - (CONV_W - 1):, :]
    return x3.reshape(B, L, D), conv_state, htail[:, SUBLANES - 1, :], ckv, kr[..., NOPE_DIM:QK_DIM]


def kernel(x_prompt, x_sample, cache_ckv, cache_krope, cache_mem_k, cache_mem_v, state_lru, state_conv, mem_prompt, norm_mix, w_in, conv_w, conv_b, lru_wa, lru_ba, lru_wi, lru_bi, lru_lambda, q_lat_norm, w_uq, kv_lat_norm, w_uk, w_uv, q_head_norm, k_head_norm, w_out, norm_mem, mem_in_norm, w_mq, w_mk, w_mv, w_mo, mq_head_norm, mk_head_norm, norm_ffn, ffn_w1, ffn_w3, ffn_w2, moe_router, moe_router_b, moe_w1, moe_w3, moe_w2):
    p = dict(norm_mix=norm_mix, w_in=w_in, conv_w=conv_w, conv_b=conv_b, lru_wa=lru_wa, lru_ba=lru_ba,
             lru_wi=lru_wi, lru_bi=lru_bi, lru_lambda=lru_lambda, q_lat_norm=q_lat_norm, w_uq=w_uq,
             kv_lat_norm=kv_lat_norm, w_uk=w_uk, w_uv=w_uv, q_head_norm=q_head_norm,
             k_head_norm=k_head_norm, w_out=w_out, norm_mem=norm_mem, w_mq=w_mq, w_mo=w_mo,
             mq_head_norm=mq_head_norm, norm_ffn=norm_ffn, ffn_w1=ffn_w1, ffn_w3=ffn_w3,
             ffn_w2=ffn_w2, moe_router=moe_router, moe_router_b=moe_router_b, moe_w1=moe_w1,
             moe_w3=moe_w3, moe_w2=moe_w2)
    depth = w_in.shape[0]
    weights = [_layer_weights(l, p) for l in range(depth)]

    Bp, Lp, D = x_prompt.shape
    Bs, Ls, _ = x_sample.shape
    t_past = cache_ckv.shape[2]
    M = mem_prompt.shape[1]

    tabs_p = _rope_tables(jnp.arange(Lp, dtype=jnp.int32))
    zero_conv = jnp.zeros((Bp, CONV_W - 1, D), F32)
    zero_h = jnp.zeros((Bp, D), F32)
    x = x_prompt
    ckv_p = None
    outs_p = [[] for _ in range(5)]
    for l in range(depth):
        mk, mv, mk_b, mv_b = _memkv(mem_prompt, mem_in_norm[l], w_mk[l], w_mv[l], mk_head_norm[l])
        x, cs, hl, ckv_p, kr = _layer(l, depth, x, tabs_p, 0, zero_conv, zero_h, None, ckv_p, mk_b, mv_b,
                                      weights[l], p)
        for acc, val in zip(outs_p, (kr, hl, cs, mk, mv)):
            acc.append(val)
    y_prompt = x

    tabs_s = _rope_tables(t_past + jnp.arange(Ls, dtype=jnp.int32))
    x = x_sample
    ckv_s = None
    outs_s = [[] for _ in range(3)]
    cache_kr = _rope_lanes(cache_krope)
    for l in range(depth):
        x, cs, hl, ckv_s, kr = _layer(l, depth, x, tabs_s, t_past, state_conv[l], state_lru[l],
                                      (cache_ckv, cache_kr, l), ckv_s,
                                      cache_mem_k[l].reshape(Bs, M, D).astype(BF16),
                                      cache_mem_v[l].reshape(Bs, M, D).astype(BF16), weights[l], p)
        for acc, val in zip(outs_s, (kr, hl, cs)):
            acc.append(val)
    y_sample = x

    return ((y_prompt, y_sample, ckv_p) + tuple(jnp.stack(a) for a in outs_p)
            + (ckv_s,) + tuple(jnp.stack(a) for a in outs_s))
```

```python
import functools
import math

import jax
import jax.numpy as jnp
from jax import lax
from jax.experimental import pallas as pl
from jax.experimental.pallas import tpu as pltpu

F32 = jnp.float32
BF16 = jnp.bfloat16

EPS = 1e-6
CHUNK = 64
LRU_C = 8.0
CONV_W = 4
N_HEADS = 16
NOPE_DIM = 64
ROPE_DIM = 32
QK_DIM = NOPE_DIM + ROPE_DIM
ROPE_BASE = 10000.0
MEM_HEADS = 4

LANES = 128
SUBLANES = 8
MXU_DIM = 256
VMEM_LIMIT = 56 * 1024 * 1024

HEAD_PAD = LANES
ROUTER_ROWS = 16
N_INPROJ_INPUTS = 16
MOE_CAPS = (112, 144, 176, 208)
SAFE_LOG2_RANGE = 60.0
FLASH_HEADS = 8
FLASH_LAG = 8
NEG = -1e30
LOG2E = 1.4426950408889634


def _resident(shape):
    nd = len(shape)
    return pl.BlockSpec(shape, lambda *_: (0,) * nd, pipeline_mode=pl.Buffered(1))


def _params(n_axes):
    return pltpu.CompilerParams(dimension_semantics=("arbitrary",) * n_axes,
                                vmem_limit_bytes=VMEM_LIMIT)


def _rms(x, gain):
    return x * lax.rsqrt(jnp.mean(x * x, axis=-1, keepdims=True) + EPS) * gain


def _dot(a, b):
    return jnp.dot(a, b, preferred_element_type=F32)


def _dot_nt(a, b):
    return lax.dot_general(a, b, (((1,), (1,)), ((), ())), preferred_element_type=F32)


def _token_tile(n, cap=512):
    t = min(n, cap)
    assert n % t == 0 and t % SUBLANES == 0
    return t


def _memkv_kernel(mem_ref, gin_ref, wk_ref, wv_ref, gk_ref, k_ref, v_ref, kb_ref, vb_ref):
    mn = _rms(mem_ref[0], gin_ref[...]).astype(BF16)
    k = _dot(mn, wk_ref[...])
    v = _dot(mn, wv_ref[...])
    hd = gk_ref.shape[-1]
    for h in range(MEM_HEADS):
        sl = slice(h * hd, (h + 1) * hd)
        kh = _rms(k[:, sl], gk_ref[...])
        k_ref[0, :, h, :] = kh
        kb_ref[0, :, sl] = kh.astype(BF16)
        v_ref[0, :, h, :] = v[:, sl]
    vb_ref[0] = v.astype(BF16)


def _memkv(mem, g_in, w_k, w_v, g_k):
    B, M, D = mem.shape
    tok = pl.BlockSpec((1, M, D), lambda b: (b, 0, 0))
    hd = D // MEM_HEADS
    tok4 = pl.BlockSpec((1, M, MEM_HEADS, hd), lambda b: (b, 0, 0, 0))
    return pl.pallas_call(
        _memkv_kernel,
        grid=(B,),
        in_specs=[tok, _resident((1, D)), _resident((D, D)), _resident((D, D)),
                  _resident((1, hd))],
        out_specs=[tok4, tok4, tok, tok],
        out_shape=[jax.ShapeDtypeStruct((B, M, MEM_HEADS, hd), F32)] * 2
        + [jax.ShapeDtypeStruct((B, M, D), BF16)] * 2,
        compiler_params=_params(1),
        name="memkv",
    )(mem, g_in.reshape(1, D), w_k.astype(BF16), w_v.astype(BF16), g_k.reshape(1, -1))


def _inproj_kernel(x_ref, prev_ref, h0_ref, gmix_ref, win_ref, cw_ref, cb_ref, wa_ref, ba_ref,
                   wi_ref, bi_ref, sp_ref, gq_ref, gkv_ref, cos_ref, sin_ref, *rest,
                   tm, d, q_lora, kv_lora):
    (grec_ref, gatt_ref, cq_ref, ckv_ref, kr_ref, ctail_ref, htail_ref,
     xext_ref, hcar_ref) = rest[-9:]
    @pl.when(pl.program_id(1) == 0)
    def _():
        xext_ref[0:SUBLANES, :] = prev_ref[0]
        hcar_ref[...] = h0_ref[0]

    hn = _rms(x_ref[0], gmix_ref[...]).astype(BF16)

    def proj(c0, c1):
        return _dot(hn, win_ref[:, c0:c1])

    x_rec = proj(0, d)
    xext_ref[SUBLANES:SUBLANES + tm, :] = x_rec
    xc = cb_ref[...] + cw_ref[CONV_W - 1:CONV_W, :] * x_rec
    for j in range(CONV_W - 1):
        off = SUBLANES - (CONV_W - 1) + j
        xc = xc + cw_ref[j:j + 1, :] * xext_ref[off:off + tm, :]
    tail = x_rec[tm - SUBLANES:tm, :]
    ctail_ref[0] = tail
    xext_ref[0:SUBLANES, :] = tail

    xcb = xc.astype(BF16)
    nb = d // MXU_DIM
    gr = jnp.concatenate(
        [_dot(xcb[:, c * MXU_DIM:(c + 1) * MXU_DIM], wa_ref[c]) for c in range(nb)], axis=1)
    gi = jnp.concatenate(
        [_dot(xcb[:, c * MXU_DIM:(c + 1) * MXU_DIM], wi_ref[c]) for c in range(nb)], axis=1)
    r = jax.nn.sigmoid(gr + ba_ref[...])
    ig = jax.nn.sigmoid(gi + bi_ref[...])
    log_a = (-LRU_C) * r * sp_ref[...]
    a = jnp.exp(log_a)
    u = jnp.sqrt(1.0 - a * a) * (ig * xc)

    row = lax.broadcasted_iota(jnp.int32, (tm, 1), 0) & (SUBLANES - 1)
    s = 1
    while s < SUBLANES:
        keep = row >= s
        u = a * jnp.where(keep, pltpu.roll(u, s, 0), 0.0) + u
        a = a * jnp.where(keep, pltpu.roll(a, s, 0), 1.0)
        s *= 2
    h_prev = hcar_ref[SUBLANES - 1:SUBLANES, :]
    groups = []
    for g in range(tm // SUBLANES):
        rows = slice(g * SUBLANES, (g + 1) * SUBLANES)
        hg = a[rows] * h_prev + u[rows]
        groups.append(hg)
        h_prev = hg[SUBLANES - 1:SUBLANES, :]
    h = jnp.concatenate(groups, axis=0)
    htail = groups[-1]
    hcar_ref[...] = htail
    htail_ref[0] = htail

    x_gate = proj(d, 2 * d)
    g_rec = proj(2 * d, 3 * d)
    grec_ref[0] = jax.nn.sigmoid(g_rec) * (jax.nn.gelu(x_gate) * h)
    gatt_ref[0] = proj(3 * d, 4 * d)

    o = 4 * d
    cq_ref[0] = _rms(proj(o, o + q_lora), gq_ref[...]).astype(BF16)
    o += q_lora
    ckv_ref[0, 0] = _rms(proj(o, o + kv_lora), gkv_ref[...])
    o += kv_lora
    kr = proj(o, o + LANES) * cos_ref[...] + proj(o + LANES, o + 2 * LANES) * sin_ref[...]
    kr_ref[0] = kr


def _inproj(x, prev8, h0_8, w, cos_k, sin_k, layer, depth, ckv_stack):
    B, L, D = x.shape
    tm = _token_tile(L, 256)
    q_lora, kv_lora = w["gq"].shape[-1], w["gkv"].shape[-1]
    ncol = w["w_in"].shape[-1]
    tok = lambda n: pl.BlockSpec((1, tm, n), lambda b, l: (b, l, 0))
    per_b = pl.BlockSpec((1, SUBLANES, D), lambda b, l: (b, 0, 0))
    tab = pl.BlockSpec((tm, LANES), lambda b, l: (l, 0))
    nb = D // MXU_DIM
    kern = functools.partial(_inproj_kernel, tm=tm, d=D, q_lora=q_lora, kv_lora=kv_lora)
    alias_args = () if ckv_stack is None else (ckv_stack,)
    alias_spec = [pl.BlockSpec(memory_space=pl.ANY)] * len(alias_args)
    return pl.pallas_call(
        kern,
        grid=(B, L // tm),
        in_specs=[tok(D), per_b, per_b, _resident((1, D)), _resident((D, ncol)),
                  _resident((CONV_W, D)), _resident((1, D)),
                  _resident((nb, MXU_DIM, MXU_DIM)), _resident((1, D)),
                  _resident((nb, MXU_DIM, MXU_DIM)), _resident((1, D)), _resident((1, D)),
                  _resident((1, q_lora)), _resident((1, kv_lora)), tab, tab] + alias_spec,
        out_specs=[tok(D), tok(D), tok(q_lora),
                   pl.BlockSpec((1, 1, tm, kv_lora), lambda b, l: (layer, b, l, 0)),
                   tok(HEAD_PAD), per_b, per_b],
        out_shape=[jax.ShapeDtypeStruct((B, L, D), F32), jax.ShapeDtypeStruct((B, L, D), F32),
                   jax.ShapeDtypeStruct((B, L, q_lora), BF16),
                   jax.ShapeDtypeStruct((depth, B, L, kv_lora), F32),
                   jax.ShapeDtypeStruct((B, L, HEAD_PAD), F32),
                   jax.ShapeDtypeStruct((B, SUBLANES, D), F32),
                   jax.ShapeDtypeStruct((B, SUBLANES, D), F32)],
        scratch_shapes=[pltpu.VMEM((SUBLANES + tm, D), F32), pltpu.VMEM((SUBLANES, D), F32)],
        input_output_aliases={N_INPROJ_INPUTS: 3} if alias_args else {},
        compiler_params=_params(2),
        name="inproj",
    )(x, prev8, h0_8, w["gmix"], w["w_in"], w["conv_w"], w["conv_b"], w["wa"], w["ba"],
      w["wi"], w["bi"], w["sp"], w["gq"], w["gkv"], cos_k, sin_k, *alias_args)


def _qprep_kernel(cq_ref, wt_ref, cos_ref, sin_ref, g_ref, q_ref):
    qt = _dot_nt(wt_ref[...], cq_ref[0])
    half = ROPE_DIM // 2
    tm = qt.shape[1]
    cos, sin = cos_ref[...], sin_ref[...]
    g_nope, g_rope = g_ref[:NOPE_DIM, :], g_ref[NOPE_DIM:QK_DIM, :]
    pad = jnp.zeros((HEAD_PAD - QK_DIM, tm), BF16)
    for h in range(N_HEADS):
        r0 = h * HEAD_PAD
        nope = qt[r0:r0 + NOPE_DIM, :]
        x1 = qt[r0 + NOPE_DIM:r0 + NOPE_DIM + half, :]
        x2 = qt[r0 + NOPE_DIM + half:r0 + QK_DIM, :]
        rope = (jnp.concatenate([x1, x2], axis=0) * cos + jnp.concatenate([-x2, x1], axis=0) * sin)
        ss = (jnp.sum(nope * nope, axis=0, keepdims=True)
              + jnp.sum(rope * rope, axis=0, keepdims=True)) * (1.0 / QK_DIM)
        inv = lax.rsqrt(ss + EPS)
        q_ref[0, r0:r0 + NOPE_DIM, :] = (nope * inv * g_nope).astype(BF16)
        q_ref[0, r0 + NOPE_DIM:r0 + QK_DIM, :] = (rope * inv * g_rope).astype(BF16)
        q_ref[0, r0 + QK_DIM:r0 + HEAD_PAD, :] = pad


def _qprep(cq, w, cos_q, sin_q):
    B, L, QL = cq.shape
    tm = _token_tile(L)
    n = N_HEADS * HEAD_PAD
    tab = pl.BlockSpec((ROPE_DIM, tm), lambda b, l: (0, l))
    return pl.pallas_call(
        _qprep_kernel,
        grid=(B, L // tm),
        in_specs=[pl.BlockSpec((1, tm, QL), lambda b, l: (b, l, 0)),
                  _resident((n, QL)), tab, tab, _resident((HEAD_PAD, 1))],
        out_specs=pl.BlockSpec((1, n, tm), lambda b, l: (b, 0, l)),
        out_shape=jax.ShapeDtypeStruct((B, n, L), BF16),
        compiler_params=_params(2),
        name="qprep",
    )(cq, w["wq_t"], cos_q, sin_q, w["gqh"])


def _kvprep_kernel(*refs, n_past, tk):
    if n_past:
        past_c_ref, past_kr_ref, new_c_ref, new_kr_ref, wk_ref, wvt_ref, g_ref, k_ref, vt_ref = refs
        is_new = pl.program_id(1) >= n_past

        def pick(new_ref, past_ref):
            new = new_ref[...].reshape(new_ref.shape[-2:])
            fill = jnp.zeros((tk - new.shape[0], new.shape[1]), new.dtype)
            return jnp.where(is_new, jnp.concatenate([new, fill], axis=0), past_ref[0, 0])

        c32, kr = pick(new_c_ref, past_c_ref), pick(new_kr_ref, past_kr_ref)
    else:
        new_c_ref, new_kr_ref, wk_ref, wvt_ref, g_ref, k_ref, vt_ref = refs
        c32, kr = new_c_ref[0, 0], new_kr_ref[0]
    c = c32.astype(BF16)
    k = _dot(c, wk_ref[...])
    for h in range(N_HEADS):
        sl = slice(h * HEAD_PAD, (h + 1) * HEAD_PAD)
        kh = k[:, sl] + kr
        ss = jnp.sum(kh * kh, axis=-1, keepdims=True) * (1.0 / QK_DIM)
        k_ref[0, :, sl] = (kh * lax.rsqrt(ss + EPS) * g_ref[...]).astype(BF16)
    vt_ref[0, 0] = _dot_nt(wvt_ref[...], c).astype(BF16)


def _kvprep(ckv, layer, kr, w, tk, past=None):
    _, B, L, KV = ckv.shape
    n = N_HEADS * HEAD_PAD
    dv = w["wv_t"].shape[0]
    if past is None:
        assert L % tk == 0
        n_past, n_tiles = 0, L // tk
        in_specs = [pl.BlockSpec((1, 1, tk, KV), lambda b, t: (layer, b, t, 0)),
                    pl.BlockSpec((1, tk, HEAD_PAD), lambda b, t: (b, t, 0))]
        args = (ckv, kr)
    else:
        cache_c, cache_kr, layer = past
        t_past = cache_c.shape[2]
        assert t_past % tk == 0 and L <= tk and L % SUBLANES == 0
        n_past = t_past // tk
        n_tiles = n_past + 1
        old = lambda width: pl.BlockSpec(
            (1, 1, tk, width), lambda b, t: (layer, b, jnp.minimum(t, n_past - 1), 0))
        in_specs = [old(KV), old(HEAD_PAD),
                    pl.BlockSpec((1, 1, L, KV), lambda b, t: (layer, b, 0, 0)),
                    pl.BlockSpec((1, L, HEAD_PAD), lambda b, t: (b, 0, 0))]
        args = (cache_c, cache_kr, ckv, kr)
    return pl.pallas_call(
        functools.partial(_kvprep_kernel, n_past=n_past, tk=tk),
        grid=(B, n_tiles),
        in_specs=in_specs + [_resident((KV, n)), _resident((dv, KV)), _resident((1, HEAD_PAD))],
        out_specs=[pl.BlockSpec((1, tk, n), lambda b, t: (b, t, 0)),
                   pl.BlockSpec((1, 1, dv, tk), lambda b, t: (b, t, 0, 0))],
        out_shape=[jax.ShapeDtypeStruct((B, n_tiles * tk, n), BF16),
                   jax.ShapeDtypeStruct((B, n_tiles, dv, tk), BF16)],
        compiler_params=_params(2),
        name="kvprep",
    )(*args, w["wk"], w["wv_t"], w["gkh"])


def _flash_kernel(q_ref, k_ref, vt_ref, o_ref, m_ref, l_ref, acc_ref, *, tq, tk, q_off, t_valid,
                  v_dim, heads, bounded, lag, split_diag):
    q_start = q_off + pl.program_id(2) * tq
    n_full = jnp.minimum(q_start, t_valid) // tk
    n_end = (jnp.minimum(q_start + tq, t_valid) + tk - 1) // tk
    shift = int(math.log2(CHUNK))

    def chunk_of(pos):
        return lax.shift_right_logical(pos, shift)

    m_ref[...] = jnp.full(m_ref.shape, NEG, F32)
    l_ref[...] = jnp.zeros(l_ref.shape, F32)
    acc_ref[...] = jnp.zeros(acc_ref.shape, F32)

    def accumulate(hh, s, vj, q0):
        lanes = slice(q0, tq)
        if bounded:
            p = jnp.exp2(s)
            l_ref[hh, :, lanes] += jnp.sum(p, axis=0, keepdims=True)
        else:
            m = m_ref[hh, :, lanes]
            m_new = jnp.maximum(m, jnp.max(s, axis=0, keepdims=True))
            alpha = jnp.exp2(m - m_new)
            p = jnp.exp2(s - m_new)
            l_ref[hh, :, lanes] = alpha * l_ref[hh, :, lanes] + jnp.sum(p, axis=0, keepdims=True)
            m_ref[hh, :, lanes] = m_new
            acc_ref[hh, :, lanes] = alpha * acc_ref[hh, :, lanes]
        acc_ref[hh, :, lanes] += _dot(vj, p.astype(BF16))

    def tile(j, masked):
        if masked:
            k_pos = j * tk + lax.broadcasted_iota(jnp.int32, (tk, 1), 0)
            q_pos = q_start + lax.broadcasted_iota(jnp.int32, (1, tq), 1)
            vis = (chunk_of(k_pos) <= chunk_of(q_pos)) & (k_pos < t_valid)
        pending = {}
        for t in range(heads + lag):
            if t < heads:
                qh = q_ref[0, t * HEAD_PAD:(t + 1) * HEAD_PAD, :]
                kj = k_ref[0, pl.ds(pl.multiple_of(j * tk, tk), tk), t * HEAD_PAD:(t + 1) * HEAD_PAD]
                pending[t] = _dot(kj, qh)
            if t >= lag:
                hh = t - lag
                s = pending.pop(hh)
                if masked:
                    s = jnp.where(vis, s, NEG)
                accumulate(hh, s, vt_ref[0, j, hh * v_dim:(hh + 1) * v_dim, :], 0)

    def diagonal(j):
        half = tk // 2
        vis = (chunk_of(lax.broadcasted_iota(jnp.int32, (half, 1), 0))
               <= chunk_of(lax.broadcasted_iota(jnp.int32, (1, tq), 1)))
        items = [(k0, hh) for k0 in (0, half) for hh in range(heads)]
        pending = {}
        for t in range(len(items) + lag):
            if t < len(items):
                k0, hh = items[t]
                qh = q_ref[0, hh * HEAD_PAD:(hh + 1) * HEAD_PAD, k0:]
                kj = k_ref[0, pl.ds(pl.multiple_of(j * tk + k0, half), half),
                           hh * HEAD_PAD:(hh + 1) * HEAD_PAD]
                pending[t] = _dot(kj, qh)
            if t >= lag:
                k0, hh = items[t - lag]
                s = jnp.where(vis[:, :tq - k0], pending.pop(t - lag), NEG)
                accumulate(hh, s, vt_ref[0, j, hh * v_dim:(hh + 1) * v_dim, k0:k0 + half], k0)

    def loop(lo, hi, body):
        lax.fori_loop(lo, hi, lambda j, c: (body(j), c)[1], 0)

    loop(0, n_full, lambda j: tile(j, False))
    if split_diag:
        diagonal(n_full)
    else:
        loop(n_full, n_end, lambda j: tile(j, True))
    o_ref[0] = jnp.concatenate([acc_ref[hh] / l_ref[hh] for hh in range(heads)], axis=0).T


def _flash(q, k, vt, *, q_off, t_valid, tk, bounded, heads=FLASH_HEADS):
    B, n, L = q.shape
    T = k.shape[1]
    dv = vt.shape[2]
    v_dim = dv // N_HEADS
    tq = _token_tile(L)
    assert tq % CHUNK == 0 and q_off % CHUNK == 0 and (heads * v_dim) % LANES == 0
    split_diag = tq == tk and q_off == 0 and t_valid == T and (tk // 2) % max(CHUNK, LANES) == 0
    kern = functools.partial(_flash_kernel, tq=tq, tk=tk, q_off=q_off, t_valid=t_valid, v_dim=v_dim,
                             heads=heads, bounded=bounded, lag=min(FLASH_LAG, heads),
                             split_diag=split_diag)
    return pl.pallas_call(
        kern,
        grid=(B, N_HEADS // heads, L // tq),
        in_specs=[pl.BlockSpec((1, heads * HEAD_PAD, tq), lambda b, p, i: (b, p, i)),
                  pl.BlockSpec((1, T, heads * HEAD_PAD), lambda b, p, i: (b, 0, p)),
                  pl.BlockSpec((1, T // tk, heads * v_dim, tk), lambda b, p, i: (b, 0, p, 0))],
        out_specs=pl.BlockSpec((1, tq, heads * v_dim), lambda b, p, i: (b, i, p)),
        out_shape=jax.ShapeDtypeStruct((B, L, dv), F32),
        scratch_shapes=[pltpu.VMEM((heads, 1, tq), F32), pltpu.VMEM((heads, 1, tq), F32),
                        pltpu.VMEM((heads, v_dim, tq), F32)],
        compiler_params=_params(3),
        name="flash",
    )(q, k, vt)


def _post_kernel(x_ref, grec_ref, gatt_ref, att_ref, wo_ref, gmem_ref, wmq_ref, gqh_ref,
                 mk_ref, mv_ref, wmo_ref, o_ref):
    mixed = (grec_ref[0] + jax.nn.sigmoid(gatt_ref[0]) * att_ref[0]).astype(BF16)
    x1 = x_ref[0] + _dot(mixed, wo_ref[...])
    qm = _dot(_rms(x1, gmem_ref[...]).astype(BF16), wmq_ref[...])
    hd = gqh_ref.shape[-1]
    outs = []
    for h in range(MEM_HEADS):
        sl = slice(h * hd, (h + 1) * hd)
        qh = (_rms(qm[:, sl], gqh_ref[...]) * (hd ** -0.5)).astype(BF16)
        s = _dot_nt(qh, mk_ref[0, :, sl])
        p = jnp.exp(s - jnp.max(s, axis=-1, keepdims=True))
        l = jnp.sum(p, axis=-1, keepdims=True)
        outs.append((_dot(p.astype(BF16), mv_ref[0, :, sl]) / l).astype(BF16))
    o_ref[0] = x1 + _dot(jnp.concatenate(outs, axis=1), wmo_ref[...])


def _post(x, grec, gatt, att, mem_k, mem_v, w):
    B, L, D = x.shape
    M = mem_k.shape[1]
    tm = _token_tile(L)
    tok = pl.BlockSpec((1, tm, D), lambda b, l: (b, l, 0))
    mem = pl.BlockSpec((1, M, D), lambda b, l: (b, 0, 0))
    return pl.pallas_call(
        _post_kernel,
        grid=(B, L // tm),
        in_specs=[tok, tok, tok, tok, _resident((D, D)), _resident((1, D)), _resident((D, D)),
                  _resident((1, D // MEM_HEADS)), mem, mem, _resident((D, D))],
        out_specs=tok,
        out_shape=jax.ShapeDtypeStruct((B, L, D), F32),
        compiler_params=_params(2),
        name="post",
    )(x, grec, gatt, att, w["w_out"], w["gmem"], w["w_mq"], w["gmqh"],
      mem_k, mem_v, w["w_mo"])


def _swiglu_mid(a, b):
    return (a * jax.nn.sigmoid(a) * b).astype(BF16)


def _ffn_kernel(x_ref, g_ref, w1_ref, w3_ref, w2_ref, o_ref, *, n_split):
    x = x_ref[...]
    hf = _rms(x, g_ref[...]).astype(BF16)
    fc = w1_ref.shape[1] // n_split
    acc = x
    for c in range(n_split):
        sl = slice(c * fc, (c + 1) * fc)
        acc = acc + _dot(_swiglu_mid(_dot(hf, w1_ref[:, sl]), _dot(hf, w3_ref[:, sl])), w2_ref[sl, :])
    o_ref[...] = acc


def _ffn(x, g, w1, w3, w2):
    N, D = x.shape
    F = w1.shape[1]
    tm = _token_tile(N)
    n_split = 2
    assert F % (n_split * LANES) == 0
    tok = pl.BlockSpec((tm, D), lambda i: (i, 0))
    return pl.pallas_call(
        functools.partial(_ffn_kernel, n_split=n_split),
        grid=(N // tm,),
        in_specs=[tok, _resident((1, D)), _resident((D, F)), _resident((D, F)), _resident((F, D))],
        out_specs=tok,
        out_shape=jax.ShapeDtypeStruct((N, D), F32),
        compiler_params=_params(1),
        name="ffn",
    )(x, g, w1, w3, w2)


def _moe_kernel(x_ref, g_ref, wrh_ref, wrl_ref, br_ref, w1_ref, w3_ref, w2_ref, o_ref,
                hf_ref, gate_ref, rank_ref, *, ts, caps, n_exp):
    e = pl.program_id(1)
    lane = lax.broadcasted_iota(jnp.int32, (1, LANES), 1)
    n_sub = hf_ref.shape[0] // ts

    @pl.when(e == 0)
    def _():
        x = x_ref[...]
        hf = _rms(x, g_ref[...])
        hi = hf.astype(BF16)
        lo = (hf - hi.astype(F32)).astype(BF16)
        logits = (_dot_nt(wrh_ref[...], hi) + _dot_nt(wrh_ref[...], lo) + _dot_nt(wrl_ref[...], hi)
                  + br_ref[...])
        sub = lax.broadcasted_iota(jnp.int32, (ROUTER_ROWS, 1), 0)
        m1 = jnp.max(logits, axis=0, keepdims=True)
        i1 = jnp.min(jnp.where(logits == m1, sub, ROUTER_ROWS), axis=0, keepdims=True)
        rest = jnp.where(sub == i1, NEG, logits)
        m2 = jnp.max(rest, axis=0, keepdims=True)
        i2 = jnp.min(jnp.where(rest == m2, sub, ROUTER_ROWS), axis=0, keepdims=True)
        e2 = jnp.exp(m2 - m1)
        den = 1.0 + e2
        gate_t = jnp.where(sub == i1, 1.0 / den, 0.0) + jnp.where(sub == i2, e2 / den, 0.0)
        tm = gate_t.shape[1]
        gate_ref[...] = jnp.concatenate(
            [gate_t, jnp.zeros((LANES - ROUTER_ROWS, tm), F32)], axis=0).T
        hf_ref[...] = hi
        o_ref[...] = x
        sel_t = ((sub == i1) | (sub == i2))[:n_exp, :]
        before = (lax.broadcasted_iota(jnp.int32, (ts, ts), 0)
                  < lax.broadcasted_iota(jnp.int32, (ts, ts), 1))
        tri = jnp.where(before, 1.0, 0.0).astype(BF16)
        for s in range(n_sub):
            sel_s = sel_t[:, s * ts:(s + 1) * ts]
            rank = _dot(jnp.where(sel_s, 1.0, 0.0).astype(BF16), tri)
            rank_ref[s] = jnp.where(sel_s, rank, -1.0)

    for s in range(n_sub):
        rows = slice(s * ts, (s + 1) * ts)
        rank_e = rank_ref[s, pl.ds(e, 1), :]
        n_tok = jnp.max(rank_e).astype(jnp.int32) + 1
        ge = jnp.sum(jnp.where(lane == e, gate_ref[rows, :], 0.0), axis=-1, keepdims=True)

        def chunk(base, size):
            want = (lax.broadcasted_iota(jnp.int32, (size, 1), 0) + base).astype(F32)
            onehot = jnp.where(rank_e == want, 1.0, 0.0).astype(BF16)
            xg = _dot(onehot, hf_ref[rows, :]).astype(BF16)
            y = _dot(_swiglu_mid(_dot(xg, w1_ref[0]), _dot(xg, w3_ref[0])), w2_ref[0])
            back = lax.dot_general(onehot, y.astype(BF16), (((0,), (0,)), ((), ())),
                                   preferred_element_type=F32)
            o_ref[rows, :] += ge * back

        n_big = n_tok // caps[-1]
        lax.fori_loop(0, n_big, lambda c, carry: (chunk(c * caps[-1], caps[-1]), carry)[1], 0)
        rest = n_tok - n_big * caps[-1]
        bucket = sum((rest > c).astype(jnp.int32) for c in (0,) + caps[:-1])
        lax.switch(bucket, [lambda: None]
                   + [functools.partial(chunk, n_big * caps[-1], c) for c in caps])


def _moe(x, g, wr_hi, wr_lo, br, w1, w3, w2):
    N, D = x.shape
    E, _, F = w1.shape
    tm = _token_tile(N, 1024)
    ts = _token_tile(tm, 512)
    assert E <= SUBLANES
    tok = pl.BlockSpec((tm, D), lambda i, e: (i, 0))
    kern = functools.partial(_moe_kernel, ts=ts, caps=MOE_CAPS, n_exp=E)
    return pl.pallas_call(
        kern,
        grid=(N // tm, E),
        in_specs=[tok, _resident((1, D)), _resident((ROUTER_ROWS, D)), _resident((ROUTER_ROWS, D)),
                  _resident((ROUTER_ROWS, 1)),
                  pl.BlockSpec((1, D, F), lambda i, e: (e, 0, 0)),
                  pl.BlockSpec((1, D, F), lambda i, e: (e, 0, 0)),
                  pl.BlockSpec((1, F, D), lambda i, e: (e, 0, 0))],
        out_specs=tok,
        out_shape=jax.ShapeDtypeStruct((N, D), F32),
        scratch_shapes=[pltpu.VMEM((tm, D), BF16), pltpu.VMEM((tm, LANES), F32),
                        pltpu.VMEM((tm // ts, E, ts), F32)],
        compiler_params=_params(2),
        name="moe",
    )(x, g, wr_hi, wr_lo, br, w1, w3, w2)


def _rot_half_cols(w):
    half = ROPE_DIM // 2
    return jnp.concatenate([-w[..., half:], w[..., :half]], axis=-1)


def _rope_lanes(a):
    return jnp.pad(a, [(0, 0)] * (a.ndim - 1) + [(NOPE_DIM, HEAD_PAD - QK_DIM)])


def _pad_cols(w, n):
    return jnp.pad(w, [(0, 0)] * (w.ndim - 1) + [(0, n - w.shape[-1])])


def _head_pad(nope, rope):
    z = jnp.zeros(nope.shape[:-1] + (HEAD_PAD - QK_DIM,), nope.dtype)
    out = jnp.concatenate([nope, rope, z], axis=-1)
    return out.reshape(out.shape[:-2] + (N_HEADS * HEAD_PAD,))


def _block_diag_tiles(w):
    nblk, bw, _ = w.shape
    per = MXU_DIM // bw
    w4 = w.reshape(nblk // per, per, bw, bw)
    eye = jnp.eye(per, dtype=w.dtype)
    return jnp.einsum("cpij,pq->cpiqj", w4, eye).reshape(nblk // per, MXU_DIM, MXU_DIM)


def _layer_weights(l, p):
    D = p["w_in"].shape[1]
    q_lora = p["q_lat_norm"].shape[-1]
    kv_lora = p["kv_lat_norm"].shape[-1]
    w_in = p["w_in"][l]
    sp = (D, 2 * D, 2 * D + q_lora, 2 * D + q_lora + kv_lora, 2 * D + q_lora + kv_lora + ROPE_DIM,
          3 * D + q_lora + kv_lora + ROPE_DIM)
    x_rec, x_gate, c_q, c_kv, k_rot, g_rec, g_att = jnp.split(w_in, sp, axis=-1)
    w_in_perm = jnp.concatenate(
        [x_rec, x_gate, g_rec, g_att, c_q, c_kv, _rope_lanes(k_rot),
         _rope_lanes(_rot_half_cols(k_rot))], axis=-1).astype(BF16)

    wq = p["w_uq"][l].reshape(q_lora, N_HEADS, QK_DIM)
    wq_n, wq_r = wq[..., :NOPE_DIM], wq[..., NOPE_DIM:]
    wk = p["w_uk"][l].reshape(kv_lora, N_HEADS, NOPE_DIM)
    gq, gk = p["q_head_norm"][l], p["k_head_norm"][l]
    q_scale = (QK_DIM ** -0.5) * LOG2E
    return dict(
        gmix=p["norm_mix"][l].reshape(1, D), w_in=w_in_perm,
        conv_w=p["conv_w"][l], conv_b=p["conv_b"][l].reshape(1, D),
        wa=_block_diag_tiles(p["lru_wa"][l]).astype(BF16), ba=p["lru_ba"][l].reshape(1, D),
        wi=_block_diag_tiles(p["lru_wi"][l]).astype(BF16), bi=p["lru_bi"][l].reshape(1, D),
        sp=jax.nn.softplus(-p["lru_lambda"][l]).reshape(1, D),
        gq=p["q_lat_norm"][l].reshape(1, q_lora), gkv=p["kv_lat_norm"][l].reshape(1, kv_lora),
        wq_t=_head_pad(wq_n, wq_r).T.astype(BF16),
        gqh=(_pad_cols(gq, HEAD_PAD) * q_scale).reshape(HEAD_PAD, 1),
        wk=_head_pad(wk, jnp.zeros((kv_lora, N_HEADS, ROPE_DIM), F32)).astype(BF16),
        wv_t=p["w_uv"][l].T.astype(BF16),
        gkh=_pad_cols(gk, HEAD_PAD).reshape(1, HEAD_PAD),
        score_bound=QK_DIM * jnp.max(jnp.abs(gq)) * jnp.max(jnp.abs(gk)) * q_scale,
        w_out=p["w_out"][l].astype(BF16), gmem=p["norm_mem"][l].reshape(1, D),
        w_mq=p["w_mq"][l].astype(BF16), gmqh=p["mq_head_norm"][l].reshape(1, -1),
        w_mo=p["w_mo"][l].astype(BF16), gffn=p["norm_ffn"][l].reshape(1, D),
    )


def _rope_tables(pos):
    half = ROPE_DIM // 2
    inv_freq = ROPE_BASE ** (-jnp.arange(half, dtype=F32) / half)
    ang = pos.astype(F32)[:, None] * inv_freq[None, :]
    cos2 = jnp.concatenate([jnp.cos(ang), jnp.cos(ang)], axis=-1)
    sin2 = jnp.concatenate([jnp.sin(ang), jnp.sin(ang)], axis=-1)
    return dict(cos_k=_rope_lanes(cos2), sin_k=_rope_lanes(sin2), cos_q=cos2.T, sin_q=sin2.T)


def _layer(l, depth, x, tabs, q_off, conv_prev, h0, past, ckv_stack, mem_k, mem_v, w, p):
    B, L, D = x.shape
    prev8 = jnp.pad(conv_prev, ((0, 0), (SUBLANES - (CONV_W - 1), 0), (0, 0)))
    h0_8 = jnp.pad(h0[:, None, :], ((0, 0), (SUBLANES - 1, 0), (0, 0)))
    grec, gatt, cq, ckv, kr, ctail, htail = _inproj(x, prev8, h0_8, w, tabs["cos_k"], tabs["sin_k"],
                                                    l, depth, ckv_stack)

    q = _qprep(cq, w, tabs["cos_q"], tabs["sin_q"])
    if past is None:
        t_valid = L
        tk = min(512, L)
    else:
        t_valid = past[0].shape[2] + L
        tk = 512
    k, vt = _kvprep(ckv, l, kr, w, tk, past)
    flash = functools.partial(_flash, q_off=q_off, t_valid=t_valid, tk=tk)
    att = lax.cond(w["score_bound"] <= SAFE_LOG2_RANGE, functools.partial(flash, bounded=True),
                   functools.partial(flash, bounded=False), q, k, vt)

    x2 = _post(x, grec, gatt, att, mem_k, mem_v, w).reshape(B * L, D)
    j = l // 2
    if l % 2 == 0:
        x3 = _ffn(x2, w["gffn"], p["ffn_w1"][j].astype(BF16), p["ffn_w3"][j].astype(BF16),
                  p["ffn_w2"][j].astype(BF16))
    else:
        E = p["moe_router"].shape[-1]
        wr = _pad_cols(p["moe_router"][j], ROUTER_ROWS).T
        wr_hi = wr.astype(BF16)
        wr_lo = (wr - wr_hi.astype(F32)).astype(BF16)
        br = jnp.concatenate([p["moe_router_b"][j],
                              jnp.full((ROUTER_ROWS - E,), NEG, F32)]).reshape(ROUTER_ROWS, 1)
        x3 = _moe(x2, w["gffn"], wr_hi, wr_lo, br, p["moe_w1"][j].astype(BF16),
                  p["moe_w3"][j].astype(BF16), p["moe_w2"][j].astype(BF16))
    conv_state = ctail[:, SUBLANES - (CONV_W - 1):, :]
    return x3.reshape(B, L, D), conv_state, htail[:, SUBLANES - 1, :], ckv, kr[..., NOPE_DIM:QK_DIM]


def kernel(x_prompt, x_sample, cache_ckv, cache_krope, cache_mem_k, cache_mem_v, state_lru, state_conv, mem_prompt, norm_mix, w_in, conv_w, conv_b, lru_wa, lru_ba, lru_wi, lru_bi, lru_lambda, q_lat_norm, w_uq, kv_lat_norm, w_uk, w_uv, q_head_norm, k_head_norm, w_out, norm_mem, mem_in_norm, w_mq, w_mk, w_mv, w_mo, mq_head_norm, mk_head_norm, norm_ffn, ffn_w1, ffn_w3, ffn_w2, moe_router, moe_router_b, moe_w1, moe_w3, moe_w2):
    p = dict(norm_mix=norm_mix, w_in=w_in, conv_w=conv_w, conv_b=conv_b, lru_wa=lru_wa, lru_ba=lru_ba,
             lru_wi=lru_wi, lru_bi=lru_bi, lru_lambda=lru_lambda, q_lat_norm=q_lat_norm, w_uq=w_uq,
             kv_lat_norm=kv_lat_norm, w_uk=w_uk, w_uv=w_uv, q_head_norm=q_head_norm,
             k_head_norm=k_head_norm, w_out=w_out, norm_mem=norm_mem, w_mq=w_mq, w_mo=w_mo,
             mq_head_norm=mq_head_norm, norm_ffn=norm_ffn, ffn_w1=ffn_w1, ffn_w3=ffn_w3,
             ffn_w2=ffn_w2, moe_router=moe_router, moe_router_b=moe_router_b, moe_w1=moe_w1,
             moe_w3=moe_w3, moe_w2=moe_w2)
    depth = w_in.shape[0]
    weights = [_layer_weights(l, p) for l in range(depth)]

    Bp, Lp, D = x_prompt.shape
    Bs, Ls, _ = x_sample.shape
    t_past = cache_ckv.shape[2]
    M = mem_prompt.shape[1]

    tabs_p = _rope_tables(jnp.arange(Lp, dtype=jnp.int32))
    zero_conv = jnp.zeros((Bp, CONV_W - 1, D), F32)
    zero_h = jnp.zeros((Bp, D), F32)
    x = x_prompt
    ckv_p = None
    outs_p = [[] for _ in range(5)]
    for l in range(depth):
        mk, mv, mk_b, mv_b = _memkv(mem_prompt, mem_in_norm[l], w_mk[l], w_mv[l], mk_head_norm[l])
        x, cs, hl, ckv_p, kr = _layer(l, depth, x, tabs_p, 0, zero_conv, zero_h, None, ckv_p, mk_b, mv_b,
                                      weights[l], p)
        for acc, val in zip(outs_p, (kr, hl, cs, mk, mv)):
            acc.append(val)
    y_prompt = x

    tabs_s = _rope_tables(t_past + jnp.arange(Ls, dtype=jnp.int32))
    x = x_sample
    ckv_s = None
    outs_s = [[] for _ in range(3)]
    cache_kr = _rope_lanes(cache_krope)
    for l in range(depth):
        x, cs, hl, ckv_s, kr = _layer(l, depth, x, tabs_s, t_past, state_conv[l], state_lru[l],
                                      (cache_ckv, cache_kr, l), ckv_s,
                                      cache_mem_k[l].reshape(Bs, M, D).astype(BF16),
                                      cache_mem_v[l].reshape(Bs, M, D).astype(BF16), weights[l], p)
        for acc, val in zip(outs_s, (kr, hl, cs)):
            acc.append(val)
    y_sample = x

    return ((y_prompt, y_sample, ckv_p) + tuple(jnp.stack(a) for a in outs_p)
            + (ckv_s,) + tuple(jnp.stack(a) for a in outs_s))
```

```python
import functools
import math

import jax
import jax.numpy as jnp
from jax import lax
from jax.experimental import pallas as pl
from jax.experimental.pallas import tpu as pltpu

F32 = jnp.float32
BF16 = jnp.bfloat16

EPS = 1e-6
CHUNK = 64
LRU_C = 8.0
CONV_W = 4
N_HEADS = 16
NOPE_DIM = 64
ROPE_DIM = 32
QK_DIM = NOPE_DIM + ROPE_DIM
ROPE_BASE = 10000.0
MEM_HEADS = 4

LANES = 128
SUBLANES = 8
MXU_DIM = 256
VMEM_LIMIT = 56 * 1024 * 1024

HEAD_PAD = LANES
ROUTER_ROWS = 16
N_INPROJ_INPUTS = 16
MOE_CAPS = (112, 144, 176, 208)
SAFE_LOG2_RANGE = 60.0
FLASH_HEADS = 8
FLASH_LAG = 8
NEG = -1e30
LOG2E = 1.4426950408889634


def _resident(shape):
    nd = len(shape)
    return pl.BlockSpec(shape, lambda *_: (0,) * nd, pipeline_mode=pl.Buffered(1))


def _params(n_axes):
    return pltpu.CompilerParams(dimension_semantics=("arbitrary",) * n_axes,
                                vmem_limit_bytes=VMEM_LIMIT)


def _rms(x, gain):
    return x * lax.rsqrt(jnp.mean(x * x, axis=-1, keepdims=True) + EPS) * gain


def _dot(a, b):
    return jnp.dot(a, b, preferred_element_type=F32)


def _dot_nt(a, b):
    return lax.dot_general(a, b, (((1,), (1,)), ((), ())), preferred_element_type=F32)


def _token_tile(n, cap=512):
    t = min(n, cap)
    assert n % t == 0 and t % SUBLANES == 0
    return t


def _memkv_kernel(mem_ref, gin_ref, wk_ref, wv_ref, gk_ref, k_ref, v_ref, kb_ref, vb_ref):
    mn = _rms(mem_ref[0], gin_ref[...]).astype(BF16)
    k = _dot(mn, wk_ref[...])
    v = _dot(mn, wv_ref[...])
    hd = gk_ref.shape[-1]
    for h in range(MEM_HEADS):
        sl = slice(h * hd, (h + 1) * hd)
        kh = _rms(k[:, sl], gk_ref[...])
        k_ref[0, :, h, :] = kh
        kb_ref[0, :, sl] = kh.astype(BF16)
        v_ref[0, :, h, :] = v[:, sl]
    vb_ref[0] = v.astype(BF16)


def _memkv(mem, g_in, w_k, w_v, g_k):
    B, M, D = mem.shape
    tok = pl.BlockSpec((1, M, D), lambda b: (b, 0, 0))
    hd = D // MEM_HEADS
    tok4 = pl.BlockSpec((1, M, MEM_HEADS, hd), lambda b: (b, 0, 0, 0))
    return pl.pallas_call(
        _memkv_kernel,
        grid=(B,),
        in_specs=[tok, _resident((1, D)), _resident((D, D)), _resident((D, D)),
                  _resident((1, hd))],
        out_specs=[tok4, tok4, tok, tok],
        out_shape=[jax.ShapeDtypeStruct((B, M, MEM_HEADS, hd), F32)] * 2
        + [jax.ShapeDtypeStruct((B, M, D), BF16)] * 2,
        compiler_params=_params(1),
        name="memkv",
    )(mem, g_in.reshape(1, D), w_k.astype(BF16), w_v.astype(BF16), g_k.reshape(1, -1))


def _inproj_kernel(x_ref, prev_ref, h0_ref, gmix_ref, win_ref, cw_ref, cb_ref, wa_ref, ba_ref,
                   wi_ref, bi_ref, sp_ref, gq_ref, gkv_ref, cos_ref, sin_ref, *rest,
                   tm, d, q_lora, kv_lora):
    (grec_ref, gatt_ref, cq_ref, ckv_ref, kr_ref, ctail_ref, htail_ref,
     xext_ref, hcar_ref) = rest[-9:]
    @pl.when(pl.program_id(1) == 0)
    def _():
        xext_ref[0:SUBLANES, :] = prev_ref[0]
        hcar_ref[...] = h0_ref[0]

    hn = _rms(x_ref[0], gmix_ref[...]).astype(BF16)

    def proj(c0, c1):
        return _dot(hn, win_ref[:, c0:c1])

    x_rec = proj(0, d)
    xext_ref[SUBLANES:SUBLANES + tm, :] = x_rec
    xc = cb_ref[...] + cw_ref[CONV_W - 1:CONV_W, :] * x_rec
    for j in range(CONV_W - 1):
        off = SUBLANES - (CONV_W - 1) + j
        xc = xc + cw_ref[j:j + 1, :] * xext_ref[off:off + tm, :]
    tail = x_rec[tm - SUBLANES:tm, :]
    ctail_ref[0] = tail
    xext_ref[0:SUBLANES, :] = tail

    xcb = xc.astype(BF16)
    nb = d // MXU_DIM
    gr = jnp.concatenate(
        [_dot(xcb[:, c * MXU_DIM:(c + 1) * MXU_DIM], wa_ref[c]) for c in range(nb)], axis=1)
    gi = jnp.concatenate(
        [_dot(xcb[:, c * MXU_DIM:(c + 1) * MXU_DIM], wi_ref[c]) for c in range(nb)], axis=1)
    r = jax.nn.sigmoid(gr + ba_ref[...])
    ig = jax.nn.sigmoid(gi + bi_ref[...])
    log_a = (-LRU_C) * r * sp_ref[...]
    a = jnp.exp(log_a)
    u = jnp.sqrt(1.0 - a * a) * (ig * xc)

    row = lax.broadcasted_iota(jnp.int32, (tm, 1), 0) & (SUBLANES - 1)
    s = 1
    while s < SUBLANES:
        keep = row >= s
        u = a * jnp.where(keep, pltpu.roll(u, s, 0), 0.0) + u
        a = a * jnp.where(keep, pltpu.roll(a, s, 0), 1.0)
        s *= 2
    h_prev = hcar_ref[SUBLANES - 1:SUBLANES, :]
    groups = []
    for g in range(tm // SUBLANES):
        rows = slice(g * SUBLANES, (g + 1) * SUBLANES)
        hg = a[rows] * h_prev + u[rows]
        groups.append(hg)
        h_prev = hg[SUBLANES - 1:SUBLANES, :]
    h = jnp.concatenate(groups, axis=0)
    htail = groups[-1]
    hcar_ref[...] = htail
    htail_ref[0] = htail

    x_gate = proj(d, 2 * d)
    g_rec = proj(2 * d, 3 * d)
    grec_ref[0] = jax.nn.sigmoid(g_rec) * (jax.nn.gelu(x_gate) * h)
    gatt_ref[0] = proj(3 * d, 4 * d)

    o = 4 * d
    cq_ref[0] = _rms(proj(o, o + q_lora), gq_ref[...]).astype(BF16)
    o += q_lora
    ckv_ref[0, 0] = _rms(proj(o, o + kv_lora), gkv_ref[...])
    o += kv_lora
    kr = proj(o, o + LANES) * cos_ref[...] + proj(o + LANES, o + 2 * LANES) * sin_ref[...]
    kr_ref[0] = kr


def _inproj(x, prev8, h0_8, w, cos_k, sin_k, layer, depth, ckv_stack):
    B, L, D = x.shape
    tm = _token_tile(L, 256)
    q_lora, kv_lora = w["gq"].shape[-1], w["gkv"].shape[-1]
    ncol = w["w_in"].shape[-1]
    tok = lambda n: pl.BlockSpec((1, tm, n), lambda b, l: (b, l, 0))
    per_b = pl.BlockSpec((1, SUBLANES, D), lambda b, l: (b, 0, 0))
    tab = pl.BlockSpec((tm, LANES), lambda b, l: (l, 0))
    nb = D // MXU_DIM
    kern = functools.partial(_inproj_kernel, tm=tm, d=D, q_lora=q_lora, kv_lora=kv_lora)
    alias_args = () if ckv_stack is None else (ckv_stack,)
    alias_spec = [pl.BlockSpec(memory_space=pl.ANY)] * len(alias_args)
    return pl.pallas_call(
        kern,
        grid=(B, L // tm),
        in_specs=[tok(D), per_b, per_b, _resident((1, D)), _resident((D, ncol)),
                  _resident((CONV_W, D)), _resident((1, D)),
                  _resident((nb, MXU_DIM, MXU_DIM)), _resident((1, D)),
                  _resident((nb, MXU_DIM, MXU_DIM)), _resident((1, D)), _resident((1, D)),
                  _resident((1, q_lora)), _resident((1, kv_lora)), tab, tab] + alias_spec,
        out_specs=[tok(D), tok(D), tok(q_lora),
                   pl.BlockSpec((1, 1, tm, kv_lora), lambda b, l: (layer, b, l, 0)),
                   tok(HEAD_PAD), per_b, per_b],
        out_shape=[jax.ShapeDtypeStruct((B, L, D), F32), jax.ShapeDtypeStruct((B, L, D), F32),
                   jax.ShapeDtypeStruct((B, L, q_lora), BF16),
                   jax.ShapeDtypeStruct((depth, B, L, kv_lora), F32),
                   jax.ShapeDtypeStruct((B, L, HEAD_PAD), F32),
                   jax.ShapeDtypeStruct((B, SUBLANES, D), F32),
                   jax.ShapeDtypeStruct((B, SUBLANES, D), F32)],
        scratch_shapes=[pltpu.VMEM((SUBLANES + tm, D), F32), pltpu.VMEM((SUBLANES, D), F32)],
        input_output_aliases={N_INPROJ_INPUTS: 3} if alias_args else {},
        compiler_params=_params(2),
        name="inproj",
    )(x, prev8, h0_8, w["gmix"], w["w_in"], w["conv_w"], w["conv_b"], w["wa"], w["ba"],
      w["wi"], w["bi"], w["sp"], w["gq"], w["gkv"], cos_k, sin_k, *alias_args)


def _qprep_kernel(cq_ref, wt_ref, cos_ref, sin_ref, g_ref, q_ref):
    qt = _dot_nt(wt_ref[...], cq_ref[0])
    half = ROPE_DIM // 2
    tm = qt.shape[1]
    cos, sin = cos_ref[...], sin_ref[...]
    g_nope, g_rope = g_ref[:NOPE_DIM, :], g_ref[NOPE_DIM:QK_DIM, :]
    pad = jnp.zeros((HEAD_PAD - QK_DIM, tm), BF16)
    for h in range(N_HEADS):
        r0 = h * HEAD_PAD
        nope = qt[r0:r0 + NOPE_DIM, :]
        x1 = qt[r0 + NOPE_DIM:r0 + NOPE_DIM + half, :]
        x2 = qt[r0 + NOPE_DIM + half:r0 + QK_DIM, :]
        rope = (jnp.concatenate([x1, x2], axis=0) * cos + jnp.concatenate([-x2, x1], axis=0) * sin)
        ss = (jnp.sum(nope * nope, axis=0, keepdims=True)
              + jnp.sum(rope * rope, axis=0, keepdims=True)) * (1.0 / QK_DIM)
        inv = lax.rsqrt(ss + EPS)
        q_ref[0, r0:r0 + NOPE_DIM, :] = (nope * inv * g_nope).astype(BF16)
        q_ref[0, r0 + NOPE_DIM:r0 + QK_DIM, :] = (rope * inv * g_rope).astype(BF16)
        q_ref[0, r0 + QK_DIM:r0 + HEAD_PAD, :] = pad


def _qprep(cq, w, cos_q, sin_q):
    B, L, QL = cq.shape
    tm = _token_tile(L)
    n = N_HEADS * HEAD_PAD
    tab = pl.BlockSpec((ROPE_DIM, tm), lambda b, l: (0, l))
    return pl.pallas_call(
        _qprep_kernel,
        grid=(B, L // tm),
        in_specs=[pl.BlockSpec((1, tm, QL), lambda b, l: (b, l, 0)),
                  _resident((n, QL)), tab, tab, _resident((HEAD_PAD, 1))],
        out_specs=pl.BlockSpec((1, n, tm), lambda b, l: (b, 0, l)),
        out_shape=jax.ShapeDtypeStruct((B, n, L), BF16),
        compiler_params=_params(2),
        name="qprep",
    )(cq, w["wq_t"], cos_q, sin_q, w["gqh"])


def _kvprep_kernel(*refs, n_past, tk):
    if n_past:
        past_c_ref, past_kr_ref, new_c_ref, new_kr_ref, wk_ref, wvt_ref, g_ref, k_ref, vt_ref = refs
        is_new = pl.program_id(1) >= n_past

        def pick(new_ref, past_ref):
            new = new_ref[...].reshape(new_ref.shape[-2:])
            fill = jnp.zeros((tk - new.shape[0], new.shape[1]), new.dtype)
            return jnp.where(is_new, jnp.concatenate([new, fill], axis=0), past_ref[0, 0])

        c32, kr = pick(new_c_ref, past_c_ref), pick(new_kr_ref, past_kr_ref)
    else:
        new_c_ref, new_kr_ref, wk_ref, wvt_ref, g_ref, k_ref, vt_ref = refs
        c32, kr = new_c_ref[0, 0], new_kr_ref[0]
    c = c32.astype(BF16)
    k = _dot(c, wk_ref[...])
    for h in range(N_HEADS):
        sl = slice(h * HEAD_PAD, (h + 1) * HEAD_PAD)
        kh = k[:, sl] + kr
        ss = jnp.sum(kh * kh, axis=-1, keepdims=True) * (1.0 / QK_DIM)
        k_ref[0, :, sl] = (kh * lax.rsqrt(ss + EPS) * g_ref[...]).astype(BF16)
    vt_ref[0, 0] = _dot_nt(wvt_ref[...], c).astype(BF16)


def _kvprep(ckv, layer, kr, w, tk, past=None):
    _, B, L, KV = ckv.shape
    n = N_HEADS * HEAD_PAD
    dv = w["wv_t"].shape[0]
    if past is None:
        assert L % tk == 0
        n_past, n_tiles = 0, L // tk
        in_specs = [pl.BlockSpec((1, 1, tk, KV), lambda b, t: (layer, b, t, 0)),
                    pl.BlockSpec((1, tk, HEAD_PAD), lambda b, t: (b, t, 0))]
        args = (ckv, kr)
    else:
        cache_c, cache_kr, layer = past
        t_past = cache_c.shape[2]
        assert t_past % tk == 0 and L <= tk and L % SUBLANES == 0
        n_past = t_past // tk
        n_tiles = n_past + 1
        old = lambda width: pl.BlockSpec(
            (1, 1, tk, width), lambda b, t: (layer, b, jnp.minimum(t, n_past - 1), 0))
        in_specs = [old(KV), old(HEAD_PAD),
                    pl.BlockSpec((1, 1, L, KV), lambda b, t: (layer, b, 0, 0)),
                    pl.BlockSpec((1, L, HEAD_PAD), lambda b, t: (b, 0, 0))]
        args = (cache_c, cache_kr, ckv, kr)
    return pl.pallas_call(
        functools.partial(_kvprep_kernel, n_past=n_past, tk=tk),
        grid=(B, n_tiles),
        in_specs=in_specs + [_resident((KV, n)), _resident((dv, KV)), _resident((1, HEAD_PAD))],
        out_specs=[pl.BlockSpec((1, tk, n), lambda b, t: (b, t, 0)),
                   pl.BlockSpec((1, 1, dv, tk), lambda b, t: (b, t, 0, 0))],
        out_shape=[jax.ShapeDtypeStruct((B, n_tiles * tk, n), BF16),
                   jax.ShapeDtypeStruct((B, n_tiles, dv, tk), BF16)],
        compiler_params=_params(2),
        name="kvprep",
    )(*args, w["wk"], w["wv_t"], w["gkh"])


def _flash_kernel(q_ref, k_ref, vt_ref, o_ref, m_ref, l_ref, acc_ref, *, tq, tk, q_off, t_valid,
                  v_dim, heads, bounded, lag, split_diag):
    q_start = q_off + pl.program_id(2) * tq
    n_full = jnp.minimum(q_start, t_valid) // tk
    n_end = (jnp.minimum(q_start + tq, t_valid) + tk - 1) // tk
    shift = int(math.log2(CHUNK))

    def chunk_of(pos):
        return lax.shift_right_logical(pos, shift)

    m_ref[...] = jnp.full(m_ref.shape, NEG, F32)
    l_ref[...] = jnp.zeros(l_ref.shape, F32)
    acc_ref[...] = jnp.zeros(acc_ref.shape, F32)

    def accumulate(hh, s, vj, q0):
        lanes = slice(q0, tq)
        if bounded:
            p = jnp.exp2(s)
            l_ref[hh, :, lanes] += jnp.sum(p, axis=0, keepdims=True)
        else:
            m = m_ref[hh, :, lanes]
            m_new = jnp.maximum(m, jnp.max(s, axis=0, keepdims=True))
            alpha = jnp.exp2(m - m_new)
            p = jnp.exp2(s - m_new)
            l_ref[hh, :, lanes] = alpha * l_ref[hh, :, lanes] + jnp.sum(p, axis=0, keepdims=True)
            m_ref[hh, :, lanes] = m_new
            acc_ref[hh, :, lanes] = alpha * acc_ref[hh, :, lanes]
        acc_ref[hh, :, lanes] += _dot(vj, p.astype(BF16))

    def tile(j, masked):
        if masked:
            k_pos = j * tk + lax.broadcasted_iota(jnp.int32, (tk, 1), 0)
            q_pos = q_start + lax.broadcasted_iota(jnp.int32, (1, tq), 1)
            vis = (chunk_of(k_pos) <= chunk_of(q_pos)) & (k_pos < t_valid)
        pending = {}
        for t in range(heads + lag):
            if t < heads:
                qh = q_ref[0, t * HEAD_PAD:(t + 1) * HEAD_PAD, :]
                kj = k_ref[0, pl.ds(pl.multiple_of(j * tk, tk), tk), t * HEAD_PAD:(t + 1) * HEAD_PAD]
                pending[t] = _dot(kj, qh)
            if t >= lag:
                hh = t - lag
                s = pending.pop(hh)
                if masked:
                    s = jnp.where(vis, s, NEG)
                accumulate(hh, s, vt_ref[0, j, hh * v_dim:(hh + 1) * v_dim, :], 0)

    def diagonal(j):
        half = tk // 2
        vis = (chunk_of(lax.broadcasted_iota(jnp.int32, (half, 1), 0))
               <= chunk_of(lax.broadcasted_iota(jnp.int32, (1, tq), 1)))
        items = [(k0, hh) for k0 in (0, half) for hh in range(heads)]
        pending = {}
        for t in range(len(items) + lag):
            if t < len(items):
                k0, hh = items[t]
                qh = q_ref[0, hh * HEAD_PAD:(hh + 1) * HEAD_PAD, k0:]
                kj = k_ref[0, pl.ds(pl.multiple_of(j * tk + k0, half), half),
                           hh * HEAD_PAD:(hh + 1) * HEAD_PAD]
                pending[t] = _dot(kj, qh)
            if t >= lag:
                k0, hh = items[t - lag]
                s = jnp.where(vis[:, :tq - k0], pending.pop(t - lag), NEG)
                accumulate(hh, s, vt_ref[0, j, hh * v_dim:(hh + 1) * v_dim, k0:k0 + half], k0)

    def loop(lo, hi, body):
        lax.fori_loop(lo, hi, lambda j, c: (body(j), c)[1], 0)

    loop(0, n_full, lambda j: tile(j, False))
    if split_diag:
        diagonal(n_full)
    else:
        loop(n_full, n_end, lambda j: tile(j, True))
    o_ref[0] = jnp.concatenate([acc_ref[hh] / l_ref[hh] for hh in range(heads)], axis=0).T


def _flash(q, k, vt, *, q_off, t_valid, tk, bounded, heads=FLASH_HEADS):
    B, n, L = q.shape
    T = k.shape[1]
    dv = vt.shape[2]
    v_dim = dv // N_HEADS
    tq = _token_tile(L)
    assert tq % CHUNK == 0 and q_off % CHUNK == 0 and (heads * v_dim) % LANES == 0
    split_diag = tq == tk and q_off == 0 and t_valid == T and (tk // 2) % max(CHUNK, LANES) == 0
    kern = functools.partial(_flash_kernel, tq=tq, tk=tk, q_off=q_off, t_valid=t_valid, v_dim=v_dim,
                             heads=heads, bounded=bounded, lag=min(FLASH_LAG, heads),
                             split_diag=split_diag)
    return pl.pallas_call(
        kern,
        grid=(B, N_HEADS // heads, L // tq),
        in_specs=[pl.BlockSpec((1, heads * HEAD_PAD, tq), lambda b, p, i: (b, p, i)),
                  pl.BlockSpec((1, T, heads * HEAD_PAD), lambda b, p, i: (b, 0, p)),
                  pl.BlockSpec((1, T // tk, heads * v_dim, tk), lambda b, p, i: (b, 0, p, 0))],
        out_specs=pl.BlockSpec((1, tq, heads * v_dim), lambda b, p, i: (b, i, p)),
        out_shape=jax.ShapeDtypeStruct((B, L, dv), F32),
        scratch_shapes=[pltpu.VMEM((heads, 1, tq), F32), pltpu.VMEM((heads, 1, tq), F32),
                        pltpu.VMEM((heads, v_dim, tq), F32)],
        compiler_params=_params(3),
        name="flash",
    )(q, k, vt)


def _post_kernel(x_ref, grec_ref, gatt_ref, att_ref, wo_ref, gmem_ref, wmq_ref, gqh_ref,
                 mk_ref, mv_ref, wmo_ref, o_ref):
    mixed = (grec_ref[0] + jax.nn.sigmoid(gatt_ref[0]) * att_ref[0]).astype(BF16)
    x1 = x_ref[0] + _dot(mixed, wo_ref[...])
    qm = _dot(_rms(x1, gmem_ref[...]).astype(BF16), wmq_ref[...])
    hd = gqh_ref.shape[-1]
    heads = [slice(h * hd, (h + 1) * hd) for h in range(MEM_HEADS)]
    scores = [_dot_nt((_rms(qm[:, sl], gqh_ref[...]) * (hd ** -0.5)).astype(BF16), mk_ref[0, :, sl])
              for sl in heads]
    outs = []
    for sl, s in zip(heads, scores):
        p = jnp.exp(s - jnp.max(s, axis=-1, keepdims=True))
        l = jnp.sum(p, axis=-1, keepdims=True)
        outs.append((_dot(p.astype(BF16), mv_ref[0, :, sl]) / l).astype(BF16))
    o_ref[0] = x1 + _dot(jnp.concatenate(outs, axis=1), wmo_ref[...])


def _post(x, grec, gatt, att, mem_k, mem_v, w):
    B, L, D = x.shape
    M = mem_k.shape[1]
    tm = _token_tile(L)
    tok = pl.BlockSpec((1, tm, D), lambda b, l: (b, l, 0))
    mem = pl.BlockSpec((1, M, D), lambda b, l: (b, 0, 0))
    return pl.pallas_call(
        _post_kernel,
        grid=(B, L // tm),
        in_specs=[tok, tok, tok, tok, _resident((D, D)), _resident((1, D)), _resident((D, D)),
                  _resident((1, D // MEM_HEADS)), mem, mem, _resident((D, D))],
        out_specs=tok,
        out_shape=jax.ShapeDtypeStruct((B, L, D), F32),
        compiler_params=_params(2),
        name="post",
    )(x, grec, gatt, att, w["w_out"], w["gmem"], w["w_mq"], w["gmqh"],
      mem_k, mem_v, w["w_mo"])


def _swiglu_mid(a, b):
    return (a * jax.nn.sigmoid(a) * b).astype(BF16)


def _ffn_kernel(x_ref, g_ref, w1_ref, w3_ref, w2_ref, o_ref, *, n_split):
    x = x_ref[...]
    hf = _rms(x, g_ref[...]).astype(BF16)
    fc = w1_ref.shape[1] // n_split
    acc = x
    for c in range(n_split):
        sl = slice(c * fc, (c + 1) * fc)
        acc = acc + _dot(_swiglu_mid(_dot(hf, w1_ref[:, sl]), _dot(hf, w3_ref[:, sl])), w2_ref[sl, :])
    o_ref[...] = acc


def _ffn(x, g, w1, w3, w2):
    N, D = x.shape
    F = w1.shape[1]
    tm = _token_tile(N)
    n_split = 2
    assert F % (n_split * LANES) == 0
    tok = pl.BlockSpec((tm, D), lambda i: (i, 0))
    return pl.pallas_call(
        functools.partial(_ffn_kernel, n_split=n_split),
        grid=(N // tm,),
        in_specs=[tok, _resident((1, D)), _resident((D, F)), _resident((D, F)), _resident((F, D))],
        out_specs=tok,
        out_shape=jax.ShapeDtypeStruct((N, D), F32),
        compiler_params=_params(1),
        name="ffn",
    )(x, g, w1, w3, w2)


def _moe_kernel(x_ref, g_ref, wrh_ref, wrl_ref, br_ref, w1_ref, w3_ref, w2_ref, o_ref,
                hf_ref, gate_ref, rank_ref, *, ts, caps, n_exp):
    e = pl.program_id(1)
    lane = lax.broadcasted_iota(jnp.int32, (1, LANES), 1)
    n_sub = hf_ref.shape[0] // ts

    @pl.when(e == 0)
    def _():
        x = x_ref[...]
        hf = _rms(x, g_ref[...])
        hi = hf.astype(BF16)
        lo = (hf - hi.astype(F32)).astype(BF16)
        logits = (_dot_nt(wrh_ref[...], hi) + _dot_nt(wrh_ref[...], lo) + _dot_nt(wrl_ref[...], hi)
                  + br_ref[...])
        sub = lax.broadcasted_iota(jnp.int32, (ROUTER_ROWS, 1), 0)
        m1 = jnp.max(logits, axis=0, keepdims=True)
        i1 = jnp.min(jnp.where(logits == m1, sub, ROUTER_ROWS), axis=0, keepdims=True)
        rest = jnp.where(sub == i1, NEG, logits)
        m2 = jnp.max(rest, axis=0, keepdims=True)
        i2 = jnp.min(jnp.where(rest == m2, sub, ROUTER_ROWS), axis=0, keepdims=True)
        e2 = jnp.exp(m2 - m1)
        den = 1.0 + e2
        gate_t = jnp.where(sub == i1, 1.0 / den, 0.0) + jnp.where(sub == i2, e2 / den, 0.0)
        tm = gate_t.shape[1]
        gate_ref[...] = jnp.concatenate(
            [gate_t, jnp.zeros((LANES - ROUTER_ROWS, tm), F32)], axis=0).T
        hf_ref[...] = hi
        o_ref[...] = x
        sel_t = ((sub == i1) | (sub == i2))[:n_exp, :]
        before = (lax.broadcasted_iota(jnp.int32, (ts, ts), 0)
                  < lax.broadcasted_iota(jnp.int32, (ts, ts), 1))
        tri = jnp.where(before, 1.0, 0.0).astype(BF16)
        for s in range(n_sub):
            sel_s = sel_t[:, s * ts:(s + 1) * ts]
            rank = _dot(jnp.where(sel_s, 1.0, 0.0).astype(BF16), tri)
            rank_ref[s] = jnp.where(sel_s, rank, -1.0)

    for s in range(n_sub):
        rows = slice(s * ts, (s + 1) * ts)
        rank_e = rank_ref[s, pl.ds(e, 1), :]
        n_tok = jnp.max(rank_e).astype(jnp.int32) + 1
        ge = jnp.sum(jnp.where(lane == e, gate_ref[rows, :], 0.0), axis=-1, keepdims=True)

        def chunk(base, size):
            want = (lax.broadcasted_iota(jnp.int32, (size, 1), 0) + base).astype(F32)
            onehot = jnp.where(rank_e == want, 1.0, 0.0).astype(BF16)
            xg = _dot(onehot, hf_ref[rows, :]).astype(BF16)
            y = _dot(_swiglu_mid(_dot(xg, w1_ref[0]), _dot(xg, w3_ref[0])), w2_ref[0])
            back = lax.dot_general(onehot, y.astype(BF16), (((0,), (0,)), ((), ())),
                                   preferred_element_type=F32)
            o_ref[rows, :] += ge * back

        n_big = n_tok // caps[-1]
        lax.fori_loop(0, n_big, lambda c, carry: (chunk(c * caps[-1], caps[-1]), carry)[1], 0)
        rest = n_tok - n_big * caps[-1]
        bucket = sum((rest > c).astype(jnp.int32) for c in (0,) + caps[:-1])
        lax.switch(bucket, [lambda: None]
                   + [functools.partial(chunk, n_big * caps[-1], c) for c in caps])


def _moe(x, g, wr_hi, wr_lo, br, w1, w3, w2):
    N, D = x.shape
    E, _, F = w1.shape
    tm = _token_tile(N, 1024)
    ts = _token_tile(tm, 512)
    assert E <= SUBLANES
    tok = pl.BlockSpec((tm, D), lambda i, e: (i, 0))
    kern = functools.partial(_moe_kernel, ts=ts, caps=MOE_CAPS, n_exp=E)
    return pl.pallas_call(
        kern,
        grid=(N // tm, E),
        in_specs=[tok, _resident((1, D)), _resident((ROUTER_ROWS, D)), _resident((ROUTER_ROWS, D)),
                  _resident((ROUTER_ROWS, 1)),
                  pl.BlockSpec((1, D, F), lambda i, e: (e, 0, 0)),
                  pl.BlockSpec((1, D, F), lambda i, e: (e, 0, 0)),
                  pl.BlockSpec((1, F, D), lambda i, e: (e, 0, 0))],
        out_specs=tok,
        out_shape=jax.ShapeDtypeStruct((N, D), F32),
        scratch_shapes=[pltpu.VMEM((tm, D), BF16), pltpu.VMEM((tm, LANES), F32),
                        pltpu.VMEM((tm // ts, E, ts), F32)],
        compiler_params=_params(2),
        name="moe",
    )(x, g, wr_hi, wr_lo, br, w1, w3, w2)


def _rot_half_cols(w):
    half = ROPE_DIM // 2
    return jnp.concatenate([-w[..., half:], w[..., :half]], axis=-1)


def _rope_lanes(a):
    return jnp.pad(a, [(0, 0)] * (a.ndim - 1) + [(NOPE_DIM, HEAD_PAD - QK_DIM)])


def _pad_cols(w, n):
    return jnp.pad(w, [(0, 0)] * (w.ndim - 1) + [(0, n - w.shape[-1])])


def _head_pad(nope, rope):
    z = jnp.zeros(nope.shape[:-1] + (HEAD_PAD - QK_DIM,), nope.dtype)
    out = jnp.concatenate([nope, rope, z], axis=-1)
    return out.reshape(out.shape[:-2] + (N_HEADS * HEAD_PAD,))


def _block_diag_tiles(w):
    nblk, bw, _ = w.shape
    per = MXU_DIM // bw
    w4 = w.reshape(nblk // per, per, bw, bw)
    eye = jnp.eye(per, dtype=w.dtype)
    return jnp.einsum("cpij,pq->cpiqj", w4, eye).reshape(nblk // per, MXU_DIM, MXU_DIM)


def _layer_weights(l, p):
    D = p["w_in"].shape[1]
    q_lora = p["q_lat_norm"].shape[-1]
    kv_lora = p["kv_lat_norm"].shape[-1]
    w_in = p["w_in"][l]
    sp = (D, 2 * D, 2 * D + q_lora, 2 * D + q_lora + kv_lora, 2 * D + q_lora + kv_lora + ROPE_DIM,
          3 * D + q_lora + kv_lora + ROPE_DIM)
    x_rec, x_gate, c_q, c_kv, k_rot, g_rec, g_att = jnp.split(w_in, sp, axis=-1)
    w_in_perm = jnp.concatenate(
        [x_rec, x_gate, g_rec, g_att, c_q, c_kv, _rope_lanes(k_rot),
         _rope_lanes(_rot_half_cols(k_rot))], axis=-1).astype(BF16)

    wq = p["w_uq"][l].reshape(q_lora, N_HEADS, QK_DIM)
    wq_n, wq_r = wq[..., :NOPE_DIM], wq[..., NOPE_DIM:]
    wk = p["w_uk"][l].reshape(kv_lora, N_HEADS, NOPE_DIM)
    gq, gk = p["q_head_norm"][l], p["k_head_norm"][l]
    q_scale = (QK_DIM ** -0.5) * LOG2E
    return dict(
        gmix=p["norm_mix"][l].reshape(1, D), w_in=w_in_perm,
        conv_w=p["conv_w"][l], conv_b=p["conv_b"][l].reshape(1, D),
        wa=_block_diag_tiles(p["lru_wa"][l]).astype(BF16), ba=p["lru_ba"][l].reshape(1, D),
        wi=_block_diag_tiles(p["lru_wi"][l]).astype(BF16), bi=p["lru_bi"][l].reshape(1, D),
        sp=jax.nn.softplus(-p["lru_lambda"][l]).reshape(1, D),
        gq=p["q_lat_norm"][l].reshape(1, q_lora), gkv=p["kv_lat_norm"][l].reshape(1, kv_lora),
        wq_t=_head_pad(wq_n, wq_r).T.astype(BF16),
        gqh=(_pad_cols(gq, HEAD_PAD) * q_scale).reshape(HEAD_PAD, 1),
        wk=_head_pad(wk, jnp.zeros((kv_lora, N_HEADS, ROPE_DIM), F32)).astype(BF16),
        wv_t=p["w_uv"][l].T.astype(BF16),
        gkh=_pad_cols(gk, HEAD_PAD).reshape(1, HEAD_PAD),
        score_bound=QK_DIM * jnp.max(jnp.abs(gq)) * jnp.max(jnp.abs(gk)) * q_scale,
        w_out=p["w_out"][l].astype(BF16), gmem=p["norm_mem"][l].reshape(1, D),
        w_mq=p["w_mq"][l].astype(BF16), gmqh=p["mq_head_norm"][l].reshape(1, -1),
        w_mo=p["w_mo"][l].astype(BF16), gffn=p["norm_ffn"][l].reshape(1, D),
    )


def _rope_tables(pos):
    half = ROPE_DIM // 2
    inv_freq = ROPE_BASE ** (-jnp.arange(half, dtype=F32) / half)
    ang = pos.astype(F32)[:, None] * inv_freq[None, :]
    cos2 = jnp.concatenate([jnp.cos(ang), jnp.cos(ang)], axis=-1)
    sin2 = jnp.concatenate([jnp.sin(ang), jnp.sin(ang)], axis=-1)
    return dict(cos_k=_rope_lanes(cos2), sin_k=_rope_lanes(sin2), cos_q=cos2.T, sin_q=sin2.T)


def _layer(l, depth, x, tabs, q_off, conv_prev, h0, past, ckv_stack, mem_k, mem_v, w, p):
    B, L, D = x.shape
    prev8 = jnp.pad(conv_prev, ((0, 0), (SUBLANES - (CONV_W - 1), 0), (0, 0)))
    h0_8 = jnp.pad(h0[:, None, :], ((0, 0), (SUBLANES - 1, 0), (0, 0)))
    grec, gatt, cq, ckv, kr, ctail, htail = _inproj(x, prev8, h0_8, w, tabs["cos_k"], tabs["sin_k"],
                                                    l, depth, ckv_stack)

    q = _qprep(cq, w, tabs["cos_q"], tabs["sin_q"])
    if past is None:
        t_valid = L
        tk = min(512, L)
    else:
        t_valid = past[0].shape[2] + L
        tk = 512
    k, vt = _kvprep(ckv, l, kr, w, tk, past)
    flash = functools.partial(_flash, q_off=q_off, t_valid=t_valid, tk=tk)
    att = lax.cond(w["score_bound"] <= SAFE_LOG2_RANGE, functools.partial(flash, bounded=True),
                   functools.partial(flash, bounded=False), q, k, vt)

    x2 = _post(x, grec, gatt, att, mem_k, mem_v, w).reshape(B * L, D)
    j = l // 2
    if l % 2 == 0:
        x3 = _ffn(x2, w["gffn"], p["ffn_w1"][j].astype(BF16), p["ffn_w3"][j].astype(BF16),
                  p["ffn_w2"][j].astype(BF16))
    else:
        E = p["moe_router"].shape[-1]
        wr = _pad_cols(p["moe_router"][j], ROUTER_ROWS).T
        wr_hi = wr.astype(BF16)
        wr_lo = (wr - wr_hi.astype(F32)).astype(BF16)
        br = jnp.concatenate([p["moe_router_b"][j],
                              jnp.full((ROUTER_ROWS - E,), NEG, F32)]).reshape(ROUTER_ROWS, 1)
        x3 = _moe(x2, w["gffn"], wr_hi, wr_lo, br, p["moe_w1"][j].astype(BF16),
                  p["moe_w3"][j].astype(BF16), p["moe_w2"][j].astype(BF16))
    conv_state = ctail[:, SUBLANES - (CONV_W - 1):, :]
    return x3.reshape(B, L, D), conv_state, htail[:, SUBLANES - 1, :], ckv, kr[..., NOPE_DIM:QK_DIM]


def kernel(x_prompt, x_sample, cache_ckv, cache_krope, cache_mem_k, cache_mem_v, state_lru, state_conv, mem_prompt, norm_mix, w_in, conv_w, conv_b, lru_wa, lru_ba, lru_wi, lru_bi, lru_lambda, q_lat_norm, w_uq, kv_lat_norm, w_uk, w_uv, q_head_norm, k_head_norm, w_out, norm_mem, mem_in_norm, w_mq, w_mk, w_mv, w_mo, mq_head_norm, mk_head_norm, norm_ffn, ffn_w1, ffn_w3, ffn_w2, moe_router, moe_router_b, moe_w1, moe_w3, moe_w2):
    p = dict(norm_mix=norm_mix, w_in=w_in, conv_w=conv_w, conv_b=conv_b, lru_wa=lru_wa, lru_ba=lru_ba,
             lru_wi=lru_wi, lru_bi=lru_bi, lru_lambda=lru_lambda, q_lat_norm=q_lat_norm, w_uq=w_uq,
             kv_lat_norm=kv_lat_norm, w_uk=w_uk, w_uv=w_uv, q_head_norm=q_head_norm,
             k_head_norm=k_head_norm, w_out=w_out, norm_mem=norm_mem, w_mq=w_mq, w_mo=w_mo,
             mq_head_norm=mq_head_norm, norm_ffn=norm_ffn, ffn_w1=ffn_w1, ffn_w3=ffn_w3,
             ffn_w2=ffn_w2, moe_router=moe_router, moe_router_b=moe_router_b, moe_w1=moe_w1,
             moe_w3=moe_w3, moe_w2=moe_w2)
    depth = w_in.shape[0]
    weights = [_layer_weights(l, p) for l in range(depth)]

    Bp, Lp, D = x_prompt.shape
    Bs, Ls, _ = x_sample.shape
    t_past = cache_ckv.shape[2]
    M = mem_prompt.shape[1]

    tabs_p = _rope_tables(jnp.arange(Lp, dtype=jnp.int32))
    zero_conv = jnp.zeros((Bp, CONV_W - 1, D), F32)
    zero_h = jnp.zeros((Bp, D), F32)
    x = x_prompt
    ckv_p = None
    outs_p = [[] for _ in range(5)]
    for l in range(depth):
        mk, mv, mk_b, mv_b = _memkv(mem_prompt, mem_in_norm[l], w_mk[l], w_mv[l], mk_head_norm[l])
        x, cs, hl, ckv_p, kr = _layer(l, depth, x, tabs_p, 0, zero_conv, zero_h, None, ckv_p, mk_b, mv_b,
                                      weights[l], p)
        for acc, val in zip(outs_p, (kr, hl, cs, mk, mv)):
            acc.append(val)
    y_prompt = x

    tabs_s = _rope_tables(t_past + jnp.arange(Ls, dtype=jnp.int32))
    x = x_sample
    ckv_s = None
    outs_s = [[] for _ in range(3)]
    cache_kr = _rope_lanes(cache_krope)
    for l in range(depth):
        x, cs, hl, ckv_s, kr = _layer(l, depth, x, tabs_s, t_past, state_conv[l], state_lru[l],
                                      (cache_ckv, cache_kr, l), ckv_s,
                                      cache_mem_k[l].reshape(Bs, M, D).astype(BF16),
                                      cache_mem_v[l].reshape(Bs, M, D).astype(BF16), weights[l], p)
        for acc, val in zip(outs_s, (kr, hl, cs)):
            acc.append(val)
    y_sample = x

    return ((y_prompt, y_sample, ckv_p) + tuple(jnp.stack(a) for a in outs_p)
            + (ckv_s,) + tuple(jnp.stack(a) for a in outs_s))
```

```python
import functools
import math

import jax
import jax.numpy as jnp
from jax import lax
from jax.experimental import pallas as pl
from jax.experimental.pallas import tpu as pltpu

F32 = jnp.float32
BF16 = jnp.bfloat16

EPS = 1e-6
CHUNK = 64
LRU_C = 8.0
CONV_W = 4
N_HEADS = 16
NOPE_DIM = 64
ROPE_DIM = 32
QK_DIM = NOPE_DIM + ROPE_DIM
ROPE_BASE = 10000.0
MEM_HEADS = 4

LANES = 128
SUBLANES = 8
MXU_DIM = 256
VMEM_LIMIT = 56 * 1024 * 1024

HEAD_PAD = LANES
ROUTER_ROWS = 16
N_INPROJ_INPUTS = 16
MOE_CAPS = (112, 144, 176, 208)
SAFE_LOG2_RANGE = 60.0
FLASH_HEADS = 8
FLASH_LAG = 8
NEG = -1e30
LOG2E = 1.4426950408889634


def _resident(shape):
    nd = len(shape)
    return pl.BlockSpec(shape, lambda *_: (0,) * nd, pipeline_mode=pl.Buffered(1))


def _params(n_axes):
    return pltpu.CompilerParams(dimension_semantics=("arbitrary",) * n_axes,
                                vmem_limit_bytes=VMEM_LIMIT)


def _rms(x, gain):
    return x * lax.rsqrt(jnp.mean(x * x, axis=-1, keepdims=True) + EPS) * gain


def _dot(a, b):
    return jnp.dot(a, b, preferred_element_type=F32)


def _dot_nt(a, b):
    return lax.dot_general(a, b, (((1,), (1,)), ((), ())), preferred_element_type=F32)


def _token_tile(n, cap=512):
    t = min(n, cap)
    assert n % t == 0 and t % SUBLANES == 0
    return t


def _memkv_kernel(mem_ref, gin_ref, wk_ref, wv_ref, gk_ref, k_ref, v_ref, kb_ref, vb_ref):
    mn = _rms(mem_ref[0], gin_ref[...]).astype(BF16)
    k = _dot(mn, wk_ref[...])
    v = _dot(mn, wv_ref[...])
    hd = gk_ref.shape[-1]
    for h in range(MEM_HEADS):
        sl = slice(h * hd, (h + 1) * hd)
        kh = _rms(k[:, sl], gk_ref[...])
        k_ref[0, :, h, :] = kh
        kb_ref[0, :, sl] = kh.astype(BF16)
        v_ref[0, :, h, :] = v[:, sl]
    vb_ref[0] = v.astype(BF16)


def _memkv(mem, g_in, w_k, w_v, g_k):
    B, M, D = mem.shape
    tok = pl.BlockSpec((1, M, D), lambda b: (b, 0, 0))
    hd = D // MEM_HEADS
    tok4 = pl.BlockSpec((1, M, MEM_HEADS, hd), lambda b: (b, 0, 0, 0))
    return pl.pallas_call(
        _memkv_kernel,
        grid=(B,),
        in_specs=[tok, _resident((1, D)), _resident((D, D)), _resident((D, D)),
                  _resident((1, hd))],
        out_specs=[tok4, tok4, tok, tok],
        out_shape=[jax.ShapeDtypeStruct((B, M, MEM_HEADS, hd), F32)] * 2
        + [jax.ShapeDtypeStruct((B, M, D), BF16)] * 2,
        compiler_params=_params(1),
        name="memkv",
    )(mem, g_in.reshape(1, D), w_k.astype(BF16), w_v.astype(BF16), g_k.reshape(1, -1))


def _inproj_kernel(x_ref, prev_ref, h0_ref, gmix_ref, win_ref, cw_ref, cb_ref, wa_ref, ba_ref,
                   wi_ref, bi_ref, sp_ref, gq_ref, gkv_ref, cos_ref, sin_ref, *rest,
                   tm, d, q_lora, kv_lora):
    (grec_ref, gatt_ref, cq_ref, ckv_ref, kr_ref, ctail_ref, htail_ref,
     xext_ref, hcar_ref) = rest[-9:]
    @pl.when(pl.program_id(1) == 0)
    def _():
        xext_ref[0:SUBLANES, :] = prev_ref[0]
        hcar_ref[...] = h0_ref[0]

    hn = _rms(x_ref[0], gmix_ref[...]).astype(BF16)

    def proj(c0, c1):
        return _dot(hn, win_ref[:, c0:c1])

    x_rec = proj(0, d)
    xext_ref[SUBLANES:SUBLANES + tm, :] = x_rec
    xc = cb_ref[...] + cw_ref[CONV_W - 1:CONV_W, :] * x_rec
    for j in range(CONV_W - 1):
        off = SUBLANES - (CONV_W - 1) + j
        xc = xc + cw_ref[j:j + 1, :] * xext_ref[off:off + tm, :]
    tail = x_rec[tm - SUBLANES:tm, :]
    ctail_ref[0] = tail
    xext_ref[0:SUBLANES, :] = tail

    xcb = xc.astype(BF16)
    nb = d // MXU_DIM
    gr = jnp.concatenate(
        [_dot(xcb[:, c * MXU_DIM:(c + 1) * MXU_DIM], wa_ref[c]) for c in range(nb)], axis=1)
    gi = jnp.concatenate(
        [_dot(xcb[:, c * MXU_DIM:(c + 1) * MXU_DIM], wi_ref[c]) for c in range(nb)], axis=1)
    x_gate = proj(d, 2 * d)
    g_rec = proj(2 * d, 3 * d)

    row = lax.broadcasted_iota(jnp.int32, (tm, 1), 0) & (SUBLANES - 1)
    half = d // 2
    for c0 in (0, half):
        cols = slice(c0, c0 + half)
        r = jax.nn.sigmoid(gr[:, cols] + ba_ref[:, cols])
        ig = jax.nn.sigmoid(gi[:, cols] + bi_ref[:, cols])
        a = jnp.exp((-LRU_C) * r * sp_ref[:, cols])
        u = jnp.sqrt(1.0 - a * a) * (ig * xc[:, cols])

        s = 1
        while s < SUBLANES:
            keep = row >= s
            u = a * jnp.where(keep, pltpu.roll(u, s, 0), 0.0) + u
            a = a * jnp.where(keep, pltpu.roll(a, s, 0), 1.0)
            s *= 2
        h_prev = hcar_ref[SUBLANES - 1:SUBLANES, cols]
        groups = []
        for g in range(tm // SUBLANES):
            rows = slice(g * SUBLANES, (g + 1) * SUBLANES)
            hg = a[rows] * h_prev + u[rows]
            groups.append(hg)
            h_prev = hg[SUBLANES - 1:SUBLANES, :]
        h = jnp.concatenate(groups, axis=0)
        hcar_ref[:, cols] = groups[-1]
        htail_ref[0, :, cols] = groups[-1]
        grec_ref[0, :, cols] = jax.nn.sigmoid(g_rec[:, cols]) * (jax.nn.gelu(x_gate[:, cols]) * h)
    gatt_ref[0] = proj(3 * d, 4 * d)

    o = 4 * d
    cq_ref[0] = _rms(proj(o, o + q_lora), gq_ref[...]).astype(BF16)
    o += q_lora
    ckv_ref[0, 0] = _rms(proj(o, o + kv_lora), gkv_ref[...])
    o += kv_lora
    kr = proj(o, o + LANES) * cos_ref[...] + proj(o + LANES, o + 2 * LANES) * sin_ref[...]
    kr_ref[0] = kr


def _inproj(x, prev8, h0_8, w, cos_k, sin_k, layer, depth, ckv_stack):
    B, L, D = x.shape
    tm = _token_tile(L, 256)
    q_lora, kv_lora = w["gq"].shape[-1], w["gkv"].shape[-1]
    ncol = w["w_in"].shape[-1]
    tok = lambda n: pl.BlockSpec((1, tm, n), lambda b, l: (b, l, 0))
    per_b = pl.BlockSpec((1, SUBLANES, D), lambda b, l: (b, 0, 0))
    tab = pl.BlockSpec((tm, LANES), lambda b, l: (l, 0))
    nb = D // MXU_DIM
    kern = functools.partial(_inproj_kernel, tm=tm, d=D, q_lora=q_lora, kv_lora=kv_lora)
    alias_args = () if ckv_stack is None else (ckv_stack,)
    alias_spec = [pl.BlockSpec(memory_space=pl.ANY)] * len(alias_args)
    return pl.pallas_call(
        kern,
        grid=(B, L // tm),
        in_specs=[tok(D), per_b, per_b, _resident((1, D)), _resident((D, ncol)),
                  _resident((CONV_W, D)), _resident((1, D)),
                  _resident((nb, MXU_DIM, MXU_DIM)), _resident((1, D)),
                  _resident((nb, MXU_DIM, MXU_DIM)), _resident((1, D)), _resident((1, D)),
                  _resident((1, q_lora)), _resident((1, kv_lora)), tab, tab] + alias_spec,
        out_specs=[tok(D), tok(D), tok(q_lora),
                   pl.BlockSpec((1, 1, tm, kv_lora), lambda b, l: (layer, b, l, 0)),
                   tok(HEAD_PAD), per_b, per_b],
        out_shape=[jax.ShapeDtypeStruct((B, L, D), F32), jax.ShapeDtypeStruct((B, L, D), F32),
                   jax.ShapeDtypeStruct((B, L, q_lora), BF16),
                   jax.ShapeDtypeStruct((depth, B, L, kv_lora), F32),
                   jax.ShapeDtypeStruct((B, L, HEAD_PAD), F32),
                   jax.ShapeDtypeStruct((B, SUBLANES, D), F32),
                   jax.ShapeDtypeStruct((B, SUBLANES, D), F32)],
        scratch_shapes=[pltpu.VMEM((SUBLANES + tm, D), F32), pltpu.VMEM((SUBLANES, D), F32)],
        input_output_aliases={N_INPROJ_INPUTS: 3} if alias_args else {},
        compiler_params=_params(2),
        name="inproj",
    )(x, prev8, h0_8, w["gmix"], w["w_in"], w["conv_w"], w["conv_b"], w["wa"], w["ba"],
      w["wi"], w["bi"], w["sp"], w["gq"], w["gkv"], cos_k, sin_k, *alias_args)


def _qprep_kernel(cq_ref, wt_ref, cos_ref, sin_ref, g_ref, q_ref):
    qt = _dot_nt(wt_ref[...], cq_ref[0])
    half = ROPE_DIM // 2
    tm = qt.shape[1]
    cos, sin = cos_ref[...], sin_ref[...]
    g_nope, g_rope = g_ref[:NOPE_DIM, :], g_ref[NOPE_DIM:QK_DIM, :]
    pad = jnp.zeros((HEAD_PAD - QK_DIM, tm), BF16)
    for h in range(N_HEADS):
        r0 = h * HEAD_PAD
        nope = qt[r0:r0 + NOPE_DIM, :]
        x1 = qt[r0 + NOPE_DIM:r0 + NOPE_DIM + half, :]
        x2 = qt[r0 + NOPE_DIM + half:r0 + QK_DIM, :]
        rope = (jnp.concatenate([x1, x2], axis=0) * cos + jnp.concatenate([-x2, x1], axis=0) * sin)
        ss = (jnp.sum(nope * nope, axis=0, keepdims=True)
              + jnp.sum(rope * rope, axis=0, keepdims=True)) * (1.0 / QK_DIM)
        inv = lax.rsqrt(ss + EPS)
        q_ref[0, r0:r0 + NOPE_DIM, :] = (nope * inv * g_nope).astype(BF16)
        q_ref[0, r0 + NOPE_DIM:r0 + QK_DIM, :] = (rope * inv * g_rope).astype(BF16)
        q_ref[0, r0 + QK_DIM:r0 + HEAD_PAD, :] = pad


def _qprep(cq, w, cos_q, sin_q):
    B, L, QL = cq.shape
    tm = _token_tile(L)
    n = N_HEADS * HEAD_PAD
    tab = pl.BlockSpec((ROPE_DIM, tm), lambda b, l: (0, l))
    return pl.pallas_call(
        _qprep_kernel,
        grid=(B, L // tm),
        in_specs=[pl.BlockSpec((1, tm, QL), lambda b, l: (b, l, 0)),
                  _resident((n, QL)), tab, tab, _resident((HEAD_PAD, 1))],
        out_specs=pl.BlockSpec((1, n, tm), lambda b, l: (b, 0, l)),
        out_shape=jax.ShapeDtypeStruct((B, n, L), BF16),
        compiler_params=_params(2),
        name="qprep",
    )(cq, w["wq_t"], cos_q, sin_q, w["gqh"])


def _kvprep_kernel(*refs, n_past, tk):
    if n_past:
        past_c_ref, past_kr_ref, new_c_ref, new_kr_ref, wk_ref, wvt_ref, g_ref, k_ref, vt_ref = refs
        is_new = pl.program_id(1) >= n_past

        def pick(new_ref, past_ref):
            new = new_ref[...].reshape(new_ref.shape[-2:])
            fill = jnp.zeros((tk - new.shape[0], new.shape[1]), new.dtype)
            return jnp.where(is_new, jnp.concatenate([new, fill], axis=0), past_ref[0, 0])

        c32, kr = pick(new_c_ref, past_c_ref), pick(new_kr_ref, past_kr_ref)
    else:
        new_c_ref, new_kr_ref, wk_ref, wvt_ref, g_ref, k_ref, vt_ref = refs
        c32, kr = new_c_ref[0, 0], new_kr_ref[0]
    c = c32.astype(BF16)
    k = _dot(c, wk_ref[...])
    for h in range(N_HEADS):
        sl = slice(h * HEAD_PAD, (h + 1) * HEAD_PAD)
        kh = k[:, sl] + kr
        ss = jnp.sum(kh * kh, axis=-1, keepdims=True) * (1.0 / QK_DIM)
        k_ref[0, :, sl] = (kh * lax.rsqrt(ss + EPS) * g_ref[...]).astype(BF16)
    vt_ref[0, 0] = _dot_nt(wvt_ref[...], c).astype(BF16)


def _kvprep(ckv, layer, kr, w, tk, past=None):
    _, B, L, KV = ckv.shape
    n = N_HEADS * HEAD_PAD
    dv = w["wv_t"].shape[0]
    if past is None:
        assert L % tk == 0
        n_past, n_tiles = 0, L // tk
        in_specs = [pl.BlockSpec((1, 1, tk, KV), lambda b, t: (layer, b, t, 0)),
                    pl.BlockSpec((1, tk, HEAD_PAD), lambda b, t: (b, t, 0))]
        args = (ckv, kr)
    else:
        cache_c, cache_kr, layer = past
        t_past = cache_c.shape[2]
        assert t_past % tk == 0 and L <= tk and L % SUBLANES == 0
        n_past = t_past // tk
        n_tiles = n_past + 1
        old = lambda width: pl.BlockSpec(
            (1, 1, tk, width), lambda b, t: (layer, b, jnp.minimum(t, n_past - 1), 0))
        in_specs = [old(KV), old(HEAD_PAD),
                    pl.BlockSpec((1, 1, L, KV), lambda b, t: (layer, b, 0, 0)),
                    pl.BlockSpec((1, L, HEAD_PAD), lambda b, t: (b, 0, 0))]
        args = (cache_c, cache_kr, ckv, kr)
    return pl.pallas_call(
        functools.partial(_kvprep_kernel, n_past=n_past, tk=tk),
        grid=(B, n_tiles),
        in_specs=in_specs + [_resident((KV, n)), _resident((dv, KV)), _resident((1, HEAD_PAD))],
        out_specs=[pl.BlockSpec((1, tk, n), lambda b, t: (b, t, 0)),
                   pl.BlockSpec((1, 1, dv, tk), lambda b, t: (b, t, 0, 0))],
        out_shape=[jax.ShapeDtypeStruct((B, n_tiles * tk, n), BF16),
                   jax.ShapeDtypeStruct((B, n_tiles, dv, tk), BF16)],
        compiler_params=_params(2),
        name="kvprep",
    )(*args, w["wk"], w["wv_t"], w["gkh"])


def _flash_kernel(q_ref, k_ref, vt_ref, o_ref, m_ref, l_ref, acc_ref, *, tq, tk, q_off, t_valid,
                  v_dim, heads, bounded, lag, split_diag):
    q_start = q_off + pl.program_id(2) * tq
    n_full = jnp.minimum(q_start, t_valid) // tk
    n_end = (jnp.minimum(q_start + tq, t_valid) + tk - 1) // tk
    shift = int(math.log2(CHUNK))

    def chunk_of(pos):
        return lax.shift_right_logical(pos, shift)

    m_ref[...] = jnp.full(m_ref.shape, NEG, F32)
    l_ref[...] = jnp.zeros(l_ref.shape, F32)
    acc_ref[...] = jnp.zeros(acc_ref.shape, F32)

    def accumulate(hh, s, vj, q0):
        lanes = slice(q0, tq)
        if bounded:
            p = jnp.exp2(s)
            l_ref[hh, :, lanes] += jnp.sum(p, axis=0, keepdims=True)
        else:
            m = m_ref[hh, :, lanes]
            m_new = jnp.maximum(m, jnp.max(s, axis=0, keepdims=True))
            alpha = jnp.exp2(m - m_new)
            p = jnp.exp2(s - m_new)
            l_ref[hh, :, lanes] = alpha * l_ref[hh, :, lanes] + jnp.sum(p, axis=0, keepdims=True)
            m_ref[hh, :, lanes] = m_new
            acc_ref[hh, :, lanes] = alpha * acc_ref[hh, :, lanes]
        acc_ref[hh, :, lanes] += _dot(vj, p.astype(BF16))

    def tile(j, masked):
        if masked:
            k_pos = j * tk + lax.broadcasted_iota(jnp.int32, (tk, 1), 0)
            q_pos = q_start + lax.broadcasted_iota(jnp.int32, (1, tq), 1)
            vis = (chunk_of(k_pos) <= chunk_of(q_pos)) & (k_pos < t_valid)
        pending = {}
        for t in range(heads + lag):
            if t < heads:
                qh = q_ref[0, t * HEAD_PAD:(t + 1) * HEAD_PAD, :]
                kj = k_ref[0, pl.ds(pl.multiple_of(j * tk, tk), tk), t * HEAD_PAD:(t + 1) * HEAD_PAD]
                pending[t] = _dot(kj, qh)
            if t >= lag:
                hh = t - lag
                s = pending.pop(hh)
                if masked:
                    s = jnp.where(vis, s, NEG)
                accumulate(hh, s, vt_ref[0, j, hh * v_dim:(hh + 1) * v_dim, :], 0)

    def diagonal(j):
        half = tk // 2
        vis = (chunk_of(lax.broadcasted_iota(jnp.int32, (half, 1), 0))
               <= chunk_of(lax.broadcasted_iota(jnp.int32, (1, tq), 1)))
        items = [(k0, hh) for k0 in (0, half) for hh in range(heads)]
        pending = {}
        for t in range(len(items) + lag):
            if t < len(items):
                k0, hh = items[t]
                qh = q_ref[0, hh * HEAD_PAD:(hh + 1) * HEAD_PAD, k0:]
                kj = k_ref[0, pl.ds(pl.multiple_of(j * tk + k0, half), half),
                           hh * HEAD_PAD:(hh + 1) * HEAD_PAD]
                pending[t] = _dot(kj, qh)
            if t >= lag:
                k0, hh = items[t - lag]
                s = jnp.where(vis[:, :tq - k0], pending.pop(t - lag), NEG)
                accumulate(hh, s, vt_ref[0, j, hh * v_dim:(hh + 1) * v_dim, k0:k0 + half], k0)

    def loop(lo, hi, body):
        lax.fori_loop(lo, hi, lambda j, c: (body(j), c)[1], 0)

    loop(0, n_full, lambda j: tile(j, False))
    if split_diag:
        diagonal(n_full)
    else:
        loop(n_full, n_end, lambda j: tile(j, True))
    o_ref[0] = jnp.concatenate([acc_ref[hh] / l_ref[hh] for hh in range(heads)], axis=0).T


def _flash(q, k, vt, *, q_off, t_valid, tk, bounded, heads=FLASH_HEADS):
    B, n, L = q.shape
    T = k.shape[1]
    dv = vt.shape[2]
    v_dim = dv // N_HEADS
    tq = _token_tile(L)
    assert tq % CHUNK == 0 and q_off % CHUNK == 0 and (heads * v_dim) % LANES == 0
    split_diag = tq == tk and q_off == 0 and t_valid == T and (tk // 2) % max(CHUNK, LANES) == 0
    kern = functools.partial(_flash_kernel, tq=tq, tk=tk, q_off=q_off, t_valid=t_valid, v_dim=v_dim,
                             heads=heads, bounded=bounded, lag=min(FLASH_LAG, heads),
                             split_diag=split_diag)
    return pl.pallas_call(
        kern,
        grid=(B, N_HEADS // heads, L // tq),
        in_specs=[pl.BlockSpec((1, heads * HEAD_PAD, tq), lambda b, p, i: (b, p, i)),
                  pl.BlockSpec((1, T, heads * HEAD_PAD), lambda b, p, i: (b, 0, p)),
                  pl.BlockSpec((1, T // tk, heads * v_dim, tk), lambda b, p, i: (b, 0, p, 0))],
        out_specs=pl.BlockSpec((1, tq, heads * v_dim), lambda b, p, i: (b, i, p)),
        out_shape=jax.ShapeDtypeStruct((B, L, dv), F32),
        scratch_shapes=[pltpu.VMEM((heads, 1, tq), F32), pltpu.VMEM((heads, 1, tq), F32),
                        pltpu.VMEM((heads, v_dim, tq), F32)],
        compiler_params=_params(3),
        name="flash",
    )(q, k, vt)


def _post_kernel(x_ref, grec_ref, gatt_ref, att_ref, wo_ref, gmem_ref, wmq_ref, gqh_ref,
                 mk_ref, mv_ref, wmo_ref, o_ref):
    mixed = (grec_ref[0] + jax.nn.sigmoid(gatt_ref[0]) * att_ref[0]).astype(BF16)
    x1 = x_ref[0] + _dot(mixed, wo_ref[...])
    qm = _dot(_rms(x1, gmem_ref[...]).astype(BF16), wmq_ref[...])
    hd = gqh_ref.shape[-1]
    heads = [slice(h * hd, (h + 1) * hd) for h in range(MEM_HEADS)]
    scores = [_dot_nt((_rms(qm[:, sl], gqh_ref[...]) * (hd ** -0.5)).astype(BF16), mk_ref[0, :, sl])
              for sl in heads]
    outs = []
    for sl, s in zip(heads, scores):
        p = jnp.exp(s - jnp.max(s, axis=-1, keepdims=True))
        l = jnp.sum(p, axis=-1, keepdims=True)
        outs.append((_dot(p.astype(BF16), mv_ref[0, :, sl]) / l).astype(BF16))
    o_ref[0] = x1 + _dot(jnp.concatenate(outs, axis=1), wmo_ref[...])


def _post(x, grec, gatt, att, mem_k, mem_v, w):
    B, L, D = x.shape
    M = mem_k.shape[1]
    tm = _token_tile(L)
    tok = pl.BlockSpec((1, tm, D), lambda b, l: (b, l, 0))
    mem = pl.BlockSpec((1, M, D), lambda b, l: (b, 0, 0))
    return pl.pallas_call(
        _post_kernel,
        grid=(B, L // tm),
        in_specs=[tok, tok, tok, tok, _resident((D, D)), _resident((1, D)), _resident((D, D)),
                  _resident((1, D // MEM_HEADS)), mem, mem, _resident((D, D))],
        out_specs=tok,
        out_shape=jax.ShapeDtypeStruct((B, L, D), F32),
        compiler_params=_params(2),
        name="post",
    )(x, grec, gatt, att, w["w_out"], w["gmem"], w["w_mq"], w["gmqh"],
      mem_k, mem_v, w["w_mo"])


def _swiglu_mid(a, b):
    return (a * jax.nn.sigmoid(a) * b).astype(BF16)


def _ffn_kernel(x_ref, g_ref, w1_ref, w3_ref, w2_ref, o_ref, *, n_split):
    x = x_ref[...]
    hf = _rms(x, g_ref[...]).astype(BF16)
    fc = w1_ref.shape[1] // n_split
    acc = x
    for c in range(n_split):
        sl = slice(c * fc, (c + 1) * fc)
        acc = acc + _dot(_swiglu_mid(_dot(hf, w1_ref[:, sl]), _dot(hf, w3_ref[:, sl])), w2_ref[sl, :])
    o_ref[...] = acc


def _ffn(x, g, w1, w3, w2):
    N, D = x.shape
    F = w1.shape[1]
    tm = _token_tile(N)
    n_split = 2
    assert F % (n_split * LANES) == 0
    tok = pl.BlockSpec((tm, D), lambda i: (i, 0))
    return pl.pallas_call(
        functools.partial(_ffn_kernel, n_split=n_split),
        grid=(N // tm,),
        in_specs=[tok, _resident((1, D)), _resident((D, F)), _resident((D, F)), _resident((F, D))],
        out_specs=tok,
        out_shape=jax.ShapeDtypeStruct((N, D), F32),
        compiler_params=_params(1),
        name="ffn",
    )(x, g, w1, w3, w2)


def _moe_kernel(x_ref, g_ref, wrh_ref, wrl_ref, br_ref, w1_ref, w3_ref, w2_ref, o_ref,
                hf_ref, gate_ref, rank_ref, *, ts, caps, n_exp):
    e = pl.program_id(1)
    lane = lax.broadcasted_iota(jnp.int32, (1, LANES), 1)
    n_sub = hf_ref.shape[0] // ts

    @pl.when(e == 0)
    def _():
        x = x_ref[...]
        hf = _rms(x, g_ref[...])
        hi = hf.astype(BF16)
        lo = (hf - hi.astype(F32)).astype(BF16)
        logits = (_dot_nt(wrh_ref[...], hi) + _dot_nt(wrh_ref[...], lo) + _dot_nt(wrl_ref[...], hi)
                  + br_ref[...])
        sub = lax.broadcasted_iota(jnp.int32, (ROUTER_ROWS, 1), 0)
        m1 = jnp.max(logits, axis=0, keepdims=True)
        i1 = jnp.min(jnp.where(logits == m1, sub, ROUTER_ROWS), axis=0, keepdims=True)
        rest = jnp.where(sub == i1, NEG, logits)
        m2 = jnp.max(rest, axis=0, keepdims=True)
        i2 = jnp.min(jnp.where(rest == m2, sub, ROUTER_ROWS), axis=0, keepdims=True)
        e2 = jnp.exp(m2 - m1)
        den = 1.0 + e2
        gate_t = jnp.where(sub == i1, 1.0 / den, 0.0) + jnp.where(sub == i2, e2 / den, 0.0)
        tm = gate_t.shape[1]
        gate_ref[...] = jnp.concatenate(
            [gate_t, jnp.zeros((LANES - ROUTER_ROWS, tm), F32)], axis=0).T
        hf_ref[...] = hi
        o_ref[...] = x
        sel_t = ((sub == i1) | (sub == i2))[:n_exp, :]
        before = (lax.broadcasted_iota(jnp.int32, (ts, ts), 0)
                  < lax.broadcasted_iota(jnp.int32, (ts, ts), 1))
        tri = jnp.where(before, 1.0, 0.0).astype(BF16)
        for s in range(n_sub):
            sel_s = sel_t[:, s * ts:(s + 1) * ts]
            rank = _dot(jnp.where(sel_s, 1.0, 0.0).astype(BF16), tri)
            rank_ref[s] = jnp.where(sel_s, rank, -1.0)

    for s in range(n_sub):
        rows = slice(s * ts, (s + 1) * ts)
        rank_e = rank_ref[s, pl.ds(e, 1), :]
        n_tok = jnp.max(rank_e).astype(jnp.int32) + 1
        ge = jnp.sum(jnp.where(lane == e, gate_ref[rows, :], 0.0), axis=-1, keepdims=True)

        def chunk(base, size):
            want = (lax.broadcasted_iota(jnp.int32, (size, 1), 0) + base).astype(F32)
            onehot = jnp.where(rank_e == want, 1.0, 0.0).astype(BF16)
            xg = _dot(onehot, hf_ref[rows, :]).astype(BF16)
            y = _dot(_swiglu_mid(_dot(xg, w1_ref[0]), _dot(xg, w3_ref[0])), w2_ref[0])
            back = lax.dot_general(onehot, y.astype(BF16), (((0,), (0,)), ((), ())),
                                   preferred_element_type=F32)
            o_ref[rows, :] += ge * back

        n_big = n_tok // caps[-1]
        lax.fori_loop(0, n_big, lambda c, carry: (chunk(c * caps[-1], caps[-1]), carry)[1], 0)
        rest = n_tok - n_big * caps[-1]
        bucket = sum((rest > c).astype(jnp.int32) for c in (0,) + caps[:-1])
        lax.switch(bucket, [lambda: None]
                   + [functools.partial(chunk, n_big * caps[-1], c) for c in caps])


def _moe(x, g, wr_hi, wr_lo, br, w1, w3, w2):
    N, D = x.shape
    E, _, F = w1.shape
    tm = _token_tile(N, 1024)
    ts = _token_tile(tm, 512)
    assert E <= SUBLANES
    tok = pl.BlockSpec((tm, D), lambda i, e: (i, 0))
    kern = functools.partial(_moe_kernel, ts=ts, caps=MOE_CAPS, n_exp=E)
    return pl.pallas_call(
        kern,
        grid=(N // tm, E),
        in_specs=[tok, _resident((1, D)), _resident((ROUTER_ROWS, D)), _resident((ROUTER_ROWS, D)),
                  _resident((ROUTER_ROWS, 1)),
                  pl.BlockSpec((1, D, F), lambda i, e: (e, 0, 0)),
                  pl.BlockSpec((1, D, F), lambda i, e: (e, 0, 0)),
                  pl.BlockSpec((1, F, D), lambda i, e: (e, 0, 0))],
        out_specs=tok,
        out_shape=jax.ShapeDtypeStruct((N, D), F32),
        scratch_shapes=[pltpu.VMEM((tm, D), BF16), pltpu.VMEM((tm, LANES), F32),
                        pltpu.VMEM((tm // ts, E, ts), F32)],
        compiler_params=_params(2),
        name="moe",
    )(x, g, wr_hi, wr_lo, br, w1, w3, w2)


def _rot_half_cols(w):
    half = ROPE_DIM // 2
    return jnp.concatenate([-w[..., half:], w[..., :half]], axis=-1)


def _rope_lanes(a):
    return jnp.pad(a, [(0, 0)] * (a.ndim - 1) + [(NOPE_DIM, HEAD_PAD - QK_DIM)])


def _pad_cols(w, n):
    return jnp.pad(w, [(0, 0)] * (w.ndim - 1) + [(0, n - w.shape[-1])])


def _head_pad(nope, rope):
    z = jnp.zeros(nope.shape[:-1] + (HEAD_PAD - QK_DIM,), nope.dtype)
    out = jnp.concatenate([nope, rope, z], axis=-1)
    return out.reshape(out.shape[:-2] + (N_HEADS * HEAD_PAD,))


def _block_diag_tiles(w):
    nblk, bw, _ = w.shape
    per = MXU_DIM // bw
    w4 = w.reshape(nblk // per, per, bw, bw)
    eye = jnp.eye(per, dtype=w.dtype)
    return jnp.einsum("cpij,pq->cpiqj", w4, eye).reshape(nblk // per, MXU_DIM, MXU_DIM)


def _layer_weights(l, p):
    D = p["w_in"].shape[1]
    q_lora = p["q_lat_norm"].shape[-1]
    kv_lora = p["kv_lat_norm"].shape[-1]
    w_in = p["w_in"][l]
    sp = (D, 2 * D, 2 * D + q_lora, 2 * D + q_lora + kv_lora, 2 * D + q_lora + kv_lora + ROPE_DIM,
          3 * D + q_lora + kv_lora + ROPE_DIM)
    x_rec, x_gate, c_q, c_kv, k_rot, g_rec, g_att = jnp.split(w_in, sp, axis=-1)
    w_in_perm = jnp.concatenate(
        [x_rec, x_gate, g_rec, g_att, c_q, c_kv, _rope_lanes(k_rot),
         _rope_lanes(_rot_half_cols(k_rot))], axis=-1).astype(BF16)

    wq = p["w_uq"][l].reshape(q_lora, N_HEADS, QK_DIM)
    wq_n, wq_r = wq[..., :NOPE_DIM], wq[..., NOPE_DIM:]
    wk = p["w_uk"][l].reshape(kv_lora, N_HEADS, NOPE_DIM)
    gq, gk = p["q_head_norm"][l], p["k_head_norm"][l]
    q_scale = (QK_DIM ** -0.5) * LOG2E
    return dict(
        gmix=p["norm_mix"][l].reshape(1, D), w_in=w_in_perm,
        conv_w=p["conv_w"][l], conv_b=p["conv_b"][l].reshape(1, D),
        wa=_block_diag_tiles(p["lru_wa"][l]).astype(BF16), ba=p["lru_ba"][l].reshape(1, D),
        wi=_block_diag_tiles(p["lru_wi"][l]).astype(BF16), bi=p["lru_bi"][l].reshape(1, D),
        sp=jax.nn.softplus(-p["lru_lambda"][l]).reshape(1, D),
        gq=p["q_lat_norm"][l].reshape(1, q_lora), gkv=p["kv_lat_norm"][l].reshape(1, kv_lora),
        wq_t=_head_pad(wq_n, wq_r).T.astype(BF16),
        gqh=(_pad_cols(gq, HEAD_PAD) * q_scale).reshape(HEAD_PAD, 1),
        wk=_head_pad(wk, jnp.zeros((kv_lora, N_HEADS, ROPE_DIM), F32)).astype(BF16),
        wv_t=p["w_uv"][l].T.astype(BF16),
        gkh=_pad_cols(gk, HEAD_PAD).reshape(1, HEAD_PAD),
        score_bound=QK_DIM * jnp.max(jnp.abs(gq)) * jnp.max(jnp.abs(gk)) * q_scale,
        w_out=p["w_out"][l].astype(BF16), gmem=p["norm_mem"][l].reshape(1, D),
        w_mq=p["w_mq"][l].astype(BF16), gmqh=p["mq_head_norm"][l].reshape(1, -1),
        w_mo=p["w_mo"][l].astype(BF16), gffn=p["norm_ffn"][l].reshape(1, D),
    )


def _rope_tables(pos):
    half = ROPE_DIM // 2
    inv_freq = ROPE_BASE ** (-jnp.arange(half, dtype=F32) / half)
    ang = pos.astype(F32)[:, None] * inv_freq[None, :]
    cos2 = jnp.concatenate([jnp.cos(ang), jnp.cos(ang)], axis=-1)
    sin2 = jnp.concatenate([jnp.sin(ang), jnp.sin(ang)], axis=-1)
    return dict(cos_k=_rope_lanes(cos2), sin_k=_rope_lanes(sin2), cos_q=cos2.T, sin_q=sin2.T)


def _layer(l, depth, x, tabs, q_off, conv_prev, h0, past, ckv_stack, mem_k, mem_v, w, p):
    B, L, D = x.shape
    prev8 = jnp.pad(conv_prev, ((0, 0), (SUBLANES - (CONV_W - 1), 0), (0, 0)))
    h0_8 = jnp.pad(h0[:, None, :], ((0, 0), (SUBLANES - 1, 0), (0, 0)))
    grec, gatt, cq, ckv, kr, ctail, htail = _inproj(x, prev8, h0_8, w, tabs["cos_k"], tabs["sin_k"],
                                                    l, depth, ckv_stack)

    q = _qprep(cq, w, tabs["cos_q"], tabs["sin_q"])
    if past is None:
        t_valid = L
        tk = min(512, L)
    else:
        t_valid = past[0].shape[2] + L
        tk = 512
    k, vt = _kvprep(ckv, l, kr, w, tk, past)
    flash = functools.partial(_flash, q_off=q_off, t_valid=t_valid, tk=tk)
    att = lax.cond(w["score_bound"] <= SAFE_LOG2_RANGE, functools.partial(flash, bounded=True),
                   functools.partial(flash, bounded=False), q, k, vt)

    x2 = _post(x, grec, gatt, att, mem_k, mem_v, w).reshape(B * L, D)
    j = l // 2
    if l % 2 == 0:
        x3 = _ffn(x2, w["gffn"], p["ffn_w1"][j].astype(BF16), p["ffn_w3"][j].astype(BF16),
                  p["ffn_w2"][j].astype(BF16))
    else:
        E = p["moe_router"].shape[-1]
        wr = _pad_cols(p["moe_router"][j], ROUTER_ROWS).T
        wr_hi = wr.astype(BF16)
        wr_lo = (wr - wr_hi.astype(F32)).astype(BF16)
        br = jnp.concatenate([p["moe_router_b"][j],
                              jnp.full((ROUTER_ROWS - E,), NEG, F32)]).reshape(ROUTER_ROWS, 1)
        x3 = _moe(x2, w["gffn"], wr_hi, wr_lo, br, p["moe_w1"][j].astype(BF16),
                  p["moe_w3"][j].astype(BF16), p["moe_w2"][j].astype(BF16))
    conv_state = ctail[:, SUBLANES - (CONV_W - 1):, :]
    return x3.reshape(B, L, D), conv_state, htail[:, SUBLANES - 1, :], ckv, kr[..., NOPE_DIM:QK_DIM]


def kernel(x_prompt, x_sample, cache_ckv, cache_krope, cache_mem_k, cache_mem_v, state_lru, state_conv, mem_prompt, norm_mix, w_in, conv_w, conv_b, lru_wa, lru_ba, lru_wi, lru_bi, lru_lambda, q_lat_norm, w_uq, kv_lat_norm, w_uk, w_uv, q_head_norm, k_head_norm, w_out, norm_mem, mem_in_norm, w_mq, w_mk, w_mv, w_mo, mq_head_norm, mk_head_norm, norm_ffn, ffn_w1, ffn_w3, ffn_w2, moe_router, moe_router_b, moe_w1, moe_w3, moe_w2):
    p = dict(norm_mix=norm_mix, w_in=w_in, conv_w=conv_w, conv_b=conv_b, lru_wa=lru_wa, lru_ba=lru_ba,
             lru_wi=lru_wi, lru_bi=lru_bi, lru_lambda=lru_lambda, q_lat_norm=q_lat_norm, w_uq=w_uq,
             kv_lat_norm=kv_lat_norm, w_uk=w_uk, w_uv=w_uv, q_head_norm=q_head_norm,
             k_head_norm=k_head_norm, w_out=w_out, norm_mem=norm_mem, w_mq=w_mq, w_mo=w_mo,
             mq_head_norm=mq_head_norm, norm_ffn=norm_ffn, ffn_w1=ffn_w1, ffn_w3=ffn_w3,
             ffn_w2=ffn_w2, moe_router=moe_router, moe_router_b=moe_router_b, moe_w1=moe_w1,
             moe_w3=moe_w3, moe_w2=moe_w2)
    depth = w_in.shape[0]
    weights = [_layer_weights(l, p) for l in range(depth)]

    Bp, Lp, D = x_prompt.shape
    Bs, Ls, _ = x_sample.shape
    t_past = cache_ckv.shape[2]
    M = mem_prompt.shape[1]

    tabs_p = _rope_tables(jnp.arange(Lp, dtype=jnp.int32))
    zero_conv = jnp.zeros((Bp, CONV_W - 1, D), F32)
    zero_h = jnp.zeros((Bp, D), F32)
    x = x_prompt
    ckv_p = None
    outs_p = [[] for _ in range(5)]
    for l in range(depth):
        mk, mv, mk_b, mv_b = _memkv(mem_prompt, mem_in_norm[l], w_mk[l], w_mv[l], mk_head_norm[l])
        x, cs, hl, ckv_p, kr = _layer(l, depth, x, tabs_p, 0, zero_conv, zero_h, None, ckv_p, mk_b, mv_b,
                                      weights[l], p)
        for acc, val in zip(outs_p, (kr, hl, cs, mk, mv)):
            acc.append(val)
    y_prompt = x

    tabs_s = _rope_tables(t_past + jnp.arange(Ls, dtype=jnp.int32))
    x = x_sample
    ckv_s = None
    outs_s = [[] for _ in range(3)]
    cache_kr = _rope_lanes(cache_krope)
    for l in range(depth):
        x, cs, hl, ckv_s, kr = _layer(l, depth, x, tabs_s, t_past, state_conv[l], state_lru[l],
                                      (cache_ckv, cache_kr, l), ckv_s,
                                      cache_mem_k[l].reshape(Bs, M, D).astype(BF16),
                                      cache_mem_v[l].reshape(Bs, M, D).astype(BF16), weights[l], p)
        for acc, val in zip(outs_s, (kr, hl, cs)):
            acc.append(val)
    y_sample = x

    return ((y_prompt, y_sample, ckv_p) + tuple(jnp.stack(a) for a in outs_p)
            + (ckv_s,) + tuple(jnp.stack(a) for a in outs_s))
```
